```python
import jax, jax.numpy as jnp
from jax import lax
import numpy as np

D_MODEL = 1024
BATCH = 16
SEQ = 2048
DEPTH = 2

ML_HEADS = 4
ML_DK = 64
ML_DV = 128
ML_CHUNK = 64
CONV_K = 4
GLA_HEADS = 4
GLA_DK = 64
GLA_DV = 128
GLA_CHUNK = 64
GLA_LOWRANK = 16
GLA_TAU = 16.0
ML_QK_W = ML_HEADS * ML_DK
ML_V_W = ML_HEADS * ML_DV
GLA_QK_W = GLA_HEADS * GLA_DK
GLA_V_W = GLA_HEADS * GLA_DV
AB_SPLITS = (2 * ML_QK_W, ML_V_W, ML_V_W, ML_HEADS, ML_HEADS, GLA_QK_W, GLA_QK_W, GLA_V_W, GLA_V_W, GLA_LOWRANK)
AB_IN = sum(AB_SPLITS)
AB_OUT = ML_V_W + GLA_V_W
SWA_Q_HEADS = 16
SWA_KV_HEADS = 2
SWA_HEAD_DIM = 64
SWA_GROUP = SWA_Q_HEADS // SWA_KV_HEADS
WINDOW = 128
ROT_DIM = SWA_HEAD_DIM // 4
ROPE_THETA = 500000.0
QKV_W = (SWA_Q_HEADS + 2 * SWA_KV_HEADS) * SWA_HEAD_DIM
N_GROUPS = 4
EXPERTS_PER_GROUP = 8
N_EXPERTS = N_GROUPS * EXPERTS_PER_GROUP
TOP_K = 2
D_FF_EXPERT = 512
MOE_BLOCK = 128
EPS = 1e-6

kernel_name = "hybrid_mlstm_gla_swa_hmoe"

F32 = jnp.float32


def rmsnorm(x, g):
    x32 = x.astype(F32)
    y = x32 * lax.rsqrt(jnp.mean(x32 * x32, axis=-1, keepdims=True) + EPS)
    return (y * g.astype(F32)).astype(x.dtype)


def causal_depthwise_conv(x, w, b):
    y = lax.conv_general_dilated(x, w[:, None, :], window_strides=(1,), padding=[(CONV_K - 1, 0)],
                                 dimension_numbers=('NWC', 'WIO', 'NWC'), feature_group_count=x.shape[-1])
    return y + b


def to_chunks(t, size):
    b, s, h, d = t.shape
    return t.reshape(b, s // size, size, h, d).transpose(0, 3, 1, 2, 4)


def from_chunks(t):
    b, h, nc, l, d = t.shape
    return t.transpose(0, 2, 3, 1, 4).reshape(b, nc * l, h, d)


def mlstm_chunkwise(q, k, v, i_pre, f_pre):
    L = ML_CHUNK
    q = to_chunks(q.astype(F32), L)
    k = to_chunks(k.astype(F32), L) * (ML_DK ** -0.5)
    v = to_chunks(v.astype(F32), L)
    ig = to_chunks(i_pre.astype(F32)[..., None], L)[..., 0]
    logf = jax.nn.log_sigmoid(to_chunks(f_pre.astype(F32)[..., None], L)[..., 0])
    bcum = jnp.cumsum(logf, axis=-1)
    g = bcum[..., -1]
    a = g[..., None] - bcum + ig
    a_max = jnp.max(a, axis=-1)
    wa = jnp.exp(a - a_max[..., None])
    c_chunk = jnp.einsum('bhcs,bhcsk,bhcsv->bhckv', wa, k, v)
    n_chunk = jnp.einsum('bhcs,bhcsk->bhck', wa, k)

    def step(carry, xs):
        c_st, n_st, m_st = carry
        cc, nc, am, gc = xs
        m_new = jnp.maximum(gc + m_st, am)
        s_old = jnp.exp(gc + m_st - m_new)
        s_in = jnp.exp(am - m_new)
        c_new = s_old[..., None, None] * c_st + s_in[..., None, None] * cc
        n_new = s_old[..., None] * n_st + s_in[..., None] * nc
        return (c_new, n_new, m_new), (c_st, n_st, m_st)

    bsz, nh, _, _, dk = k.shape
    dv = v.shape[-1]
    init = (jnp.zeros((bsz, nh, dk, dv), F32), jnp.zeros((bsz, nh, dk), F32), jnp.full((bsz, nh), -jnp.inf, F32))
    mv = lambda t: jnp.moveaxis(t, 2, 0)
    _, (c_prev, n_prev, m_prev) = lax.scan(step, init, (mv(c_chunk), mv(n_chunk), mv(a_max), mv(g)))
    c_prev = jnp.moveaxis(c_prev, 0, 2)
    n_prev = jnp.moveaxis(n_prev, 0, 2)
    m_prev = jnp.moveaxis(m_prev, 0, 2)

    causal = jnp.tril(jnp.ones((L, L), dtype=bool))
    d_log = jnp.where(causal, bcum[..., :, None] - bcum[..., None, :] + ig[..., None, :], -jnp.inf)
    inter_log = bcum + m_prev[..., None]
    m_t = jnp.maximum(inter_log, jnp.max(d_log, axis=-1))
    s_inter = jnp.exp(inter_log - m_t)
    qk = jnp.einsum('bhctk,bhcsk->bhcts', q, k) * jnp.exp(d_log - m_t[..., None])
    num = s_inter[..., None] * jnp.einsum('bhctk,bhckv->bhctv', q, c_prev) + jnp.einsum('bhcts,bhcsv->bhctv', qk, v)
    den = s_inter * jnp.einsum('bhctk,bhck->bhct', q, n_prev) + jnp.sum(qk, axis=-1)
    h = num / jnp.maximum(jnp.abs(den), jnp.exp(-m_t))[..., None]
    return from_chunks(h)


def gla_chunkwise(q, k, v, log_alpha):
    L = GLA_CHUNK
    q = to_chunks(q.astype(F32), L)
    k = to_chunks(k.astype(F32), L) * (GLA_DK ** -0.5)
    v = to_chunks(v.astype(F32), L)
    bcum = jnp.cumsum(to_chunks(log_alpha.astype(F32), L), axis=3)
    g = bcum[..., -1, :]
    q_dec = q * jnp.exp(bcum)
    k_inv = k * jnp.exp(-bcum)
    k_end = k * jnp.exp(g[..., None, :] - bcum)
    s_chunk = jnp.einsum('bhcsk,bhcsv->bhckv', k_end, v)

    def step(s, xs):
        sc, gc = xs
        return jnp.exp(gc)[..., None] * s + sc, s

    bsz, nh, _, _, dk = k.shape
    dv = v.shape[-1]
    mv = lambda t: jnp.moveaxis(t, 2, 0)
    _, s_prev = lax.scan(step, jnp.zeros((bsz, nh, dk, dv), F32), (mv(s_chunk), mv(g)))
    s_prev = jnp.moveaxis(s_prev, 0, 2)
    causal = jnp.tril(jnp.ones((L, L), dtype=bool))
    att = jnp.where(causal, jnp.einsum('bhctk,bhcsk->bhcts', q_dec, k_inv), 0.0)
    o = jnp.einsum('bhctk,bhckv->bhctv', q_dec, s_prev) + jnp.einsum('bhcts,bhcsv->bhctv', att, v)
    return from_chunks(o)


def mlstm_gla_mixer(u, w_in, conv_w, conv_b, ig_b, fg_b, ml_norm, lr_up, gate_b, gla_norm, w_out):
    bsz, s, _ = u.shape
    idx = np.cumsum(AB_SPLITS)[:-1].tolist()
    (ml_qk, ml_v, ml_o, ml_i, ml_f, gq, gk, gv, gg, glr) = jnp.split(u @ w_in, idx, axis=-1)
    ml_qk = jax.nn.silu(causal_depthwise_conv(ml_qk, conv_w, conv_b))
    ml_q, ml_k = jnp.split(ml_qk, 2, axis=-1)
    ml_h = mlstm_chunkwise(ml_q.reshape(bsz, s, ML_HEADS, ML_DK), ml_k.reshape(bsz, s, ML_HEADS, ML_DK),
                           ml_v.reshape(bsz, s, ML_HEADS, ML_DV), ml_i + ig_b, ml_f + fg_b)
    mu = jnp.mean(ml_h, axis=-1, keepdims=True)
    ml_h = (ml_h - mu) * lax.rsqrt(jnp.mean(jnp.square(ml_h - mu), axis=-1, keepdims=True) + EPS)
    ml_out = ml_h.reshape(bsz, s, ML_V_W) * ml_norm.astype(F32) * jax.nn.sigmoid(ml_o.astype(F32))
    log_alpha = jax.nn.log_sigmoid((glr @ lr_up + gate_b).astype(F32)) / GLA_TAU
    gla_h = gla_chunkwise(gq.reshape(bsz, s, GLA_HEADS, GLA_DK), gk.reshape(bsz, s, GLA_HEADS, GLA_DK),
                          gv.reshape(bsz, s, GLA_HEADS, GLA_DV), log_alpha.reshape(bsz, s, GLA_HEADS, GLA_DK))
    gla_h = gla_h * lax.rsqrt(jnp.mean(gla_h * gla_h, axis=-1, keepdims=True) + EPS)
    gla_out = gla_h.reshape(bsz, s, GLA_V_W) * gla_norm.astype(F32) * jax.nn.silu(gg.astype(F32))
    y = jnp.concatenate([ml_out, gla_out], axis=-1).astype(u.dtype)
    return y @ w_out


def partial_rotary(t, cos, sin):
    half = ROT_DIM // 2
    t1 = t[..., :half].astype(F32)
    t2 = t[..., half:ROT_DIM].astype(F32)
    rot = jnp.concatenate([t1 * cos - t2 * sin, t2 * cos + t1 * sin], axis=-1).astype(t.dtype)
    return jnp.concatenate([rot, t[..., ROT_DIM:]], axis=-1)


def swa_sink_attention(u, positions, w_qkv, b_qkv, sinks, w_o, b_o):
    bsz, s, _ = u.shape
    q, k, v = jnp.split(u @ w_qkv + b_qkv, [SWA_Q_HEADS * SWA_HEAD_DIM, (SWA_Q_HEADS + SWA_KV_HEADS) * SWA_HEAD_DIM], axis=-1)
    q = q.reshape(bsz, s, SWA_Q_HEADS, SWA_HEAD_DIM)
    k = k.reshape(bsz, s, SWA_KV_HEADS, SWA_HEAD_DIM)
    v = v.reshape(bsz, s, SWA_KV_HEADS, SWA_HEAD_DIM)
    inv_freq = ROPE_THETA ** (-jnp.arange(0, ROT_DIM, 2, dtype=F32) / ROT_DIM)
    ang = positions.astype(F32)[..., None] * inv_freq
    cos, sin = jnp.cos(ang)[:, :, None, :], jnp.sin(ang)[:, :, None, :]
    q = partial_rotary(q, cos, sin)
    k = partial_rotary(k, cos, sin)
    nb = s // WINDOW
    qb = q.reshape(bsz, nb, WINDOW, SWA_KV_HEADS, SWA_GROUP, SWA_HEAD_DIM)
    kb = k.reshape(bsz, nb, WINDOW, SWA_KV_HEADS, SWA_HEAD_DIM)
    vb = v.reshape(bsz, nb, WINDOW, SWA_KV_HEADS, SWA_HEAD_DIM)
    prev = lambda t: jnp.concatenate([jnp.zeros_like(t[:, :1]), t[:, :-1]], axis=1)
    kk = jnp.concatenate([prev(kb), kb], axis=2)
    vv = jnp.concatenate([prev(vb), vb], axis=2)
    scores = jnp.einsum('bnqhgd,bnkhd->bnhgqk', qb, kk, preferred_element_type=F32) * (SWA_HEAD_DIM ** -0.5)
    qi = jnp.arange(WINDOW)[:, None] + WINDOW
    kj = jnp.arange(2 * WINDOW)[None, :]
    rel = qi - kj
    valid = ((rel >= 0) & (rel < WINDOW))[None] & ((jnp.arange(nb)[:, None, None] > 0) | (kj[None] >= WINDOW))
    scores = jnp.where(valid[None, :, None, None], scores, -jnp.inf)
    sink = sinks.astype(F32).reshape(SWA_KV_HEADS, SWA_GROUP)[None, None, :, :, None]
    m = jnp.maximum(jnp.max(scores, axis=-1), sink)
    p = jnp.exp(scores - m[..., None])
    denom = jnp.sum(p, axis=-1) + jnp.exp(sink - m)
    o = jnp.einsum('bnhgqk,bnkhd->bnhgqd', p, vv.astype(F32)) / denom[..., None]
    o = o.transpose(0, 1, 4, 2, 3, 5).reshape(bsz, s, SWA_Q_HEADS * SWA_HEAD_DIM).astype(u.dtype)
    return o @ w_o + b_o


def hierarchical_moe(u, wg, bg, we, be, w_gate, w_up, w_down):
    bsz, s, d = u.shape
    t = bsz * s
    xt = u.reshape(t, d)
    g_probs = jax.nn.softmax((xt @ wg).astype(F32) + bg.astype(F32), axis=-1)
    g_p, g_idx = lax.top_k(g_probs, 1)
    e_logits = ((xt @ we).astype(F32) + be.astype(F32)).reshape(t, N_GROUPS, EXPERTS_PER_GROUP)
    e_in_group = jnp.take_along_axis(e_logits, g_idx[:, :, None], axis=1)[:, 0]
    top_l, top_i = lax.top_k(e_in_group, TOP_K)
    gates = g_p * jax.nn.softmax(top_l, axis=-1)
    expert_ids = g_idx * EXPERTS_PER_GROUP + top_i
    n_assign = t * TOP_K
    n_blocks = (n_assign + MOE_BLOCK - 1) // MOE_BLOCK + N_EXPERTS
    n_pad = n_blocks * MOE_BLOCK
    flat_e = expert_ids.reshape(-1)
    flat_tok = jnp.repeat(jnp.arange(t, dtype=jnp.int32), TOP_K)
    flat_w = gates.reshape(-1)
    order = jnp.argsort(flat_e)
    e_sorted = flat_e[order]
    counts = jnp.bincount(flat_e, length=N_EXPERTS)
    padded = (counts + MOE_BLOCK - 1) // MOE_BLOCK * MOE_BLOCK
    pad_end = jnp.cumsum(padded)
    pad_start = pad_end - padded
    start = jnp.cumsum(counts) - counts
    dest = pad_start[e_sorted] + jnp.arange(n_assign, dtype=jnp.int32) - start[e_sorted]
    tok_buf = jnp.zeros((n_pad,), jnp.int32).at[dest].set(flat_tok[order])
    w_buf = jnp.zeros((n_pad,), F32).at[dest].set(flat_w[order])
    block_expert = jnp.minimum(jnp.searchsorted(pad_end, jnp.arange(n_blocks, dtype=jnp.int32) * MOE_BLOCK, side='right'), N_EXPERTS - 1)
    x_buf = xt[tok_buf].reshape(n_blocks, MOE_BLOCK, d)

    def expert_block(args):
        xb, e = args
        return (jax.nn.silu(xb @ w_gate[e]) * (xb @ w_up[e])) @ w_down[e]

    y_buf = lax.map(expert_block, (x_buf, block_expert)).reshape(n_pad, d)
    out = jnp.zeros((t, d), F32).at[tok_buf].add(y_buf.astype(F32) * w_buf[:, None])
    return out.astype(u.dtype).reshape(bsz, s, d)


def setup_inputs(seed: int = 0) -> dict:
    key = jax.random.key(seed)
    ks = jax.random.split(key, 32)
    n_even = (DEPTH + 1) // 2
    n_odd = DEPTH // 2
    nrm = lambda k, shape, scale: jax.random.normal(k, shape, F32) * scale
    gain = lambda k, shape: 1.0 + 0.02 * jax.random.normal(k, shape, F32)
    offset = jax.random.randint(ks[1], (BATCH, 1), 0, 4096, dtype=jnp.int32)
    positions = offset + jnp.arange(SEQ, dtype=jnp.int32)[None, :]
    fgate_b = jnp.linspace(3.0, 6.0, ML_HEADS, dtype=F32)[None, :] + 0.1 * jax.random.normal(ks[9], (n_even, ML_HEADS), F32)
    return {
        "x": nrm(ks[0], (BATCH, SEQ, D_MODEL), 1.0),
        "positions": positions,
        "mix_norm": gain(ks[2], (DEPTH, D_MODEL)),
        "ffn_norm": gain(ks[3], (DEPTH, D_MODEL)),
        "final_norm": gain(ks[4], (D_MODEL,)),
        "ab_w_in": nrm(ks[5], (n_even, D_MODEL, AB_IN), D_MODEL ** -0.5),
        "ab_conv_w": nrm(ks[6], (n_even, CONV_K, 2 * ML_QK_W), CONV_K ** -0.5),
        "ab_conv_b": nrm(ks[7], (n_even, 2 * ML_QK_W), 0.02),
        "ml_igate_b": nrm(ks[8], (n_even, ML_HEADS), 0.1),
        "ml_fgate_b": fgate_b,
        "ml_head_norm": gain(ks[10], (n_even, ML_V_W)),
        "gla_w_lr_up": nrm(ks[11], (n_even, GLA_LOWRANK, GLA_QK_W), GLA_LOWRANK ** -0.5),
        "gla_gate_b": nrm(ks[12], (n_even, GLA_QK_W), 0.1),
        "gla_head_norm": gain(ks[13], (n_even, GLA_V_W)),
        "ab_w_out": nrm(ks[14], (n_even, AB_OUT, D_MODEL), AB_OUT ** -0.5),
        "swa_w_qkv": nrm(ks[15], (n_odd, D_MODEL, QKV_W), D_MODEL ** -0.5),
        "swa_b_qkv": nrm(ks[16], (n_odd, QKV_W), 0.02),
        "swa_sinks": nrm(ks[17], (n_odd, SWA_Q_HEADS), 0.5),
        "swa_w_o": nrm(ks[18], (n_odd, SWA_Q_HEADS * SWA_HEAD_DIM, D_MODEL), (SWA_Q_HEADS * SWA_HEAD_DIM) ** -0.5),
        "swa_b_o": nrm(ks[19], (n_odd, D_MODEL), 0.02),
        "router_group_w": nrm(ks[20], (DEPTH, D_MODEL, N_GROUPS), D_MODEL ** -0.5),
        "router_group_b": nrm(ks[21], (DEPTH, N_GROUPS), 0.01),
        "router_expert_w": nrm(ks[22], (DEPTH, D_MODEL, N_EXPERTS), D_MODEL ** -0.5),
        "router_expert_b": nrm(ks[23], (DEPTH, N_EXPERTS), 0.01),
        "expert_w_gate": nrm(ks[24], (DEPTH, N_EXPERTS, D_MODEL, D_FF_EXPERT), D_MODEL ** -0.5),
        "expert_w_up": nrm(ks[25], (DEPTH, N_EXPERTS, D_MODEL, D_FF_EXPERT), D_MODEL ** -0.5),
        "expert_w_down": nrm(ks[26], (DEPTH, N_EXPERTS, D_FF_EXPERT, D_MODEL), D_FF_EXPERT ** -0.5),
    }


def reference(x, positions, mix_norm, ffn_norm, final_norm, ab_w_in, ab_conv_w, ab_conv_b, ml_igate_b, ml_fgate_b,
              ml_head_norm, gla_w_lr_up, gla_gate_b, gla_head_norm, ab_w_out, swa_w_qkv, swa_b_qkv, swa_sinks,
              swa_w_o, swa_b_o, router_group_w, router_group_b, router_expert_w, router_expert_b,
              expert_w_gate, expert_w_up, expert_w_down):
    h = x
    for layer in range(DEPTH):
        j = layer // 2
        u = rmsnorm(h, mix_norm[layer])
        if layer % 2 == 0:
            mix = mlstm_gla_mixer(u, ab_w_in[j], ab_conv_w[j], ab_conv_b[j], ml_igate_b[j], ml_fgate_b[j],
                                  ml_head_norm[j], gla_w_lr_up[j], gla_gate_b[j], gla_head_norm[j], ab_w_out[j])
        else:
            mix = swa_sink_attention(u, positions, swa_w_qkv[j], swa_b_qkv[j], swa_sinks[j], swa_w_o[j], swa_b_o[j])
        h = h + mix.astype(h.dtype)
        u = rmsnorm(h, ffn_norm[layer])
        h = h + hierarchical_moe(u, router_group_w[layer], router_group_b[layer], router_expert_w[layer],
                                 router_expert_b[layer], expert_w_gate[layer], expert_w_up[layer],
                                 expert_w_down[layer]).astype(h.dtype)
    return rmsnorm(h, final_norm)
```

```python
import functools

import jax
import jax.numpy as jnp
import numpy as np
from jax import lax
from jax.experimental import pallas as pl
from jax.experimental.pallas import tpu as pltpu

F32 = jnp.float32
BF16 = jnp.bfloat16
HIGHEST = lax.Precision.HIGHEST

D_MODEL = 1024
EPS = 1e-6
ML_HEADS = 4
ML_DK = 64
ML_DV = 128
GLA_HEADS = 4
GLA_DK = 64
GLA_DV = 128
CHUNK = 64
CONV_K = 4
GLA_LOWRANK = 16
GLA_TAU = 16.0
ML_QK_W = ML_HEADS * ML_DK
ML_V_W = ML_HEADS * ML_DV
GLA_QK_W = GLA_HEADS * GLA_DK
GLA_V_W = GLA_HEADS * GLA_DV
AB_SPLITS = (2 * ML_QK_W, ML_V_W, ML_V_W, ML_HEADS, ML_HEADS, GLA_QK_W, GLA_QK_W, GLA_V_W, GLA_V_W, GLA_LOWRANK)
OFF_MQK = 0
OFF_MV = OFF_MQK + 2 * ML_QK_W
OFF_MO = OFF_MV + ML_V_W
OFF_GQ = OFF_MO + ML_V_W
OFF_GK = OFF_GQ + GLA_QK_W
OFF_GV = OFF_GK + GLA_QK_W
OFF_GG = OFF_GV + GLA_V_W
Z_MAIN_W = OFF_GG + GLA_V_W
SM_I = 0
SM_F = ML_HEADS
SM_LR = 2 * ML_HEADS
LANES = 128
SWA_Q_HEADS = 16
SWA_KV_HEADS = 2
SWA_HEAD_DIM = 64
SWA_GROUP = SWA_Q_HEADS // SWA_KV_HEADS
WINDOW = 128
ROT_DIM = SWA_HEAD_DIM // 4
ROPE_THETA = 500000.0
Q_W = SWA_Q_HEADS * SWA_HEAD_DIM
KV_W = SWA_KV_HEADS * SWA_HEAD_DIM
N_GROUPS = 4
EXPERTS_PER_GROUP = 8
N_EXPERTS = N_GROUPS * EXPERTS_PER_GROUP
TOP_K = 2
D_FF = 512
MOE_BM = 256

VMEM_LIMIT = 56 * 1024 * 1024


def _cparams(sem):
    return pltpu.CompilerParams(dimension_semantics=sem, vmem_limit_bytes=VMEM_LIMIT)


def _rms(x, g):
    return x * lax.rsqrt(jnp.mean(x * x, axis=-1, keepdims=True) + EPS) * g


def _log_sigmoid(x):
    return jnp.minimum(x, 0.0) - jnp.log1p(jnp.exp(-jnp.abs(x)))


def _sigmoid(x):
    return 1.0 / (1.0 + jnp.exp(-x))


def _dot(a, b):
    return jnp.dot(a, b, preferred_element_type=F32)


def _dot_nt(a, b):
    return lax.dot_general(a, b, (((1,), (1,)), ((), ())), preferred_element_type=F32)


def _dot_tn(a, b):
    return lax.dot_general(a, b, (((0,), (0,)), ((), ())), preferred_element_type=F32)


def _in_proj_kernel(x_ref, g_ref, w_ref, ws_ref, zm_ref, zs_ref, *, n_chunk):
    u = _rms(x_ref[...], g_ref[...]).astype(BF16)
    for n0 in range(0, w_ref.shape[1], n_chunk):
        zm_ref[:, n0:n0 + n_chunk] = _dot(u, w_ref[:, n0:n0 + n_chunk]).astype(zm_ref.dtype)
    zs_ref[...] = _dot(u, ws_ref[...])


def _in_proj(h, g, w_main, w_small, tm):
    t = h.shape[0]
    return pl.pallas_call(
        functools.partial(_in_proj_kernel, n_chunk=768),
        out_shape=(jax.ShapeDtypeStruct((t, Z_MAIN_W), BF16), jax.ShapeDtypeStruct((t, LANES), F32)),
        grid=(t // tm,),
        in_specs=[pl.BlockSpec((tm, D_MODEL), lambda i: (i, 0)),
                  pl.BlockSpec((1, D_MODEL), lambda i: (0, 0)),
                  pl.BlockSpec((D_MODEL, Z_MAIN_W), lambda i: (0, 0)),
                  pl.BlockSpec((D_MODEL, LANES), lambda i: (0, 0))],
        out_specs=(pl.BlockSpec((tm, Z_MAIN_W), lambda i: (i, 0)),
                   pl.BlockSpec((tm, LANES), lambda i: (i, 0))),
        compiler_params=_cparams(("parallel",)),
        name="in_proj",
    )(h, g, w_main, w_small)


def _mixer_kernel(zm_ref, zs_ref, convw_ref, convb_ref, igb_ref, fgb_ref, lrup_ref, gateb_ref, mlnorm_ref,
                  glanorm_ref, y_ref, xpad, q_s, k_s, ig_s, lf_s, la_s, hml_s, hgla_s, c_s, n_s, m_s, st_s, *, tb):
    @pl.when(pl.program_id(1) == 0)
    def _():
        xpad[0:8, :] = jnp.zeros((8, 2 * ML_QK_W), F32)
        c_s[...] = jnp.zeros(c_s.shape, F32)
        n_s[...] = jnp.zeros(n_s.shape, F32)
        st_s[...] = jnp.zeros(st_s.shape, F32)
        m_s[...] = jnp.full(m_s.shape, -jnp.inf, F32)

    xpad[8:8 + tb, :] = zm_ref[:, OFF_MQK:OFF_MQK + 2 * ML_QK_W].astype(F32)
    conv = convb_ref[...] + convw_ref[3:4, :] * xpad[8:8 + tb, :]
    for j in range(CONV_K - 1):
        conv = conv + convw_ref[j:j + 1, :] * xpad[5 + j:5 + j + tb, :]
    xpad[0:8, :] = xpad[tb:tb + 8, :]
    qk = conv * _sigmoid(conv)
    q_s[...] = qk[:, :ML_QK_W]
    k_s[...] = qk[:, ML_QK_W:] * (ML_DK ** -0.5)

    zs = zs_ref[...]
    ig_s[...] = zs + igb_ref[...]
    lf_s[...] = _log_sigmoid(pltpu.roll(zs, LANES - SM_F, 1) + fgb_ref[...])
    la_s[...] = _log_sigmoid(_dot(zs.astype(BF16), lrup_ref[...]) + gateb_ref[...]) * (1.0 / GLA_TAU)

    row = lax.broadcasted_iota(jnp.int32, (CHUNK, CHUNK), 0)
    col = lax.broadcasted_iota(jnp.int32, (CHUNK, CHUNK), 1)
    causal = col <= row
    tri = causal.astype(F32)

    def chunk(c, carry):
        rows = pl.ds(pl.multiple_of(c * CHUNK, CHUNK), CHUNK)
        ig = ig_s[rows, :]
        bc = jnp.dot(tri, lf_s[rows, :], precision=HIGHEST, preferred_element_type=F32)
        g_row = bc[CHUNK - 1:CHUNK, :]
        a = g_row - bc + ig
        a_max = jnp.max(a, axis=0, keepdims=True)
        wa = jnp.exp(a - a_max)
        m_row = m_s[...]
        m_new = jnp.maximum(g_row + m_row, a_max)
        s_old = jnp.exp(g_row + m_row - m_new)
        s_in = jnp.exp(a_max - m_new)
        il_all = bc + m_row
        e = ig - bc
        et = jnp.transpose(jnp.concatenate([e, e], axis=0))
        q = q_s[rows, :]
        k = k_s[rows, :]
        for h in range(ML_HEADS):
            qh = q[:, h * ML_DK:(h + 1) * ML_DK]
            kh = k[:, h * ML_DK:(h + 1) * ML_DK]
            vh = zm_ref[rows, OFF_MV + h * ML_DV:OFF_MV + (h + 1) * ML_DV]
            kw = kh * wa[:, h:h + 1]
            c_chunk = _dot_tn(kw.astype(BF16), vh)
            n_chunk = jnp.sum(kw, axis=0, keepdims=True)
            d_log = jnp.where(causal, bc[:, h:h + 1] + et[h:h + 1, 0:CHUNK], -jnp.inf)
            il = il_all[:, h:h + 1]
            m_t = jnp.maximum(il, jnp.max(d_log, axis=-1, keepdims=True))
            s_inter = jnp.exp(il - m_t)
            qhb = qh.astype(BF16)
            qkm = _dot_nt(qhb, kh.astype(BF16)) * jnp.exp(d_log - m_t)
            c_st = c_s[h]
            n_st = n_s[h]
            num = s_inter * _dot(qhb, c_st.astype(BF16)) + _dot(qkm.astype(BF16), vh)
            den = s_inter * jnp.sum(qh * n_st, axis=-1, keepdims=True) + jnp.sum(qkm, axis=-1, keepdims=True)
            hml_s[rows, h * ML_DV:(h + 1) * ML_DV] = num / jnp.maximum(jnp.abs(den), jnp.exp(-m_t))
            c_s[h] = s_old[:, h:h + 1] * c_st + s_in[:, h:h + 1] * c_chunk
            n_s[h] = s_old[:, h:h + 1] * n_st + s_in[:, h:h + 1] * n_chunk
        m_s[...] = m_new
        bcg = jnp.dot(tri, la_s[rows, :], precision=HIGHEST, preferred_element_type=F32)
        gg_row = bcg[CHUNK - 1:CHUNK, :]
        gq = zm_ref[rows, OFF_GQ:OFF_GQ + GLA_QK_W].astype(F32)
        gk = zm_ref[rows, OFF_GK:OFF_GK + GLA_QK_W].astype(F32) * (GLA_DK ** -0.5)
        q_dec = (gq * jnp.exp(bcg)).astype(BF16)
        k_inv = (gk * jnp.exp(-bcg)).astype(BF16)
        k_end = (gk * jnp.exp(gg_row - bcg)).astype(BF16)
        eg = jnp.exp(gg_row)
        for h in range(GLA_HEADS):
            sl = slice(h * GLA_DK, (h + 1) * GLA_DK)
            vh = zm_ref[rows, OFF_GV + h * GLA_DV:OFF_GV + (h + 1) * GLA_DV]
            att = jnp.where(causal, _dot_nt(q_dec[:, sl], k_inv[:, sl]), 0.0)
            st = st_s[h]
            hgla_s[rows, h * GLA_DV:(h + 1) * GLA_DV] = _dot_nt(q_dec[:, sl], st.astype(BF16)) + _dot(att.astype(BF16), vh)
            st_s[h] = st * eg[:, sl] + _dot_tn(vh, k_end[:, sl])
        return carry

    lax.fori_loop(0, tb // CHUNK, chunk, 0)

    for h in range(ML_HEADS):
        sl = slice(h * ML_DV, (h + 1) * ML_DV)
        hh = hml_s[:, sl]
        d = hh - jnp.mean(hh, axis=-1, keepdims=True)
        hn = d * lax.rsqrt(jnp.mean(d * d, axis=-1, keepdims=True) + EPS)
        og = zm_ref[:, OFF_MO + h * ML_DV:OFF_MO + (h + 1) * ML_DV].astype(F32)
        y_ref[:, sl] = (hn * mlnorm_ref[:, sl] * _sigmoid(og)).astype(y_ref.dtype)
    for h in range(GLA_HEADS):
        sl = slice(h * GLA_DV, (h + 1) * GLA_DV)
        o = hgla_s[:, sl]
        on = o * lax.rsqrt(jnp.mean(o * o, axis=-1, keepdims=True) + EPS)
        gg = zm_ref[:, OFF_GG + h * GLA_DV:OFF_GG + (h + 1) * GLA_DV].astype(F32)
        y_ref[:, ML_V_W + h * GLA_DV:ML_V_W + (h + 1) * GLA_DV] = (on * glanorm_ref[:, sl] * (gg * _sigmoid(gg))).astype(y_ref.dtype)


def _mixer(zm, zs, convw, convb, igb, fgb, lrup, gateb, mlnorm, glanorm, bsz, seq, tb):
    nt = seq // tb
    const = lambda shape: pl.BlockSpec(shape, lambda b, i: (0,) * len(shape))
    return pl.pallas_call(
        functools.partial(_mixer_kernel, tb=tb),
        out_shape=jax.ShapeDtypeStruct((bsz * seq, ML_V_W + GLA_V_W), BF16),
        grid=(bsz, nt),
        in_specs=[pl.BlockSpec((tb, Z_MAIN_W), lambda b, i: (b * nt + i, 0)),
                  pl.BlockSpec((tb, LANES), lambda b, i: (b * nt + i, 0)),
                  const((CONV_K, 2 * ML_QK_W)), const((1, 2 * ML_QK_W)), const((1, LANES)), const((1, LANES)),
                  const((LANES, GLA_QK_W)), const((1, GLA_QK_W)), const((1, ML_V_W)), const((1, GLA_V_W))],
        out_specs=pl.BlockSpec((tb, ML_V_W + GLA_V_W), lambda b, i: (b * nt + i, 0)),
        scratch_shapes=[pltpu.VMEM((tb + 8, 2 * ML_QK_W), F32),
                        pltpu.VMEM((tb, ML_QK_W), F32), pltpu.VMEM((tb, ML_QK_W), F32),
                        pltpu.VMEM((tb, LANES), F32), pltpu.VMEM((tb, LANES), F32),
                        pltpu.VMEM((tb, GLA_QK_W), F32),
                        pltpu.VMEM((tb, ML_V_W), F32), pltpu.VMEM((tb, GLA_V_W), F32),
                        pltpu.VMEM((ML_HEADS, ML_DK, ML_DV), F32), pltpu.VMEM((ML_HEADS, 1, ML_DK), F32),
                        pltpu.VMEM((1, LANES), F32), pltpu.VMEM((GLA_HEADS, GLA_DV, GLA_DK), F32)],
        compiler_params=_cparams(("parallel", "arbitrary")),
        name="mlstm_gla",
    )(zm, zs, convw, convb, igb, fgb, lrup, gateb, mlnorm, glanorm)


def _proj_router_kernel(y_ref, w_ref, b_ref, h_ref, g_ref, wr_ref, br_ref, h1_ref, u2_ref, rt_ref):
    h1 = h_ref[...] + (_dot(y_ref[...], w_ref[...]) + b_ref[...])
    h1_ref[...] = h1
    u2 = _rms(h1, g_ref[...])
    u2_ref[...] = u2
    logits = jnp.dot(u2, wr_ref[...], precision=HIGHEST, preferred_element_type=F32) + br_ref[...]
    lane = lax.broadcasted_iota(jnp.int32, logits.shape, 1)
    lane_f = lane.astype(F32)
    big = float(LANES)
    gl = jnp.where(lane < N_GROUPS, logits, -jnp.inf)
    g_max = jnp.max(gl, axis=-1, keepdims=True)
    g_idx = jnp.min(jnp.where(gl == g_max, lane_f, big), axis=-1, keepdims=True)
    g_p = 1.0 / jnp.sum(jnp.exp(gl - g_max), axis=-1, keepdims=True)
    e_grp = ((lane - N_GROUPS) // EXPERTS_PER_GROUP).astype(F32)
    in_grp = (lane >= N_GROUPS) & (lane < N_GROUPS + N_EXPERTS) & (e_grp == g_idx)
    el = jnp.where(in_grp, logits, -jnp.inf)
    t1 = jnp.max(el, axis=-1, keepdims=True)
    i1 = jnp.min(jnp.where(el == t1, lane_f, big), axis=-1, keepdims=True)
    el2 = jnp.where(lane_f == i1, -jnp.inf, el)
    t2 = jnp.max(el2, axis=-1, keepdims=True)
    i2 = jnp.min(jnp.where(el2 == t2, lane_f, big), axis=-1, keepdims=True)
    e21 = jnp.exp(t2 - t1)
    p1 = 1.0 / (1.0 + e21)
    rt = jnp.where(lane == 0, i1 - N_GROUPS,
                   jnp.where(lane == 1, i2 - N_GROUPS,
                             jnp.where(lane == 2, g_p * p1, jnp.where(lane == 3, g_p * (e21 * p1), 0.0))))
    rt_ref[...] = rt


def _proj_router(y, w, b, h, g, wr, br, tm):
    t, kdim = y.shape
    row = lambda i: (i, 0)
    fixed = lambda i: (0, 0)
    return pl.pallas_call(
        _proj_router_kernel,
        out_shape=(jax.ShapeDtypeStruct((t, D_MODEL), F32), jax.ShapeDtypeStruct((t, D_MODEL), F32),
                   jax.ShapeDtypeStruct((t, LANES), F32)),
        grid=(t // tm,),
        in_specs=[pl.BlockSpec((tm, kdim), row), pl.BlockSpec((kdim, D_MODEL), fixed),
                  pl.BlockSpec((1, D_MODEL), fixed), pl.BlockSpec((tm, D_MODEL), row),
                  pl.BlockSpec((1, D_MODEL), fixed), pl.BlockSpec((D_MODEL, LANES), fixed),
                  pl.BlockSpec((1, LANES), fixed)],
        out_specs=(pl.BlockSpec((tm, D_MODEL), row), pl.BlockSpec((tm, D_MODEL), row),
                   pl.BlockSpec((tm, LANES), row)),
        compiler_params=_cparams(("parallel",)),
        name="proj_router",
    )(y, w, b, h, g, wr, br)


def _rank_kernel(rt_ref, rk_ref, cnt_ref, base_s):
    @pl.when(pl.program_id(0) == 0)
    def _():
        base_s[...] = jnp.zeros(base_s.shape, F32)

    rt = rt_ref[...]
    tt = rt.shape[0]
    lane = lax.broadcasted_iota(jnp.int32, rt.shape, 1)
    lane_f = lane.astype(F32)
    oh0 = lane_f == rt[:, 0:1]
    oh1 = lane_f == rt[:, 1:2]
    oh = jnp.where(oh0 | oh1, 1.0, 0.0)
    r = lax.broadcasted_iota(jnp.int32, (tt, tt), 0)
    c = lax.broadcasted_iota(jnp.int32, (tt, tt), 1)
    strict = jnp.where(c < r, 1.0, 0.0).astype(BF16)
    before = _dot(strict, oh.astype(BF16)) + base_s[...]
    r0 = jnp.sum(jnp.where(oh0, before, 0.0), axis=-1, keepdims=True)
    r1 = jnp.sum(jnp.where(oh1, before, 0.0), axis=-1, keepdims=True)
    rk_ref[...] = jnp.where(lane == 0, r0, jnp.where(lane == 1, r1, 0.0))
    base_s[...] = base_s[...] + jnp.sum(oh, axis=0, keepdims=True)
    cnt_ref[...] = base_s[...]


def _rank(rt, tt):
    t = rt.shape[0]
    return pl.pallas_call(
        _rank_kernel,
        out_shape=(jax.ShapeDtypeStruct((t, LANES), F32), jax.ShapeDtypeStruct((1, LANES), F32)),
        grid=(t // tt,),
        in_specs=[pl.BlockSpec((tt, LANES), lambda i: (i, 0))],
        out_specs=(pl.BlockSpec((tt, LANES), lambda i: (i, 0)), pl.BlockSpec((1, LANES), lambda i: (0, 0))),
        scratch_shapes=[pltpu.VMEM((1, LANES), F32)],
        compiler_params=_cparams(("arbitrary",)),
        name="expert_rank",
    )(rt)


def _row_copy(src, dst, sem):
    return pltpu.make_async_copy(src, dst, sem)


def _dispatch_kernel(dest_ref, u_hbm, xin_hbm, xout_hbm, sem, *, tt):
    del xin_hbm
    base = pl.program_id(0) * tt

    def issue(j, carry):
        src = u_hbm.at[pl.ds(base + j, 1)]
        for k in range(TOP_K):
            _row_copy(src, xout_hbm.at[pl.ds(dest_ref[TOP_K * j + k], 1)], sem).start()
        return carry

    lax.fori_loop(0, tt, issue, 0)

    def drain(j, carry):
        _row_copy(u_hbm.at[pl.ds(0, 1)], xout_hbm.at[pl.ds(0, 1)], sem).wait()
        return carry

    lax.fori_loop(0, TOP_K * tt, drain, 0)


def _dispatch(dest_flat, u2, n_pad, tt):
    t = u2.shape[0]
    xinit = jnp.zeros((n_pad, D_MODEL), F32)
    return pl.pallas_call(
        functools.partial(_dispatch_kernel, tt=tt),
        out_shape=jax.ShapeDtypeStruct((n_pad, D_MODEL), F32),
        grid=(t // tt,),
        in_specs=[pl.BlockSpec((TOP_K * tt,), lambda i: (i,), memory_space=pltpu.SMEM),
                  pl.BlockSpec(memory_space=pl.ANY), pl.BlockSpec(memory_space=pl.ANY)],
        out_specs=pl.BlockSpec(memory_space=pl.ANY),
        scratch_shapes=[pltpu.SemaphoreType.DMA],
        input_output_aliases={2: 0},
        compiler_params=_cparams(("arbitrary",)),
        name="moe_dispatch",
    )(dest_flat, u2, xinit)


def _ffn_kernel(be_ref, nu_ref, x_ref, wg_ref, wu_ref, wd_ref, y_ref):
    del be_ref

    @pl.when(pl.program_id(0) < nu_ref[0])
    def _():
        x = x_ref[...].astype(BF16)
        a = _dot(x, wg_ref[...])
        u = _dot(x, wu_ref[...])
        y_ref[...] = _dot(((a * _sigmoid(a)) * u).astype(BF16), wd_ref[...])

    @pl.when(pl.program_id(0) >= nu_ref[0])
    def _():
        y_ref[...] = jnp.zeros(y_ref.shape, y_ref.dtype)


def _ffn(block_expert, n_used, xbuf, wg, wu, wd):
    n_pad = xbuf.shape[0]
    n_blocks = n_pad // MOE_BM
    rows = lambda b, be, nu: (b, 0)
    wmap = lambda b, be, nu: (be[b], 0, 0)
    return pl.pallas_call(
        _ffn_kernel,
        out_shape=jax.ShapeDtypeStruct((n_pad, D_MODEL), F32),
        grid_spec=pltpu.PrefetchScalarGridSpec(
            num_scalar_prefetch=2,
            grid=(n_blocks,),
            in_specs=[pl.BlockSpec((MOE_BM, D_MODEL), rows),
                      pl.BlockSpec((None, D_MODEL, D_FF), wmap),
                      pl.BlockSpec((None, D_MODEL, D_FF), wmap),
                      pl.BlockSpec((None, D_FF, D_MODEL), wmap)],
            out_specs=pl.BlockSpec((MOE_BM, D_MODEL), rows)),
        compiler_params=_cparams(("arbitrary",)),
        name="moe_ffn",
    )(block_expert, n_used, xbuf, wg, wu, wd)


def _combine_kernel(dest_ref, y_hbm, h_ref, rt_ref, g_ref, o_ref, ybuf, sem, *, tt, final_norm):
    def issue(j, carry):
        for k in range(TOP_K):
            _row_copy(y_hbm.at[pl.ds(dest_ref[TOP_K * j + k], 1)], ybuf.at[k, pl.ds(j, 1)], sem).start()
        return carry

    lax.fori_loop(0, tt, issue, 0)

    def drain(j, carry):
        _row_copy(y_hbm.at[pl.ds(0, 1)], ybuf.at[0, pl.ds(0, 1)], sem).wait()
        return carry

    lax.fori_loop(0, TOP_K * tt, drain, 0)
    rt = rt_ref[...]
    out = h_ref[...] + (ybuf[0] * rt[:, 2:3] + ybuf[1] * rt[:, 3:4])
    if final_norm:
        out = _rms(out, g_ref[...])
    o_ref[...] = out


def _combine(dest_flat, ybuf, h1, rt, g, tt, final_norm):
    t = h1.shape[0]
    return pl.pallas_call(
        functools.partial(_combine_kernel, tt=tt, final_norm=final_norm),
        out_shape=jax.ShapeDtypeStruct((t, D_MODEL), F32),
        grid=(t // tt,),
        in_specs=[pl.BlockSpec((TOP_K * tt,), lambda i: (i,), memory_space=pltpu.SMEM),
                  pl.BlockSpec(memory_space=pl.ANY),
                  pl.BlockSpec((tt, D_MODEL), lambda i: (i, 0)),
                  pl.BlockSpec((tt, LANES), lambda i: (i, 0)),
                  pl.BlockSpec((1, D_MODEL), lambda i: (0, 0))],
        out_specs=pl.BlockSpec((tt, D_MODEL), lambda i: (i, 0)),
        scratch_shapes=[pltpu.VMEM((TOP_K, tt, D_MODEL), F32), pltpu.SemaphoreType.DMA],
        compiler_params=_cparams(("arbitrary",)),
        name="moe_combine",
    )(dest_flat, ybuf, h1, rt, g)


def _moe(h1, u2, rt, wg, wu, wd, g_final, final_norm):
    t = h1.shape[0]
    rk, cnt = _rank(rt, 256)
    counts = cnt[0, :N_EXPERTS].astype(jnp.int32)
    padded = (counts + MOE_BM - 1) // MOE_BM * MOE_BM
    pad_end = jnp.cumsum(padded)
    pad_start = pad_end - padded
    n_blocks = (t * TOP_K) // MOE_BM + N_EXPERTS
    eid = rt[:, :TOP_K].astype(jnp.int32)
    dest = (pad_start[eid] + rk[:, :TOP_K].astype(jnp.int32)).reshape(-1)
    n_used = (pad_end[-1] // MOE_BM).astype(jnp.int32)
    blk = jnp.minimum(jnp.arange(n_blocks, dtype=jnp.int32), n_used - 1) * MOE_BM
    block_expert = jnp.minimum(jnp.searchsorted(pad_end, blk, side='right'), N_EXPERTS - 1).astype(jnp.int32)
    xbuf = _dispatch(dest, u2, n_blocks * MOE_BM, 256)
    ybuf = _ffn(block_expert, n_used.reshape(1), xbuf, wg, wu, wd)
    return _combine(dest, ybuf, h1, rt, g_final, 256, final_norm)


def _qkv_kernel(x_ref, g_ref, w_ref, b_ref, pos_ref, rot_ref, q_ref, kv_ref):
    u = _rms(x_ref[...], g_ref[...]).astype(BF16)
    ang = pos_ref[...].astype(F32) * rot_ref[0:1, :]
    cosv = jnp.cos(ang)
    sinv = jnp.sin(ang)
    c_coef = jnp.where(rot_ref[3:4, :] > 0.0, cosv, 1.0)
    s_lo = -sinv * rot_ref[1:2, :]
    s_hi = sinv * rot_ref[2:3, :]

    def rotate(z):
        return z * c_coef + pltpu.roll(z, LANES - ROT_DIM // 2, 1) * s_lo + pltpu.roll(z, ROT_DIM // 2, 1) * s_hi

    scale = SWA_HEAD_DIM ** -0.5
    for j in range(Q_W // LANES):
        sl = slice(j * LANES, (j + 1) * LANES)
        q_ref[:, sl] = (rotate(_dot(u, w_ref[:, sl]) + b_ref[:, sl]) * scale).astype(q_ref.dtype)
    ksl = slice(Q_W, Q_W + KV_W)
    kv_ref[:, 0:KV_W] = rotate(_dot(u, w_ref[:, ksl]) + b_ref[:, ksl]).astype(kv_ref.dtype)
    vsl = slice(Q_W + KV_W, Q_W + 2 * KV_W)
    kv_ref[:, KV_W:2 * KV_W] = (_dot(u, w_ref[:, vsl]) + b_ref[:, vsl]).astype(kv_ref.dtype)


def _rot_table():
    lane = np.arange(LANES)
    d = lane % SWA_HEAD_DIM
    inv_freq = ROPE_THETA ** (-jnp.arange(0, ROT_DIM, 2, dtype=F32) / ROT_DIM)
    half = ROT_DIM // 2
    tab = jnp.zeros((8, LANES), F32)
    tab = tab.at[0].set(jnp.where(d < ROT_DIM, inv_freq[d % half], 0.0))
    tab = tab.at[1].set((d < half).astype(np.float32))
    tab = tab.at[2].set(((d >= half) & (d < ROT_DIM)).astype(np.float32))
    tab = tab.at[3].set((d < ROT_DIM).astype(np.float32))
    return tab


def _qkv(h, g, w, b, pos, tm):
    t = h.shape[0]
    row = lambda i: (i, 0)
    fixed = lambda i: (0, 0)
    wtot = Q_W + 2 * KV_W
    return pl.pallas_call(
        _qkv_kernel,
        out_shape=(jax.ShapeDtypeStruct((t, Q_W), BF16), jax.ShapeDtypeStruct((t, 2 * KV_W), BF16)),
        grid=(t // tm,),
        in_specs=[pl.BlockSpec((tm, D_MODEL), row), pl.BlockSpec((1, D_MODEL), fixed),
                  pl.BlockSpec((D_MODEL, wtot), fixed), pl.BlockSpec((1, wtot), fixed),
                  pl.BlockSpec((tm, 1), row), pl.BlockSpec((8, LANES), fixed)],
        out_specs=(pl.BlockSpec((tm, Q_W), row), pl.BlockSpec((tm, 2 * KV_W), row)),
        compiler_params=_cparams(("parallel",)),
        name="qkv_rotary",
    )(h, g, w, b, pos, _rot_table())


def _swa_kernel(q_ref, kvc_ref, kvp_ref, sink_ref, o_ref):
    has_prev = pl.program_id(1) > 0
    qi = lax.broadcasted_iota(jnp.int32, (WINDOW, WINDOW), 0)
    kj = lax.broadcasted_iota(jnp.int32, (WINDOW, WINDOW), 1)
    mask_cur = kj <= qi
    mask_prev = (kj > qi) & has_prev
    for h in range(SWA_Q_HEADS):
        g = h // SWA_GROUP
        ksl = slice(g * SWA_HEAD_DIM, (g + 1) * SWA_HEAD_DIM)
        vsl = slice(KV_W + g * SWA_HEAD_DIM, KV_W + (g + 1) * SWA_HEAD_DIM)
        qh = q_ref[:, h * SWA_HEAD_DIM:(h + 1) * SWA_HEAD_DIM]
        sc = jnp.where(mask_cur, _dot_nt(qh, kvc_ref[:, ksl]), -jnp.inf)
        sp = jnp.where(mask_prev, _dot_nt(qh, kvp_ref[:, ksl]), -jnp.inf)
        sink = sink_ref[:, h:h + 1]
        m = jnp.maximum(jnp.maximum(jnp.max(sc, axis=-1, keepdims=True), jnp.max(sp, axis=-1, keepdims=True)), sink)
        pc = jnp.exp(sc - m)
        pp = jnp.exp(sp - m)
        denom = jnp.sum(pc, axis=-1, keepdims=True) + jnp.sum(pp, axis=-1, keepdims=True) + jnp.exp(sink - m)
        o = (_dot(pc.astype(BF16), kvc_ref[:, vsl]) + _dot(pp.astype(BF16), kvp_ref[:, vsl])) / denom
        o_ref[:, h * SWA_HEAD_DIM:(h + 1) * SWA_HEAD_DIM] = o.astype(o_ref.dtype)


def _swa(q, kv, sinks, bsz, seq):
    nb = seq // WINDOW
    return pl.pallas_call(
        _swa_kernel,
        out_shape=jax.ShapeDtypeStruct((bsz * seq, Q_W), BF16),
        grid=(bsz, nb),
        in_specs=[pl.BlockSpec((WINDOW, Q_W), lambda b, n: (b * nb + n, 0)),
                  pl.BlockSpec((WINDOW, 2 * KV_W), lambda b, n: (b * nb + n, 0)),
                  pl.BlockSpec((WINDOW, 2 * KV_W), lambda b, n: (b * nb + jnp.maximum(n - 1, 0), 0)),
                  pl.BlockSpec((1, LANES), lambda b, n: (0, 0))],
        out_specs=pl.BlockSpec((WINDOW, Q_W), lambda b, n: (b * nb + n, 0)),
        compiler_params=_cparams(("parallel", "parallel")),
        name="swa_attention",
    )(q, kv, kv, sinks)


def _pack_in_proj(w_in):
    offs = np.cumsum((0,) + AB_SPLITS)
    part = lambda i: w_in[:, offs[i]:offs[i + 1]]
    w_main = jnp.concatenate([part(0), part(1), part(2), part(5), part(6), part(7), part(8)], axis=1).astype(BF16)
    small = jnp.concatenate([part(3), part(4), part(9)], axis=1)
    w_small = jnp.pad(small, ((0, 0), (0, LANES - small.shape[1]))).astype(BF16)
    return w_main, w_small


def _lane_row(v, offset=0):
    return jnp.zeros((1, LANES), F32).at[0, offset:offset + v.shape[0]].set(v.astype(F32))


def kernel(x, positions, mix_norm, ffn_norm, final_norm, ab_w_in, ab_conv_w, ab_conv_b, ml_igate_b, ml_fgate_b, ml_head_norm, gla_w_lr_up, gla_gate_b, gla_head_norm, ab_w_out, swa_w_qkv, swa_b_qkv, swa_sinks, swa_w_o, swa_b_o, router_group_w, router_group_b, router_expert_w, router_expert_b, expert_w_gate, expert_w_up, expert_w_down):
    bsz, seq, d = x.shape
    t = bsz * seq
    h = x.reshape(t, d)
    row = lambda v: v.reshape(1, -1).astype(F32)

    def router_params(layer):
        wr = jnp.zeros((d, LANES), F32).at[:, :N_GROUPS].set(router_group_w[layer])
        wr = wr.at[:, N_GROUPS:N_GROUPS + N_EXPERTS].set(router_expert_w[layer])
        br = jnp.zeros((1, LANES), F32).at[0, :N_GROUPS].set(router_group_b[layer])
        br = br.at[0, N_GROUPS:N_GROUPS + N_EXPERTS].set(router_expert_b[layer])
        return wr, br

    def experts(layer):
        return (expert_w_gate[layer].astype(BF16), expert_w_up[layer].astype(BF16), expert_w_down[layer].astype(BF16))

    w_main, w_small = _pack_in_proj(ab_w_in[0])
    zm, zs = _in_proj(h, row(mix_norm[0]), w_main, w_small, 512)
    lrup = jnp.zeros((LANES, GLA_QK_W), F32).at[SM_LR:SM_LR + GLA_LOWRANK].set(gla_w_lr_up[0]).astype(BF16)
    y = _mixer(zm, zs, ab_conv_w[0], row(ab_conv_b[0]), _lane_row(ml_igate_b[0]), _lane_row(ml_fgate_b[0]), lrup,
               row(gla_gate_b[0]), row(ml_head_norm[0]), row(gla_head_norm[0]), bsz, seq, 256)
    wr, br = router_params(0)
    h1, u2, rt = _proj_router(y, ab_w_out[0].astype(BF16), jnp.zeros((1, d), F32), h, row(ffn_norm[0]), wr, br, 512)
    h = _moe(h1, u2, rt, *experts(0), row(final_norm), False)

    q, kv = _qkv(h, row(mix_norm[1]), swa_w_qkv[0].astype(BF16), row(swa_b_qkv[0]), positions.reshape(t, 1), 512)
    o = _swa(q, kv, _lane_row(swa_sinks[0]), bsz, seq)
    wr, br = router_params(1)
    h1, u2, rt = _proj_router(o, swa_w_o[0].astype(BF16), row(swa_b_o[0]), h, row(ffn_norm[1]), wr, br, 512)
    out = _moe(h1, u2, rt, *experts(1), row(final_norm), True)
    return out.reshape(bsz, seq, d)
```

```python
import functools

import jax
import jax.numpy as jnp
import numpy as np
from jax import lax
from jax.experimental import pallas as pl
from jax.experimental.pallas import tpu as pltpu

F32 = jnp.float32
BF16 = jnp.bfloat16
HIGHEST = lax.Precision.HIGHEST

D_MODEL = 1024
EPS = 1e-6
ML_HEADS = 4
ML_DK = 64
ML_DV = 128
GLA_HEADS = 4
GLA_DK = 64
GLA_DV = 128
CHUNK = 64
CONV_K = 4
GLA_LOWRANK = 16
GLA_TAU = 16.0
ML_QK_W = ML_HEADS * ML_DK
ML_V_W = ML_HEADS * ML_DV
GLA_QK_W = GLA_HEADS * GLA_DK
GLA_V_W = GLA_HEADS * GLA_DV
AB_SPLITS = (2 * ML_QK_W, ML_V_W, ML_V_W, ML_HEADS, ML_HEADS, GLA_QK_W, GLA_QK_W, GLA_V_W, GLA_V_W, GLA_LOWRANK)
OFF_MQK = 0
OFF_MV = OFF_MQK + 2 * ML_QK_W
OFF_MO = OFF_MV + ML_V_W
OFF_GQ = OFF_MO + ML_V_W
OFF_GK = OFF_GQ + GLA_QK_W
OFF_GV = OFF_GK + GLA_QK_W
OFF_GG = OFF_GV + GLA_V_W
Z_MAIN_W = OFF_GG + GLA_V_W
SM_I = 0
SM_F = ML_HEADS
SM_LR = 2 * ML_HEADS
LANES = 128
SWA_Q_HEADS = 16
SWA_KV_HEADS = 2
SWA_HEAD_DIM = 64
SWA_GROUP = SWA_Q_HEADS // SWA_KV_HEADS
WINDOW = 128
ROT_DIM = SWA_HEAD_DIM // 4
ROPE_THETA = 500000.0
Q_W = SWA_Q_HEADS * SWA_HEAD_DIM
KV_W = SWA_KV_HEADS * SWA_HEAD_DIM
KV_OUT_W = 4 * KV_W
N_GROUPS = 4
EXPERTS_PER_GROUP = 8
N_EXPERTS = N_GROUPS * EXPERTS_PER_GROUP
TOP_K = 2
D_FF = 512
MOE_BM = 256

VMEM_LIMIT = 56 * 1024 * 1024


def _cparams(sem):
    return pltpu.CompilerParams(dimension_semantics=sem, vmem_limit_bytes=VMEM_LIMIT)


def _rms(x, g):
    return x * lax.rsqrt(jnp.mean(x * x, axis=-1, keepdims=True) + EPS) * g


def _log_sigmoid(x):
    return jnp.minimum(x, 0.0) - jnp.log1p(jnp.exp(-jnp.abs(x)))


def _sigmoid(x):
    return 1.0 / (1.0 + jnp.exp(-x))


def _dot(a, b):
    return jnp.dot(a, b, preferred_element_type=F32)


def _dot_nt(a, b):
    return lax.dot_general(a, b, (((1,), (1,)), ((), ())), preferred_element_type=F32)


def _dot_tn(a, b):
    return lax.dot_general(a, b, (((0,), (0,)), ((), ())), preferred_element_type=F32)


def _in_proj_kernel(x_ref, g_ref, w_ref, ws_ref, zm_ref, zs_ref, *, n_chunk):
    u = _rms(x_ref[...], g_ref[...]).astype(BF16)
    for n0 in range(0, w_ref.shape[1], n_chunk):
        zm_ref[:, n0:n0 + n_chunk] = _dot(u, w_ref[:, n0:n0 + n_chunk]).astype(zm_ref.dtype)
    zs_ref[...] = _dot(u, ws_ref[...])


def _in_proj(h, g, w_main, w_small, tm):
    t = h.shape[0]
    return pl.pallas_call(
        functools.partial(_in_proj_kernel, n_chunk=768),
        out_shape=(jax.ShapeDtypeStruct((t, Z_MAIN_W), BF16), jax.ShapeDtypeStruct((t, LANES), F32)),
        grid=(t // tm,),
        in_specs=[pl.BlockSpec((tm, D_MODEL), lambda i: (i, 0)),
                  pl.BlockSpec((1, D_MODEL), lambda i: (0, 0)),
                  pl.BlockSpec((D_MODEL, Z_MAIN_W), lambda i: (0, 0)),
                  pl.BlockSpec((D_MODEL, LANES), lambda i: (0, 0))],
        out_specs=(pl.BlockSpec((tm, Z_MAIN_W), lambda i: (i, 0)),
                   pl.BlockSpec((tm, LANES), lambda i: (i, 0))),
        compiler_params=_cparams(("parallel",)),
        name="in_proj",
    )(h, g, w_main, w_small)


def _mixer_kernel(zm_ref, zs_ref, convw_ref, convb_ref, igb_ref, fgb_ref, lrup_ref, gateb_ref, mlnorm_ref,
                  glanorm_ref, y_ref, xpad, q_s, k_s, ig_s, lf_s, la_s, hml_s, hgla_s, c_s, n_s, m_s, st_s, *, tb):
    @pl.when(pl.program_id(1) == 0)
    def _():
        xpad[0:8, :] = jnp.zeros((8, 2 * ML_QK_W), F32)
        c_s[...] = jnp.zeros(c_s.shape, F32)
        n_s[...] = jnp.zeros(n_s.shape, F32)
        st_s[...] = jnp.zeros(st_s.shape, F32)
        m_s[...] = jnp.full(m_s.shape, -jnp.inf, F32)

    xpad[8:8 + tb, :] = zm_ref[:, OFF_MQK:OFF_MQK + 2 * ML_QK_W].astype(F32)
    conv = convb_ref[...] + convw_ref[3:4, :] * xpad[8:8 + tb, :]
    for j in range(CONV_K - 1):
        conv = conv + convw_ref[j:j + 1, :] * xpad[5 + j:5 + j + tb, :]
    xpad[0:8, :] = xpad[tb:tb + 8, :]
    qk = conv * _sigmoid(conv)
    q_s[...] = qk[:, :ML_QK_W]
    k_s[...] = qk[:, ML_QK_W:] * (ML_DK ** -0.5)

    zs = zs_ref[...]
    ig_s[...] = zs + igb_ref[...]
    lf_s[...] = _log_sigmoid(pltpu.roll(zs, LANES - SM_F, 1) + fgb_ref[...])
    la_s[...] = _log_sigmoid(_dot(zs.astype(BF16), lrup_ref[...]) + gateb_ref[...]) * (1.0 / GLA_TAU)

    row = lax.broadcasted_iota(jnp.int32, (CHUNK, CHUNK), 0)
    col = lax.broadcasted_iota(jnp.int32, (CHUNK, CHUNK), 1)
    causal = col <= row
    tri = causal.astype(F32)

    def chunk(c, carry):
        rows = pl.ds(pl.multiple_of(c * CHUNK, CHUNK), CHUNK)
        ig = ig_s[rows, :]
        bc = jnp.dot(tri, lf_s[rows, :], precision=HIGHEST, preferred_element_type=F32)
        g_row = bc[CHUNK - 1:CHUNK, :]
        a = g_row - bc + ig
        a_max = jnp.max(a, axis=0, keepdims=True)
        wa = jnp.exp(a - a_max)
        m_row = m_s[...]
        m_new = jnp.maximum(g_row + m_row, a_max)
        s_old = jnp.exp(g_row + m_row - m_new)
        s_in = jnp.exp(a_max - m_new)
        il_all = bc + m_row
        e = ig - bc
        et = jnp.transpose(jnp.concatenate([e, e], axis=0))
        q = q_s[rows, :]
        k = k_s[rows, :]
        for h in range(ML_HEADS):
            qh = q[:, h * ML_DK:(h + 1) * ML_DK]
            kh = k[:, h * ML_DK:(h + 1) * ML_DK]
            vh = zm_ref[rows, OFF_MV + h * ML_DV:OFF_MV + (h + 1) * ML_DV]
            kw = kh * wa[:, h:h + 1]
            c_chunk = _dot_tn(kw.astype(BF16), vh)
            n_chunk = jnp.sum(kw, axis=0, keepdims=True)
            d_log = jnp.where(causal, bc[:, h:h + 1] + et[h:h + 1, 0:CHUNK], -jnp.inf)
            il = il_all[:, h:h + 1]
            m_t = jnp.maximum(il, jnp.max(d_log, axis=-1, keepdims=True))
            s_inter = jnp.exp(il - m_t)
            qhb = qh.astype(BF16)
            qkm = _dot_nt(qhb, kh.astype(BF16)) * jnp.exp(d_log - m_t)
            c_st = c_s[h]
            n_st = n_s[h]
            num = s_inter * _dot(qhb, c_st.astype(BF16)) + _dot(qkm.astype(BF16), vh)
            den = s_inter * jnp.sum(qh * n_st, axis=-1, keepdims=True) + jnp.sum(qkm, axis=-1, keepdims=True)
            hml_s[rows, h * ML_DV:(h + 1) * ML_DV] = num / jnp.maximum(jnp.abs(den), jnp.exp(-m_t))
            c_s[h] = s_old[:, h:h + 1] * c_st + s_in[:, h:h + 1] * c_chunk
            n_s[h] = s_old[:, h:h + 1] * n_st + s_in[:, h:h + 1] * n_chunk
        m_s[...] = m_new
        bcg = jnp.dot(tri, la_s[rows, :], precision=HIGHEST, preferred_element_type=F32)
        gg_row = bcg[CHUNK - 1:CHUNK, :]
        gq = zm_ref[rows, OFF_GQ:OFF_GQ + GLA_QK_W].astype(F32)
        gk = zm_ref[rows, OFF_GK:OFF_GK + GLA_QK_W].astype(F32) * (GLA_DK ** -0.5)
        q_dec = (gq * jnp.exp(bcg)).astype(BF16)
        k_inv = (gk * jnp.exp(-bcg)).astype(BF16)
        k_end = (gk * jnp.exp(gg_row - bcg)).astype(BF16)
        eg = jnp.exp(gg_row)
        for h in range(GLA_HEADS):
            sl = slice(h * GLA_DK, (h + 1) * GLA_DK)
            vh = zm_ref[rows, OFF_GV + h * GLA_DV:OFF_GV + (h + 1) * GLA_DV]
            att = jnp.where(causal, _dot_nt(q_dec[:, sl], k_inv[:, sl]), 0.0)
            st = st_s[h]
            hgla_s[rows, h * GLA_DV:(h + 1) * GLA_DV] = _dot_nt(q_dec[:, sl], st.astype(BF16)) + _dot(att.astype(BF16), vh)
            st_s[h] = st * eg[:, sl] + _dot_tn(vh, k_end[:, sl])
        return carry

    lax.fori_loop(0, tb // CHUNK, chunk, 0)

    for h in range(ML_HEADS):
        sl = slice(h * ML_DV, (h + 1) * ML_DV)
        hh = hml_s[:, sl]
        d = hh - jnp.mean(hh, axis=-1, keepdims=True)
        hn = d * lax.rsqrt(jnp.mean(d * d, axis=-1, keepdims=True) + EPS)
        og = zm_ref[:, OFF_MO + h * ML_DV:OFF_MO + (h + 1) * ML_DV].astype(F32)
        y_ref[:, sl] = (hn * mlnorm_ref[:, sl] * _sigmoid(og)).astype(y_ref.dtype)
    for h in range(GLA_HEADS):
        sl = slice(h * GLA_DV, (h + 1) * GLA_DV)
        o = hgla_s[:, sl]
        on = o * lax.rsqrt(jnp.mean(o * o, axis=-1, keepdims=True) + EPS)
        gg = zm_ref[:, OFF_GG + h * GLA_DV:OFF_GG + (h + 1) * GLA_DV].astype(F32)
        y_ref[:, ML_V_W + h * GLA_DV:ML_V_W + (h + 1) * GLA_DV] = (on * glanorm_ref[:, sl] * (gg * _sigmoid(gg))).astype(y_ref.dtype)


def _mixer(zm, zs, convw, convb, igb, fgb, lrup, gateb, mlnorm, glanorm, bsz, seq, tb):
    nt = seq // tb
    const = lambda shape: pl.BlockSpec(shape, lambda b, i: (0,) * len(shape))
    return pl.pallas_call(
        functools.partial(_mixer_kernel, tb=tb),
        out_shape=jax.ShapeDtypeStruct((bsz * seq, ML_V_W + GLA_V_W), BF16),
        grid=(bsz, nt),
        in_specs=[pl.BlockSpec((tb, Z_MAIN_W), lambda b, i: (b * nt + i, 0)),
                  pl.BlockSpec((tb, LANES), lambda b, i: (b * nt + i, 0)),
                  const((CONV_K, 2 * ML_QK_W)), const((1, 2 * ML_QK_W)), const((1, LANES)), const((1, LANES)),
                  const((LANES, GLA_QK_W)), const((1, GLA_QK_W)), const((1, ML_V_W)), const((1, GLA_V_W))],
        out_specs=pl.BlockSpec((tb, ML_V_W + GLA_V_W), lambda b, i: (b * nt + i, 0)),
        scratch_shapes=[pltpu.VMEM((tb + 8, 2 * ML_QK_W), F32),
                        pltpu.VMEM((tb, ML_QK_W), F32), pltpu.VMEM((tb, ML_QK_W), F32),
                        pltpu.VMEM((tb, LANES), F32), pltpu.VMEM((tb, LANES), F32),
                        pltpu.VMEM((tb, GLA_QK_W), F32),
                        pltpu.VMEM((tb, ML_V_W), F32), pltpu.VMEM((tb, GLA_V_W), F32),
                        pltpu.VMEM((ML_HEADS, ML_DK, ML_DV), F32), pltpu.VMEM((ML_HEADS, 1, ML_DK), F32),
                        pltpu.VMEM((1, LANES), F32), pltpu.VMEM((GLA_HEADS, GLA_DV, GLA_DK), F32)],
        compiler_params=_cparams(("parallel", "arbitrary")),
        name="mlstm_gla",
    )(zm, zs, convw, convb, igb, fgb, lrup, gateb, mlnorm, glanorm)


def _proj_router_kernel(y_ref, w_ref, b_ref, h_ref, g_ref, wr_ref, br_ref, h1_ref, u2_ref, rt_ref):
    h1 = h_ref[...] + (_dot(y_ref[...], w_ref[...]) + b_ref[...])
    h1_ref[...] = h1
    u2 = _rms(h1, g_ref[...])
    u2_ref[...] = u2
    logits = jnp.dot(u2, wr_ref[...], precision=HIGHEST, preferred_element_type=F32) + br_ref[...]
    lane = lax.broadcasted_iota(jnp.int32, logits.shape, 1)
    lane_f = lane.astype(F32)
    big = float(LANES)
    gl = jnp.where(lane < N_GROUPS, logits, -jnp.inf)
    g_max = jnp.max(gl, axis=-1, keepdims=True)
    g_idx = jnp.min(jnp.where(gl == g_max, lane_f, big), axis=-1, keepdims=True)
    g_p = 1.0 / jnp.sum(jnp.exp(gl - g_max), axis=-1, keepdims=True)
    e_grp = ((lane - N_GROUPS) // EXPERTS_PER_GROUP).astype(F32)
    in_grp = (lane >= N_GROUPS) & (lane < N_GROUPS + N_EXPERTS) & (e_grp == g_idx)
    el = jnp.where(in_grp, logits, -jnp.inf)
    t1 = jnp.max(el, axis=-1, keepdims=True)
    i1 = jnp.min(jnp.where(el == t1, lane_f, big), axis=-1, keepdims=True)
    el2 = jnp.where(lane_f == i1, -jnp.inf, el)
    t2 = jnp.max(el2, axis=-1, keepdims=True)
    i2 = jnp.min(jnp.where(el2 == t2, lane_f, big), axis=-1, keepdims=True)
    e21 = jnp.exp(t2 - t1)
    p1 = 1.0 / (1.0 + e21)
    rt = jnp.where(lane == 0, i1 - N_GROUPS,
                   jnp.where(lane == 1, i2 - N_GROUPS,
                             jnp.where(lane == 2, g_p * p1, jnp.where(lane == 3, g_p * (e21 * p1), 0.0))))
    rt_ref[...] = rt


def _proj_router(y, w, b, h, g, wr, br, tm):
    t, kdim = y.shape
    row = lambda i: (i, 0)
    fixed = lambda i: (0, 0)
    return pl.pallas_call(
        _proj_router_kernel,
        out_shape=(jax.ShapeDtypeStruct((t, D_MODEL), F32), jax.ShapeDtypeStruct((t, D_MODEL), F32),
                   jax.ShapeDtypeStruct((t, LANES), F32)),
        grid=(t // tm,),
        in_specs=[pl.BlockSpec((tm, kdim), row), pl.BlockSpec((kdim, D_MODEL), fixed),
                  pl.BlockSpec((1, D_MODEL), fixed), pl.BlockSpec((tm, D_MODEL), row),
                  pl.BlockSpec((1, D_MODEL), fixed), pl.BlockSpec((D_MODEL, LANES), fixed),
                  pl.BlockSpec((1, LANES), fixed)],
        out_specs=(pl.BlockSpec((tm, D_MODEL), row), pl.BlockSpec((tm, D_MODEL), row),
                   pl.BlockSpec((tm, LANES), row)),
        compiler_params=_cparams(("parallel",)),
        name="proj_router",
    )(y, w, b, h, g, wr, br)


def _rank_kernel(rt_ref, rk_ref, cnt_ref, base_s):
    @pl.when(pl.program_id(0) == 0)
    def _():
        base_s[...] = jnp.zeros(base_s.shape, F32)

    rt = rt_ref[...]
    tt = rt.shape[0]
    lane = lax.broadcasted_iota(jnp.int32, rt.shape, 1)
    lane_f = lane.astype(F32)
    oh0 = lane_f == rt[:, 0:1]
    oh1 = lane_f == rt[:, 1:2]
    oh = jnp.where(oh0 | oh1, 1.0, 0.0)
    r = lax.broadcasted_iota(jnp.int32, (tt, tt), 0)
    c = lax.broadcasted_iota(jnp.int32, (tt, tt), 1)
    strict = jnp.where(c < r, 1.0, 0.0).astype(BF16)
    before = _dot(strict, oh.astype(BF16)) + base_s[...]
    r0 = jnp.sum(jnp.where(oh0, before, 0.0), axis=-1, keepdims=True)
    r1 = jnp.sum(jnp.where(oh1, before, 0.0), axis=-1, keepdims=True)
    rk_ref[...] = jnp.where(lane == 0, r0, jnp.where(lane == 1, r1, 0.0))
    base_s[...] = base_s[...] + jnp.sum(oh, axis=0, keepdims=True)
    cnt_ref[...] = base_s[...]


def _rank(rt, tt):
    t = rt.shape[0]
    return pl.pallas_call(
        _rank_kernel,
        out_shape=(jax.ShapeDtypeStruct((t, LANES), F32), jax.ShapeDtypeStruct((1, LANES), F32)),
        grid=(t // tt,),
        in_specs=[pl.BlockSpec((tt, LANES), lambda i: (i, 0))],
        out_specs=(pl.BlockSpec((tt, LANES), lambda i: (i, 0)), pl.BlockSpec((1, LANES), lambda i: (0, 0))),
        scratch_shapes=[pltpu.VMEM((1, LANES), F32)],
        compiler_params=_cparams(("arbitrary",)),
        name="expert_rank",
    )(rt)


def _row_copy(src, dst, sem):
    return pltpu.make_async_copy(src, dst, sem)


def _dispatch_kernel(dest_ref, u_ref, xin_hbm, xout_hbm, sem, *, tt):
    del xin_hbm

    def issue(j, carry):
        src = u_ref.at[pl.ds(j, 1)]
        for k in range(TOP_K):
            _row_copy(src, xout_hbm.at[pl.ds(dest_ref[TOP_K * j + k], 1)], sem).start()
        return carry

    lax.fori_loop(0, tt, issue, 0, unroll=4)
    for k in range(TOP_K):
        _row_copy(u_ref, xout_hbm.at[pl.ds(0, tt)], sem).wait()


def _dispatch(dest_flat, u2, n_pad, tt):
    t = u2.shape[0]
    xinit = jnp.zeros((n_pad, D_MODEL), F32)
    return pl.pallas_call(
        functools.partial(_dispatch_kernel, tt=tt),
        out_shape=jax.ShapeDtypeStruct((n_pad, D_MODEL), F32),
        grid=(t // tt,),
        in_specs=[pl.BlockSpec((TOP_K * tt,), lambda i: (i,), memory_space=pltpu.SMEM),
                  pl.BlockSpec((tt, D_MODEL), lambda i: (i, 0)), pl.BlockSpec(memory_space=pl.ANY)],
        out_specs=pl.BlockSpec(memory_space=pl.ANY),
        scratch_shapes=[pltpu.SemaphoreType.DMA],
        input_output_aliases={2: 0},
        compiler_params=_cparams(("arbitrary",)),
        name="moe_dispatch",
    )(dest_flat, u2, xinit)


def _ffn_kernel(be_ref, nu_ref, x_ref, wg_ref, wu_ref, wd_ref, y_ref, wg_s, wu_s, wd_s):
    b = pl.program_id(0)

    @pl.when((b == 0) | (be_ref[b] != be_ref[jnp.maximum(b - 1, 0)]))
    def _():
        wg_s[...] = wg_ref[...].astype(BF16)
        wu_s[...] = wu_ref[...].astype(BF16)
        wd_s[...] = wd_ref[...].astype(BF16)

    @pl.when(b < nu_ref[0])
    def _():
        x = x_ref[...].astype(BF16)
        a = _dot(x, wg_s[...])
        u = _dot(x, wu_s[...])
        y_ref[...] = _dot(((a * _sigmoid(a)) * u).astype(BF16), wd_s[...])

    @pl.when(b >= nu_ref[0])
    def _():
        y_ref[...] = jnp.zeros(y_ref.shape, y_ref.dtype)


def _ffn(block_expert, n_used, xbuf, wg, wu, wd):
    n_pad = xbuf.shape[0]
    n_blocks = n_pad // MOE_BM
    rows = lambda b, be, nu: (b, 0)
    wmap = lambda b, be, nu: (be[b], 0, 0)
    return pl.pallas_call(
        _ffn_kernel,
        out_shape=jax.ShapeDtypeStruct((n_pad, D_MODEL), F32),
        grid_spec=pltpu.PrefetchScalarGridSpec(
            num_scalar_prefetch=2,
            grid=(n_blocks,),
            in_specs=[pl.BlockSpec((MOE_BM, D_MODEL), rows),
                      pl.BlockSpec((None, D_MODEL, D_FF), wmap),
                      pl.BlockSpec((None, D_MODEL, D_FF), wmap),
                      pl.BlockSpec((None, D_FF, D_MODEL), wmap)],
            out_specs=pl.BlockSpec((MOE_BM, D_MODEL), rows),
            scratch_shapes=[pltpu.VMEM((D_MODEL, D_FF), BF16), pltpu.VMEM((D_MODEL, D_FF), BF16),
                            pltpu.VMEM((D_FF, D_MODEL), BF16)]),
        compiler_params=_cparams(("arbitrary",)),
        name="moe_ffn",
    )(block_expert, n_used, xbuf, wg, wu, wd)


def _combine_kernel(dest_ref, y_hbm, h_ref, rt_ref, g_ref, o_ref, ybuf, sem, *, tt, final_norm):
    def issue(j, carry):
        for k in range(TOP_K):
            _row_copy(y_hbm.at[pl.ds(dest_ref[TOP_K * j + k], 1)], ybuf.at[k, pl.ds(j, 1)], sem).start()
        return carry

    lax.fori_loop(0, tt, issue, 0)

    for k in range(TOP_K):
        _row_copy(y_hbm.at[pl.ds(0, tt)], ybuf.at[k], sem).wait()
    rt = rt_ref[...]
    out = h_ref[...] + (ybuf[0] * rt[:, 2:3] + ybuf[1] * rt[:, 3:4])
    if final_norm:
        out = _rms(out, g_ref[...])
    o_ref[...] = out


def _combine(dest_flat, ybuf, h1, rt, g, tt, final_norm):
    t = h1.shape[0]
    return pl.pallas_call(
        functools.partial(_combine_kernel, tt=tt, final_norm=final_norm),
        out_shape=jax.ShapeDtypeStruct((t, D_MODEL), F32),
        grid=(t // tt,),
        in_specs=[pl.BlockSpec((TOP_K * tt,), lambda i: (i,), memory_space=pltpu.SMEM),
                  pl.BlockSpec(memory_space=pl.ANY),
                  pl.BlockSpec((tt, D_MODEL), lambda i: (i, 0)),
                  pl.BlockSpec((tt, LANES), lambda i: (i, 0)),
                  pl.BlockSpec((1, D_MODEL), lambda i: (0, 0))],
        out_specs=pl.BlockSpec((tt, D_MODEL), lambda i: (i, 0)),
        scratch_shapes=[pltpu.VMEM((TOP_K, tt, D_MODEL), F32), pltpu.SemaphoreType.DMA],
        compiler_params=_cparams(("arbitrary",)),
        name="moe_combine",
    )(dest_flat, ybuf, h1, rt, g)


def _moe(h1, u2, rt, wg, wu, wd, g_final, final_norm):
    t = h1.shape[0]
    rk, cnt = _rank(rt, 256)
    counts = cnt[0, :N_EXPERTS].astype(jnp.int32)
    padded = (counts + MOE_BM - 1) // MOE_BM * MOE_BM
    pad_end = jnp.cumsum(padded)
    pad_start = pad_end - padded
    n_blocks = (t * TOP_K) // MOE_BM + N_EXPERTS
    eid = rt[:, :TOP_K].astype(jnp.int32)
    dest = (pad_start[eid] + rk[:, :TOP_K].astype(jnp.int32)).reshape(-1)
    n_used = (pad_end[-1] // MOE_BM).astype(jnp.int32)
    blk = jnp.minimum(jnp.arange(n_blocks, dtype=jnp.int32), n_used - 1) * MOE_BM
    block_expert = jnp.minimum(jnp.sum(pad_end[None, :] <= blk[:, None], axis=1), N_EXPERTS - 1).astype(jnp.int32)
    xbuf = _dispatch(dest, u2, n_blocks * MOE_BM, 256)
    ybuf = _ffn(block_expert, n_used.reshape(1), xbuf, wg, wu, wd)
    return _combine(dest, ybuf, h1, rt, g_final, 256, final_norm)


def _qkv_kernel(x_ref, g_ref, w_ref, b_ref, pos_ref, rot_ref, q_ref, kv_ref):
    u = _rms(x_ref[...], g_ref[...]).astype(BF16)
    ang = pos_ref[...].astype(F32) * rot_ref[0:1, :]
    cosv = jnp.cos(ang)
    sinv = jnp.sin(ang)
    c_coef = jnp.where(rot_ref[3:4, :] > 0.0, cosv, 1.0)
    s_lo = -sinv * rot_ref[1:2, :]
    s_hi = sinv * rot_ref[2:3, :]

    def rotate(z):
        return z * c_coef + pltpu.roll(z, LANES - ROT_DIM // 2, 1) * s_lo + pltpu.roll(z, ROT_DIM // 2, 1) * s_hi

    scale = SWA_HEAD_DIM ** -0.5
    for j in range(Q_W // LANES):
        sl = slice(j * LANES, (j + 1) * LANES)
        q_ref[:, sl] = (rotate(_dot(u, w_ref[:, sl]) + b_ref[:, sl]) * scale).astype(q_ref.dtype)
    for j in range(KV_OUT_W // LANES):
        sl = slice(Q_W + j * LANES, Q_W + (j + 1) * LANES)
        z = _dot(u, w_ref[:, sl]) + b_ref[:, sl]
        kv_ref[:, j * LANES:(j + 1) * LANES] = (rotate(z) if j % 2 == 0 else z).astype(kv_ref.dtype)


def _rot_table():
    lane = np.arange(LANES)
    d = lane % SWA_HEAD_DIM
    inv_freq = ROPE_THETA ** (-jnp.arange(0, ROT_DIM, 2, dtype=F32) / ROT_DIM)
    half = ROT_DIM // 2
    tab = jnp.zeros((8, LANES), F32)
    tab = tab.at[0].set(jnp.where(d < ROT_DIM, inv_freq[d % half], 0.0))
    tab = tab.at[1].set((d < half).astype(np.float32))
    tab = tab.at[2].set(((d >= half) & (d < ROT_DIM)).astype(np.float32))
    tab = tab.at[3].set((d < ROT_DIM).astype(np.float32))
    return tab


def _qkv(h, g, w, b, pos, tm):
    t = h.shape[0]
    row = lambda i: (i, 0)
    fixed = lambda i: (0, 0)
    wtot = Q_W + KV_OUT_W
    return pl.pallas_call(
        _qkv_kernel,
        out_shape=(jax.ShapeDtypeStruct((t, Q_W), BF16), jax.ShapeDtypeStruct((t, KV_OUT_W), BF16)),
        grid=(t // tm,),
        in_specs=[pl.BlockSpec((tm, D_MODEL), row), pl.BlockSpec((1, D_MODEL), fixed),
                  pl.BlockSpec((D_MODEL, wtot), fixed), pl.BlockSpec((1, wtot), fixed),
                  pl.BlockSpec((tm, 1), row), pl.BlockSpec((8, LANES), fixed)],
        out_specs=(pl.BlockSpec((tm, Q_W), row), pl.BlockSpec((tm, KV_OUT_W), row)),
        compiler_params=_cparams(("parallel",)),
        name="qkv_rotary",
    )(h, g, w, b, pos, _rot_table())


def _swa_kernel(q_ref, kvc_ref, kvp_ref, sink_ref, o_ref):
    neg = jnp.where(pl.program_id(1) > 0, 0.0, -jnp.inf).astype(F32)
    qi = lax.broadcasted_iota(jnp.int32, (WINDOW, WINDOW), 0)
    kj = lax.broadcasted_iota(jnp.int32, (WINDOW, WINDOW), 1)
    mask_cur = kj <= qi
    low = kj < SWA_HEAD_DIM
    lane = lax.broadcasted_iota(jnp.int32, (1, LANES), 1)
    keep_lo = jnp.where(lane < SWA_HEAD_DIM, 1.0, 0.0).astype(BF16)
    keep_hi = jnp.where(lane < SWA_HEAD_DIM, 0.0, 1.0).astype(BF16)

    def arranged(ref):
        k_nat, v_nat = ref[:, 0:KV_W], ref[:, KV_W:2 * KV_W]
        k_swp, v_swp = ref[:, 2 * KV_W:3 * KV_W], ref[:, 3 * KV_W:4 * KV_W]
        return {(0, 0): (k_nat * keep_lo, v_nat * keep_lo), (0, 1): (k_swp * keep_hi, v_swp * keep_hi),
                (1, 0): (k_swp * keep_lo, v_swp * keep_lo), (1, 1): (k_nat * keep_hi, v_nat * keep_hi)}

    cur = arranged(kvc_ref)
    prev = arranged(kvp_ref)
    heads = range(SWA_Q_HEADS)
    key = lambda h: (h // SWA_GROUP, h % 2)
    scores = []
    for h in heads:
        qp = q_ref[:, (h // 2) * LANES:(h // 2 + 1) * LANES]
        sc = _dot_nt(qp, cur[key(h)][0])
        sp = _dot_nt(qp, prev[key(h)][0])
        scores.append(jnp.where(mask_cur, sc, sp + neg))
    sinks = [sink_ref[:, h:h + 1] for h in heads]
    maxes = [jnp.maximum(jnp.max(scores[h], axis=-1, keepdims=True), sinks[h]) for h in heads]
    probs = [jnp.exp(scores[h] - maxes[h]) for h in heads]
    denoms = [jnp.sum(probs[h], axis=-1, keepdims=True) + jnp.exp(sinks[h] - maxes[h]) for h in heads]
    for pr in range(SWA_Q_HEADS // 2):
        num = None
        for h in (2 * pr, 2 * pr + 1):
            pc = jnp.where(mask_cur, probs[h], 0.0).astype(BF16)
            pp = jnp.where(mask_cur, 0.0, probs[h]).astype(BF16)
            part = _dot(pc, cur[key(h)][1]) + _dot(pp, prev[key(h)][1])
            num = part if num is None else num + part
        den = jnp.where(low, denoms[2 * pr], denoms[2 * pr + 1])
        o_ref[:, pr * LANES:(pr + 1) * LANES] = (num / den).astype(o_ref.dtype)


def _swa(q, kv, sinks, bsz, seq):
    nb = seq // WINDOW
    return pl.pallas_call(
        _swa_kernel,
        out_shape=jax.ShapeDtypeStruct((bsz * seq, Q_W), BF16),
        grid=(bsz, nb),
        in_specs=[pl.BlockSpec((WINDOW, Q_W), lambda b, n: (b * nb + n, 0)),
                  pl.BlockSpec((WINDOW, KV_OUT_W), lambda b, n: (b * nb + n, 0)),
                  pl.BlockSpec((WINDOW, KV_OUT_W), lambda b, n: (b * nb + jnp.maximum(n - 1, 0), 0)),
                  pl.BlockSpec((1, LANES), lambda b, n: (0, 0))],
        out_specs=pl.BlockSpec((WINDOW, Q_W), lambda b, n: (b * nb + n, 0)),
        compiler_params=_cparams(("parallel", "parallel")),
        name="swa_attention",
    )(q, kv, kv, sinks)


def _pack_in_proj(w_in):
    offs = np.cumsum((0,) + AB_SPLITS)
    part = lambda i: w_in[:, offs[i]:offs[i + 1]]
    w_main = jnp.concatenate([part(0), part(1), part(2), part(5), part(6), part(7), part(8)], axis=1).astype(BF16)
    small = jnp.concatenate([part(3), part(4), part(9)], axis=1)
    w_small = jnp.pad(small, ((0, 0), (0, LANES - small.shape[1]))).astype(BF16)
    return w_main, w_small


def _pack_qkv(w, b):
    hd = SWA_HEAD_DIM
    k0, k1, v0, v1 = (slice(Q_W + i * hd, Q_W + (i + 1) * hd) for i in range(4))
    cols = lambda a: jnp.concatenate([a[..., :Q_W + 2 * KV_W], a[..., k1], a[..., k0], a[..., v1], a[..., v0]], axis=-1)
    w_ext = cols(w)
    return w_ext.astype(BF16), cols(b).reshape(1, -1).astype(F32)


def _lane_row(v, offset=0):
    return jnp.zeros((1, LANES), F32).at[0, offset:offset + v.shape[0]].set(v.astype(F32))


def kernel(x, positions, mix_norm, ffn_norm, final_norm, ab_w_in, ab_conv_w, ab_conv_b, ml_igate_b, ml_fgate_b, ml_head_norm, gla_w_lr_up, gla_gate_b, gla_head_norm, ab_w_out, swa_w_qkv, swa_b_qkv, swa_sinks, swa_w_o, swa_b_o, router_group_w, router_group_b, router_expert_w, router_expert_b, expert_w_gate, expert_w_up, expert_w_down):
    bsz, seq, d = x.shape
    t = bsz * seq
    h = x.reshape(t, d)
    row = lambda v: v.reshape(1, -1).astype(F32)

    def router_params(layer):
        wr = jnp.zeros((d, LANES), F32).at[:, :N_GROUPS].set(router_group_w[layer])
        wr = wr.at[:, N_GROUPS:N_GROUPS + N_EXPERTS].set(router_expert_w[layer])
        br = jnp.zeros((1, LANES), F32).at[0, :N_GROUPS].set(router_group_b[layer])
        br = br.at[0, N_GROUPS:N_GROUPS + N_EXPERTS].set(router_expert_b[layer])
        return wr, br

    def experts(layer):
        return expert_w_gate[layer], expert_w_up[layer], expert_w_down[layer]

    w_main, w_small = _pack_in_proj(ab_w_in[0])
    zm, zs = _in_proj(h, row(mix_norm[0]), w_main, w_small, 512)
    lrup = jnp.zeros((LANES, GLA_QK_W), F32).at[SM_LR:SM_LR + GLA_LOWRANK].set(gla_w_lr_up[0]).astype(BF16)
    y = _mixer(zm, zs, ab_conv_w[0], row(ab_conv_b[0]), _lane_row(ml_igate_b[0]), _lane_row(ml_fgate_b[0]), lrup,
               row(gla_gate_b[0]), row(ml_head_norm[0]), row(gla_head_norm[0]), bsz, seq, 256)
    wr, br = router_params(0)
    h1, u2, rt = _proj_router(y, ab_w_out[0].astype(BF16), jnp.zeros((1, d), F32), h, row(ffn_norm[0]), wr, br, 512)
    h = _moe(h1, u2, rt, *experts(0), row(final_norm), False)

    w_qkv, b_qkv = _pack_qkv(swa_w_qkv[0], swa_b_qkv[0])
    q, kv = _qkv(h, row(mix_norm[1]), w_qkv, b_qkv, positions.reshape(t, 1), 512)
    o = _swa(q, kv, _lane_row(swa_sinks[0]), bsz, seq)
    wr, br = router_params(1)
    h1, u2, rt = _proj_router(o, swa_w_o[0].astype(BF16), row(swa_b_o[0]), h, row(ffn_norm[1]), wr, br, 512)
    out = _moe(h1, u2, rt, *experts(1), row(final_norm), True)
    return out.reshape(bsz, seq, d)
```

```python
import functools

import jax
import jax.numpy as jnp
import numpy as np
from jax import lax
from jax.experimental import pallas as pl
from jax.experimental.pallas import tpu as pltpu

F32 = jnp.float32
BF16 = jnp.bfloat16
HIGHEST = lax.Precision.HIGHEST

D_MODEL = 1024
EPS = 1e-6
ML_HEADS = 4
ML_DK = 64
ML_DV = 128
GLA_HEADS = 4
GLA_DK = 64
GLA_DV = 128
CHUNK = 64
CONV_K = 4
GLA_LOWRANK = 16
GLA_TAU = 16.0
ML_QK_W = ML_HEADS * ML_DK
ML_V_W = ML_HEADS * ML_DV
GLA_QK_W = GLA_HEADS * GLA_DK
GLA_V_W = GLA_HEADS * GLA_DV
AB_SPLITS = (2 * ML_QK_W, ML_V_W, ML_V_W, ML_HEADS, ML_HEADS, GLA_QK_W, GLA_QK_W, GLA_V_W, GLA_V_W, GLA_LOWRANK)
OFF_MQK = 0
OFF_MV = OFF_MQK + 2 * ML_QK_W
OFF_MO = OFF_MV + ML_V_W
OFF_GQ = OFF_MO + ML_V_W
OFF_GK = OFF_GQ + GLA_QK_W
OFF_GV = OFF_GK + GLA_QK_W
OFF_GG = OFF_GV + GLA_V_W
Z_MAIN_W = OFF_GG + GLA_V_W
SM_I = 0
SM_F = ML_HEADS
SM_LR = 2 * ML_HEADS
LANES = 128
SWA_Q_HEADS = 16
SWA_KV_HEADS = 2
SWA_HEAD_DIM = 64
SWA_GROUP = SWA_Q_HEADS // SWA_KV_HEADS
WINDOW = 128
ROT_DIM = SWA_HEAD_DIM // 4
ROPE_THETA = 500000.0
Q_W = SWA_Q_HEADS * SWA_HEAD_DIM
KV_W = SWA_KV_HEADS * SWA_HEAD_DIM
KV_OUT_W = 3 * KV_W
N_GROUPS = 4
EXPERTS_PER_GROUP = 8
N_EXPERTS = N_GROUPS * EXPERTS_PER_GROUP
TOP_K = 2
D_FF = 512
MOE_BM = 256

VMEM_LIMIT = 56 * 1024 * 1024


def _cparams(sem):
    return pltpu.CompilerParams(dimension_semantics=sem, vmem_limit_bytes=VMEM_LIMIT)


def _rms(x, g):
    return x * lax.rsqrt(jnp.mean(x * x, axis=-1, keepdims=True) + EPS) * g


def _log_sigmoid(x):
    return jnp.minimum(x, 0.0) - jnp.log1p(jnp.exp(-jnp.abs(x)))


def _sigmoid(x):
    return 1.0 / (1.0 + jnp.exp(-x))


def _dot(a, b):
    return jnp.dot(a, b, preferred_element_type=F32)


def _dot_nt(a, b):
    return lax.dot_general(a, b, (((1,), (1,)), ((), ())), preferred_element_type=F32)


def _dot_tn(a, b):
    return lax.dot_general(a, b, (((0,), (0,)), ((), ())), preferred_element_type=F32)


def _in_proj_kernel(x_ref, g_ref, w_ref, ws_ref, zm_ref, zs_ref, *, n_chunk):
    u = _rms(x_ref[...], g_ref[...]).astype(BF16)
    for n0 in range(0, w_ref.shape[1], n_chunk):
        zm_ref[:, n0:n0 + n_chunk] = _dot(u, w_ref[:, n0:n0 + n_chunk]).astype(zm_ref.dtype)
    zs_ref[...] = _dot(u, ws_ref[...])


def _in_proj(h, g, w_main, w_small, tm):
    t = h.shape[0]
    return pl.pallas_call(
        functools.partial(_in_proj_kernel, n_chunk=768),
        out_shape=(jax.ShapeDtypeStruct((t, Z_MAIN_W), BF16), jax.ShapeDtypeStruct((t, LANES), F32)),
        grid=(t // tm,),
        in_specs=[pl.BlockSpec((tm, D_MODEL), lambda i: (i, 0)),
                  pl.BlockSpec((1, D_MODEL), lambda i: (0, 0)),
                  pl.BlockSpec((D_MODEL, Z_MAIN_W), lambda i: (0, 0)),
                  pl.BlockSpec((D_MODEL, LANES), lambda i: (0, 0))],
        out_specs=(pl.BlockSpec((tm, Z_MAIN_W), lambda i: (i, 0)),
                   pl.BlockSpec((tm, LANES), lambda i: (i, 0))),
        compiler_params=_cparams(("parallel",)),
        name="in_proj",
    )(h, g, w_main, w_small)


def _mixer_kernel(zm_ref, zs_ref, convw_ref, convb_ref, igb_ref, fgb_ref, lrup_ref, gateb_ref, mlnorm_ref,
                  glanorm_ref, y_ref, xpad, q_s, k_s, ig_s, lf_s, la_s, hml_s, hgla_s, c_s, n_s, m_s, st_s, *, tb):
    @pl.when(pl.program_id(1) == 0)
    def _():
        xpad[0:8, :] = jnp.zeros((8, 2 * ML_QK_W), F32)
        c_s[...] = jnp.zeros(c_s.shape, F32)
        n_s[...] = jnp.zeros(n_s.shape, F32)
        st_s[...] = jnp.zeros(st_s.shape, F32)
        m_s[...] = jnp.full(m_s.shape, -jnp.inf, F32)

    xpad[8:8 + tb, :] = zm_ref[:, OFF_MQK:OFF_MQK + 2 * ML_QK_W].astype(F32)
    conv = convb_ref[...] + convw_ref[3:4, :] * xpad[8:8 + tb, :]
    for j in range(CONV_K - 1):
        conv = conv + convw_ref[j:j + 1, :] * xpad[5 + j:5 + j + tb, :]
    xpad[0:8, :] = xpad[tb:tb + 8, :]
    qk = conv * _sigmoid(conv)
    q_s[...] = qk[:, :ML_QK_W]
    k_s[...] = qk[:, ML_QK_W:] * (ML_DK ** -0.5)

    zs = zs_ref[...]
    ig_s[...] = zs + igb_ref[...]
    lf_s[...] = _log_sigmoid(pltpu.roll(zs, LANES - SM_F, 1) + fgb_ref[...])
    la_s[...] = _log_sigmoid(_dot(zs.astype(BF16), lrup_ref[...]) + gateb_ref[...]) * (1.0 / GLA_TAU)

    row = lax.broadcasted_iota(jnp.int32, (CHUNK, CHUNK), 0)
    col = lax.broadcasted_iota(jnp.int32, (CHUNK, CHUNK), 1)
    causal = col <= row
    tri = causal.astype(F32)

    def chunk(c, carry):
        rows = pl.ds(pl.multiple_of(c * CHUNK, CHUNK), CHUNK)
        ig = ig_s[rows, :]
        bc = jnp.dot(tri, lf_s[rows, :], precision=HIGHEST, preferred_element_type=F32)
        g_row = bc[CHUNK - 1:CHUNK, :]
        a = g_row - bc + ig
        a_max = jnp.max(a, axis=0, keepdims=True)
        wa = jnp.exp(a - a_max)
        m_row = m_s[...]
        m_new = jnp.maximum(g_row + m_row, a_max)
        s_old = jnp.exp(g_row + m_row - m_new)
        s_in = jnp.exp(a_max - m_new)
        il_all = bc + m_row
        e = ig - bc
        et = jnp.transpose(jnp.concatenate([e, e], axis=0))
        q = q_s[rows, :]
        k = k_s[rows, :]
        for h in range(ML_HEADS):
            qh = q[:, h * ML_DK:(h + 1) * ML_DK]
            kh = k[:, h * ML_DK:(h + 1) * ML_DK]
            vh = zm_ref[rows, OFF_MV + h * ML_DV:OFF_MV + (h + 1) * ML_DV]
            kw = kh * wa[:, h:h + 1]
            c_chunk = _dot_tn(kw.astype(BF16), vh)
            n_chunk = jnp.sum(kw, axis=0, keepdims=True)
            d_log = jnp.where(causal, bc[:, h:h + 1] + et[h:h + 1, 0:CHUNK], -jnp.inf)
            il = il_all[:, h:h + 1]
            m_t = jnp.maximum(il, jnp.max(d_log, axis=-1, keepdims=True))
            s_inter = jnp.exp(il - m_t)
            qhb = qh.astype(BF16)
            qkm = _dot_nt(qhb, kh.astype(BF16)) * jnp.exp(d_log - m_t)
            c_st = c_s[h]
            n_st = n_s[h]
            num = s_inter * _dot(qhb, c_st.astype(BF16)) + _dot(qkm.astype(BF16), vh)
            den = s_inter * jnp.sum(qh * n_st, axis=-1, keepdims=True) + jnp.sum(qkm, axis=-1, keepdims=True)
            hml_s[rows, h * ML_DV:(h + 1) * ML_DV] = num / jnp.maximum(jnp.abs(den), jnp.exp(-m_t))
            c_s[h] = s_old[:, h:h + 1] * c_st + s_in[:, h:h + 1] * c_chunk
            n_s[h] = s_old[:, h:h + 1] * n_st + s_in[:, h:h + 1] * n_chunk
        m_s[...] = m_new
        bcg = jnp.dot(tri, la_s[rows, :], precision=HIGHEST, preferred_element_type=F32)
        gg_row = bcg[CHUNK - 1:CHUNK, :]
        gq = zm_ref[rows, OFF_GQ:OFF_GQ + GLA_QK_W].astype(F32)
        gk = zm_ref[rows, OFF_GK:OFF_GK + GLA_QK_W].astype(F32) * (GLA_DK ** -0.5)
        q_dec = (gq * jnp.exp(bcg)).astype(BF16)
        k_inv = (gk * jnp.exp(-bcg)).astype(BF16)
        k_end = (gk * jnp.exp(gg_row - bcg)).astype(BF16)
        eg = jnp.exp(gg_row)
        for h in range(GLA_HEADS):
            sl = slice(h * GLA_DK, (h + 1) * GLA_DK)
            vh = zm_ref[rows, OFF_GV + h * GLA_DV:OFF_GV + (h + 1) * GLA_DV]
            att = jnp.where(causal, _dot_nt(q_dec[:, sl], k_inv[:, sl]), 0.0)
            st = st_s[h]
            hgla_s[rows, h * GLA_DV:(h + 1) * GLA_DV] = _dot_nt(q_dec[:, sl], st.astype(BF16)) + _dot(att.astype(BF16), vh)
            st_s[h] = st * eg[:, sl] + _dot_tn(vh, k_end[:, sl])
        return carry

    lax.fori_loop(0, tb // CHUNK, chunk, 0)

    for h in range(ML_HEADS):
        sl = slice(h * ML_DV, (h + 1) * ML_DV)
        hh = hml_s[:, sl]
        d = hh - jnp.mean(hh, axis=-1, keepdims=True)
        hn = d * lax.rsqrt(jnp.mean(d * d, axis=-1, keepdims=True) + EPS)
        og = zm_ref[:, OFF_MO + h * ML_DV:OFF_MO + (h + 1) * ML_DV].astype(F32)
        y_ref[:, sl] = (hn * mlnorm_ref[:, sl] * _sigmoid(og)).astype(y_ref.dtype)
    for h in range(GLA_HEADS):
        sl = slice(h * GLA_DV, (h + 1) * GLA_DV)
        o = hgla_s[:, sl]
        on = o * lax.rsqrt(jnp.mean(o * o, axis=-1, keepdims=True) + EPS)
        gg = zm_ref[:, OFF_GG + h * GLA_DV:OFF_GG + (h + 1) * GLA_DV].astype(F32)
        y_ref[:, ML_V_W + h * GLA_DV:ML_V_W + (h + 1) * GLA_DV] = (on * glanorm_ref[:, sl] * (gg * _sigmoid(gg))).astype(y_ref.dtype)


def _mixer(zm, zs, convw, convb, igb, fgb, lrup, gateb, mlnorm, glanorm, bsz, seq, tb):
    nt = seq // tb
    const = lambda shape: pl.BlockSpec(shape, lambda b, i: (0,) * len(shape))
    return pl.pallas_call(
        functools.partial(_mixer_kernel, tb=tb),
        out_shape=jax.ShapeDtypeStruct((bsz * seq, ML_V_W + GLA_V_W), BF16),
        grid=(bsz, nt),
        in_specs=[pl.BlockSpec((tb, Z_MAIN_W), lambda b, i: (b * nt + i, 0)),
                  pl.BlockSpec((tb, LANES), lambda b, i: (b * nt + i, 0)),
                  const((CONV_K, 2 * ML_QK_W)), const((1, 2 * ML_QK_W)), const((1, LANES)), const((1, LANES)),
                  const((LANES, GLA_QK_W)), const((1, GLA_QK_W)), const((1, ML_V_W)), const((1, GLA_V_W))],
        out_specs=pl.BlockSpec((tb, ML_V_W + GLA_V_W), lambda b, i: (b * nt + i, 0)),
        scratch_shapes=[pltpu.VMEM((tb + 8, 2 * ML_QK_W), F32),
                        pltpu.VMEM((tb, ML_QK_W), F32), pltpu.VMEM((tb, ML_QK_W), F32),
                        pltpu.VMEM((tb, LANES), F32), pltpu.VMEM((tb, LANES), F32),
                        pltpu.VMEM((tb, GLA_QK_W), F32),
                        pltpu.VMEM((tb, ML_V_W), F32), pltpu.VMEM((tb, GLA_V_W), F32),
                        pltpu.VMEM((ML_HEADS, ML_DK, ML_DV), F32), pltpu.VMEM((ML_HEADS, 1, ML_DK), F32),
                        pltpu.VMEM((1, LANES), F32), pltpu.VMEM((GLA_HEADS, GLA_DV, GLA_DK), F32)],
        compiler_params=_cparams(("parallel", "arbitrary")),
        name="mlstm_gla",
    )(zm, zs, convw, convb, igb, fgb, lrup, gateb, mlnorm, glanorm)


def _proj_router_kernel(y_ref, w_ref, b_ref, h_ref, g_ref, wr_ref, br_ref, h1_ref, u2_ref, rt_ref):
    h1 = h_ref[...] + (_dot(y_ref[...], w_ref[...]) + b_ref[...])
    h1_ref[...] = h1
    u2 = _rms(h1, g_ref[...])
    u2_ref[...] = u2
    u_hi = u2.astype(BF16)
    u_lo = (u2 - u_hi.astype(F32)).astype(BF16)
    part = _dot(u_hi, wr_ref[...])
    logits = part[:, :LANES] + (part[:, LANES:] + _dot(u_lo, wr_ref[:, :LANES])) + br_ref[...]
    lane = lax.broadcasted_iota(jnp.int32, logits.shape, 1)
    lane_f = lane.astype(F32)
    big = float(LANES)
    gl = jnp.where(lane < N_GROUPS, logits, -jnp.inf)
    g_max = jnp.max(gl, axis=-1, keepdims=True)
    g_idx = jnp.min(jnp.where(gl == g_max, lane_f, big), axis=-1, keepdims=True)
    g_p = 1.0 / jnp.sum(jnp.exp(gl - g_max), axis=-1, keepdims=True)
    e_grp = ((lane - N_GROUPS) // EXPERTS_PER_GROUP).astype(F32)
    in_grp = (lane >= N_GROUPS) & (lane < N_GROUPS + N_EXPERTS) & (e_grp == g_idx)
    el = jnp.where(in_grp, logits, -jnp.inf)
    t1 = jnp.max(el, axis=-1, keepdims=True)
    i1 = jnp.min(jnp.where(el == t1, lane_f, big), axis=-1, keepdims=True)
    el2 = jnp.where(lane_f == i1, -jnp.inf, el)
    t2 = jnp.max(el2, axis=-1, keepdims=True)
    i2 = jnp.min(jnp.where(el2 == t2, lane_f, big), axis=-1, keepdims=True)
    e21 = jnp.exp(t2 - t1)
    p1 = 1.0 / (1.0 + e21)
    rt = jnp.where(lane == 0, i1 - N_GROUPS,
                   jnp.where(lane == 1, i2 - N_GROUPS,
                             jnp.where(lane == 2, g_p * p1, jnp.where(lane == 3, g_p * (e21 * p1), 0.0))))
    rt_ref[...] = rt


def _proj_router(y, w, b, h, g, wr, br, tm):
    t, kdim = y.shape
    row = lambda i: (i, 0)
    fixed = lambda i: (0, 0)
    return pl.pallas_call(
        _proj_router_kernel,
        out_shape=(jax.ShapeDtypeStruct((t, D_MODEL), F32), jax.ShapeDtypeStruct((t, D_MODEL), F32),
                   jax.ShapeDtypeStruct((t, LANES), F32)),
        grid=(t // tm,),
        in_specs=[pl.BlockSpec((tm, kdim), row), pl.BlockSpec((kdim, D_MODEL), fixed),
                  pl.BlockSpec((1, D_MODEL), fixed), pl.BlockSpec((tm, D_MODEL), row),
                  pl.BlockSpec((1, D_MODEL), fixed), pl.BlockSpec((D_MODEL, 2 * LANES), fixed),
                  pl.BlockSpec((1, LANES), fixed)],
        out_specs=(pl.BlockSpec((tm, D_MODEL), row), pl.BlockSpec((tm, D_MODEL), row),
                   pl.BlockSpec((tm, LANES), row)),
        compiler_params=_cparams(("parallel",)),
        name="proj_router",
    )(y, w, b, h, g, wr, br)


def _rank_kernel(rt_ref, rk_ref, cnt_ref, base_s, strict_s):
    tt = rt_ref.shape[0]

    @pl.when(pl.program_id(0) == 0)
    def _():
        base_s[...] = jnp.zeros(base_s.shape, F32)
        r = lax.broadcasted_iota(jnp.int32, (tt, tt), 0)
        c = lax.broadcasted_iota(jnp.int32, (tt, tt), 1)
        strict_s[...] = jnp.where(c < r, 1.0, 0.0).astype(BF16)

    rt = rt_ref[...]
    lane = lax.broadcasted_iota(jnp.int32, rt.shape, 1)
    lane_f = lane.astype(F32)
    e0, e1 = rt[:, 0:1], rt[:, 1:2]
    oh0 = lane_f == e0
    oh1 = lane_f == e1
    oh = jnp.where(oh0 | oh1, 1.0, 0.0)
    before = _dot(strict_s[...], oh.astype(BF16)) + base_s[...]
    r0 = jnp.sum(jnp.where(oh0, before, 0.0), axis=-1, keepdims=True)
    r1 = jnp.sum(jnp.where(oh1, before, 0.0), axis=-1, keepdims=True)
    table = jnp.where(lane == 0, r0, jnp.where(lane == 1, r1, jnp.where(lane == 2, e0, jnp.where(lane == 3, e1, 0.0))))
    rk_ref[...] = jnp.transpose(table)[0:8, :].astype(jnp.int32)
    base_s[...] = base_s[...] + jnp.sum(oh, axis=0, keepdims=True)
    cnt_ref[...] = base_s[...]


def _rank(rt, tt):
    t = rt.shape[0]
    return pl.pallas_call(
        _rank_kernel,
        out_shape=(jax.ShapeDtypeStruct((8, t), jnp.int32), jax.ShapeDtypeStruct((1, LANES), F32)),
        grid=(t // tt,),
        in_specs=[pl.BlockSpec((tt, LANES), lambda i: (i, 0))],
        out_specs=(pl.BlockSpec((8, tt), lambda i: (0, i)), pl.BlockSpec((1, LANES), lambda i: (0, 0))),
        scratch_shapes=[pltpu.VMEM((1, LANES), F32), pltpu.VMEM((tt, tt), BF16)],
        compiler_params=_cparams(("arbitrary",)),
        name="expert_rank",
    )(rt)


def _row_copy(src, dst, sem):
    return pltpu.make_async_copy(src, dst, sem)


def _dispatch_kernel(dest_ref, u_ref, xin_hbm, xout_hbm, sem, *, tt):
    del xin_hbm

    def issue(j, carry):
        src = u_ref.at[pl.ds(j, 1)]
        for k in range(TOP_K):
            _row_copy(src, xout_hbm.at[pl.ds(dest_ref[k, j], 1)], sem).start()
        return carry

    lax.fori_loop(0, tt, issue, 0, unroll=8)
    for k in range(TOP_K):
        _row_copy(u_ref, xout_hbm.at[pl.ds(0, tt)], sem).wait()


def _dispatch(dest, u2, xinit, tt):
    t = u2.shape[0]
    return pl.pallas_call(
        functools.partial(_dispatch_kernel, tt=tt),
        out_shape=jax.ShapeDtypeStruct(xinit.shape, F32),
        grid=(t // tt,),
        in_specs=[pl.BlockSpec((TOP_K, tt), lambda i: (0, i), memory_space=pltpu.SMEM),
                  pl.BlockSpec((tt, D_MODEL), lambda i: (i, 0)), pl.BlockSpec(memory_space=pl.ANY)],
        out_specs=pl.BlockSpec(memory_space=pl.ANY),
        scratch_shapes=[pltpu.SemaphoreType.DMA],
        input_output_aliases={2: 0},
        compiler_params=_cparams(("arbitrary",)),
        name="moe_dispatch",
    )(dest, u2, xinit)


def _ffn_kernel(be_ref, nu_ref, x_ref, wg_ref, wu_ref, wd_ref, y_ref, wg_s, wu_s, wd_s):
    b = pl.program_id(0)

    @pl.when((b == 0) | (be_ref[b] != be_ref[jnp.maximum(b - 1, 0)]))
    def _():
        wg_s[...] = wg_ref[...].astype(BF16)
        wu_s[...] = wu_ref[...].astype(BF16)
        wd_s[...] = wd_ref[...].astype(BF16)

    @pl.when(b < nu_ref[0])
    def _():
        x = x_ref[...].astype(BF16)
        a = _dot(x, wg_s[...])
        u = _dot(x, wu_s[...])
        y_ref[...] = _dot(((a * _sigmoid(a)) * u).astype(BF16), wd_s[...])

    @pl.when(b >= nu_ref[0])
    def _():
        y_ref[...] = jnp.zeros(y_ref.shape, y_ref.dtype)


def _ffn(block_expert, n_used, xbuf, wg, wu, wd):
    n_pad = xbuf.shape[0]
    n_blocks = n_pad // MOE_BM
    rows = lambda b, be, nu: (b, 0)
    wmap = lambda b, be, nu: (be[b], 0, 0)
    return pl.pallas_call(
        _ffn_kernel,
        out_shape=jax.ShapeDtypeStruct((n_pad, D_MODEL), F32),
        grid_spec=pltpu.PrefetchScalarGridSpec(
            num_scalar_prefetch=2,
            grid=(n_blocks,),
            in_specs=[pl.BlockSpec((MOE_BM, D_MODEL), rows),
                      pl.BlockSpec((None, D_MODEL, D_FF), wmap),
                      pl.BlockSpec((None, D_MODEL, D_FF), wmap),
                      pl.BlockSpec((None, D_FF, D_MODEL), wmap)],
            out_specs=pl.BlockSpec((MOE_BM, D_MODEL), rows),
            scratch_shapes=[pltpu.VMEM((D_MODEL, D_FF), BF16), pltpu.VMEM((D_MODEL, D_FF), BF16),
                            pltpu.VMEM((D_FF, D_MODEL), BF16)]),
        compiler_params=_cparams(("arbitrary",)),
        name="moe_ffn",
    )(block_expert, n_used, xbuf, wg, wu, wd)


COMBINE_GROUP = 8


def _combine_kernel(dcur_ref, dnxt_ref, y_hbm, h_ref, rt_ref, g_ref, o_ref, ybuf, sems, *, tt, final_norm):
    i = pl.program_id(0)
    last = pl.num_programs(0) - 1
    slot = i % 2

    def gather(dref, s, j):
        for k in range(TOP_K):
            _row_copy(y_hbm.at[pl.ds(dref[k, j], 1)], ybuf.at[s, k, pl.ds(j, 1)], sems.at[s]).start()

    def wait_slot(s):
        for k in range(TOP_K):
            _row_copy(y_hbm.at[pl.ds(0, tt)], ybuf.at[s, k], sems.at[s]).wait()

    @pl.when(i == 0)
    def _():
        def first(j, carry):
            gather(dcur_ref, 0, j)
            return carry
        lax.fori_loop(0, tt, first, 0, unroll=8)

    wait_slot(slot)

    def group(gi, carry):
        r0 = pl.multiple_of(gi * COMBINE_GROUP, COMBINE_GROUP)
        for jj in range(COMBINE_GROUP):
            gather(dnxt_ref, 1 - slot, r0 + jj)
        rows = pl.ds(r0, COMBINE_GROUP)
        rt = rt_ref[rows, :]
        out = h_ref[rows, :] + (ybuf[slot, 0, rows, :] * rt[:, 2:3] + ybuf[slot, 1, rows, :] * rt[:, 3:4])
        if final_norm:
            out = _rms(out, g_ref[...])
        o_ref[rows, :] = out
        return carry

    lax.fori_loop(0, tt // COMBINE_GROUP, group, 0)

    @pl.when(i == last)
    def _():
        wait_slot(1 - slot)


def _combine(dest, ybuf, h1, rt, g, tt, final_norm):
    t = h1.shape[0]
    nsteps = t // tt
    return pl.pallas_call(
        functools.partial(_combine_kernel, tt=tt, final_norm=final_norm),
        out_shape=jax.ShapeDtypeStruct((t, D_MODEL), F32),
        grid=(nsteps,),
        in_specs=[pl.BlockSpec((TOP_K, tt), lambda i: (0, i), memory_space=pltpu.SMEM),
                  pl.BlockSpec((TOP_K, tt), lambda i: (0, jnp.minimum(i + 1, nsteps - 1)), memory_space=pltpu.SMEM),
                  pl.BlockSpec(memory_space=pl.ANY),
                  pl.BlockSpec((tt, D_MODEL), lambda i: (i, 0)),
                  pl.BlockSpec((tt, LANES), lambda i: (i, 0)),
                  pl.BlockSpec((1, D_MODEL), lambda i: (0, 0))],
        out_specs=pl.BlockSpec((tt, D_MODEL), lambda i: (i, 0)),
        scratch_shapes=[pltpu.VMEM((2, TOP_K, tt, D_MODEL), F32), pltpu.SemaphoreType.DMA((2,))],
        compiler_params=_cparams(("arbitrary",)),
        name="moe_combine",
    )(dest, dest, ybuf, h1, rt, g)


def _moe_rows(t):
    return ((t * TOP_K) // MOE_BM + N_EXPERTS) * MOE_BM


def _moe(h1, u2, rt, wg, wu, wd, g_final, final_norm, xinit):
    rk, cnt = _rank(rt, 1024)
    counts = cnt[0, :N_EXPERTS].astype(jnp.int32)
    padded = (counts + MOE_BM - 1) // MOE_BM * MOE_BM
    pad_end = jnp.cumsum(padded)
    pad_start = pad_end - padded
    n_blocks = xinit.shape[0] // MOE_BM
    dest = pad_start[rk[TOP_K:2 * TOP_K]] + rk[0:TOP_K]
    n_used = (pad_end[-1] // MOE_BM).astype(jnp.int32)
    blk = jnp.minimum(jnp.arange(n_blocks, dtype=jnp.int32), n_used - 1) * MOE_BM
    block_expert = jnp.minimum(jnp.sum(pad_end[None, :] <= blk[:, None], axis=1), N_EXPERTS - 1).astype(jnp.int32)
    xbuf = _dispatch(dest, u2, xinit, 256)
    ybuf = _ffn(block_expert, n_used.reshape(1), xbuf, wg, wu, wd)
    return _combine(dest, ybuf, h1, rt, g_final, 256, final_norm), xbuf


def _qkv_kernel(x_ref, g_ref, w_ref, b_ref, pos_ref, rot_ref, q_ref, kv_ref):
    u = _rms(x_ref[...], g_ref[...]).astype(BF16)
    ang = pos_ref[...].astype(F32) * rot_ref[0:1, :]
    cosv = jnp.cos(ang)
    sinv = jnp.sin(ang)
    c_coef = jnp.where(rot_ref[3:4, :] > 0.0, cosv, 1.0)
    s_lo = -sinv * rot_ref[1:2, :]
    s_hi = sinv * rot_ref[2:3, :]

    def rotate(z):
        return z * c_coef + pltpu.roll(z, LANES - ROT_DIM // 2, 1) * s_lo + pltpu.roll(z, ROT_DIM // 2, 1) * s_hi

    scale = SWA_HEAD_DIM ** -0.5
    for j in range(Q_W // LANES):
        sl = slice(j * LANES, (j + 1) * LANES)
        q_ref[:, sl] = (rotate(_dot(u, w_ref[:, sl]) + b_ref[:, sl]) * scale).astype(q_ref.dtype)
    for j in range(KV_OUT_W // LANES):
        sl = slice(Q_W + j * LANES, Q_W + (j + 1) * LANES)
        z = _dot(u, w_ref[:, sl]) + b_ref[:, sl]
        kv_ref[:, j * LANES:(j + 1) * LANES] = (rotate(z) if j % 2 == 0 else z).astype(kv_ref.dtype)


def _rot_table():
    lane = np.arange(LANES)
    d = lane % SWA_HEAD_DIM
    inv_freq = ROPE_THETA ** (-jnp.arange(0, ROT_DIM, 2, dtype=F32) / ROT_DIM)
    half = ROT_DIM // 2
    tab = jnp.zeros((8, LANES), F32)
    tab = tab.at[0].set(jnp.where(d < ROT_DIM, inv_freq[d % half], 0.0))
    tab = tab.at[1].set((d < half).astype(np.float32))
    tab = tab.at[2].set(((d >= half) & (d < ROT_DIM)).astype(np.float32))
    tab = tab.at[3].set((d < ROT_DIM).astype(np.float32))
    return tab


def _qkv(h, g, w, b, pos, tm):
    t = h.shape[0]
    row = lambda i: (i, 0)
    fixed = lambda i: (0, 0)
    wtot = Q_W + KV_OUT_W
    return pl.pallas_call(
        _qkv_kernel,
        out_shape=(jax.ShapeDtypeStruct((t, Q_W), BF16), jax.ShapeDtypeStruct((t, KV_OUT_W), BF16)),
        grid=(t // tm,),
        in_specs=[pl.BlockSpec((tm, D_MODEL), row), pl.BlockSpec((1, D_MODEL), fixed),
                  pl.BlockSpec((D_MODEL, wtot), fixed), pl.BlockSpec((1, wtot), fixed),
                  pl.BlockSpec((tm, 1), row), pl.BlockSpec((8, LANES), fixed)],
        out_specs=(pl.BlockSpec((tm, Q_W), row), pl.BlockSpec((tm, KV_OUT_W), row)),
        compiler_params=_cparams(("parallel",)),
        name="qkv_rotary",
    )(h, g, w, b, pos, _rot_table())


def _swa_kernel(q_ref, kvc_ref, kvp_ref, sink_ref, o_ref):
    neg = jnp.where(pl.program_id(1) > 0, 0.0, -jnp.inf).astype(F32)
    kj = lax.broadcasted_iota(jnp.int32, (WINDOW, WINDOW), 0)
    qi = lax.broadcasted_iota(jnp.int32, (WINDOW, WINDOW), 1)
    mask_cur = kj <= qi
    top = kj < SWA_HEAD_DIM
    lane = lax.broadcasted_iota(jnp.int32, (1, LANES), 1)
    keep_lo = jnp.where(lane < SWA_HEAD_DIM, 1.0, 0.0).astype(BF16)
    keep_hi = jnp.where(lane < SWA_HEAD_DIM, 0.0, 1.0).astype(BF16)
    zeros_half = jnp.zeros((SWA_HEAD_DIM, WINDOW), F32)

    def arranged(ref):
        k_nat, k_swp = ref[:, 0:KV_W], ref[:, 2 * KV_W:3 * KV_W]
        vt = jnp.transpose(ref[:, KV_W:2 * KV_W].astype(F32))
        vt_top = lambda g: jnp.concatenate([vt[g * SWA_HEAD_DIM:(g + 1) * SWA_HEAD_DIM], zeros_half], axis=0).astype(BF16)
        vt_bot = lambda g: jnp.concatenate([zeros_half, vt[g * SWA_HEAD_DIM:(g + 1) * SWA_HEAD_DIM]], axis=0).astype(BF16)
        return {(0, 0): (k_nat * keep_lo, vt_top(0)), (0, 1): (k_swp * keep_hi, vt_bot(0)),
                (1, 0): (k_swp * keep_lo, vt_top(1)), (1, 1): (k_nat * keep_hi, vt_bot(1))}

    cur = arranged(kvc_ref)
    prev = arranged(kvp_ref)
    heads = range(SWA_Q_HEADS)
    key = lambda h: (h // SWA_GROUP, h % 2)
    scores = []
    for h in heads:
        qp = q_ref[:, (h // 2) * LANES:(h // 2 + 1) * LANES]
        sc = _dot_nt(cur[key(h)][0], qp)
        sp = _dot_nt(prev[key(h)][0], qp)
        scores.append(jnp.where(mask_cur, sc, sp + neg))
    sinks = [sink_ref[:, h:h + 1] for h in heads]
    maxes = [jnp.maximum(jnp.max(scores[h], axis=0, keepdims=True), sinks[h]) for h in heads]
    probs = [jnp.exp(scores[h] - maxes[h]) for h in heads]
    denoms = [jnp.sum(probs[h], axis=0, keepdims=True) + jnp.exp(sinks[h] - maxes[h]) for h in heads]
    for pr in range(SWA_Q_HEADS // 2):
        num = None
        for h in (2 * pr, 2 * pr + 1):
            pc = jnp.where(mask_cur, probs[h], 0.0).astype(BF16)
            pp = jnp.where(mask_cur, 0.0, probs[h]).astype(BF16)
            part = _dot(cur[key(h)][1], pc) + _dot(prev[key(h)][1], pp)
            num = part if num is None else num + part
        den = jnp.where(top, denoms[2 * pr], denoms[2 * pr + 1])
        o_ref[:, pr * LANES:(pr + 1) * LANES] = jnp.transpose(num / den).astype(o_ref.dtype)


def _swa(q, kv, sinks, bsz, seq):
    nb = seq // WINDOW
    return pl.pallas_call(
        _swa_kernel,
        out_shape=jax.ShapeDtypeStruct((bsz * seq, Q_W), BF16),
        grid=(bsz, nb),
        in_specs=[pl.BlockSpec((WINDOW, Q_W), lambda b, n: (b * nb + n, 0)),
                  pl.BlockSpec((WINDOW, KV_OUT_W), lambda b, n: (b * nb + n, 0)),
                  pl.BlockSpec((WINDOW, KV_OUT_W), lambda b, n: (b * nb + jnp.maximum(n - 1, 0), 0)),
                  pl.BlockSpec((1, LANES), lambda b, n: (0, 0))],
        out_specs=pl.BlockSpec((WINDOW, Q_W), lambda b, n: (b * nb + n, 0)),
        compiler_params=_cparams(("parallel", "parallel")),
        name="swa_attention",
    )(q, kv, kv, sinks)


def _pack_in_proj(w_in):
    offs = np.cumsum((0,) + AB_SPLITS)
    part = lambda i: w_in[:, offs[i]:offs[i + 1]]
    w_main = jnp.concatenate([part(0), part(1), part(2), part(5), part(6), part(7), part(8)], axis=1).astype(BF16)
    small = jnp.concatenate([part(3), part(4), part(9)], axis=1)
    w_small = jnp.pad(small, ((0, 0), (0, LANES - small.shape[1]))).astype(BF16)
    return w_main, w_small


def _pack_qkv(w, b):
    hd = SWA_HEAD_DIM
    k0, k1 = slice(Q_W, Q_W + hd), slice(Q_W + hd, Q_W + 2 * hd)
    cols = lambda a: jnp.concatenate([a[..., :Q_W + 2 * KV_W], a[..., k1], a[..., k0]], axis=-1)
    return cols(w).astype(BF16), cols(b).reshape(1, -1).astype(F32)


def _lane_row(v, offset=0):
    return jnp.zeros((1, LANES), F32).at[0, offset:offset + v.shape[0]].set(v.astype(F32))


def kernel(x, positions, mix_norm, ffn_norm, final_norm, ab_w_in, ab_conv_w, ab_conv_b, ml_igate_b, ml_fgate_b, ml_head_norm, gla_w_lr_up, gla_gate_b, gla_head_norm, ab_w_out, swa_w_qkv, swa_b_qkv, swa_sinks, swa_w_o, swa_b_o, router_group_w, router_group_b, router_expert_w, router_expert_b, expert_w_gate, expert_w_up, expert_w_down):
    bsz, seq, d = x.shape
    t = bsz * seq
    h = x.reshape(t, d)
    row = lambda v: v.reshape(1, -1).astype(F32)

    def router_params(layer):
        wr = jnp.zeros((d, LANES), F32).at[:, :N_GROUPS].set(router_group_w[layer])
        wr = wr.at[:, N_GROUPS:N_GROUPS + N_EXPERTS].set(router_expert_w[layer])
        br = jnp.zeros((1, LANES), F32).at[0, :N_GROUPS].set(router_group_b[layer])
        br = br.at[0, N_GROUPS:N_GROUPS + N_EXPERTS].set(router_expert_b[layer])
        w_hi = wr.astype(BF16)
        w_lo = (wr - w_hi.astype(F32)).astype(BF16)
        return jnp.concatenate([w_hi, w_lo], axis=1), br

    def experts(layer):
        return expert_w_gate[layer], expert_w_up[layer], expert_w_down[layer]

    w_main, w_small = _pack_in_proj(ab_w_in[0])
    zm, zs = _in_proj(h, row(mix_norm[0]), w_main, w_small, 512)
    lrup = jnp.zeros((LANES, GLA_QK_W), F32).at[SM_LR:SM_LR + GLA_LOWRANK].set(gla_w_lr_up[0]).astype(BF16)
    y = _mixer(zm, zs, ab_conv_w[0], row(ab_conv_b[0]), _lane_row(ml_igate_b[0]), _lane_row(ml_fgate_b[0]), lrup,
               row(gla_gate_b[0]), row(ml_head_norm[0]), row(gla_head_norm[0]), bsz, seq, 256)
    wr, br = router_params(0)
    h1, u2, rt = _proj_router(y, ab_w_out[0].astype(BF16), jnp.zeros((1, d), F32), h, row(ffn_norm[0]), wr, br, 512)
    h, xbuf = _moe(h1, u2, rt, *experts(0), row(final_norm), False, jnp.zeros((_moe_rows(t), d), F32))

    w_qkv, b_qkv = _pack_qkv(swa_w_qkv[0], swa_b_qkv[0])
    q, kv = _qkv(h, row(mix_norm[1]), w_qkv, b_qkv, positions.reshape(t, 1), 512)
    o = _swa(q, kv, _lane_row(swa_sinks[0]), bsz, seq)
    wr, br = router_params(1)
    h1, u2, rt = _proj_router(o, swa_w_o[0].astype(BF16), row(swa_b_o[0]), h, row(ffn_norm[1]), wr, br, 512)
    out, _ = _moe(h1, u2, rt, *experts(1), row(final_norm), True, xbuf)
    return out.reshape(bsz, seq, d)
```

```python
import functools

import jax
import jax.numpy as jnp
import numpy as np
from jax import lax
from jax.experimental import pallas as pl
from jax.experimental.pallas import tpu as pltpu

F32 = jnp.float32
BF16 = jnp.bfloat16
HIGHEST = lax.Precision.HIGHEST

D_MODEL = 1024
EPS = 1e-6
ML_HEADS = 4
ML_DK = 64
ML_DV = 128
GLA_HEADS = 4
GLA_DK = 64
GLA_DV = 128
CHUNK = 64
CONV_K = 4
GLA_LOWRANK = 16
GLA_TAU = 16.0
ML_QK_W = ML_HEADS * ML_DK
ML_V_W = ML_HEADS * ML_DV
GLA_QK_W = GLA_HEADS * GLA_DK
GLA_V_W = GLA_HEADS * GLA_DV
AB_SPLITS = (2 * ML_QK_W, ML_V_W, ML_V_W, ML_HEADS, ML_HEADS, GLA_QK_W, GLA_QK_W, GLA_V_W, GLA_V_W, GLA_LOWRANK)
OFF_MQK = 0
OFF_MV = OFF_MQK + 2 * ML_QK_W
OFF_MO = OFF_MV + ML_V_W
OFF_GQ = OFF_MO + ML_V_W
OFF_GK = OFF_GQ + GLA_QK_W
OFF_GV = OFF_GK + GLA_QK_W
OFF_GG = OFF_GV + GLA_V_W
Z_MAIN_W = OFF_GG + GLA_V_W
SM_I = 0
SM_F = ML_HEADS
SM_LR = 2 * ML_HEADS
LANES = 128
SWA_Q_HEADS = 16
SWA_KV_HEADS = 2
SWA_HEAD_DIM = 64
SWA_GROUP = SWA_Q_HEADS // SWA_KV_HEADS
WINDOW = 128
ROT_DIM = SWA_HEAD_DIM // 4
ROPE_THETA = 500000.0
Q_W = SWA_Q_HEADS * SWA_HEAD_DIM
KV_W = SWA_KV_HEADS * SWA_HEAD_DIM
KV_OUT_W = 3 * KV_W
N_GROUPS = 4
EXPERTS_PER_GROUP = 8
N_EXPERTS = N_GROUPS * EXPERTS_PER_GROUP
TOP_K = 2
D_FF = 512
MOE_BM = 256
ROW_TILES = D_MODEL // LANES

VMEM_LIMIT = 56 * 1024 * 1024


def _cparams(sem):
    return pltpu.CompilerParams(dimension_semantics=sem, vmem_limit_bytes=VMEM_LIMIT)


def _rms(x, g):
    return x * lax.rsqrt(jnp.mean(x * x, axis=-1, keepdims=True) + EPS) * g


def _log_sigmoid(x):
    return jnp.minimum(x, 0.0) - jnp.log1p(jnp.exp(-jnp.abs(x)))


def _sigmoid(x):
    return 1.0 / (1.0 + jnp.exp(-x))


def _dot(a, b):
    return jnp.dot(a, b, preferred_element_type=F32)


def _dot_nt(a, b):
    return lax.dot_general(a, b, (((1,), (1,)), ((), ())), preferred_element_type=F32)


def _dot_tn(a, b):
    return lax.dot_general(a, b, (((0,), (0,)), ((), ())), preferred_element_type=F32)


def _in_proj_kernel(x_ref, g_ref, w_ref, ws_ref, zm_ref, zs_ref, *, n_chunk):
    u = _rms(x_ref[...], g_ref[...]).astype(BF16)
    for n0 in range(0, w_ref.shape[1], n_chunk):
        zm_ref[:, n0:n0 + n_chunk] = _dot(u, w_ref[:, n0:n0 + n_chunk]).astype(zm_ref.dtype)
    zs_ref[...] = _dot(u, ws_ref[...])


def _in_proj(h, g, w_main, w_small, tm):
    t = h.shape[0]
    return pl.pallas_call(
        functools.partial(_in_proj_kernel, n_chunk=768),
        out_shape=(jax.ShapeDtypeStruct((t, Z_MAIN_W), BF16), jax.ShapeDtypeStruct((t, LANES), F32)),
        grid=(t // tm,),
        in_specs=[pl.BlockSpec((tm, D_MODEL), lambda i: (i, 0)),
                  pl.BlockSpec((1, D_MODEL), lambda i: (0, 0)),
                  pl.BlockSpec((D_MODEL, Z_MAIN_W), lambda i: (0, 0)),
                  pl.BlockSpec((D_MODEL, LANES), lambda i: (0, 0))],
        out_specs=(pl.BlockSpec((tm, Z_MAIN_W), lambda i: (i, 0)),
                   pl.BlockSpec((tm, LANES), lambda i: (i, 0))),
        compiler_params=_cparams(("parallel",)),
        name="in_proj",
    )(h, g, w_main, w_small)


def _mixer_kernel(zm_ref, zs_ref, convw_ref, convb_ref, igb_ref, fgb_ref, lrup_ref, gateb_ref, mlnorm_ref,
                  glanorm_ref, y_ref, xpad, q_s, k_s, ig_s, lf_s, la_s, hml_s, hgla_s, c_s, n_s, m_s, st_s, *, tb):
    @pl.when(pl.program_id(1) == 0)
    def _():
        xpad[0:8, :] = jnp.zeros((8, 2 * ML_QK_W), F32)
        c_s[...] = jnp.zeros(c_s.shape, F32)
        n_s[...] = jnp.zeros(n_s.shape, F32)
        st_s[...] = jnp.zeros(st_s.shape, F32)
        m_s[...] = jnp.full(m_s.shape, -jnp.inf, F32)

    xpad[8:8 + tb, :] = zm_ref[:, OFF_MQK:OFF_MQK + 2 * ML_QK_W].astype(F32)
    conv = convb_ref[...] + convw_ref[3:4, :] * xpad[8:8 + tb, :]
    for j in range(CONV_K - 1):
        conv = conv + convw_ref[j:j + 1, :] * xpad[5 + j:5 + j + tb, :]
    xpad[0:8, :] = xpad[tb:tb + 8, :]
    qk = conv * _sigmoid(conv)
    q_s[...] = qk[:, :ML_QK_W]
    k_s[...] = qk[:, ML_QK_W:] * (ML_DK ** -0.5)

    zs = zs_ref[...]
    ig_s[...] = zs + igb_ref[...]
    lf_s[...] = _log_sigmoid(pltpu.roll(zs, LANES - SM_F, 1) + fgb_ref[...])
    la_s[...] = _log_sigmoid(_dot(zs.astype(BF16), lrup_ref[...]) + gateb_ref[...]) * (1.0 / GLA_TAU)

    row = lax.broadcasted_iota(jnp.int32, (CHUNK, CHUNK), 0)
    col = lax.broadcasted_iota(jnp.int32, (CHUNK, CHUNK), 1)
    causal = col <= row
    tri = causal.astype(F32)

    def chunk(c, carry):
        rows = pl.ds(pl.multiple_of(c * CHUNK, CHUNK), CHUNK)
        ig = ig_s[rows, :]
        bc = jnp.dot(tri, lf_s[rows, :], precision=HIGHEST, preferred_element_type=F32)
        g_row = bc[CHUNK - 1:CHUNK, :]
        a = g_row - bc + ig
        a_max = jnp.max(a, axis=0, keepdims=True)
        wa = jnp.exp(a - a_max)
        m_row = m_s[...]
        m_new = jnp.maximum(g_row + m_row, a_max)
        s_old = jnp.exp(g_row + m_row - m_new)
        s_in = jnp.exp(a_max - m_new)
        il_all = bc + m_row
        e = ig - bc
        et = jnp.transpose(jnp.concatenate([e, e], axis=0))
        q = q_s[rows, :]
        k = k_s[rows, :]
        for h in range(ML_HEADS):
            qh = q[:, h * ML_DK:(h + 1) * ML_DK]
            kh = k[:, h * ML_DK:(h + 1) * ML_DK]
            vh = zm_ref[rows, OFF_MV + h * ML_DV:OFF_MV + (h + 1) * ML_DV]
            kw = kh * wa[:, h:h + 1]
            c_chunk = _dot_tn(kw.astype(BF16), vh)
            n_chunk = jnp.sum(kw, axis=0, keepdims=True)
            d_log = jnp.where(causal, bc[:, h:h + 1] + et[h:h + 1, 0:CHUNK], -jnp.inf)
            il = il_all[:, h:h + 1]
            m_t = jnp.maximum(il, jnp.max(d_log, axis=-1, keepdims=True))
            s_inter = jnp.exp(il - m_t)
            qhb = qh.astype(BF16)
            qkm = _dot_nt(qhb, kh.astype(BF16)) * jnp.exp(d_log - m_t)
            c_st = c_s[h]
            n_st = n_s[h]
            num = s_inter * _dot(qhb, c_st.astype(BF16)) + _dot(qkm.astype(BF16), vh)
            den = s_inter * jnp.sum(qh * n_st, axis=-1, keepdims=True) + jnp.sum(qkm, axis=-1, keepdims=True)
            hml_s[rows, h * ML_DV:(h + 1) * ML_DV] = num / jnp.maximum(jnp.abs(den), jnp.exp(-m_t))
            c_s[h] = s_old[:, h:h + 1] * c_st + s_in[:, h:h + 1] * c_chunk
            n_s[h] = s_old[:, h:h + 1] * n_st + s_in[:, h:h + 1] * n_chunk
        m_s[...] = m_new
        bcg = jnp.dot(tri, la_s[rows, :], precision=HIGHEST, preferred_element_type=F32)
        gg_row = bcg[CHUNK - 1:CHUNK, :]
        gq = zm_ref[rows, OFF_GQ:OFF_GQ + GLA_QK_W].astype(F32)
        gk = zm_ref[rows, OFF_GK:OFF_GK + GLA_QK_W].astype(F32) * (GLA_DK ** -0.5)
        q_dec = (gq * jnp.exp(bcg)).astype(BF16)
        k_inv = (gk * jnp.exp(-bcg)).astype(BF16)
        k_end = (gk * jnp.exp(gg_row - bcg)).astype(BF16)
        eg = jnp.exp(gg_row)
        for h in range(GLA_HEADS):
            sl = slice(h * GLA_DK, (h + 1) * GLA_DK)
            vh = zm_ref[rows, OFF_GV + h * GLA_DV:OFF_GV + (h + 1) * GLA_DV]
            att = jnp.where(causal, _dot_nt(q_dec[:, sl], k_inv[:, sl]), 0.0)
            st = st_s[h]
            hgla_s[rows, h * GLA_DV:(h + 1) * GLA_DV] = _dot_nt(q_dec[:, sl], st.astype(BF16)) + _dot(att.astype(BF16), vh)
            st_s[h] = st * eg[:, sl] + _dot_tn(vh, k_end[:, sl])
        return carry

    lax.fori_loop(0, tb // CHUNK, chunk, 0)

    for h in range(ML_HEADS):
        sl = slice(h * ML_DV, (h + 1) * ML_DV)
        hh = hml_s[:, sl]
        d = hh - jnp.mean(hh, axis=-1, keepdims=True)
        hn = d * lax.rsqrt(jnp.mean(d * d, axis=-1, keepdims=True) + EPS)
        og = zm_ref[:, OFF_MO + h * ML_DV:OFF_MO + (h + 1) * ML_DV].astype(F32)
        y_ref[:, sl] = (hn * mlnorm_ref[:, sl] * _sigmoid(og)).astype(y_ref.dtype)
    for h in range(GLA_HEADS):
        sl = slice(h * GLA_DV, (h + 1) * GLA_DV)
        o = hgla_s[:, sl]
        on = o * lax.rsqrt(jnp.mean(o * o, axis=-1, keepdims=True) + EPS)
        gg = zm_ref[:, OFF_GG + h * GLA_DV:OFF_GG + (h + 1) * GLA_DV].astype(F32)
        y_ref[:, ML_V_W + h * GLA_DV:ML_V_W + (h + 1) * GLA_DV] = (on * glanorm_ref[:, sl] * (gg * _sigmoid(gg))).astype(y_ref.dtype)


def _mixer(zm, zs, convw, convb, igb, fgb, lrup, gateb, mlnorm, glanorm, bsz, seq, tb):
    nt = seq // tb
    const = lambda shape: pl.BlockSpec(shape, lambda b, i: (0,) * len(shape))
    return pl.pallas_call(
        functools.partial(_mixer_kernel, tb=tb),
        out_shape=jax.ShapeDtypeStruct((bsz * seq, ML_V_W + GLA_V_W), BF16),
        grid=(bsz, nt),
        in_specs=[pl.BlockSpec((tb, Z_MAIN_W), lambda b, i: (b * nt + i, 0)),
                  pl.BlockSpec((tb, LANES), lambda b, i: (b * nt + i, 0)),
                  const((CONV_K, 2 * ML_QK_W)), const((1, 2 * ML_QK_W)), const((1, LANES)), const((1, LANES)),
                  const((LANES, GLA_QK_W)), const((1, GLA_QK_W)), const((1, ML_V_W)), const((1, GLA_V_W))],
        out_specs=pl.BlockSpec((tb, ML_V_W + GLA_V_W), lambda b, i: (b * nt + i, 0)),
        scratch_shapes=[pltpu.VMEM((tb + 8, 2 * ML_QK_W), F32),
                        pltpu.VMEM((tb, ML_QK_W), F32), pltpu.VMEM((tb, ML_QK_W), F32),
                        pltpu.VMEM((tb, LANES), F32), pltpu.VMEM((tb, LANES), F32),
                        pltpu.VMEM((tb, GLA_QK_W), F32),
                        pltpu.VMEM((tb, ML_V_W), F32), pltpu.VMEM((tb, GLA_V_W), F32),
                        pltpu.VMEM((ML_HEADS, ML_DK, ML_DV), F32), pltpu.VMEM((ML_HEADS, 1, ML_DK), F32),
                        pltpu.VMEM((1, LANES), F32), pltpu.VMEM((GLA_HEADS, GLA_DV, GLA_DK), F32)],
        compiler_params=_cparams(("parallel", "arbitrary")),
        name="mlstm_gla",
    )(zm, zs, convw, convb, igb, fgb, lrup, gateb, mlnorm, glanorm)


def _proj_router_kernel(y_ref, w_ref, b_ref, h_ref, g_ref, wr_ref, br_ref, h1_ref, u2_ref, rt_ref):
    h1 = h_ref[...] + (_dot(y_ref[...], w_ref[...]) + b_ref[...])
    h1_ref[...] = h1
    u2 = _rms(h1, g_ref[...])
    for c in range(ROW_TILES):
        u2_ref[pl.ds(c, u2.shape[0], stride=ROW_TILES), :] = u2[:, c * LANES:(c + 1) * LANES]
    u_hi = u2.astype(BF16)
    u_lo = (u2 - u_hi.astype(F32)).astype(BF16)
    part = _dot(u_hi, wr_ref[...])
    logits = part[:, :LANES] + (part[:, LANES:] + _dot(u_lo, wr_ref[:, :LANES])) + br_ref[...]
    lane = lax.broadcasted_iota(jnp.int32, logits.shape, 1)
    lane_f = lane.astype(F32)
    big = float(LANES)
    gl = jnp.where(lane < N_GROUPS, logits, -jnp.inf)
    g_max = jnp.max(gl, axis=-1, keepdims=True)
    g_idx = jnp.min(jnp.where(gl == g_max, lane_f, big), axis=-1, keepdims=True)
    g_p = 1.0 / jnp.sum(jnp.exp(gl - g_max), axis=-1, keepdims=True)
    e_grp = ((lane - N_GROUPS) // EXPERTS_PER_GROUP).astype(F32)
    in_grp = (lane >= N_GROUPS) & (lane < N_GROUPS + N_EXPERTS) & (e_grp == g_idx)
    el = jnp.where(in_grp, logits, -jnp.inf)
    t1 = jnp.max(el, axis=-1, keepdims=True)
    i1 = jnp.min(jnp.where(el == t1, lane_f, big), axis=-1, keepdims=True)
    el2 = jnp.where(lane_f == i1, -jnp.inf, el)
    t2 = jnp.max(el2, axis=-1, keepdims=True)
    i2 = jnp.min(jnp.where(el2 == t2, lane_f, big), axis=-1, keepdims=True)
    e21 = jnp.exp(t2 - t1)
    p1 = 1.0 / (1.0 + e21)
    rt = jnp.where(lane == 0, i1 - N_GROUPS,
                   jnp.where(lane == 1, i2 - N_GROUPS,
                             jnp.where(lane == 2, g_p * p1, jnp.where(lane == 3, g_p * (e21 * p1), 0.0))))
    rt_ref[...] = rt


def _proj_router(y, w, b, h, g, wr, br, tm):
    t, kdim = y.shape
    row = lambda i: (i, 0)
    fixed = lambda i: (0, 0)
    return pl.pallas_call(
        _proj_router_kernel,
        out_shape=(jax.ShapeDtypeStruct((t, D_MODEL), F32), jax.ShapeDtypeStruct((t * ROW_TILES, LANES), F32),
                   jax.ShapeDtypeStruct((t, LANES), F32)),
        grid=(t // tm,),
        in_specs=[pl.BlockSpec((tm, kdim), row), pl.BlockSpec((kdim, D_MODEL), fixed),
                  pl.BlockSpec((1, D_MODEL), fixed), pl.BlockSpec((tm, D_MODEL), row),
                  pl.BlockSpec((1, D_MODEL), fixed), pl.BlockSpec((D_MODEL, 2 * LANES), fixed),
                  pl.BlockSpec((1, LANES), fixed)],
        out_specs=(pl.BlockSpec((tm, D_MODEL), row), pl.BlockSpec((tm * ROW_TILES, LANES), row),
                   pl.BlockSpec((tm, LANES), row)),
        compiler_params=_cparams(("parallel",)),
        name="proj_router",
    )(y, w, b, h, g, wr, br)


def _rank_kernel(rt_ref, rk_ref, cnt_ref, base_s, strict_s):
    tt = rt_ref.shape[0]

    @pl.when(pl.program_id(0) == 0)
    def _():
        base_s[...] = jnp.zeros(base_s.shape, F32)
        r = lax.broadcasted_iota(jnp.int32, (tt, tt), 0)
        c = lax.broadcasted_iota(jnp.int32, (tt, tt), 1)
        strict_s[...] = jnp.where(c < r, 1.0, 0.0).astype(BF16)

    rt = rt_ref[...]
    lane = lax.broadcasted_iota(jnp.int32, rt.shape, 1)
    lane_f = lane.astype(F32)
    e0, e1 = rt[:, 0:1], rt[:, 1:2]
    oh0 = lane_f == e0
    oh1 = lane_f == e1
    oh = jnp.where(oh0 | oh1, 1.0, 0.0)
    before = _dot(strict_s[...], oh.astype(BF16)) + base_s[...]
    r0 = jnp.sum(jnp.where(oh0, before, 0.0), axis=-1, keepdims=True)
    r1 = jnp.sum(jnp.where(oh1, before, 0.0), axis=-1, keepdims=True)
    table = jnp.where(lane == 0, r0, jnp.where(lane == 1, r1, jnp.where(lane == 2, e0, jnp.where(lane == 3, e1, 0.0))))
    rk_ref[...] = jnp.transpose(table)[0:8, :].astype(jnp.int32)
    base_s[...] = base_s[...] + jnp.sum(oh, axis=0, keepdims=True)
    cnt_ref[...] = base_s[...]


def _rank(rt, tt):
    t = rt.shape[0]
    return pl.pallas_call(
        _rank_kernel,
        out_shape=(jax.ShapeDtypeStruct((8, t), jnp.int32), jax.ShapeDtypeStruct((1, LANES), F32)),
        grid=(t // tt,),
        in_specs=[pl.BlockSpec((tt, LANES), lambda i: (i, 0))],
        out_specs=(pl.BlockSpec((8, tt), lambda i: (0, i)), pl.BlockSpec((1, LANES), lambda i: (0, 0))),
        scratch_shapes=[pltpu.VMEM((1, LANES), F32), pltpu.VMEM((tt, tt), BF16)],
        compiler_params=_cparams(("arbitrary",)),
        name="expert_rank",
    )(rt)


DMA_GROUP = 8


def _row_copy(src, dst, sem):
    return pltpu.make_async_copy(src, dst, sem)


def _row_tile(r):
    return pl.ds(pl.multiple_of(r * ROW_TILES, ROW_TILES), ROW_TILES)


def _dispatch_kernel(dest_ref, u_ref, xin_hbm, xout_hbm, sem, *, tt):
    del xin_hbm

    def issue(g, carry):
        for jj in range(DMA_GROUP):
            j = g * DMA_GROUP + jj
            src = u_ref.at[_row_tile(j)]
            for k in range(TOP_K):
                _row_copy(src, xout_hbm.at[_row_tile(dest_ref[k, j])], sem).start()
        return carry

    lax.fori_loop(0, tt // DMA_GROUP, issue, 0)
    for k in range(TOP_K):
        _row_copy(u_ref, xout_hbm.at[pl.ds(0, tt * ROW_TILES)], sem).wait()


def _dispatch(dest, u2, xinit, tt):
    t = u2.shape[0] // ROW_TILES
    return pl.pallas_call(
        functools.partial(_dispatch_kernel, tt=tt),
        out_shape=jax.ShapeDtypeStruct(xinit.shape, F32),
        grid=(t // tt,),
        in_specs=[pl.BlockSpec((TOP_K, tt), lambda i: (0, i), memory_space=pltpu.SMEM),
                  pl.BlockSpec((tt * ROW_TILES, LANES), lambda i: (i, 0)), pl.BlockSpec(memory_space=pl.ANY)],
        out_specs=pl.BlockSpec(memory_space=pl.ANY),
        scratch_shapes=[pltpu.SemaphoreType.DMA],
        input_output_aliases={2: 0},
        compiler_params=_cparams(("arbitrary",)),
        name="moe_dispatch",
    )(dest, u2, xinit)


def _ffn_kernel(be_ref, nu_ref, x_ref, wg_ref, wu_ref, wd_ref, y_ref, wg_s, wu_s, wd_s, x_s):
    b = pl.program_id(0)

    @pl.when((b == 0) | (be_ref[b] != be_ref[jnp.maximum(b - 1, 0)]))
    def _():
        wg_s[...] = wg_ref[...].astype(BF16)
        wu_s[...] = wu_ref[...].astype(BF16)
        wd_s[...] = wd_ref[...].astype(BF16)

    @pl.when(b < nu_ref[0])
    def _():
        for c in range(ROW_TILES):
            x_s[:, c * LANES:(c + 1) * LANES] = x_ref[pl.ds(c, MOE_BM, stride=ROW_TILES), :].astype(BF16)
        x = x_s[...]
        a = _dot(x, wg_s[...])
        u = _dot(x, wu_s[...])
        y = _dot(((a * _sigmoid(a)) * u).astype(BF16), wd_s[...])
        for c in range(ROW_TILES):
            y_ref[pl.ds(c, MOE_BM, stride=ROW_TILES), :] = y[:, c * LANES:(c + 1) * LANES]

    @pl.when(b >= nu_ref[0])
    def _():
        y_ref[...] = jnp.zeros(y_ref.shape, y_ref.dtype)


def _ffn(block_expert, n_used, xbuf, wg, wu, wd, layer):
    n_blocks = xbuf.shape[0] // (MOE_BM * ROW_TILES)
    rows = lambda b, be, nu: (b, 0)
    wmap = lambda b, be, nu: (layer, be[b], 0, 0)
    return pl.pallas_call(
        _ffn_kernel,
        out_shape=jax.ShapeDtypeStruct(xbuf.shape, F32),
        grid_spec=pltpu.PrefetchScalarGridSpec(
            num_scalar_prefetch=2,
            grid=(n_blocks,),
            in_specs=[pl.BlockSpec((MOE_BM * ROW_TILES, LANES), rows),
                      pl.BlockSpec((None, None, D_MODEL, D_FF), wmap),
                      pl.BlockSpec((None, None, D_MODEL, D_FF), wmap),
                      pl.BlockSpec((None, None, D_FF, D_MODEL), wmap)],
            out_specs=pl.BlockSpec((MOE_BM * ROW_TILES, LANES), rows),
            scratch_shapes=[pltpu.VMEM((D_MODEL, D_FF), BF16), pltpu.VMEM((D_MODEL, D_FF), BF16),
                            pltpu.VMEM((D_FF, D_MODEL), BF16), pltpu.VMEM((MOE_BM, D_MODEL), BF16)]),
        compiler_params=_cparams(("arbitrary",)),
        name="moe_ffn",
    )(block_expert, n_used, xbuf, wg, wu, wd)


def _combine_kernel(dcur_ref, dnxt_ref, y_hbm, h_ref, rt_ref, g_ref, o_ref, ybuf, sems, *, tt, final_norm):
    i = pl.program_id(0)
    last = pl.num_programs(0) - 1
    slot = i % 2

    def gather(dref, s, j):
        for k in range(TOP_K):
            _row_copy(y_hbm.at[_row_tile(dref[k, j])], ybuf.at[s, k, _row_tile(j)], sems.at[s]).start()

    def wait_slot(s):
        for k in range(TOP_K):
            _row_copy(y_hbm.at[pl.ds(0, tt * ROW_TILES)], ybuf.at[s, k], sems.at[s]).wait()

    @pl.when(i == 0)
    def _():
        def first(g, carry):
            for jj in range(DMA_GROUP):
                gather(dcur_ref, 0, g * DMA_GROUP + jj)
            return carry
        lax.fori_loop(0, tt // DMA_GROUP, first, 0)

    wait_slot(slot)

    def group(g, carry):
        r0 = pl.multiple_of(g * DMA_GROUP, DMA_GROUP)
        for jj in range(DMA_GROUP):
            gather(dnxt_ref, 1 - slot, r0 + jj)
        rows = pl.ds(r0, DMA_GROUP)
        rt = rt_ref[rows, :]
        w0, w1 = rt[:, 2:3], rt[:, 3:4]
        for c in range(ROW_TILES):
            lanes = slice(c * LANES, (c + 1) * LANES)
            tile_rows = pl.ds(r0 * ROW_TILES + c, DMA_GROUP, stride=ROW_TILES)
            o_ref[rows, lanes] = h_ref[rows, lanes] + (ybuf[slot, 0, tile_rows, :] * w0 + ybuf[slot, 1, tile_rows, :] * w1)
        return carry

    lax.fori_loop(0, tt // DMA_GROUP, group, 0)
    if final_norm:
        o_ref[...] = _rms(o_ref[...], g_ref[...])

    @pl.when(i == last)
    def _():
        wait_slot(1 - slot)


def _combine(dest, ybuf, h1, rt, g, tt, final_norm):
    t = h1.shape[0]
    nsteps = t // tt
    return pl.pallas_call(
        functools.partial(_combine_kernel, tt=tt, final_norm=final_norm),
        out_shape=jax.ShapeDtypeStruct((t, D_MODEL), F32),
        grid=(nsteps,),
        in_specs=[pl.BlockSpec((TOP_K, tt), lambda i: (0, i), memory_space=pltpu.SMEM),
                  pl.BlockSpec((TOP_K, tt), lambda i: (0, jnp.minimum(i + 1, nsteps - 1)), memory_space=pltpu.SMEM),
                  pl.BlockSpec(memory_space=pl.ANY),
                  pl.BlockSpec((tt, D_MODEL), lambda i: (i, 0)),
                  pl.BlockSpec((tt, LANES), lambda i: (i, 0)),
                  pl.BlockSpec((1, D_MODEL), lambda i: (0, 0))],
        out_specs=pl.BlockSpec((tt, D_MODEL), lambda i: (i, 0)),
        scratch_shapes=[pltpu.VMEM((2, TOP_K, tt * ROW_TILES, LANES), F32), pltpu.SemaphoreType.DMA((2,))],
        compiler_params=_cparams(("arbitrary",)),
        name="moe_combine",
    )(dest, dest, ybuf, h1, rt, g)


def _moe_rows(t):
    return ((t * TOP_K) // MOE_BM + N_EXPERTS) * MOE_BM


def _moe(h1, u2, rt, wg, wu, wd, layer, g_final, final_norm, xinit):
    rk, cnt = _rank(rt, 1024)
    counts = cnt[0, :N_EXPERTS].astype(jnp.int32)
    padded = (counts + MOE_BM - 1) // MOE_BM * MOE_BM
    pad_end = jnp.cumsum(padded)
    pad_start = pad_end - padded
    n_blocks = xinit.shape[0] // (MOE_BM * ROW_TILES)
    is_expert = rk[TOP_K:2 * TOP_K][None] == jnp.arange(N_EXPERTS, dtype=jnp.int32)[:, None, None]
    dest = jnp.sum(jnp.where(is_expert, pad_start[:, None, None], 0), axis=0) + rk[0:TOP_K]
    n_used = (pad_end[-1] // MOE_BM).astype(jnp.int32)
    blk = jnp.minimum(jnp.arange(n_blocks, dtype=jnp.int32), n_used - 1) * MOE_BM
    block_expert = jnp.minimum(jnp.sum(pad_end[None, :] <= blk[:, None], axis=1), N_EXPERTS - 1).astype(jnp.int32)
    xbuf = _dispatch(dest, u2, xinit, 256)
    ybuf = _ffn(block_expert, n_used.reshape(1), xbuf, wg, wu, wd, layer)
    return _combine(dest, ybuf, h1, rt, g_final, 256, final_norm), xbuf


def _qkv_kernel(x_ref, g_ref, w_ref, b_ref, pos_ref, rot_ref, q_ref, kv_ref):
    u = _rms(x_ref[...], g_ref[...]).astype(BF16)
    ang = pos_ref[...].astype(F32) * rot_ref[0:1, :]
    cosv = jnp.cos(ang)
    sinv = jnp.sin(ang)
    c_coef = jnp.where(rot_ref[3:4, :] > 0.0, cosv, 1.0)
    s_lo = -sinv * rot_ref[1:2, :]
    s_hi = sinv * rot_ref[2:3, :]

    def rotate(z):
        return z * c_coef + pltpu.roll(z, LANES - ROT_DIM // 2, 1) * s_lo + pltpu.roll(z, ROT_DIM // 2, 1) * s_hi

    scale = SWA_HEAD_DIM ** -0.5
    for j in range(Q_W // LANES):
        sl = slice(j * LANES, (j + 1) * LANES)
        q_ref[:, sl] = (rotate(_dot(u, w_ref[:, sl]) + b_ref[:, sl]) * scale).astype(q_ref.dtype)
    for j in range(KV_OUT_W // LANES):
        sl = slice(Q_W + j * LANES, Q_W + (j + 1) * LANES)
        z = _dot(u, w_ref[:, sl]) + b_ref[:, sl]
        kv_ref[:, j * LANES:(j + 1) * LANES] = (rotate(z) if j % 2 == 0 else z).astype(kv_ref.dtype)


def _rot_table():
    lane = np.arange(LANES)
    d = lane % SWA_HEAD_DIM
    inv_freq = ROPE_THETA ** (-jnp.arange(0, ROT_DIM, 2, dtype=F32) / ROT_DIM)
    half = ROT_DIM // 2
    tab = jnp.zeros((8, LANES), F32)
    tab = tab.at[0].set(jnp.where(d < ROT_DIM, inv_freq[d % half], 0.0))
    tab = tab.at[1].set((d < half).astype(np.float32))
    tab = tab.at[2].set(((d >= half) & (d < ROT_DIM)).astype(np.float32))
    tab = tab.at[3].set((d < ROT_DIM).astype(np.float32))
    return tab


def _qkv(h, g, w, b, pos, tm):
    t = h.shape[0]
    row = lambda i: (i, 0)
    fixed = lambda i: (0, 0)
    wtot = Q_W + KV_OUT_W
    return pl.pallas_call(
        _qkv_kernel,
        out_shape=(jax.ShapeDtypeStruct((t, Q_W), BF16), jax.ShapeDtypeStruct((t, KV_OUT_W), BF16)),
        grid=(t // tm,),
        in_specs=[pl.BlockSpec((tm, D_MODEL), row), pl.BlockSpec((1, D_MODEL), fixed),
                  pl.BlockSpec((D_MODEL, wtot), fixed), pl.BlockSpec((1, wtot), fixed),
                  pl.BlockSpec((tm, 1), row), pl.BlockSpec((8, LANES), fixed)],
        out_specs=(pl.BlockSpec((tm, Q_W), row), pl.BlockSpec((tm, KV_OUT_W), row)),
        compiler_params=_cparams(("parallel",)),
        name="qkv_rotary",
    )(h, g, w, b, pos, _rot_table())


def _swa_kernel(q_ref, kvc_ref, kvp_ref, sink_ref, o_ref):
    neg = jnp.where(pl.program_id(1) > 0, 0.0, -jnp.inf).astype(F32)
    kj = lax.broadcasted_iota(jnp.int32, (WINDOW, WINDOW), 0)
    qi = lax.broadcasted_iota(jnp.int32, (WINDOW, WINDOW), 1)
    mask_cur = kj <= qi
    top = kj < SWA_HEAD_DIM
    lane = lax.broadcasted_iota(jnp.int32, (1, LANES), 1)
    keep_lo = jnp.where(lane < SWA_HEAD_DIM, 1.0, 0.0).astype(BF16)
    keep_hi = jnp.where(lane < SWA_HEAD_DIM, 0.0, 1.0).astype(BF16)
    zeros_half = jnp.zeros((SWA_HEAD_DIM, WINDOW), F32)

    def arranged(ref):
        k_nat, k_swp = ref[:, 0:KV_W], ref[:, 2 * KV_W:3 * KV_W]
        vt = jnp.transpose(ref[:, KV_W:2 * KV_W].astype(F32))
        vt_top = lambda g: jnp.concatenate([vt[g * SWA_HEAD_DIM:(g + 1) * SWA_HEAD_DIM], zeros_half], axis=0).astype(BF16)
        vt_bot = lambda g: jnp.concatenate([zeros_half, vt[g * SWA_HEAD_DIM:(g + 1) * SWA_HEAD_DIM]], axis=0).astype(BF16)
        return {(0, 0): (k_nat * keep_lo, vt_top(0)), (0, 1): (k_swp * keep_hi, vt_bot(0)),
                (1, 0): (k_swp * keep_lo, vt_top(1)), (1, 1): (k_nat * keep_hi, vt_bot(1))}

    cur = arranged(kvc_ref)
    prev = arranged(kvp_ref)
    heads = range(SWA_Q_HEADS)
    key = lambda h: (h // SWA_GROUP, h % 2)
    scores = []
    for h in heads:
        qp = q_ref[:, (h // 2) * LANES:(h // 2 + 1) * LANES]
        sc = _dot_nt(cur[key(h)][0], qp)
        sp = _dot_nt(prev[key(h)][0], qp)
        scores.append(jnp.where(mask_cur, sc, sp + neg))
    sinks = [sink_ref[:, h:h + 1] for h in heads]
    maxes = [jnp.maximum(jnp.max(scores[h], axis=0, keepdims=True), sinks[h]) for h in heads]
    probs = [jnp.exp(scores[h] - maxes[h]) for h in heads]
    denoms = [jnp.sum(probs[h], axis=0, keepdims=True) + jnp.exp(sinks[h] - maxes[h]) for h in heads]
    for pr in range(SWA_Q_HEADS // 2):
        num = None
        for h in (2 * pr, 2 * pr + 1):
            pc = jnp.where(mask_cur, probs[h], 0.0).astype(BF16)
            pp = jnp.where(mask_cur, 0.0, probs[h]).astype(BF16)
            part = _dot(cur[key(h)][1], pc) + _dot(prev[key(h)][1], pp)
            num = part if num is None else num + part
        den = jnp.where(top, denoms[2 * pr], denoms[2 * pr + 1])
        o_ref[:, pr * LANES:(pr + 1) * LANES] = jnp.transpose(num / den).astype(o_ref.dtype)


def _swa(q, kv, sinks, bsz, seq):
    nb = seq // WINDOW
    return pl.pallas_call(
        _swa_kernel,
        out_shape=jax.ShapeDtypeStruct((bsz * seq, Q_W), BF16),
        grid=(bsz, nb),
        in_specs=[pl.BlockSpec((WINDOW, Q_W), lambda b, n: (b * nb + n, 0)),
                  pl.BlockSpec((WINDOW, KV_OUT_W), lambda b, n: (b * nb + n, 0)),
                  pl.BlockSpec((WINDOW, KV_OUT_W), lambda b, n: (b * nb + jnp.maximum(n - 1, 0), 0)),
                  pl.BlockSpec((1, LANES), lambda b, n: (0, 0))],
        out_specs=pl.BlockSpec((WINDOW, Q_W), lambda b, n: (b * nb + n, 0)),
        compiler_params=_cparams(("parallel", "parallel")),
        name="swa_attention",
    )(q, kv, kv, sinks)


def _pack_in_proj(w_in):
    offs = np.cumsum((0,) + AB_SPLITS)
    part = lambda i: w_in[:, offs[i]:offs[i + 1]]
    w_main = jnp.concatenate([part(0), part(1), part(2), part(5), part(6), part(7), part(8)], axis=1).astype(BF16)
    small = jnp.concatenate([part(3), part(4), part(9)], axis=1)
    w_small = jnp.pad(small, ((0, 0), (0, LANES - small.shape[1]))).astype(BF16)
    return w_main, w_small


def _pack_qkv(w, b):
    hd = SWA_HEAD_DIM
    k0, k1 = slice(Q_W, Q_W + hd), slice(Q_W + hd, Q_W + 2 * hd)
    cols = lambda a: jnp.concatenate([a[..., :Q_W + 2 * KV_W], a[..., k1], a[..., k0]], axis=-1)
    return cols(w).astype(BF16), cols(b).reshape(1, -1).astype(F32)


def _lane_row(v, offset=0):
    return jnp.zeros((1, LANES), F32).at[0, offset:offset + v.shape[0]].set(v.astype(F32))


def kernel(x, positions, mix_norm, ffn_norm, final_norm, ab_w_in, ab_conv_w, ab_conv_b, ml_igate_b, ml_fgate_b, ml_head_norm, gla_w_lr_up, gla_gate_b, gla_head_norm, ab_w_out, swa_w_qkv, swa_b_qkv, swa_sinks, swa_w_o, swa_b_o, router_group_w, router_group_b, router_expert_w, router_expert_b, expert_w_gate, expert_w_up, expert_w_down):
    bsz, seq, d = x.shape
    t = bsz * seq
    h = x.reshape(t, d)
    row = lambda v: v.reshape(1, -1).astype(F32)

    def router_params(layer):
        wr = jnp.zeros((d, LANES), F32).at[:, :N_GROUPS].set(router_group_w[layer])
        wr = wr.at[:, N_GROUPS:N_GROUPS + N_EXPERTS].set(router_expert_w[layer])
        br = jnp.zeros((1, LANES), F32).at[0, :N_GROUPS].set(router_group_b[layer])
        br = br.at[0, N_GROUPS:N_GROUPS + N_EXPERTS].set(router_expert_b[layer])
        w_hi = wr.astype(BF16)
        w_lo = (wr - w_hi.astype(F32)).astype(BF16)
        return jnp.concatenate([w_hi, w_lo], axis=1), br

    def experts(layer):
        return expert_w_gate, expert_w_up, expert_w_down, layer

    w_main, w_small = _pack_in_proj(ab_w_in[0])
    zm, zs = _in_proj(h, row(mix_norm[0]), w_main, w_small, 512)
    lrup = jnp.zeros((LANES, GLA_QK_W), F32).at[SM_LR:SM_LR + GLA_LOWRANK].set(gla_w_lr_up[0]).astype(BF16)
    y = _mixer(zm, zs, ab_conv_w[0], row(ab_conv_b[0]), _lane_row(ml_igate_b[0]), _lane_row(ml_fgate_b[0]), lrup,
               row(gla_gate_b[0]), row(ml_head_norm[0]), row(gla_head_norm[0]), bsz, seq, 256)
    wr, br = router_params(0)
    h1, u2, rt = _proj_router(y, ab_w_out[0].astype(BF16), jnp.zeros((1, d), F32), h, row(ffn_norm[0]), wr, br, 512)
    h, xbuf = _moe(h1, u2, rt, *experts(0), row(final_norm), False, jnp.zeros((_moe_rows(t) * ROW_TILES, LANES), F32))

    w_qkv, b_qkv = _pack_qkv(swa_w_qkv[0], swa_b_qkv[0])
    q, kv = _qkv(h, row(mix_norm[1]), w_qkv, b_qkv, positions.reshape(t, 1), 512)
    o = _swa(q, kv, _lane_row(swa_sinks[0]), bsz, seq)
    wr, br = router_params(1)
    h1, u2, rt = _proj_router(o, swa_w_o[0].astype(BF16), row(swa_b_o[0]), h, row(ffn_norm[1]), wr, br, 512)
    out, _ = _moe(h1, u2, rt, *experts(1), row(final_norm), True, xbuf)
    return out.reshape(bsz, seq, d)
```

```python
import functools

import jax
import jax.numpy as jnp
import numpy as np
from jax import lax
from jax.experimental import pallas as pl
from jax.experimental.pallas import tpu as pltpu

F32 = jnp.float32
BF16 = jnp.bfloat16
HIGHEST = lax.Precision.HIGHEST

D_MODEL = 1024
EPS = 1e-6
ML_HEADS = 4
ML_DK = 64
ML_DV = 128
GLA_HEADS = 4
GLA_DK = 64
GLA_DV = 128
CHUNK = 64
CONV_K = 4
GLA_LOWRANK = 16
GLA_TAU = 16.0
ML_QK_W = ML_HEADS * ML_DK
ML_V_W = ML_HEADS * ML_DV
GLA_QK_W = GLA_HEADS * GLA_DK
GLA_V_W = GLA_HEADS * GLA_DV
AB_SPLITS = (2 * ML_QK_W, ML_V_W, ML_V_W, ML_HEADS, ML_HEADS, GLA_QK_W, GLA_QK_W, GLA_V_W, GLA_V_W, GLA_LOWRANK)
OFF_MQK = 0
OFF_MV = OFF_MQK + 2 * ML_QK_W
OFF_MO = OFF_MV + ML_V_W
OFF_GQ = OFF_MO + ML_V_W
OFF_GK = OFF_GQ + GLA_QK_W
OFF_GV = OFF_GK + GLA_QK_W
OFF_GG = OFF_GV + GLA_V_W
Z_MAIN_W = OFF_GG + GLA_V_W
SM_I = 0
SM_F = ML_HEADS
SM_LR = 2 * ML_HEADS
LANES = 128
SWA_Q_HEADS = 16
SWA_KV_HEADS = 2
SWA_HEAD_DIM = 64
SWA_GROUP = SWA_Q_HEADS // SWA_KV_HEADS
WINDOW = 128
ROT_DIM = SWA_HEAD_DIM // 4
ROPE_THETA = 500000.0
Q_W = SWA_Q_HEADS * SWA_HEAD_DIM
KV_W = SWA_KV_HEADS * SWA_HEAD_DIM
KV_OUT_W = 3 * KV_W
N_GROUPS = 4
EXPERTS_PER_GROUP = 8
N_EXPERTS = N_GROUPS * EXPERTS_PER_GROUP
TOP_K = 2
D_FF = 512
MOE_BM = 256
ROW_TILES = D_MODEL // LANES

VMEM_LIMIT = 56 * 1024 * 1024


def _cparams(sem):
    return pltpu.CompilerParams(dimension_semantics=sem, vmem_limit_bytes=VMEM_LIMIT)


def _rms(x, g):
    return x * lax.rsqrt(jnp.mean(x * x, axis=-1, keepdims=True) + EPS) * g


def _log_sigmoid(x):
    return jnp.minimum(x, 0.0) - jnp.log1p(jnp.exp(-jnp.abs(x)))


def _sigmoid(x):
    return 1.0 / (1.0 + jnp.exp(-x))


def _dot(a, b):
    return jnp.dot(a, b, preferred_element_type=F32)


def _dot_nt(a, b):
    return lax.dot_general(a, b, (((1,), (1,)), ((), ())), preferred_element_type=F32)


def _dot_tn(a, b):
    return lax.dot_general(a, b, (((0,), (0,)), ((), ())), preferred_element_type=F32)


def _in_proj_kernel(x_ref, g_ref, w_ref, ws_ref, zm_ref, zs_ref, *, n_chunk):
    u = _rms(x_ref[...], g_ref[...]).astype(BF16)
    for n0 in range(0, w_ref.shape[1], n_chunk):
        zm_ref[:, n0:n0 + n_chunk] = _dot(u, w_ref[:, n0:n0 + n_chunk]).astype(zm_ref.dtype)
    zs_ref[...] = _dot(u, ws_ref[...])


def _in_proj(h, g, w_main, w_small, tm):
    t = h.shape[0]
    return pl.pallas_call(
        functools.partial(_in_proj_kernel, n_chunk=768),
        out_shape=(jax.ShapeDtypeStruct((t, Z_MAIN_W), BF16), jax.ShapeDtypeStruct((t, LANES), F32)),
        grid=(t // tm,),
        in_specs=[pl.BlockSpec((tm, D_MODEL), lambda i: (i, 0)),
                  pl.BlockSpec((1, D_MODEL), lambda i: (0, 0)),
                  pl.BlockSpec((D_MODEL, Z_MAIN_W), lambda i: (0, 0)),
                  pl.BlockSpec((D_MODEL, LANES), lambda i: (0, 0))],
        out_specs=(pl.BlockSpec((tm, Z_MAIN_W), lambda i: (i, 0)),
                   pl.BlockSpec((tm, LANES), lambda i: (i, 0))),
        compiler_params=_cparams(("parallel",)),
        name="in_proj",
    )(h, g, w_main, w_small)


def _mixer_kernel(zm_ref, zs_ref, convw_ref, convb_ref, igb_ref, fgb_ref, lrup_ref, gateb_ref, mlnorm_ref,
                  glanorm_ref, y_ref, xpad, q_s, k_s, ig_s, lf_s, la_s, hml_s, hgla_s, c_s, n_s, m_s, st_s, *, tb):
    @pl.when(pl.program_id(1) == 0)
    def _():
        xpad[0:8, :] = jnp.zeros((8, 2 * ML_QK_W), F32)
        c_s[...] = jnp.zeros(c_s.shape, F32)
        n_s[...] = jnp.zeros(n_s.shape, F32)
        st_s[...] = jnp.zeros(st_s.shape, F32)
        m_s[...] = jnp.full(m_s.shape, -jnp.inf, F32)

    xpad[8:8 + tb, :] = zm_ref[:, OFF_MQK:OFF_MQK + 2 * ML_QK_W].astype(F32)
    conv = convb_ref[...] + convw_ref[3:4, :] * xpad[8:8 + tb, :]
    for j in range(CONV_K - 1):
        conv = conv + convw_ref[j:j + 1, :] * xpad[5 + j:5 + j + tb, :]
    xpad[0:8, :] = xpad[tb:tb + 8, :]
    qk = conv * _sigmoid(conv)
    q_s[...] = qk[:, :ML_QK_W]
    k_s[...] = qk[:, ML_QK_W:] * (ML_DK ** -0.5)

    zs = zs_ref[...]
    ig_s[...] = zs + igb_ref[...]
    lf_s[...] = _log_sigmoid(pltpu.roll(zs, LANES - SM_F, 1) + fgb_ref[...])
    la_s[...] = _log_sigmoid(_dot(zs.astype(BF16), lrup_ref[...]) + gateb_ref[...]) * (1.0 / GLA_TAU)

    row = lax.broadcasted_iota(jnp.int32, (CHUNK, CHUNK), 0)
    col = lax.broadcasted_iota(jnp.int32, (CHUNK, CHUNK), 1)
    causal = col <= row
    tri = causal.astype(F32)

    def chunk(c, carry):
        rows = pl.ds(pl.multiple_of(c * CHUNK, CHUNK), CHUNK)
        ig = ig_s[rows, :]
        bc = jnp.dot(tri, lf_s[rows, :], precision=HIGHEST, preferred_element_type=F32)
        g_row = bc[CHUNK - 1:CHUNK, :]
        a = g_row - bc + ig
        a_max = jnp.max(a, axis=0, keepdims=True)
        wa = jnp.exp(a - a_max)
        m_row = m_s[...]
        m_new = jnp.maximum(g_row + m_row, a_max)
        s_old = jnp.exp(g_row + m_row - m_new)
        s_in = jnp.exp(a_max - m_new)
        il_all = bc + m_row
        e = ig - bc
        et = jnp.transpose(jnp.concatenate([e, e], axis=0))
        q = q_s[rows, :]
        k = k_s[rows, :]
        for h in range(ML_HEADS):
            qh = q[:, h * ML_DK:(h + 1) * ML_DK]
            kh = k[:, h * ML_DK:(h + 1) * ML_DK]
            vh = zm_ref[rows, OFF_MV + h * ML_DV:OFF_MV + (h + 1) * ML_DV]
            kw = kh * wa[:, h:h + 1]
            c_chunk = _dot_tn(kw.astype(BF16), vh)
            n_chunk = jnp.sum(kw, axis=0, keepdims=True)
            d_log = jnp.where(causal, bc[:, h:h + 1] + et[h:h + 1, 0:CHUNK], -jnp.inf)
            il = il_all[:, h:h + 1]
            m_t = jnp.maximum(il, jnp.max(d_log, axis=-1, keepdims=True))
            s_inter = jnp.exp(il - m_t)
            qhb = qh.astype(BF16)
            qkm = _dot_nt(qhb, kh.astype(BF16)) * jnp.exp(d_log - m_t)
            c_st = c_s[h]
            n_st = n_s[h]
            num = s_inter * _dot(qhb, c_st.astype(BF16)) + _dot(qkm.astype(BF16), vh)
            den = s_inter * jnp.sum(qh * n_st, axis=-1, keepdims=True) + jnp.sum(qkm, axis=-1, keepdims=True)
            hml_s[rows, h * ML_DV:(h + 1) * ML_DV] = num / jnp.maximum(jnp.abs(den), jnp.exp(-m_t))
            c_s[h] = s_old[:, h:h + 1] * c_st + s_in[:, h:h + 1] * c_chunk
            n_s[h] = s_old[:, h:h + 1] * n_st + s_in[:, h:h + 1] * n_chunk
        m_s[...] = m_new
        bcg = jnp.dot(tri, la_s[rows, :], precision=HIGHEST, preferred_element_type=F32)
        gg_row = bcg[CHUNK - 1:CHUNK, :]
        gq = zm_ref[rows, OFF_GQ:OFF_GQ + GLA_QK_W].astype(F32)
        gk = zm_ref[rows, OFF_GK:OFF_GK + GLA_QK_W].astype(F32) * (GLA_DK ** -0.5)
        q_dec = (gq * jnp.exp(bcg)).astype(BF16)
        k_inv = (gk * jnp.exp(-bcg)).astype(BF16)
        k_end = (gk * jnp.exp(gg_row - bcg)).astype(BF16)
        eg = jnp.exp(gg_row)
        for h in range(GLA_HEADS):
            sl = slice(h * GLA_DK, (h + 1) * GLA_DK)
            vh = zm_ref[rows, OFF_GV + h * GLA_DV:OFF_GV + (h + 1) * GLA_DV]
            att = jnp.where(causal, _dot_nt(q_dec[:, sl], k_inv[:, sl]), 0.0)
            st = st_s[h]
            hgla_s[rows, h * GLA_DV:(h + 1) * GLA_DV] = _dot_nt(q_dec[:, sl], st.astype(BF16)) + _dot(att.astype(BF16), vh)
            st_s[h] = st * eg[:, sl] + _dot_tn(vh, k_end[:, sl])
        return carry

    lax.fori_loop(0, tb // CHUNK, chunk, 0)

    for h in range(ML_HEADS):
        sl = slice(h * ML_DV, (h + 1) * ML_DV)
        hh = hml_s[:, sl]
        d = hh - jnp.mean(hh, axis=-1, keepdims=True)
        hn = d * lax.rsqrt(jnp.mean(d * d, axis=-1, keepdims=True) + EPS)
        og = zm_ref[:, OFF_MO + h * ML_DV:OFF_MO + (h + 1) * ML_DV].astype(F32)
        y_ref[:, sl] = (hn * mlnorm_ref[:, sl] * _sigmoid(og)).astype(y_ref.dtype)
    for h in range(GLA_HEADS):
        sl = slice(h * GLA_DV, (h + 1) * GLA_DV)
        o = hgla_s[:, sl]
        on = o * lax.rsqrt(jnp.mean(o * o, axis=-1, keepdims=True) + EPS)
        gg = zm_ref[:, OFF_GG + h * GLA_DV:OFF_GG + (h + 1) * GLA_DV].astype(F32)
        y_ref[:, ML_V_W + h * GLA_DV:ML_V_W + (h + 1) * GLA_DV] = (on * glanorm_ref[:, sl] * (gg * _sigmoid(gg))).astype(y_ref.dtype)


def _split_terms(x, n):
    terms = []
    for _ in range(n):
        t = x.astype(BF16)
        terms.append(t)
        x = x - t.astype(F32)
    return terms


def _dot01_left(m01, x, n):
    return sum(_dot(m01, t) for t in _split_terms(x, n))


def _dot01_right(x, m01, n):
    return sum(_dot(t, m01) for t in _split_terms(x, n))


def _mixer2_kernel(zm_ref, zs_ref, convw_ref, convb_ref, igb_ref, fgb_ref, lrup_ref, gateb_ref, mlnorm_ref,
                   glanorm_ref, y_ref, xpad, q_s, k_s, e_pad, hml_s, hgla_s, cn_s, m_s, st_s, *, tb):
    nc = tb // CHUNK

    @pl.when(pl.program_id(1) == 0)
    def _():
        xpad[0:8, :] = jnp.zeros((8, 2 * ML_QK_W), F32)
        cn_s[...] = jnp.zeros(cn_s.shape, F32)
        st_s[...] = jnp.zeros(st_s.shape, F32)
        m_s[...] = jnp.full(m_s.shape, -jnp.inf, F32)

    xpad[8:8 + tb, :] = zm_ref[:, OFF_MQK:OFF_MQK + 2 * ML_QK_W].astype(F32)
    conv = convb_ref[...] + convw_ref[3:4, :] * xpad[8:8 + tb, :]
    for j in range(CONV_K - 1):
        conv = conv + convw_ref[j:j + 1, :] * xpad[5 + j:5 + j + tb, :]
    xpad[0:8, :] = xpad[tb:tb + 8, :]
    qk = conv * _sigmoid(conv)
    q_s[...] = qk[:, :ML_QK_W]
    k_s[...] = qk[:, ML_QK_W:] * (ML_DK ** -0.5)

    r_t = lax.broadcasted_iota(jnp.int32, (tb, tb), 0)
    c_t = lax.broadcasted_iota(jnp.int32, (tb, tb), 1)
    tri_blk = jnp.where((r_t // CHUNK == c_t // CHUNK) & (c_t <= r_t), 1.0, 0.0).astype(BF16)
    r_e = lax.broadcasted_iota(jnp.int32, (LANES, 2 * LANES), 0)
    c_e = lax.broadcasted_iota(jnp.int32, (LANES, 2 * LANES), 1)
    spread_dk = jnp.where(c_e // ML_DK == r_e, 1.0, 0.0).astype(BF16)
    mean_dv = jnp.full((ML_DV, ML_DV), 1.0 / ML_DV, F32).astype(BF16)
    row_c = lax.broadcasted_iota(jnp.int32, (CHUNK, CHUNK), 0)
    col_c = lax.broadcasted_iota(jnp.int32, (CHUNK, CHUNK), 1)
    causal = col_c <= row_c
    lane_c = lax.broadcasted_iota(jnp.int32, (CHUNK, LANES), 1)
    ones_dv = jnp.ones((CHUNK, ML_DV), BF16)
    chunk_rows = lambda c: slice(c * CHUNK, (c + 1) * CHUNK)

    zs = zs_ref[...]
    ig = zs + igb_ref[...]
    lf = _log_sigmoid(pltpu.roll(zs, LANES - SM_F, 1) + fgb_ref[...])
    bc = _dot01_left(tri_blk, lf, 3)
    bc3 = bc.reshape(nc, CHUNK, LANES)
    g3 = bc3[:, CHUNK - 1:CHUNK, :]
    a3 = g3 - bc3 + ig.reshape(nc, CHUNK, LANES)
    amax3 = jnp.max(a3, axis=1, keepdims=True)
    wa = jnp.exp(a3 - amax3).reshape(tb, LANES)
    m_run = m_s[...]
    m_prev, s_old, s_in = [], [], []
    for c in range(nc):
        m_new = jnp.maximum(g3[c] + m_run, amax3[c])
        m_prev.append(m_run)
        s_old.append(jnp.exp(g3[c] + m_run - m_new))
        s_in.append(jnp.exp(amax3[c] - m_new))
        m_run = m_new
    m_s[...] = m_run
    e_nat = ig - bc
    e_pad[0:CHUNK, :] = jnp.full((CHUNK, LANES), -jnp.inf, F32)
    e_pad[CHUNK:CHUNK + tb, :] = e_nat
    pos = lax.broadcasted_iota(jnp.int32, (tb, LANES), 0) % CHUNK
    shift = 1
    while shift < CHUNK:
        shifted = e_pad[CHUNK - shift:CHUNK - shift + tb, :]
        e_pad[CHUNK:CHUNK + tb, :] = jnp.maximum(e_pad[CHUNK:CHUNK + tb, :], jnp.where(pos >= shift, shifted, -jnp.inf))
        shift *= 2
    m_intra = bc + e_pad[CHUNK:CHUNK + tb, :]
    il = jnp.concatenate([bc3[c] + m_prev[c] for c in range(nc)], axis=0)
    mt = jnp.maximum(il, m_intra)
    s_inter = jnp.exp(il - mt)
    exp_neg = jnp.exp(-mt)
    xn = bc - mt
    e_up = pltpu.roll(e_nat, ML_HEADS, 1)
    wa_x = _dot01_right(wa, spread_dk, 2)
    si_x = _dot01_right(s_inter, spread_dk, 2)
    q_all = q_s[...]
    k_all = k_s[...]
    kw_b = (k_all * wa_x).astype(BF16)
    qs_b = (q_all * si_x).astype(BF16)
    q_b = q_all.astype(BF16)
    k_b = k_all.astype(BF16)

    qkm, upd, vo = {}, {}, {}
    for c in range(nc):
        rows = chunk_rows(c)
        for h in range(ML_HEADS):
            dk = slice(h * ML_DK, (h + 1) * ML_DK)
            x_t = jnp.where(lane_c == h, xn[rows], jnp.where(lane_c == h + ML_HEADS, 1.0, 0.0))
            y_t = jnp.where(lane_c == h, 1.0, jnp.where(lane_c == h + ML_HEADS, e_up[rows], 0.0))
            d = lax.dot_general(x_t, y_t, (((1,), (1,)), ((), ())), precision=HIGHEST, preferred_element_type=F32)
            p = jnp.where(causal, jnp.exp(d), 0.0)
            qkm[c, h] = (_dot_nt(q_b[rows, dk], k_b[rows, dk]) * p).astype(BF16)
            vo[c, h] = jnp.concatenate([zm_ref[rows, OFF_MV + h * ML_DV:OFF_MV + (h + 1) * ML_DV], ones_dv], axis=1)
            upd[c, h] = _dot_tn(kw_b[rows, dk], vo[c, h])
    for h in range(ML_HEADS):
        dk = slice(h * ML_DK, (h + 1) * ML_DK)
        dv = slice(h * ML_DV, (h + 1) * ML_DV)
        cn = cn_s[h]
        for c in range(nc):
            rows = chunk_rows(c)
            res = _dot(qs_b[rows, dk], cn.astype(BF16)) + _dot(qkm[c, h], vo[c, h])
            num, den = res[:, :ML_DV], res[:, ML_DV:]
            hml_s[rows, dv] = num / jnp.maximum(jnp.abs(den), exp_neg[rows, h:h + 1])
            cn = s_old[c][:, h:h + 1] * cn + s_in[c][:, h:h + 1] * upd[c, h]
        cn_s[h] = cn

    la = _log_sigmoid(_dot(zs.astype(BF16), lrup_ref[...]) + gateb_ref[...]) * (1.0 / GLA_TAU)
    bcg = _dot01_left(tri_blk, la, 3)
    bcg3 = bcg.reshape(nc, CHUNK, GLA_QK_W)
    gg3 = bcg3[:, CHUNK - 1:CHUNK, :]
    gq = zm_ref[:, OFF_GQ:OFF_GQ + GLA_QK_W].astype(F32)
    gk = zm_ref[:, OFF_GK:OFF_GK + GLA_QK_W].astype(F32) * (GLA_DK ** -0.5)
    q_dec = (gq * jnp.exp(bcg)).astype(BF16)
    k_inv = (gk * jnp.exp(-bcg)).astype(BF16)
    k_end = (gk * jnp.exp(gg3 - bcg3).reshape(tb, GLA_QK_W)).astype(BF16)
    eg3 = jnp.exp(gg3)
    att, updg = {}, {}
    for c in range(nc):
        rows = chunk_rows(c)
        for h in range(GLA_HEADS):
            dk = slice(h * GLA_DK, (h + 1) * GLA_DK)
            vh = zm_ref[rows, OFF_GV + h * GLA_DV:OFF_GV + (h + 1) * GLA_DV]
            att[c, h] = jnp.where(causal, _dot_nt(q_dec[rows, dk], k_inv[rows, dk]), 0.0).astype(BF16)
            updg[c, h] = _dot_tn(vh, k_end[rows, dk])
    for h in range(GLA_HEADS):
        dk = slice(h * GLA_DK, (h + 1) * GLA_DK)
        st = st_s[h]
        for c in range(nc):
            rows = chunk_rows(c)
            vh = zm_ref[rows, OFF_GV + h * GLA_DV:OFF_GV + (h + 1) * GLA_DV]
            hgla_s[rows, h * GLA_DV:(h + 1) * GLA_DV] = _dot_nt(q_dec[rows, dk], st.astype(BF16)) + _dot(att[c, h], vh)
            st = st * eg3[c][:, dk] + updg[c, h]
        st_s[h] = st

    mean = lambda x: _dot01_right(x, mean_dv, 2)
    for h in range(ML_HEADS):
        sl = slice(h * ML_DV, (h + 1) * ML_DV)
        hh = hml_s[:, sl]
        d = hh - mean(hh)
        hn = d * lax.rsqrt(mean(d * d) + EPS)
        og = zm_ref[:, OFF_MO + h * ML_DV:OFF_MO + (h + 1) * ML_DV].astype(F32)
        y_ref[:, sl] = (hn * mlnorm_ref[:, sl] * _sigmoid(og)).astype(y_ref.dtype)
    for h in range(GLA_HEADS):
        sl = slice(h * GLA_DV, (h + 1) * GLA_DV)
        o = hgla_s[:, sl]
        on = o * lax.rsqrt(mean(o * o) + EPS)
        gg = zm_ref[:, OFF_GG + h * GLA_DV:OFF_GG + (h + 1) * GLA_DV].astype(F32)
        y_ref[:, ML_V_W + h * GLA_DV:ML_V_W + (h + 1) * GLA_DV] = (on * glanorm_ref[:, sl] * (gg * _sigmoid(gg))).astype(y_ref.dtype)


def _mixer(zm, zs, convw, convb, igb, fgb, lrup, gateb, mlnorm, glanorm, bsz, seq, tb):
    nt = seq // tb
    const = lambda shape: pl.BlockSpec(shape, lambda b, i: (0,) * len(shape))
    return pl.pallas_call(
        functools.partial(_mixer2_kernel, tb=tb),
        out_shape=jax.ShapeDtypeStruct((bsz * seq, ML_V_W + GLA_V_W), BF16),
        grid=(bsz, nt),
        in_specs=[pl.BlockSpec((tb, Z_MAIN_W), lambda b, i: (b * nt + i, 0)),
                  pl.BlockSpec((tb, LANES), lambda b, i: (b * nt + i, 0)),
                  const((CONV_K, 2 * ML_QK_W)), const((1, 2 * ML_QK_W)), const((1, LANES)), const((1, LANES)),
                  const((LANES, GLA_QK_W)), const((1, GLA_QK_W)), const((1, ML_V_W)), const((1, GLA_V_W))],
        out_specs=pl.BlockSpec((tb, ML_V_W + GLA_V_W), lambda b, i: (b * nt + i, 0)),
        scratch_shapes=[pltpu.VMEM((tb + 8, 2 * ML_QK_W), F32),
                        pltpu.VMEM((tb, ML_QK_W), F32), pltpu.VMEM((tb, ML_QK_W), F32),
                        pltpu.VMEM((tb + CHUNK, LANES), F32),
                        pltpu.VMEM((tb, ML_V_W), F32), pltpu.VMEM((tb, GLA_V_W), F32),
                        pltpu.VMEM((ML_HEADS, ML_DK, ML_DV + LANES), F32),
                        pltpu.VMEM((1, LANES), F32), pltpu.VMEM((GLA_HEADS, GLA_DV, GLA_DK), F32)],
        compiler_params=_cparams(("parallel", "arbitrary")),
        name="mlstm_gla",
    )(zm, zs, convw, convb, igb, fgb, lrup, gateb, mlnorm, glanorm)


def _proj_router_kernel(y_ref, w_ref, b_ref, h_ref, g_ref, wr_ref, br_ref, h1_ref, u2_ref, rt_ref):
    h1 = h_ref[...] + (_dot(y_ref[...], w_ref[...]) + b_ref[...])
    h1_ref[...] = h1
    u2 = _rms(h1, g_ref[...])
    for c in range(ROW_TILES):
        u2_ref[pl.ds(c, u2.shape[0], stride=ROW_TILES), :] = u2[:, c * LANES:(c + 1) * LANES]
    u_hi = u2.astype(BF16)
    u_lo = (u2 - u_hi.astype(F32)).astype(BF16)
    part = _dot(u_hi, wr_ref[...])
    logits = part[:, :LANES] + (part[:, LANES:] + _dot(u_lo, wr_ref[:, :LANES])) + br_ref[...]
    lane = lax.broadcasted_iota(jnp.int32, logits.shape, 1)
    lane_f = lane.astype(F32)
    big = float(LANES)
    gl = jnp.where(lane < N_GROUPS, logits, -jnp.inf)
    g_max = jnp.max(gl, axis=-1, keepdims=True)
    g_idx = jnp.min(jnp.where(gl == g_max, lane_f, big), axis=-1, keepdims=True)
    g_p = 1.0 / jnp.sum(jnp.exp(gl - g_max), axis=-1, keepdims=True)
    e_grp = ((lane - N_GROUPS) // EXPERTS_PER_GROUP).astype(F32)
    in_grp = (lane >= N_GROUPS) & (lane < N_GROUPS + N_EXPERTS) & (e_grp == g_idx)
    el = jnp.where(in_grp, logits, -jnp.inf)
    t1 = jnp.max(el, axis=-1, keepdims=True)
    i1 = jnp.min(jnp.where(el == t1, lane_f, big), axis=-1, keepdims=True)
    el2 = jnp.where(lane_f == i1, -jnp.inf, el)
    t2 = jnp.max(el2, axis=-1, keepdims=True)
    i2 = jnp.min(jnp.where(el2 == t2, lane_f, big), axis=-1, keepdims=True)
    e21 = jnp.exp(t2 - t1)
    p1 = 1.0 / (1.0 + e21)
    rt = jnp.where(lane == 0, i1 - N_GROUPS,
                   jnp.where(lane == 1, i2 - N_GROUPS,
                             jnp.where(lane == 2, g_p * p1, jnp.where(lane == 3, g_p * (e21 * p1), 0.0))))
    rt_ref[...] = rt


def _proj_router(y, w, b, h, g, wr, br, tm):
    t, kdim = y.shape
    row = lambda i: (i, 0)
    fixed = lambda i: (0, 0)
    return pl.pallas_call(
        _proj_router_kernel,
        out_shape=(jax.ShapeDtypeStruct((t, D_MODEL), F32), jax.ShapeDtypeStruct((t * ROW_TILES, LANES), F32),
                   jax.ShapeDtypeStruct((t, LANES), F32)),
        grid=(t // tm,),
        in_specs=[pl.BlockSpec((tm, kdim), row), pl.BlockSpec((kdim, D_MODEL), fixed),
                  pl.BlockSpec((1, D_MODEL), fixed), pl.BlockSpec((tm, D_MODEL), row),
                  pl.BlockSpec((1, D_MODEL), fixed), pl.BlockSpec((D_MODEL, 2 * LANES), fixed),
                  pl.BlockSpec((1, LANES), fixed)],
        out_specs=(pl.BlockSpec((tm, D_MODEL), row), pl.BlockSpec((tm * ROW_TILES, LANES), row),
                   pl.BlockSpec((tm, LANES), row)),
        compiler_params=_cparams(("parallel",)),
        name="proj_router",
    )(y, w, b, h, g, wr, br)


def _rank_kernel(rt_ref, rk_ref, cnt_ref, base_s, strict_s):
    tt = rt_ref.shape[0]

    @pl.when(pl.program_id(0) == 0)
    def _():
        base_s[...] = jnp.zeros(base_s.shape, F32)
        r = lax.broadcasted_iota(jnp.int32, (tt, tt), 0)
        c = lax.broadcasted_iota(jnp.int32, (tt, tt), 1)
        strict_s[...] = jnp.where(c < r, 1.0, 0.0).astype(BF16)

    rt = rt_ref[...]
    lane = lax.broadcasted_iota(jnp.int32, rt.shape, 1)
    lane_f = lane.astype(F32)
    e0, e1 = rt[:, 0:1], rt[:, 1:2]
    oh0 = lane_f == e0
    oh1 = lane_f == e1
    oh = jnp.where(oh0 | oh1, 1.0, 0.0)
    before = _dot(strict_s[...], oh.astype(BF16)) + base_s[...]
    r0 = jnp.sum(jnp.where(oh0, before, 0.0), axis=-1, keepdims=True)
    r1 = jnp.sum(jnp.where(oh1, before, 0.0), axis=-1, keepdims=True)
    table = jnp.where(lane == 0, r0, jnp.where(lane == 1, r1, jnp.where(lane == 2, e0, jnp.where(lane == 3, e1, 0.0))))
    rk_ref[...] = jnp.transpose(table)[0:8, :].astype(jnp.int32)
    base_s[...] = base_s[...] + jnp.sum(oh, axis=0, keepdims=True)
    cnt_ref[...] = base_s[...]


def _rank(rt, tt):
    t = rt.shape[0]
    return pl.pallas_call(
        _rank_kernel,
        out_shape=(jax.ShapeDtypeStruct((8, t), jnp.int32), jax.ShapeDtypeStruct((1, LANES), F32)),
        grid=(t // tt,),
        in_specs=[pl.BlockSpec((tt, LANES), lambda i: (i, 0))],
        out_specs=(pl.BlockSpec((8, tt), lambda i: (0, i)), pl.BlockSpec((1, LANES), lambda i: (0, 0))),
        scratch_shapes=[pltpu.VMEM((1, LANES), F32), pltpu.VMEM((tt, tt), BF16)],
        compiler_params=_cparams(("arbitrary",)),
        name="expert_rank",
    )(rt)


DMA_GROUP = 8


def _row_copy(src, dst, sem):
    return pltpu.make_async_copy(src, dst, sem)


def _row_tile(r):
    return pl.ds(pl.multiple_of(r * ROW_TILES, ROW_TILES), ROW_TILES)


def _dispatch_kernel(dest_ref, u_ref, xin_hbm, xout_hbm, sem, *, tt):
    del xin_hbm

    def issue(g, carry):
        for jj in range(DMA_GROUP):
            j = g * DMA_GROUP + jj
            src = u_ref.at[_row_tile(j)]
            for k in range(TOP_K):
                _row_copy(src, xout_hbm.at[_row_tile(dest_ref[k, j])], sem).start(priority=k)
        return carry

    lax.fori_loop(0, tt // DMA_GROUP, issue, 0)
    for k in range(TOP_K):
        _row_copy(u_ref, xout_hbm.at[pl.ds(0, tt * ROW_TILES)], sem).wait()


def _dispatch(dest, u2, xinit, tt):
    t = u2.shape[0] // ROW_TILES
    return pl.pallas_call(
        functools.partial(_dispatch_kernel, tt=tt),
        out_shape=jax.ShapeDtypeStruct(xinit.shape, F32),
        grid=(t // tt,),
        in_specs=[pl.BlockSpec((TOP_K, tt), lambda i: (0, i), memory_space=pltpu.SMEM),
                  pl.BlockSpec((tt * ROW_TILES, LANES), lambda i: (i, 0)), pl.BlockSpec(memory_space=pl.ANY)],
        out_specs=pl.BlockSpec(memory_space=pl.ANY),
        scratch_shapes=[pltpu.SemaphoreType.DMA],
        input_output_aliases={2: 0},
        compiler_params=_cparams(("arbitrary",)),
        name="moe_dispatch",
    )(dest, u2, xinit)


def _ffn_kernel(be_ref, nu_ref, x_ref, wg_ref, wu_ref, wd_ref, y_ref, wg_s, wu_s, wd_s, x_s):
    b = pl.program_id(0)

    @pl.when((b == 0) | (be_ref[b] != be_ref[jnp.maximum(b - 1, 0)]))
    def _():
        wg_s[...] = wg_ref[...].astype(BF16)
        wu_s[...] = wu_ref[...].astype(BF16)
        wd_s[...] = wd_ref[...].astype(BF16)

    @pl.when(b < nu_ref[0])
    def _():
        for c in range(ROW_TILES):
            x_s[:, c * LANES:(c + 1) * LANES] = x_ref[pl.ds(c, MOE_BM, stride=ROW_TILES), :].astype(BF16)
        x = x_s[...]
        a = _dot(x, wg_s[...])
        u = _dot(x, wu_s[...])
        y = _dot(((a * _sigmoid(a)) * u).astype(BF16), wd_s[...])
        for c in range(ROW_TILES):
            y_ref[pl.ds(c, MOE_BM, stride=ROW_TILES), :] = y[:, c * LANES:(c + 1) * LANES]

    @pl.when(b >= nu_ref[0])
    def _():
        y_ref[...] = jnp.zeros(y_ref.shape, y_ref.dtype)


def _ffn(block_expert, n_used, xbuf, wg, wu, wd, layer):
    n_blocks = xbuf.shape[0] // (MOE_BM * ROW_TILES)
    rows = lambda b, be, nu: (b, 0)
    wmap = lambda b, be, nu: (layer, be[b], 0, 0)
    return pl.pallas_call(
        _ffn_kernel,
        out_shape=jax.ShapeDtypeStruct(xbuf.shape, F32),
        grid_spec=pltpu.PrefetchScalarGridSpec(
            num_scalar_prefetch=2,
            grid=(n_blocks,),
            in_specs=[pl.BlockSpec((MOE_BM * ROW_TILES, LANES), rows),
                      pl.BlockSpec((None, None, D_MODEL, D_FF), wmap),
                      pl.BlockSpec((None, None, D_MODEL, D_FF), wmap),
                      pl.BlockSpec((None, None, D_FF, D_MODEL), wmap)],
            out_specs=pl.BlockSpec((MOE_BM * ROW_TILES, LANES), rows),
            scratch_shapes=[pltpu.VMEM((D_MODEL, D_FF), BF16), pltpu.VMEM((D_MODEL, D_FF), BF16),
                            pltpu.VMEM((D_FF, D_MODEL), BF16), pltpu.VMEM((MOE_BM, D_MODEL), BF16)]),
        compiler_params=_cparams(("arbitrary",)),
        name="moe_ffn",
    )(block_expert, n_used, xbuf, wg, wu, wd)


def _combine_kernel(dcur_ref, dnxt_ref, y_hbm, h_ref, rt_ref, g_ref, o_ref, ybuf, sems, *, tt, final_norm):
    i = pl.program_id(0)
    last = pl.num_programs(0) - 1
    slot = i % 2

    def gather(dref, s, j):
        for k in range(TOP_K):
            _row_copy(y_hbm.at[_row_tile(dref[k, j])], ybuf.at[s, k, _row_tile(j)], sems.at[s]).start(priority=k)

    def wait_slot(s):
        for k in range(TOP_K):
            _row_copy(y_hbm.at[pl.ds(0, tt * ROW_TILES)], ybuf.at[s, k], sems.at[s]).wait()

    @pl.when(i == 0)
    def _():
        def first(g, carry):
            for jj in range(DMA_GROUP):
                gather(dcur_ref, 0, g * DMA_GROUP + jj)
            return carry
        lax.fori_loop(0, tt // DMA_GROUP, first, 0)

    wait_slot(slot)

    def group(g, carry):
        r0 = pl.multiple_of(g * DMA_GROUP, DMA_GROUP)
        for jj in range(DMA_GROUP):
            gather(dnxt_ref, 1 - slot, r0 + jj)
        rows = pl.ds(r0, DMA_GROUP)
        rt = rt_ref[rows, :]
        w0, w1 = rt[:, 2:3], rt[:, 3:4]
        for c in range(ROW_TILES):
            lanes = slice(c * LANES, (c + 1) * LANES)
            tile_rows = pl.ds(r0 * ROW_TILES + c, DMA_GROUP, stride=ROW_TILES)
            o_ref[rows, lanes] = h_ref[rows, lanes] + (ybuf[slot, 0, tile_rows, :] * w0 + ybuf[slot, 1, tile_rows, :] * w1)
        return carry

    lax.fori_loop(0, tt // DMA_GROUP, group, 0)
    if final_norm:
        o_ref[...] = _rms(o_ref[...], g_ref[...])

    @pl.when(i == last)
    def _():
        wait_slot(1 - slot)


def _combine(dest, ybuf, h1, rt, g, tt, final_norm):
    t = h1.shape[0]
    nsteps = t // tt
    return pl.pallas_call(
        functools.partial(_combine_kernel, tt=tt, final_norm=final_norm),
        out_shape=jax.ShapeDtypeStruct((t, D_MODEL), F32),
        grid=(nsteps,),
        in_specs=[pl.BlockSpec((TOP_K, tt), lambda i: (0, i), memory_space=pltpu.SMEM),
                  pl.BlockSpec((TOP_K, tt), lambda i: (0, jnp.minimum(i + 1, nsteps - 1)), memory_space=pltpu.SMEM),
                  pl.BlockSpec(memory_space=pl.ANY),
                  pl.BlockSpec((tt, D_MODEL), lambda i: (i, 0)),
                  pl.BlockSpec((tt, LANES), lambda i: (i, 0)),
                  pl.BlockSpec((1, D_MODEL), lambda i: (0, 0))],
        out_specs=pl.BlockSpec((tt, D_MODEL), lambda i: (i, 0)),
        scratch_shapes=[pltpu.VMEM((2, TOP_K, tt * ROW_TILES, LANES), F32), pltpu.SemaphoreType.DMA((2,))],
        compiler_params=_cparams(("arbitrary",)),
        name="moe_combine",
    )(dest, dest, ybuf, h1, rt, g)


def _moe_rows(t):
    return ((t * TOP_K) // MOE_BM + N_EXPERTS) * MOE_BM


def _moe(h1, u2, rt, wg, wu, wd, layer, g_final, final_norm, xinit):
    rk, cnt = _rank(rt, 1024)
    counts = cnt[0, :N_EXPERTS].astype(jnp.int32)
    padded = (counts + MOE_BM - 1) // MOE_BM * MOE_BM
    pad_end = jnp.cumsum(padded)
    pad_start = pad_end - padded
    n_blocks = xinit.shape[0] // (MOE_BM * ROW_TILES)
    is_expert = rk[TOP_K:2 * TOP_K][None] == jnp.arange(N_EXPERTS, dtype=jnp.int32)[:, None, None]
    dest = jnp.sum(jnp.where(is_expert, pad_start[:, None, None], 0), axis=0) + rk[0:TOP_K]
    n_used = (pad_end[-1] // MOE_BM).astype(jnp.int32)
    blk = jnp.minimum(jnp.arange(n_blocks, dtype=jnp.int32), n_used - 1) * MOE_BM
    block_expert = jnp.minimum(jnp.sum(pad_end[None, :] <= blk[:, None], axis=1), N_EXPERTS - 1).astype(jnp.int32)
    xbuf = _dispatch(dest, u2, xinit, 256)
    ybuf = _ffn(block_expert, n_used.reshape(1), xbuf, wg, wu, wd, layer)
    return _combine(dest, ybuf, h1, rt, g_final, 256, final_norm), xbuf


def _qkv_kernel(x_ref, g_ref, w_ref, b_ref, pos_ref, rot_ref, q_ref, kv_ref):
    u = _rms(x_ref[...], g_ref[...]).astype(BF16)
    ang = pos_ref[...].astype(F32) * rot_ref[0:1, :]
    cosv = jnp.cos(ang)
    sinv = jnp.sin(ang)
    c_coef = jnp.where(rot_ref[3:4, :] > 0.0, cosv, 1.0)
    s_lo = -sinv * rot_ref[1:2, :]
    s_hi = sinv * rot_ref[2:3, :]

    def rotate(z):
        return z * c_coef + pltpu.roll(z, LANES - ROT_DIM // 2, 1) * s_lo + pltpu.roll(z, ROT_DIM // 2, 1) * s_hi

    scale = SWA_HEAD_DIM ** -0.5
    for j in range(Q_W // LANES):
        sl = slice(j * LANES, (j + 1) * LANES)
        q_ref[:, sl] = (rotate(_dot(u, w_ref[:, sl]) + b_ref[:, sl]) * scale).astype(q_ref.dtype)
    for j in range(KV_OUT_W // LANES):
        sl = slice(Q_W + j * LANES, Q_W + (j + 1) * LANES)
        z = _dot(u, w_ref[:, sl]) + b_ref[:, sl]
        kv_ref[:, j * LANES:(j + 1) * LANES] = (rotate(z) if j % 2 == 0 else z).astype(kv_ref.dtype)


def _rot_table():
    lane = np.arange(LANES)
    d = lane % SWA_HEAD_DIM
    inv_freq = ROPE_THETA ** (-jnp.arange(0, ROT_DIM, 2, dtype=F32) / ROT_DIM)
    half = ROT_DIM // 2
    tab = jnp.zeros((8, LANES), F32)
    tab = tab.at[0].set(jnp.where(d < ROT_DIM, inv_freq[d % half], 0.0))
    tab = tab.at[1].set((d < half).astype(np.float32))
    tab = tab.at[2].set(((d >= half) & (d < ROT_DIM)).astype(np.float32))
    tab = tab.at[3].set((d < ROT_DIM).astype(np.float32))
    return tab


def _qkv(h, g, w, b, pos, tm):
    t = h.shape[0]
    row = lambda i: (i, 0)
    fixed = lambda i: (0, 0)
    wtot = Q_W + KV_OUT_W
    return pl.pallas_call(
        _qkv_kernel,
        out_shape=(jax.ShapeDtypeStruct((t, Q_W), BF16), jax.ShapeDtypeStruct((t, KV_OUT_W), BF16)),
        grid=(t // tm,),
        in_specs=[pl.BlockSpec((tm, D_MODEL), row), pl.BlockSpec((1, D_MODEL), fixed),
                  pl.BlockSpec((D_MODEL, wtot), fixed), pl.BlockSpec((1, wtot), fixed),
                  pl.BlockSpec((tm, 1), row), pl.BlockSpec((8, LANES), fixed)],
        out_specs=(pl.BlockSpec((tm, Q_W), row), pl.BlockSpec((tm, KV_OUT_W), row)),
        compiler_params=_cparams(("parallel",)),
        name="qkv_rotary",
    )(h, g, w, b, pos, _rot_table())


def _swa_kernel(q_ref, kvc_ref, kvp_ref, sink_ref, o_ref):
    neg = jnp.where(pl.program_id(1) > 0, 0.0, -jnp.inf).astype(F32)
    kj = lax.broadcasted_iota(jnp.int32, (WINDOW, WINDOW), 0)
    qi = lax.broadcasted_iota(jnp.int32, (WINDOW, WINDOW), 1)
    mask_cur = kj <= qi
    top = kj < SWA_HEAD_DIM
    lane = lax.broadcasted_iota(jnp.int32, (1, LANES), 1)
    keep_lo = jnp.where(lane < SWA_HEAD_DIM, 1.0, 0.0).astype(BF16)
    keep_hi = jnp.where(lane < SWA_HEAD_DIM, 0.0, 1.0).astype(BF16)
    zeros_half = jnp.zeros((SWA_HEAD_DIM, WINDOW), F32)

    def arranged(ref):
        k_nat, k_swp = ref[:, 0:KV_W], ref[:, 2 * KV_W:3 * KV_W]
        vt = jnp.transpose(ref[:, KV_W:2 * KV_W].astype(F32))
        vt_top = lambda g: jnp.concatenate([vt[g * SWA_HEAD_DIM:(g + 1) * SWA_HEAD_DIM], zeros_half], axis=0).astype(BF16)
        vt_bot = lambda g: jnp.concatenate([zeros_half, vt[g * SWA_HEAD_DIM:(g + 1) * SWA_HEAD_DIM]], axis=0).astype(BF16)
        return {(0, 0): (k_nat * keep_lo, vt_top(0)), (0, 1): (k_swp * keep_hi, vt_bot(0)),
                (1, 0): (k_swp * keep_lo, vt_top(1)), (1, 1): (k_nat * keep_hi, vt_bot(1))}

    cur = arranged(kvc_ref)
    prev = arranged(kvp_ref)
    heads = range(SWA_Q_HEADS)
    key = lambda h: (h // SWA_GROUP, h % 2)
    scores = []
    for h in heads:
        qp = q_ref[:, (h // 2) * LANES:(h // 2 + 1) * LANES]
        sc = _dot_nt(cur[key(h)][0], qp)
        sp = _dot_nt(prev[key(h)][0], qp)
        scores.append(jnp.where(mask_cur, sc, sp + neg))
    sinks = [sink_ref[:, h:h + 1] for h in heads]
    maxes = [jnp.maximum(jnp.max(scores[h], axis=0, keepdims=True), sinks[h]) for h in heads]
    probs = [jnp.exp(scores[h] - maxes[h]) for h in heads]
    denoms = [jnp.sum(probs[h], axis=0, keepdims=True) + jnp.exp(sinks[h] - maxes[h]) for h in heads]
    for pr in range(SWA_Q_HEADS // 2):
        num = None
        for h in (2 * pr, 2 * pr + 1):
            pc = jnp.where(mask_cur, probs[h], 0.0).astype(BF16)
            pp = jnp.where(mask_cur, 0.0, probs[h]).astype(BF16)
            part = _dot(cur[key(h)][1], pc) + _dot(prev[key(h)][1], pp)
            num = part if num is None else num + part
        den = jnp.where(top, denoms[2 * pr], denoms[2 * pr + 1])
        o_ref[:, pr * LANES:(pr + 1) * LANES] = jnp.transpose(num / den).astype(o_ref.dtype)


def _swa(q, kv, sinks, bsz, seq):
    nb = seq // WINDOW
    return pl.pallas_call(
        _swa_kernel,
        out_shape=jax.ShapeDtypeStruct((bsz * seq, Q_W), BF16),
        grid=(bsz, nb),
        in_specs=[pl.BlockSpec((WINDOW, Q_W), lambda b, n: (b * nb + n, 0)),
                  pl.BlockSpec((WINDOW, KV_OUT_W), lambda b, n: (b * nb + n, 0)),
                  pl.BlockSpec((WINDOW, KV_OUT_W), lambda b, n: (b * nb + jnp.maximum(n - 1, 0), 0)),
                  pl.BlockSpec((1, LANES), lambda b, n: (0, 0))],
        out_specs=pl.BlockSpec((WINDOW, Q_W), lambda b, n: (b * nb + n, 0)),
        compiler_params=_cparams(("parallel", "parallel")),
        name="swa_attention",
    )(q, kv, kv, sinks)


def _pack_in_proj(w_in):
    offs = np.cumsum((0,) + AB_SPLITS)
    part = lambda i: w_in[:, offs[i]:offs[i + 1]]
    w_main = jnp.concatenate([part(0), part(1), part(2), part(5), part(6), part(7), part(8)], axis=1).astype(BF16)
    small = jnp.concatenate([part(3), part(4), part(9)], axis=1)
    w_small = jnp.pad(small, ((0, 0), (0, LANES - small.shape[1]))).astype(BF16)
    return w_main, w_small


def _pack_qkv(w, b):
    hd = SWA_HEAD_DIM
    k0, k1 = slice(Q_W, Q_W + hd), slice(Q_W + hd, Q_W + 2 * hd)
    cols = lambda a: jnp.concatenate([a[..., :Q_W + 2 * KV_W], a[..., k1], a[..., k0]], axis=-1)
    return cols(w).astype(BF16), cols(b).reshape(1, -1).astype(F32)


def _lane_row(v, offset=0):
    return jnp.zeros((1, LANES), F32).at[0, offset:offset + v.shape[0]].set(v.astype(F32))


def kernel(x, positions, mix_norm, ffn_norm, final_norm, ab_w_in, ab_conv_w, ab_conv_b, ml_igate_b, ml_fgate_b, ml_head_norm, gla_w_lr_up, gla_gate_b, gla_head_norm, ab_w_out, swa_w_qkv, swa_b_qkv, swa_sinks, swa_w_o, swa_b_o, router_group_w, router_group_b, router_expert_w, router_expert_b, expert_w_gate, expert_w_up, expert_w_down):
    bsz, seq, d = x.shape
    t = bsz * seq
    h = x.reshape(t, d)
    row = lambda v: v.reshape(1, -1).astype(F32)

    def router_params(layer):
        wr = jnp.zeros((d, LANES), F32).at[:, :N_GROUPS].set(router_group_w[layer])
        wr = wr.at[:, N_GROUPS:N_GROUPS + N_EXPERTS].set(router_expert_w[layer])
        br = jnp.zeros((1, LANES), F32).at[0, :N_GROUPS].set(router_group_b[layer])
        br = br.at[0, N_GROUPS:N_GROUPS + N_EXPERTS].set(router_expert_b[layer])
        w_hi = wr.astype(BF16)
        w_lo = (wr - w_hi.astype(F32)).astype(BF16)
        return jnp.concatenate([w_hi, w_lo], axis=1), br

    def experts(layer):
        return expert_w_gate, expert_w_up, expert_w_down, layer

    w_main, w_small = _pack_in_proj(ab_w_in[0])
    zm, zs = _in_proj(h, row(mix_norm[0]), w_main, w_small, 512)
    lrup = jnp.zeros((LANES, GLA_QK_W), F32).at[SM_LR:SM_LR + GLA_LOWRANK].set(gla_w_lr_up[0]).astype(BF16)
    y = _mixer(zm, zs, ab_conv_w[0], row(ab_conv_b[0]), _lane_row(ml_igate_b[0]), _lane_row(ml_fgate_b[0]), lrup,
               row(gla_gate_b[0]), row(ml_head_norm[0]), row(gla_head_norm[0]), bsz, seq, 256)
    wr, br = router_params(0)
    h1, u2, rt = _proj_router(y, ab_w_out[0].astype(BF16), jnp.zeros((1, d), F32), h, row(ffn_norm[0]), wr, br, 512)
    h, xbuf = _moe(h1, u2, rt, *experts(0), row(final_norm), False, jnp.zeros((_moe_rows(t) * ROW_TILES, LANES), F32))

    w_qkv, b_qkv = _pack_qkv(swa_w_qkv[0], swa_b_qkv[0])
    q, kv = _qkv(h, row(mix_norm[1]), w_qkv, b_qkv, positions.reshape(t, 1), 512)
    o = _swa(q, kv, _lane_row(swa_sinks[0]), bsz, seq)
    wr, br = router_params(1)
    h1, u2, rt = _proj_router(o, swa_w_o[0].astype(BF16), row(swa_b_o[0]), h, row(ffn_norm[1]), wr, br, 512)
    out, _ = _moe(h1, u2, rt, *experts(1), row(final_norm), True, xbuf)
    return out.reshape(bsz, seq, d)
```

```python
import functools

import jax
import jax.numpy as jnp
import numpy as np
from jax import lax
from jax.experimental import pallas as pl
from jax.experimental.pallas import tpu as pltpu

F32 = jnp.float32
BF16 = jnp.bfloat16
HIGHEST = lax.Precision.HIGHEST

D_MODEL = 1024
EPS = 1e-6
ML_HEADS = 4
ML_DK = 64
ML_DV = 128
GLA_HEADS = 4
GLA_DK = 64
GLA_DV = 128
CHUNK = 64
CONV_K = 4
GLA_LOWRANK = 16
GLA_TAU = 16.0
ML_QK_W = ML_HEADS * ML_DK
ML_V_W = ML_HEADS * ML_DV
GLA_QK_W = GLA_HEADS * GLA_DK
GLA_V_W = GLA_HEADS * GLA_DV
AB_SPLITS = (2 * ML_QK_W, ML_V_W, ML_V_W, ML_HEADS, ML_HEADS, GLA_QK_W, GLA_QK_W, GLA_V_W, GLA_V_W, GLA_LOWRANK)
OFF_MQK = 0
OFF_MV = OFF_MQK + 2 * ML_QK_W
OFF_MO = OFF_MV + ML_V_W
OFF_GQ = OFF_MO + ML_V_W
OFF_GK = OFF_GQ + GLA_QK_W
OFF_GV = OFF_GK + GLA_QK_W
OFF_GG = OFF_GV + GLA_V_W
Z_MAIN_W = OFF_GG + GLA_V_W
SM_I = 0
SM_F = ML_HEADS
SM_LR = 2 * ML_HEADS
LANES = 128
SWA_Q_HEADS = 16
SWA_KV_HEADS = 2
SWA_HEAD_DIM = 64
SWA_GROUP = SWA_Q_HEADS // SWA_KV_HEADS
WINDOW = 128
ROT_DIM = SWA_HEAD_DIM // 4
ROPE_THETA = 500000.0
Q_W = SWA_Q_HEADS * SWA_HEAD_DIM
KV_W = SWA_KV_HEADS * SWA_HEAD_DIM
KV_OUT_W = 3 * KV_W
N_GROUPS = 4
EXPERTS_PER_GROUP = 8
N_EXPERTS = N_GROUPS * EXPERTS_PER_GROUP
TOP_K = 2
D_FF = 512
MOE_BM = 512
ROW_TILES = D_MODEL // LANES

VMEM_LIMIT = 56 * 1024 * 1024


def _cparams(sem):
    return pltpu.CompilerParams(dimension_semantics=sem, vmem_limit_bytes=VMEM_LIMIT)


def _rms(x, g):
    return x * lax.rsqrt(jnp.mean(x * x, axis=-1, keepdims=True) + EPS) * g


def _log_sigmoid(x):
    return jnp.minimum(x, 0.0) - jnp.log1p(jnp.exp(-jnp.abs(x)))


def _sigmoid(x):
    return 1.0 / (1.0 + jnp.exp(-x))


def _dot(a, b):
    return jnp.dot(a, b, preferred_element_type=F32)


def _dot_nt(a, b):
    return lax.dot_general(a, b, (((1,), (1,)), ((), ())), preferred_element_type=F32)


def _dot_tn(a, b):
    return lax.dot_general(a, b, (((0,), (0,)), ((), ())), preferred_element_type=F32)


def _in_proj_kernel(x_ref, g_ref, w_ref, ws_ref, zm_ref, zs_ref, *, n_chunk):
    u = _rms(x_ref[...], g_ref[...]).astype(BF16)
    for n0 in range(0, w_ref.shape[1], n_chunk):
        zm_ref[:, n0:n0 + n_chunk] = _dot(u, w_ref[:, n0:n0 + n_chunk]).astype(zm_ref.dtype)
    zs_ref[...] = _dot(u, ws_ref[...])


def _in_proj(h, g, w_main, w_small, tm):
    t = h.shape[0]
    return pl.pallas_call(
        functools.partial(_in_proj_kernel, n_chunk=768),
        out_shape=(jax.ShapeDtypeStruct((t, Z_MAIN_W), BF16), jax.ShapeDtypeStruct((t, LANES), F32)),
        grid=(t // tm,),
        in_specs=[pl.BlockSpec((tm, D_MODEL), lambda i: (i, 0)),
                  pl.BlockSpec((1, D_MODEL), lambda i: (0, 0)),
                  pl.BlockSpec((D_MODEL, Z_MAIN_W), lambda i: (0, 0)),
                  pl.BlockSpec((D_MODEL, LANES), lambda i: (0, 0))],
        out_specs=(pl.BlockSpec((tm, Z_MAIN_W), lambda i: (i, 0)),
                   pl.BlockSpec((tm, LANES), lambda i: (i, 0))),
        compiler_params=_cparams(("parallel",)),
        name="in_proj",
    )(h, g, w_main, w_small)


def _split_terms(x, n):
    terms = []
    for _ in range(n):
        t = x.astype(BF16)
        terms.append(t)
        x = x - t.astype(F32)
    return terms


def _dot01_left(m01, x, n):
    return sum(_dot(m01, t) for t in _split_terms(x, n))


def _dot01_right(x, m01, n):
    return sum(_dot(t, m01) for t in _split_terms(x, n))


def _mixer_kernel(zm_ref, zs_ref, convw_ref, convb_ref, igb_ref, fgb_ref, lrup_ref, gateb_ref, mlnorm_ref,
                  glanorm_ref, y_ref, xpad, q_s, k_s, e_pad, hml_s, hgla_s, cn_s, m_s, st_s, *, tb):
    nc = tb // CHUNK

    @pl.when(pl.program_id(1) == 0)
    def _():
        xpad[0:8, :] = jnp.zeros((8, 2 * ML_QK_W), F32)
        cn_s[...] = jnp.zeros(cn_s.shape, F32)
        st_s[...] = jnp.zeros(st_s.shape, F32)
        m_s[...] = jnp.full(m_s.shape, -jnp.inf, F32)

    xpad[8:8 + tb, :] = zm_ref[:, OFF_MQK:OFF_MQK + 2 * ML_QK_W].astype(F32)
    conv = convb_ref[...] + convw_ref[3:4, :] * xpad[8:8 + tb, :]
    for j in range(CONV_K - 1):
        conv = conv + convw_ref[j:j + 1, :] * xpad[5 + j:5 + j + tb, :]
    xpad[0:8, :] = xpad[tb:tb + 8, :]
    qk = conv * _sigmoid(conv)
    q_s[...] = qk[:, :ML_QK_W]
    k_s[...] = qk[:, ML_QK_W:] * (ML_DK ** -0.5)

    r_t = lax.broadcasted_iota(jnp.int32, (tb, tb), 0)
    c_t = lax.broadcasted_iota(jnp.int32, (tb, tb), 1)
    tri_blk = jnp.where((r_t // CHUNK == c_t // CHUNK) & (c_t <= r_t), 1.0, 0.0).astype(BF16)
    r_e = lax.broadcasted_iota(jnp.int32, (LANES, 2 * LANES), 0)
    c_e = lax.broadcasted_iota(jnp.int32, (LANES, 2 * LANES), 1)
    spread_dk = jnp.where(c_e // ML_DK == r_e, 1.0, 0.0).astype(BF16)
    mean_dv = jnp.full((ML_DV, ML_DV), 1.0 / ML_DV, F32).astype(BF16)
    row_c = lax.broadcasted_iota(jnp.int32, (CHUNK, CHUNK), 0)
    col_c = lax.broadcasted_iota(jnp.int32, (CHUNK, CHUNK), 1)
    causal = col_c <= row_c
    lane_c = lax.broadcasted_iota(jnp.int32, (CHUNK, LANES), 1)
    ones_dv = jnp.ones((CHUNK, ML_DV), BF16)
    chunk_rows = lambda c: slice(c * CHUNK, (c + 1) * CHUNK)

    zs = zs_ref[...]
    ig = zs + igb_ref[...]
    lf = _log_sigmoid(pltpu.roll(zs, LANES - SM_F, 1) + fgb_ref[...])
    bc = _dot01_left(tri_blk, lf, 3)
    bc3 = bc.reshape(nc, CHUNK, LANES)
    g3 = bc3[:, CHUNK - 1:CHUNK, :]
    a3 = g3 - bc3 + ig.reshape(nc, CHUNK, LANES)
    amax3 = jnp.max(a3, axis=1, keepdims=True)
    wa = jnp.exp(a3 - amax3).reshape(tb, LANES)
    m_run = m_s[...]
    m_prev, s_old, s_in = [], [], []
    for c in range(nc):
        m_new = jnp.maximum(g3[c] + m_run, amax3[c])
        m_prev.append(m_run)
        s_old.append(jnp.exp(g3[c] + m_run - m_new))
        s_in.append(jnp.exp(amax3[c] - m_new))
        m_run = m_new
    m_s[...] = m_run
    e_nat = ig - bc
    e_pad[0:CHUNK, :] = jnp.full((CHUNK, LANES), -jnp.inf, F32)
    e_pad[CHUNK:CHUNK + tb, :] = e_nat
    pos = lax.broadcasted_iota(jnp.int32, (tb, LANES), 0) % CHUNK
    shift = 1
    while shift < CHUNK:
        shifted = e_pad[CHUNK - shift:CHUNK - shift + tb, :]
        e_pad[CHUNK:CHUNK + tb, :] = jnp.maximum(e_pad[CHUNK:CHUNK + tb, :], jnp.where(pos >= shift, shifted, -jnp.inf))
        shift *= 2
    m_intra = bc + e_pad[CHUNK:CHUNK + tb, :]
    il = jnp.concatenate([bc3[c] + m_prev[c] for c in range(nc)], axis=0)
    mt = jnp.maximum(il, m_intra)
    s_inter = jnp.exp(il - mt)
    exp_neg = jnp.exp(-mt)
    xn = bc - mt
    e_up = pltpu.roll(e_nat, ML_HEADS, 1)
    wa_x = _dot01_right(wa, spread_dk, 2)
    si_x = _dot01_right(s_inter, spread_dk, 2)
    q_all = q_s[...]
    k_all = k_s[...]
    kw_b = (k_all * wa_x).astype(BF16)
    qs_b = (q_all * si_x).astype(BF16)
    q_b = q_all.astype(BF16)
    k_b = k_all.astype(BF16)

    qkm, upd, vo = {}, {}, {}
    for c in range(nc):
        rows = chunk_rows(c)
        for h in range(ML_HEADS):
            dk = slice(h * ML_DK, (h + 1) * ML_DK)
            x_t = jnp.where(lane_c == h, xn[rows], jnp.where(lane_c == h + ML_HEADS, 1.0, 0.0))
            y_t = jnp.where(lane_c == h, 1.0, jnp.where(lane_c == h + ML_HEADS, e_up[rows], 0.0))
            d = lax.dot_general(x_t, y_t, (((1,), (1,)), ((), ())), precision=HIGHEST, preferred_element_type=F32)
            p = jnp.where(causal, jnp.exp(d), 0.0)
            qkm[c, h] = (_dot_nt(q_b[rows, dk], k_b[rows, dk]) * p).astype(BF16)
            vo[c, h] = jnp.concatenate([zm_ref[rows, OFF_MV + h * ML_DV:OFF_MV + (h + 1) * ML_DV], ones_dv], axis=1)
            upd[c, h] = _dot_tn(kw_b[rows, dk], vo[c, h])
    for h in range(ML_HEADS):
        dk = slice(h * ML_DK, (h + 1) * ML_DK)
        dv = slice(h * ML_DV, (h + 1) * ML_DV)
        cn = cn_s[h]
        for c in range(nc):
            rows = chunk_rows(c)
            res = _dot(qs_b[rows, dk], cn.astype(BF16)) + _dot(qkm[c, h], vo[c, h])
            num, den = res[:, :ML_DV], res[:, ML_DV:]
            hml_s[rows, dv] = num / jnp.maximum(jnp.abs(den), exp_neg[rows, h:h + 1])
            cn = s_old[c][:, h:h + 1] * cn + s_in[c][:, h:h + 1] * upd[c, h]
        cn_s[h] = cn

    la = _log_sigmoid(_dot(zs.astype(BF16), lrup_ref[...]) + gateb_ref[...]) * (1.0 / GLA_TAU)
    bcg = _dot01_left(tri_blk, la, 3)
    bcg3 = bcg.reshape(nc, CHUNK, GLA_QK_W)
    gg3 = bcg3[:, CHUNK - 1:CHUNK, :]
    gq = zm_ref[:, OFF_GQ:OFF_GQ + GLA_QK_W].astype(F32)
    gk = zm_ref[:, OFF_GK:OFF_GK + GLA_QK_W].astype(F32) * (GLA_DK ** -0.5)
    q_dec = (gq * jnp.exp(bcg)).astype(BF16)
    k_inv = (gk * jnp.exp(-bcg)).astype(BF16)
    k_end = (gk * jnp.exp(gg3 - bcg3).reshape(tb, GLA_QK_W)).astype(BF16)
    eg3 = jnp.exp(gg3)
    att, updg = {}, {}
    for c in range(nc):
        rows = chunk_rows(c)
        for h in range(GLA_HEADS):
            dk = slice(h * GLA_DK, (h + 1) * GLA_DK)
            vh = zm_ref[rows, OFF_GV + h * GLA_DV:OFF_GV + (h + 1) * GLA_DV]
            att[c, h] = jnp.where(causal, _dot_nt(q_dec[rows, dk], k_inv[rows, dk]), 0.0).astype(BF16)
            updg[c, h] = _dot_tn(vh, k_end[rows, dk])
    for h in range(GLA_HEADS):
        dk = slice(h * GLA_DK, (h + 1) * GLA_DK)
        st = st_s[h]
        for c in range(nc):
            rows = chunk_rows(c)
            vh = zm_ref[rows, OFF_GV + h * GLA_DV:OFF_GV + (h + 1) * GLA_DV]
            hgla_s[rows, h * GLA_DV:(h + 1) * GLA_DV] = _dot_nt(q_dec[rows, dk], st.astype(BF16)) + _dot(att[c, h], vh)
            st = st * eg3[c][:, dk] + updg[c, h]
        st_s[h] = st

    mean = lambda x: _dot01_right(x, mean_dv, 2)
    for h in range(ML_HEADS):
        sl = slice(h * ML_DV, (h + 1) * ML_DV)
        hh = hml_s[:, sl]
        d = hh - mean(hh)
        hn = d * lax.rsqrt(mean(d * d) + EPS)
        og = zm_ref[:, OFF_MO + h * ML_DV:OFF_MO + (h + 1) * ML_DV].astype(F32)
        y_ref[:, sl] = (hn * mlnorm_ref[:, sl] * _sigmoid(og)).astype(y_ref.dtype)
    for h in range(GLA_HEADS):
        sl = slice(h * GLA_DV, (h + 1) * GLA_DV)
        o = hgla_s[:, sl]
        on = o * lax.rsqrt(mean(o * o) + EPS)
        gg = zm_ref[:, OFF_GG + h * GLA_DV:OFF_GG + (h + 1) * GLA_DV].astype(F32)
        y_ref[:, ML_V_W + h * GLA_DV:ML_V_W + (h + 1) * GLA_DV] = (on * glanorm_ref[:, sl] * (gg * _sigmoid(gg))).astype(y_ref.dtype)


def _mixer(zm, zs, convw, convb, igb, fgb, lrup, gateb, mlnorm, glanorm, bsz, seq, tb):
    nt = seq // tb
    const = lambda shape: pl.BlockSpec(shape, lambda b, i: (0,) * len(shape))
    return pl.pallas_call(
        functools.partial(_mixer_kernel, tb=tb),
        out_shape=jax.ShapeDtypeStruct((bsz * seq, ML_V_W + GLA_V_W), BF16),
        grid=(bsz, nt),
        in_specs=[pl.BlockSpec((tb, Z_MAIN_W), lambda b, i: (b * nt + i, 0)),
                  pl.BlockSpec((tb, LANES), lambda b, i: (b * nt + i, 0)),
                  const((CONV_K, 2 * ML_QK_W)), const((1, 2 * ML_QK_W)), const((1, LANES)), const((1, LANES)),
                  const((LANES, GLA_QK_W)), const((1, GLA_QK_W)), const((1, ML_V_W)), const((1, GLA_V_W))],
        out_specs=pl.BlockSpec((tb, ML_V_W + GLA_V_W), lambda b, i: (b * nt + i, 0)),
        scratch_shapes=[pltpu.VMEM((tb + 8, 2 * ML_QK_W), F32),
                        pltpu.VMEM((tb, ML_QK_W), F32), pltpu.VMEM((tb, ML_QK_W), F32),
                        pltpu.VMEM((tb + CHUNK, LANES), F32),
                        pltpu.VMEM((tb, ML_V_W), F32), pltpu.VMEM((tb, GLA_V_W), F32),
                        pltpu.VMEM((ML_HEADS, ML_DK, ML_DV + LANES), F32),
                        pltpu.VMEM((1, LANES), F32), pltpu.VMEM((GLA_HEADS, GLA_DV, GLA_DK), F32)],
        compiler_params=_cparams(("parallel", "arbitrary")),
        name="mlstm_gla",
    )(zm, zs, convw, convb, igb, fgb, lrup, gateb, mlnorm, glanorm)


def _proj_router_kernel(y_ref, w_ref, b_ref, h_ref, g_ref, wr_ref, br_ref, h1_ref, u2_ref, rt_ref):
    h1 = h_ref[...] + (_dot(y_ref[...], w_ref[...]) + b_ref[...])
    h1_ref[...] = h1
    u2 = _rms(h1, g_ref[...])
    for c in range(ROW_TILES):
        u2_ref[pl.ds(c, u2.shape[0], stride=ROW_TILES), :] = u2[:, c * LANES:(c + 1) * LANES]
    u_hi = u2.astype(BF16)
    u_lo = (u2 - u_hi.astype(F32)).astype(BF16)
    part = _dot(u_hi, wr_ref[...])
    logits = part[:, :LANES] + (part[:, LANES:] + _dot(u_lo, wr_ref[:, :LANES])) + br_ref[...]
    lane = lax.broadcasted_iota(jnp.int32, logits.shape, 1)
    lane_f = lane.astype(F32)
    big = float(LANES)
    gl = jnp.where(lane < N_GROUPS, logits, -jnp.inf)
    g_max = jnp.max(gl, axis=-1, keepdims=True)
    g_idx = jnp.min(jnp.where(gl == g_max, lane_f, big), axis=-1, keepdims=True)
    g_p = 1.0 / jnp.sum(jnp.exp(gl - g_max), axis=-1, keepdims=True)
    e_grp = ((lane - N_GROUPS) // EXPERTS_PER_GROUP).astype(F32)
    in_grp = (lane >= N_GROUPS) & (lane < N_GROUPS + N_EXPERTS) & (e_grp == g_idx)
    el = jnp.where(in_grp, logits, -jnp.inf)
    t1 = jnp.max(el, axis=-1, keepdims=True)
    i1 = jnp.min(jnp.where(el == t1, lane_f, big), axis=-1, keepdims=True)
    el2 = jnp.where(lane_f == i1, -jnp.inf, el)
    t2 = jnp.max(el2, axis=-1, keepdims=True)
    i2 = jnp.min(jnp.where(el2 == t2, lane_f, big), axis=-1, keepdims=True)
    e21 = jnp.exp(t2 - t1)
    p1 = 1.0 / (1.0 + e21)
    rt = jnp.where(lane == 0, i1 - N_GROUPS,
                   jnp.where(lane == 1, i2 - N_GROUPS,
                             jnp.where(lane == 2, g_p * p1, jnp.where(lane == 3, g_p * (e21 * p1), 0.0))))
    rt_ref[...] = rt


def _proj_router(y, w, b, h, g, wr, br, tm):
    t, kdim = y.shape
    row = lambda i: (i, 0)
    fixed = lambda i: (0, 0)
    return pl.pallas_call(
        _proj_router_kernel,
        out_shape=(jax.ShapeDtypeStruct((t, D_MODEL), F32), jax.ShapeDtypeStruct((t * ROW_TILES, LANES), F32),
                   jax.ShapeDtypeStruct((t, LANES), F32)),
        grid=(t // tm,),
        in_specs=[pl.BlockSpec((tm, kdim), row), pl.BlockSpec((kdim, D_MODEL), fixed),
                  pl.BlockSpec((1, D_MODEL), fixed), pl.BlockSpec((tm, D_MODEL), row),
                  pl.BlockSpec((1, D_MODEL), fixed), pl.BlockSpec((D_MODEL, 2 * LANES), fixed),
                  pl.BlockSpec((1, LANES), fixed)],
        out_specs=(pl.BlockSpec((tm, D_MODEL), row), pl.BlockSpec((tm * ROW_TILES, LANES), row),
                   pl.BlockSpec((tm, LANES), row)),
        compiler_params=_cparams(("parallel",)),
        name="proj_router",
    )(y, w, b, h, g, wr, br)


def _rank_kernel(rt_ref, rk_ref, cnt_ref, base_s, strict_s):
    tt = rt_ref.shape[0]

    @pl.when(pl.program_id(0) == 0)
    def _():
        base_s[...] = jnp.zeros(base_s.shape, F32)
        r = lax.broadcasted_iota(jnp.int32, (tt, tt), 0)
        c = lax.broadcasted_iota(jnp.int32, (tt, tt), 1)
        strict_s[...] = jnp.where(c < r, 1.0, 0.0).astype(BF16)

    rt = rt_ref[...]
    lane = lax.broadcasted_iota(jnp.int32, rt.shape, 1)
    lane_f = lane.astype(F32)
    e0, e1 = rt[:, 0:1], rt[:, 1:2]
    oh0 = lane_f == e0
    oh1 = lane_f == e1
    oh = jnp.where(oh0 | oh1, 1.0, 0.0)
    before = _dot(strict_s[...], oh.astype(BF16)) + base_s[...]
    r0 = jnp.sum(jnp.where(oh0, before, 0.0), axis=-1, keepdims=True)
    r1 = jnp.sum(jnp.where(oh1, before, 0.0), axis=-1, keepdims=True)
    table = jnp.where(lane == 0, r0, jnp.where(lane == 1, r1, jnp.where(lane == 2, e0, jnp.where(lane == 3, e1, 0.0))))
    rk_ref[...] = jnp.transpose(table)[0:8, :].astype(jnp.int32)
    base_s[...] = base_s[...] + jnp.sum(oh, axis=0, keepdims=True)
    cnt_ref[...] = base_s[...]


def _rank(rt, tt):
    t = rt.shape[0]
    return pl.pallas_call(
        _rank_kernel,
        out_shape=(jax.ShapeDtypeStruct((8, t), jnp.int32), jax.ShapeDtypeStruct((1, LANES), F32)),
        grid=(t // tt,),
        in_specs=[pl.BlockSpec((tt, LANES), lambda i: (i, 0))],
        out_specs=(pl.BlockSpec((8, tt), lambda i: (0, i)), pl.BlockSpec((1, LANES), lambda i: (0, 0))),
        scratch_shapes=[pltpu.VMEM((1, LANES), F32), pltpu.VMEM((tt, tt), BF16)],
        compiler_params=_cparams(("arbitrary",)),
        name="expert_rank",
    )(rt)


DMA_GROUP = 8


def _row_copy(src, dst, sem):
    return pltpu.make_async_copy(src, dst, sem)


def _row_tile(r):
    return pl.ds(pl.multiple_of(r * ROW_TILES, ROW_TILES), ROW_TILES)


def _dispatch_kernel(dest_ref, u_ref, xin_hbm, xout_hbm, sem, *, tt):
    del xin_hbm

    def issue(g, carry):
        for jj in range(DMA_GROUP):
            j = g * DMA_GROUP + jj
            src = u_ref.at[_row_tile(j)]
            for k in range(TOP_K):
                _row_copy(src, xout_hbm.at[_row_tile(dest_ref[k, j])], sem).start(priority=k)
        return carry

    lax.fori_loop(0, tt // DMA_GROUP, issue, 0)
    for k in range(TOP_K):
        _row_copy(u_ref, xout_hbm.at[pl.ds(0, tt * ROW_TILES)], sem).wait()


def _dispatch(dest, u2, xinit, tt):
    t = u2.shape[0] // ROW_TILES
    return pl.pallas_call(
        functools.partial(_dispatch_kernel, tt=tt),
        out_shape=jax.ShapeDtypeStruct(xinit.shape, F32),
        grid=(t // tt,),
        in_specs=[pl.BlockSpec((TOP_K, tt), lambda i: (0, i), memory_space=pltpu.SMEM),
                  pl.BlockSpec((tt * ROW_TILES, LANES), lambda i: (i, 0)), pl.BlockSpec(memory_space=pl.ANY)],
        out_specs=pl.BlockSpec(memory_space=pl.ANY),
        scratch_shapes=[pltpu.SemaphoreType.DMA],
        input_output_aliases={2: 0},
        compiler_params=_cparams(("arbitrary",)),
        name="moe_dispatch",
    )(dest, u2, xinit)


def _ffn_kernel(be_ref, nu_ref, x_ref, wg_ref, wu_ref, wd_ref, y_ref, wg_s, wu_s, wd_s, x_s):
    b = pl.program_id(0)

    @pl.when((b == 0) | (be_ref[b] != be_ref[jnp.maximum(b - 1, 0)]))
    def _():
        wg_s[...] = wg_ref[...].astype(BF16)
        wu_s[...] = wu_ref[...].astype(BF16)
        wd_s[...] = wd_ref[...].astype(BF16)

    @pl.when(b < nu_ref[0])
    def _():
        for c in range(ROW_TILES):
            x_s[:, c * LANES:(c + 1) * LANES] = x_ref[pl.ds(c, MOE_BM, stride=ROW_TILES), :].astype(BF16)
        x = x_s[...]
        a = _dot(x, wg_s[...])
        u = _dot(x, wu_s[...])
        y = _dot(((a * _sigmoid(a)) * u).astype(BF16), wd_s[...])
        for c in range(ROW_TILES):
            y_ref[pl.ds(c, MOE_BM, stride=ROW_TILES), :] = y[:, c * LANES:(c + 1) * LANES]

    @pl.when(b >= nu_ref[0])
    def _():
        y_ref[...] = jnp.zeros(y_ref.shape, y_ref.dtype)


def _ffn(block_expert, n_used, xbuf, wg, wu, wd, layer):
    n_blocks = xbuf.shape[0] // (MOE_BM * ROW_TILES)
    rows = lambda b, be, nu: (b, 0)
    wmap = lambda b, be, nu: (layer, be[b], 0, 0)
    return pl.pallas_call(
        _ffn_kernel,
        out_shape=jax.ShapeDtypeStruct(xbuf.shape, F32),
        grid_spec=pltpu.PrefetchScalarGridSpec(
            num_scalar_prefetch=2,
            grid=(n_blocks,),
            in_specs=[pl.BlockSpec((MOE_BM * ROW_TILES, LANES), rows),
                      pl.BlockSpec((None, None, D_MODEL, D_FF), wmap),
                      pl.BlockSpec((None, None, D_MODEL, D_FF), wmap),
                      pl.BlockSpec((None, None, D_FF, D_MODEL), wmap)],
            out_specs=pl.BlockSpec((MOE_BM * ROW_TILES, LANES), rows),
            scratch_shapes=[pltpu.VMEM((D_MODEL, D_FF), BF16), pltpu.VMEM((D_MODEL, D_FF), BF16),
                            pltpu.VMEM((D_FF, D_MODEL), BF16), pltpu.VMEM((MOE_BM, D_MODEL), BF16)]),
        compiler_params=_cparams(("arbitrary",)),
        name="moe_ffn",
    )(block_expert, n_used, xbuf, wg, wu, wd)


def _combine_kernel(dcur_ref, dnxt_ref, y_hbm, h_ref, rt_ref, g_ref, o_ref, ybuf, sems, *, tt, final_norm):
    i = pl.program_id(0)
    last = pl.num_programs(0) - 1
    slot = i % 2

    def gather(dref, s, j):
        for k in range(TOP_K):
            _row_copy(y_hbm.at[_row_tile(dref[k, j])], ybuf.at[s, k, _row_tile(j)], sems.at[s]).start(priority=k)

    def wait_slot(s):
        for k in range(TOP_K):
            _row_copy(y_hbm.at[pl.ds(0, tt * ROW_TILES)], ybuf.at[s, k], sems.at[s]).wait()

    @pl.when(i == 0)
    def _():
        def first(g, carry):
            for jj in range(DMA_GROUP):
                gather(dcur_ref, 0, g * DMA_GROUP + jj)
            return carry
        lax.fori_loop(0, tt // DMA_GROUP, first, 0)

    wait_slot(slot)

    def group(g, carry):
        r0 = pl.multiple_of(g * DMA_GROUP, DMA_GROUP)
        for jj in range(DMA_GROUP):
            gather(dnxt_ref, 1 - slot, r0 + jj)
        rows = pl.ds(r0, DMA_GROUP)
        rt = rt_ref[rows, :]
        w0, w1 = rt[:, 2:3], rt[:, 3:4]
        for c in range(ROW_TILES):
            lanes = slice(c * LANES, (c + 1) * LANES)
            tile_rows = pl.ds(r0 * ROW_TILES + c, DMA_GROUP, stride=ROW_TILES)
            o_ref[rows, lanes] = h_ref[rows, lanes] + (ybuf[slot, 0, tile_rows, :] * w0 + ybuf[slot, 1, tile_rows, :] * w1)
        return carry

    lax.fori_loop(0, tt // DMA_GROUP, group, 0)
    if final_norm:
        o_ref[...] = _rms(o_ref[...], g_ref[...])

    @pl.when(i == last)
    def _():
        wait_slot(1 - slot)


def _combine(dest, ybuf, h1, rt, g, tt, final_norm):
    t = h1.shape[0]
    nsteps = t // tt
    return pl.pallas_call(
        functools.partial(_combine_kernel, tt=tt, final_norm=final_norm),
        out_shape=jax.ShapeDtypeStruct((t, D_MODEL), F32),
        grid=(nsteps,),
        in_specs=[pl.BlockSpec((TOP_K, tt), lambda i: (0, i), memory_space=pltpu.SMEM),
                  pl.BlockSpec((TOP_K, tt), lambda i: (0, jnp.minimum(i + 1, nsteps - 1)), memory_space=pltpu.SMEM),
                  pl.BlockSpec(memory_space=pl.ANY),
                  pl.BlockSpec((tt, D_MODEL), lambda i: (i, 0)),
                  pl.BlockSpec((tt, LANES), lambda i: (i, 0)),
                  pl.BlockSpec((1, D_MODEL), lambda i: (0, 0))],
        out_specs=pl.BlockSpec((tt, D_MODEL), lambda i: (i, 0)),
        scratch_shapes=[pltpu.VMEM((2, TOP_K, tt * ROW_TILES, LANES), F32), pltpu.SemaphoreType.DMA((2,))],
        compiler_params=_cparams(("arbitrary",)),
        name="moe_combine",
    )(dest, dest, ybuf, h1, rt, g)


def _moe_rows(t):
    return ((t * TOP_K) // MOE_BM + N_EXPERTS) * MOE_BM


def _moe(h1, u2, rt, wg, wu, wd, layer, g_final, final_norm, xinit):
    rk, cnt = _rank(rt, 1024)
    counts = cnt[0, :N_EXPERTS].astype(jnp.int32)
    padded = (counts + MOE_BM - 1) // MOE_BM * MOE_BM
    pad_end = jnp.cumsum(padded)
    pad_start = pad_end - padded
    n_blocks = xinit.shape[0] // (MOE_BM * ROW_TILES)
    is_expert = rk[TOP_K:2 * TOP_K][None] == jnp.arange(N_EXPERTS, dtype=jnp.int32)[:, None, None]
    dest = jnp.sum(jnp.where(is_expert, pad_start[:, None, None], 0), axis=0) + rk[0:TOP_K]
    n_used = (pad_end[-1] // MOE_BM).astype(jnp.int32)
    blk = jnp.minimum(jnp.arange(n_blocks, dtype=jnp.int32), n_used - 1) * MOE_BM
    block_expert = jnp.minimum(jnp.sum(pad_end[None, :] <= blk[:, None], axis=1), N_EXPERTS - 1).astype(jnp.int32)
    xbuf = _dispatch(dest, u2, xinit, 256)
    ybuf = _ffn(block_expert, n_used.reshape(1), xbuf, wg, wu, wd, layer)
    return _combine(dest, ybuf, h1, rt, g_final, 256, final_norm), xbuf


def _qkv_kernel(x_ref, g_ref, w_ref, b_ref, pos_ref, freq_ref, sp_ref, q_ref, kv_ref):
    u = _rms(x_ref[...], g_ref[...]).astype(BF16)
    ang = freq_ref[...] * pos_ref[...].astype(F32)
    spread = lambda v, m: sum(_dot_tn(t.astype(F32), m) for t in _split_terms(v, 3))
    cosv = jnp.cos(ang)
    sinv = jnp.sin(ang)
    c_coef = spread(cosv, sp_ref[0]) + sp_ref[3, 0:1, :]
    s_lo = spread(sinv, sp_ref[1])
    s_hi = spread(sinv, sp_ref[2])

    def rotate(z):
        return z * c_coef + pltpu.roll(z, LANES - ROT_DIM // 2, 1) * s_lo + pltpu.roll(z, ROT_DIM // 2, 1) * s_hi

    scale = SWA_HEAD_DIM ** -0.5
    for j in range(Q_W // LANES):
        sl = slice(j * LANES, (j + 1) * LANES)
        q_ref[:, sl] = (rotate(_dot(u, w_ref[:, sl]) + b_ref[:, sl]) * scale).astype(q_ref.dtype)
    for j in range(KV_OUT_W // LANES):
        sl = slice(Q_W + j * LANES, Q_W + (j + 1) * LANES)
        z = _dot(u, w_ref[:, sl]) + b_ref[:, sl]
        kv_ref[:, j * LANES:(j + 1) * LANES] = (rotate(z) if j % 2 == 0 else z).astype(kv_ref.dtype)


def _rot_tables():
    half = ROT_DIM // 2
    inv_freq = (ROPE_THETA ** (-jnp.arange(0, ROT_DIM, 2, dtype=F32) / ROT_DIM)).reshape(half, 1)
    d = np.arange(LANES) % SWA_HEAD_DIM
    f = np.arange(half)[:, None]
    sp = np.zeros((4, half, LANES), np.float32)
    sp[0] = (d[None, :] < ROT_DIM) & (d[None, :] % half == f)
    sp[1] = -((d[None, :] < half) & (d[None, :] == f)).astype(np.float32)
    sp[2] = (d[None, :] >= half) & (d[None, :] < ROT_DIM) & (d[None, :] - half == f)
    sp[3, 0] = d >= ROT_DIM
    return inv_freq, jnp.asarray(sp)


def _qkv(h, g, w, b, pos, tm):
    t = h.shape[0]
    row = lambda i: (i, 0)
    fixed = lambda i: (0, 0)
    wtot = Q_W + KV_OUT_W
    return pl.pallas_call(
        _qkv_kernel,
        out_shape=(jax.ShapeDtypeStruct((t, Q_W), BF16), jax.ShapeDtypeStruct((t, KV_OUT_W), BF16)),
        grid=(t // tm,),
        in_specs=[pl.BlockSpec((tm, D_MODEL), row), pl.BlockSpec((1, D_MODEL), fixed),
                  pl.BlockSpec((D_MODEL, wtot), fixed), pl.BlockSpec((1, wtot), fixed),
                  pl.BlockSpec((1, tm), lambda i: (0, i)), pl.BlockSpec((ROT_DIM // 2, 1), fixed),
                  pl.BlockSpec((4, ROT_DIM // 2, LANES), lambda i: (0, 0, 0))],
        out_specs=(pl.BlockSpec((tm, Q_W), row), pl.BlockSpec((tm, KV_OUT_W), row)),
        compiler_params=_cparams(("parallel",)),
        name="qkv_rotary",
    )(h, g, w, b, pos, *_rot_tables())


def _swa_kernel(q_ref, kvc_ref, kvp_ref, sink_ref, o_ref):
    neg = jnp.where(pl.program_id(1) > 0, 0.0, -jnp.inf).astype(F32)
    kj = lax.broadcasted_iota(jnp.int32, (WINDOW, WINDOW), 0)
    qi = lax.broadcasted_iota(jnp.int32, (WINDOW, WINDOW), 1)
    mask_cur = kj <= qi
    top = kj < SWA_HEAD_DIM
    lane = lax.broadcasted_iota(jnp.int32, (1, LANES), 1)
    keep_lo = jnp.where(lane < SWA_HEAD_DIM, 1.0, 0.0).astype(BF16)
    keep_hi = jnp.where(lane < SWA_HEAD_DIM, 0.0, 1.0).astype(BF16)
    zeros_half = jnp.zeros((SWA_HEAD_DIM, WINDOW), F32)

    def arranged(ref):
        k_nat, k_swp = ref[:, 0:KV_W], ref[:, 2 * KV_W:3 * KV_W]
        vt = jnp.transpose(ref[:, KV_W:2 * KV_W].astype(F32))
        vt_top = lambda g: jnp.concatenate([vt[g * SWA_HEAD_DIM:(g + 1) * SWA_HEAD_DIM], zeros_half], axis=0).astype(BF16)
        vt_bot = lambda g: jnp.concatenate([zeros_half, vt[g * SWA_HEAD_DIM:(g + 1) * SWA_HEAD_DIM]], axis=0).astype(BF16)
        return {(0, 0): (k_nat * keep_lo, vt_top(0)), (0, 1): (k_swp * keep_hi, vt_bot(0)),
                (1, 0): (k_swp * keep_lo, vt_top(1)), (1, 1): (k_nat * keep_hi, vt_bot(1))}

    cur = arranged(kvc_ref)
    prev = arranged(kvp_ref)
    heads = range(SWA_Q_HEADS)
    key = lambda h: (h // SWA_GROUP, h % 2)
    scores = []
    for h in heads:
        qp = q_ref[:, (h // 2) * LANES:(h // 2 + 1) * LANES]
        sc = _dot_nt(cur[key(h)][0], qp)
        sp = _dot_nt(prev[key(h)][0], qp)
        scores.append(jnp.where(mask_cur, sc, sp + neg))
    sinks = [sink_ref[:, h:h + 1] for h in heads]
    maxes = [jnp.maximum(jnp.max(scores[h], axis=0, keepdims=True), sinks[h]) for h in heads]
    probs = [jnp.exp(scores[h] - maxes[h]) for h in heads]
    denoms = [jnp.sum(probs[h], axis=0, keepdims=True) + jnp.exp(sinks[h] - maxes[h]) for h in heads]
    for pr in range(SWA_Q_HEADS // 2):
        num = None
        for h in (2 * pr, 2 * pr + 1):
            pc = jnp.where(mask_cur, probs[h], 0.0).astype(BF16)
            pp = jnp.where(mask_cur, 0.0, probs[h]).astype(BF16)
            part = _dot(cur[key(h)][1], pc) + _dot(prev[key(h)][1], pp)
            num = part if num is None else num + part
        den = jnp.where(top, denoms[2 * pr], denoms[2 * pr + 1])
        o_ref[:, pr * LANES:(pr + 1) * LANES] = jnp.transpose(num / den).astype(o_ref.dtype)


def _swa(q, kv, sinks, bsz, seq):
    nb = seq // WINDOW
    return pl.pallas_call(
        _swa_kernel,
        out_shape=jax.ShapeDtypeStruct((bsz * seq, Q_W), BF16),
        grid=(bsz, nb),
        in_specs=[pl.BlockSpec((WINDOW, Q_W), lambda b, n: (b * nb + n, 0)),
                  pl.BlockSpec((WINDOW, KV_OUT_W), lambda b, n: (b * nb + n, 0)),
                  pl.BlockSpec((WINDOW, KV_OUT_W), lambda b, n: (b * nb + jnp.maximum(n - 1, 0), 0)),
                  pl.BlockSpec((1, LANES), lambda b, n: (0, 0))],
        out_specs=pl.BlockSpec((WINDOW, Q_W), lambda b, n: (b * nb + n, 0)),
        compiler_params=_cparams(("parallel", "parallel")),
        name="swa_attention",
    )(q, kv, kv, sinks)


def _pack_in_proj(w_in):
    offs = np.cumsum((0,) + AB_SPLITS)
    part = lambda i: w_in[:, offs[i]:offs[i + 1]]
    w_main = jnp.concatenate([part(0), part(1), part(2), part(5), part(6), part(7), part(8)], axis=1).astype(BF16)
    small = jnp.concatenate([part(3), part(4), part(9)], axis=1)
    w_small = jnp.pad(small, ((0, 0), (0, LANES - small.shape[1]))).astype(BF16)
    return w_main, w_small


def _pack_qkv(w, b):
    hd = SWA_HEAD_DIM
    k0, k1 = slice(Q_W, Q_W + hd), slice(Q_W + hd, Q_W + 2 * hd)
    cols = lambda a: jnp.concatenate([a[..., :Q_W + 2 * KV_W], a[..., k1], a[..., k0]], axis=-1)
    return cols(w).astype(BF16), cols(b).reshape(1, -1).astype(F32)


def _lane_row(v, offset=0):
    return jnp.zeros((1, LANES), F32).at[0, offset:offset + v.shape[0]].set(v.astype(F32))


def kernel(x, positions, mix_norm, ffn_norm, final_norm, ab_w_in, ab_conv_w, ab_conv_b, ml_igate_b, ml_fgate_b, ml_head_norm, gla_w_lr_up, gla_gate_b, gla_head_norm, ab_w_out, swa_w_qkv, swa_b_qkv, swa_sinks, swa_w_o, swa_b_o, router_group_w, router_group_b, router_expert_w, router_expert_b, expert_w_gate, expert_w_up, expert_w_down):
    bsz, seq, d = x.shape
    t = bsz * seq
    h = x.reshape(t, d)
    row = lambda v: v.reshape(1, -1).astype(F32)

    def router_params(layer):
        wr = jnp.zeros((d, LANES), F32).at[:, :N_GROUPS].set(router_group_w[layer])
        wr = wr.at[:, N_GROUPS:N_GROUPS + N_EXPERTS].set(router_expert_w[layer])
        br = jnp.zeros((1, LANES), F32).at[0, :N_GROUPS].set(router_group_b[layer])
        br = br.at[0, N_GROUPS:N_GROUPS + N_EXPERTS].set(router_expert_b[layer])
        w_hi = wr.astype(BF16)
        w_lo = (wr - w_hi.astype(F32)).astype(BF16)
        return jnp.concatenate([w_hi, w_lo], axis=1), br

    def experts(layer):
        return expert_w_gate, expert_w_up, expert_w_down, layer

    w_main, w_small = _pack_in_proj(ab_w_in[0])
    zm, zs = _in_proj(h, row(mix_norm[0]), w_main, w_small, 512)
    lrup = jnp.zeros((LANES, GLA_QK_W), F32).at[SM_LR:SM_LR + GLA_LOWRANK].set(gla_w_lr_up[0]).astype(BF16)
    y = _mixer(zm, zs, ab_conv_w[0], row(ab_conv_b[0]), _lane_row(ml_igate_b[0]), _lane_row(ml_fgate_b[0]), lrup,
               row(gla_gate_b[0]), row(ml_head_norm[0]), row(gla_head_norm[0]), bsz, seq, 256)
    wr, br = router_params(0)
    h1, u2, rt = _proj_router(y, ab_w_out[0].astype(BF16), jnp.zeros((1, d), F32), h, row(ffn_norm[0]), wr, br, 512)
    h, xbuf = _moe(h1, u2, rt, *experts(0), row(final_norm), False, jnp.zeros((_moe_rows(t) * ROW_TILES, LANES), F32))

    w_qkv, b_qkv = _pack_qkv(swa_w_qkv[0], swa_b_qkv[0])
    q, kv = _qkv(h, row(mix_norm[1]), w_qkv, b_qkv, positions.reshape(1, t), 512)
    o = _swa(q, kv, _lane_row(swa_sinks[0]), bsz, seq)
    wr, br = router_params(1)
    h1, u2, rt = _proj_router(o, swa_w_o[0].astype(BF16), row(swa_b_o[0]), h, row(ffn_norm[1]), wr, br, 512)
    out, _ = _moe(h1, u2, rt, *experts(1), row(final_norm), True, xbuf)
    return out.reshape(bsz, seq, d)
```

```python
import functools

import jax
import jax.numpy as jnp
import numpy as np
from jax import lax
from jax.experimental import pallas as pl
from jax.experimental.pallas import tpu as pltpu

F32 = jnp.float32
BF16 = jnp.bfloat16
HIGHEST = lax.Precision.HIGHEST

D_MODEL = 1024
EPS = 1e-6
ML_HEADS = 4
ML_DK = 64
ML_DV = 128
GLA_HEADS = 4
GLA_DK = 64
GLA_DV = 128
CHUNK = 64
CONV_K = 4
GLA_LOWRANK = 16
GLA_TAU = 16.0
ML_QK_W = ML_HEADS * ML_DK
ML_V_W = ML_HEADS * ML_DV
GLA_QK_W = GLA_HEADS * GLA_DK
GLA_V_W = GLA_HEADS * GLA_DV
AB_SPLITS = (2 * ML_QK_W, ML_V_W, ML_V_W, ML_HEADS, ML_HEADS, GLA_QK_W, GLA_QK_W, GLA_V_W, GLA_V_W, GLA_LOWRANK)
OFF_MQK = 0
OFF_MV = OFF_MQK + 2 * ML_QK_W
OFF_MO = OFF_MV + ML_V_W
OFF_GQ = OFF_MO + ML_V_W
OFF_GK = OFF_GQ + GLA_QK_W
OFF_GV = OFF_GK + GLA_QK_W
OFF_GG = OFF_GV + GLA_V_W
Z_MAIN_W = OFF_GG + GLA_V_W
SM_I = 0
SM_F = ML_HEADS
SM_LR = 2 * ML_HEADS
LANES = 128
SWA_Q_HEADS = 16
SWA_KV_HEADS = 2
SWA_HEAD_DIM = 64
SWA_GROUP = SWA_Q_HEADS // SWA_KV_HEADS
WINDOW = 128
ROT_DIM = SWA_HEAD_DIM // 4
ROPE_THETA = 500000.0
Q_W = SWA_Q_HEADS * SWA_HEAD_DIM
KV_W = SWA_KV_HEADS * SWA_HEAD_DIM
KV_OUT_W = 3 * KV_W
N_GROUPS = 4
EXPERTS_PER_GROUP = 8
N_EXPERTS = N_GROUPS * EXPERTS_PER_GROUP
TOP_K = 2
D_FF = 512
MOE_BM = 512
ROW_TILES = D_MODEL // LANES

VMEM_LIMIT = 56 * 1024 * 1024


def _cparams(sem):
    return pltpu.CompilerParams(dimension_semantics=sem, vmem_limit_bytes=VMEM_LIMIT)


def _rms(x, g):
    return x * lax.rsqrt(jnp.mean(x * x, axis=-1, keepdims=True) + EPS) * g


def _log_sigmoid(x):
    return jnp.minimum(x, 0.0) - jnp.log1p(jnp.exp(-jnp.abs(x)))


def _sigmoid(x):
    return 1.0 / (1.0 + jnp.exp(-x))


def _dot(a, b):
    return jnp.dot(a, b, preferred_element_type=F32)


def _dot_nt(a, b):
    return lax.dot_general(a, b, (((1,), (1,)), ((), ())), preferred_element_type=F32)


def _dot_tn(a, b):
    return lax.dot_general(a, b, (((0,), (0,)), ((), ())), preferred_element_type=F32)


def _in_proj_kernel(x_ref, g_ref, w_ref, ws_ref, zm_ref, zs_ref, *, n_chunk):
    u = _rms(x_ref[...], g_ref[...]).astype(BF16)
    for n0 in range(0, w_ref.shape[1], n_chunk):
        zm_ref[:, n0:n0 + n_chunk] = _dot(u, w_ref[:, n0:n0 + n_chunk]).astype(zm_ref.dtype)
    zs_ref[...] = _dot(u, ws_ref[...])


def _in_proj(h, g, w_main, w_small, tm):
    t = h.shape[0]
    return pl.pallas_call(
        functools.partial(_in_proj_kernel, n_chunk=768),
        out_shape=(jax.ShapeDtypeStruct((t, Z_MAIN_W), BF16), jax.ShapeDtypeStruct((t, LANES), F32)),
        grid=(t // tm,),
        in_specs=[pl.BlockSpec((tm, D_MODEL), lambda i: (i, 0)),
                  pl.BlockSpec((1, D_MODEL), lambda i: (0, 0)),
                  pl.BlockSpec((D_MODEL, Z_MAIN_W), lambda i: (0, 0)),
                  pl.BlockSpec((D_MODEL, LANES), lambda i: (0, 0))],
        out_specs=(pl.BlockSpec((tm, Z_MAIN_W), lambda i: (i, 0)),
                   pl.BlockSpec((tm, LANES), lambda i: (i, 0))),
        compiler_params=_cparams(("parallel",)),
        name="in_proj",
    )(h, g, w_main, w_small)


def _split_terms(x, n):
    terms = []
    for _ in range(n):
        t = x.astype(BF16)
        terms.append(t)
        x = x - t.astype(F32)
    return terms


def _dot01_left(m01, x, n):
    return sum(_dot(m01, t) for t in _split_terms(x, n))


def _dot01_right(x, m01, n):
    return sum(_dot(t, m01) for t in _split_terms(x, n))


def _mixer_kernel(zm_ref, zs_ref, convw_ref, convb_ref, igb_ref, fgb_ref, lrup_ref, gateb_ref, mlnorm_ref,
                  glanorm_ref, y_ref, xpad, q_s, k_s, e_pad, hml_s, hgla_s, cn_s, m_s, st_s, *, tb):
    nc = tb // CHUNK

    @pl.when(pl.program_id(1) == 0)
    def _():
        xpad[0:8, :] = jnp.zeros((8, 2 * ML_QK_W), F32)
        cn_s[...] = jnp.zeros(cn_s.shape, F32)
        st_s[...] = jnp.zeros(st_s.shape, F32)
        m_s[...] = jnp.full(m_s.shape, -jnp.inf, F32)

    xpad[8:8 + tb, :] = zm_ref[:, OFF_MQK:OFF_MQK + 2 * ML_QK_W].astype(F32)
    conv = convb_ref[...] + convw_ref[3:4, :] * xpad[8:8 + tb, :]
    for j in range(CONV_K - 1):
        conv = conv + convw_ref[j:j + 1, :] * xpad[5 + j:5 + j + tb, :]
    xpad[0:8, :] = xpad[tb:tb + 8, :]
    qk = conv * _sigmoid(conv)
    q_s[...] = qk[:, :ML_QK_W]
    k_s[...] = qk[:, ML_QK_W:] * (ML_DK ** -0.5)

    r_t = lax.broadcasted_iota(jnp.int32, (tb, tb), 0)
    c_t = lax.broadcasted_iota(jnp.int32, (tb, tb), 1)
    tri_blk = jnp.where((r_t // CHUNK == c_t // CHUNK) & (c_t <= r_t), 1.0, 0.0).astype(BF16)
    r_e = lax.broadcasted_iota(jnp.int32, (LANES, 2 * LANES), 0)
    c_e = lax.broadcasted_iota(jnp.int32, (LANES, 2 * LANES), 1)
    spread_dk = jnp.where(c_e // ML_DK == r_e, 1.0, 0.0).astype(BF16)
    mean_dv = jnp.full((ML_DV, ML_DV), 1.0 / ML_DV, F32).astype(BF16)
    row_c = lax.broadcasted_iota(jnp.int32, (CHUNK, CHUNK), 0)
    col_c = lax.broadcasted_iota(jnp.int32, (CHUNK, CHUNK), 1)
    causal = col_c <= row_c
    lane_c = lax.broadcasted_iota(jnp.int32, (CHUNK, LANES), 1)
    ones_dv = jnp.ones((CHUNK, ML_DV), BF16)
    chunk_rows = lambda c: slice(c * CHUNK, (c + 1) * CHUNK)

    zs = zs_ref[...]
    ig = zs + igb_ref[...]
    lf = _log_sigmoid(pltpu.roll(zs, LANES - SM_F, 1) + fgb_ref[...])
    bc = _dot01_left(tri_blk, lf, 3)
    bc3 = bc.reshape(nc, CHUNK, LANES)
    g3 = bc3[:, CHUNK - 1:CHUNK, :]
    a3 = g3 - bc3 + ig.reshape(nc, CHUNK, LANES)
    amax3 = jnp.max(a3, axis=1, keepdims=True)
    wa = jnp.exp(a3 - amax3).reshape(tb, LANES)
    m_run = m_s[...]
    m_prev, s_old, s_in = [], [], []
    for c in range(nc):
        m_new = jnp.maximum(g3[c] + m_run, amax3[c])
        m_prev.append(m_run)
        s_old.append(jnp.exp(g3[c] + m_run - m_new))
        s_in.append(jnp.exp(amax3[c] - m_new))
        m_run = m_new
    m_s[...] = m_run
    e_nat = ig - bc
    e_pad[0:CHUNK, :] = jnp.full((CHUNK, LANES), -jnp.inf, F32)
    e_pad[CHUNK:CHUNK + tb, :] = e_nat
    pos = lax.broadcasted_iota(jnp.int32, (tb, LANES), 0) % CHUNK
    shift = 1
    while shift < CHUNK:
        shifted = e_pad[CHUNK - shift:CHUNK - shift + tb, :]
        e_pad[CHUNK:CHUNK + tb, :] = jnp.maximum(e_pad[CHUNK:CHUNK + tb, :], jnp.where(pos >= shift, shifted, -jnp.inf))
        shift *= 2
    m_intra = bc + e_pad[CHUNK:CHUNK + tb, :]
    il = jnp.concatenate([bc3[c] + m_prev[c] for c in range(nc)], axis=0)
    mt = jnp.maximum(il, m_intra)
    s_inter = jnp.exp(il - mt)
    exp_neg = jnp.exp(-mt)
    xn = bc - mt
    e_up = pltpu.roll(e_nat, ML_HEADS, 1)
    wa_x = _dot01_right(wa, spread_dk, 2)
    si_x = _dot01_right(s_inter, spread_dk, 2)
    q_all = q_s[...]
    k_all = k_s[...]
    kw_b = (k_all * wa_x).astype(BF16)
    qs_b = (q_all * si_x).astype(BF16)
    q_b = q_all.astype(BF16)
    k_b = k_all.astype(BF16)

    qkm, upd, vo = {}, {}, {}
    for c in range(nc):
        rows = chunk_rows(c)
        for h in range(ML_HEADS):
            dk = slice(h * ML_DK, (h + 1) * ML_DK)
            x_t = jnp.where(lane_c == h, xn[rows], jnp.where(lane_c == h + ML_HEADS, 1.0, 0.0))
            y_t = jnp.where(lane_c == h, 1.0, jnp.where(lane_c == h + ML_HEADS, e_up[rows], 0.0))
            d = lax.dot_general(x_t, y_t, (((1,), (1,)), ((), ())), precision=HIGHEST, preferred_element_type=F32)
            p = jnp.where(causal, jnp.exp(d), 0.0)
            qkm[c, h] = (_dot_nt(q_b[rows, dk], k_b[rows, dk]) * p).astype(BF16)
            vo[c, h] = jnp.concatenate([zm_ref[rows, OFF_MV + h * ML_DV:OFF_MV + (h + 1) * ML_DV], ones_dv], axis=1)
            upd[c, h] = _dot_tn(kw_b[rows, dk], vo[c, h])
    for h in range(ML_HEADS):
        dk = slice(h * ML_DK, (h + 1) * ML_DK)
        dv = slice(h * ML_DV, (h + 1) * ML_DV)
        cn = cn_s[h]
        for c in range(nc):
            rows = chunk_rows(c)
            res = _dot(qs_b[rows, dk], cn.astype(BF16)) + _dot(qkm[c, h], vo[c, h])
            num, den = res[:, :ML_DV], res[:, ML_DV:]
            hml_s[rows, dv] = num / jnp.maximum(jnp.abs(den), exp_neg[rows, h:h + 1])
            cn = s_old[c][:, h:h + 1] * cn + s_in[c][:, h:h + 1] * upd[c, h]
        cn_s[h] = cn

    la = _log_sigmoid(_dot(zs.astype(BF16), lrup_ref[...]) + gateb_ref[...]) * (1.0 / GLA_TAU)
    bcg = _dot01_left(tri_blk, la, 3)
    bcg3 = bcg.reshape(nc, CHUNK, GLA_QK_W)
    gg3 = bcg3[:, CHUNK - 1:CHUNK, :]
    gq = zm_ref[:, OFF_GQ:OFF_GQ + GLA_QK_W].astype(F32)
    gk = zm_ref[:, OFF_GK:OFF_GK + GLA_QK_W].astype(F32) * (GLA_DK ** -0.5)
    q_dec = (gq * jnp.exp(bcg)).astype(BF16)
    k_inv = (gk * jnp.exp(-bcg)).astype(BF16)
    k_end = (gk * jnp.exp(gg3 - bcg3).reshape(tb, GLA_QK_W)).astype(BF16)
    eg3 = jnp.exp(gg3)
    att, updg = {}, {}
    for c in range(nc):
        rows = chunk_rows(c)
        for h in range(GLA_HEADS):
            dk = slice(h * GLA_DK, (h + 1) * GLA_DK)
            vh = zm_ref[rows, OFF_GV + h * GLA_DV:OFF_GV + (h + 1) * GLA_DV]
            att[c, h] = jnp.where(causal, _dot_nt(q_dec[rows, dk], k_inv[rows, dk]), 0.0).astype(BF16)
            updg[c, h] = _dot_tn(vh, k_end[rows, dk])
    for h in range(GLA_HEADS):
        dk = slice(h * GLA_DK, (h + 1) * GLA_DK)
        st = st_s[h]
        for c in range(nc):
            rows = chunk_rows(c)
            vh = zm_ref[rows, OFF_GV + h * GLA_DV:OFF_GV + (h + 1) * GLA_DV]
            hgla_s[rows, h * GLA_DV:(h + 1) * GLA_DV] = _dot_nt(q_dec[rows, dk], st.astype(BF16)) + _dot(att[c, h], vh)
            st = st * eg3[c][:, dk] + updg[c, h]
        st_s[h] = st

    mean = lambda x: _dot01_right(x, mean_dv, 2)
    for h in range(ML_HEADS):
        sl = slice(h * ML_DV, (h + 1) * ML_DV)
        hh = hml_s[:, sl]
        d = hh - mean(hh)
        hn = d * lax.rsqrt(mean(d * d) + EPS)
        og = zm_ref[:, OFF_MO + h * ML_DV:OFF_MO + (h + 1) * ML_DV].astype(F32)
        y_ref[:, sl] = (hn * mlnorm_ref[:, sl] * _sigmoid(og)).astype(y_ref.dtype)
    for h in range(GLA_HEADS):
        sl = slice(h * GLA_DV, (h + 1) * GLA_DV)
        o = hgla_s[:, sl]
        on = o * lax.rsqrt(mean(o * o) + EPS)
        gg = zm_ref[:, OFF_GG + h * GLA_DV:OFF_GG + (h + 1) * GLA_DV].astype(F32)
        y_ref[:, ML_V_W + h * GLA_DV:ML_V_W + (h + 1) * GLA_DV] = (on * glanorm_ref[:, sl] * (gg * _sigmoid(gg))).astype(y_ref.dtype)


def _mixer(zm, zs, convw, convb, igb, fgb, lrup, gateb, mlnorm, glanorm, bsz, seq, tb):
    nt = seq // tb
    const = lambda shape: pl.BlockSpec(shape, lambda b, i: (0,) * len(shape))
    return pl.pallas_call(
        functools.partial(_mixer_kernel, tb=tb),
        out_shape=jax.ShapeDtypeStruct((bsz * seq, ML_V_W + GLA_V_W), BF16),
        grid=(bsz, nt),
        in_specs=[pl.BlockSpec((tb, Z_MAIN_W), lambda b, i: (b * nt + i, 0)),
                  pl.BlockSpec((tb, LANES), lambda b, i: (b * nt + i, 0)),
                  const((CONV_K, 2 * ML_QK_W)), const((1, 2 * ML_QK_W)), const((1, LANES)), const((1, LANES)),
                  const((LANES, GLA_QK_W)), const((1, GLA_QK_W)), const((1, ML_V_W)), const((1, GLA_V_W))],
        out_specs=pl.BlockSpec((tb, ML_V_W + GLA_V_W), lambda b, i: (b * nt + i, 0)),
        scratch_shapes=[pltpu.VMEM((tb + 8, 2 * ML_QK_W), F32),
                        pltpu.VMEM((tb, ML_QK_W), F32), pltpu.VMEM((tb, ML_QK_W), F32),
                        pltpu.VMEM((tb + CHUNK, LANES), F32),
                        pltpu.VMEM((tb, ML_V_W), F32), pltpu.VMEM((tb, GLA_V_W), F32),
                        pltpu.VMEM((ML_HEADS, ML_DK, ML_DV + LANES), F32),
                        pltpu.VMEM((1, LANES), F32), pltpu.VMEM((GLA_HEADS, GLA_DV, GLA_DK), F32)],
        compiler_params=_cparams(("parallel", "arbitrary")),
        name="mlstm_gla",
    )(zm, zs, convw, convb, igb, fgb, lrup, gateb, mlnorm, glanorm)


def _proj_router_kernel(y_ref, w_ref, b_ref, h_ref, g_ref, wr_ref, br_ref, h1_ref, u2_ref, rt_ref):
    h1 = h_ref[...] + (_dot(y_ref[...], w_ref[...]) + b_ref[...])
    h1_ref[...] = h1
    u2 = _rms(h1, g_ref[...])
    for c in range(ROW_TILES):
        u2_ref[pl.ds(c, u2.shape[0], stride=ROW_TILES), :] = u2[:, c * LANES:(c + 1) * LANES]
    u_hi = u2.astype(BF16)
    u_lo = (u2 - u_hi.astype(F32)).astype(BF16)
    part = _dot(u_hi, wr_ref[...])
    logits = part[:, :LANES] + (part[:, LANES:] + _dot(u_lo, wr_ref[:, :LANES])) + br_ref[...]
    lane = lax.broadcasted_iota(jnp.int32, logits.shape, 1)
    lane_f = lane.astype(F32)
    big = float(LANES)
    gl = jnp.where(lane < N_GROUPS, logits, -jnp.inf)
    g_max = jnp.max(gl, axis=-1, keepdims=True)
    g_idx = jnp.min(jnp.where(gl == g_max, lane_f, big), axis=-1, keepdims=True)
    g_p = 1.0 / jnp.sum(jnp.exp(gl - g_max), axis=-1, keepdims=True)
    e_grp = ((lane - N_GROUPS) // EXPERTS_PER_GROUP).astype(F32)
    in_grp = (lane >= N_GROUPS) & (lane < N_GROUPS + N_EXPERTS) & (e_grp == g_idx)
    el = jnp.where(in_grp, logits, -jnp.inf)
    t1 = jnp.max(el, axis=-1, keepdims=True)
    i1 = jnp.min(jnp.where(el == t1, lane_f, big), axis=-1, keepdims=True)
    el2 = jnp.where(lane_f == i1, -jnp.inf, el)
    t2 = jnp.max(el2, axis=-1, keepdims=True)
    i2 = jnp.min(jnp.where(el2 == t2, lane_f, big), axis=-1, keepdims=True)
    e21 = jnp.exp(t2 - t1)
    p1 = 1.0 / (1.0 + e21)
    rt = jnp.where(lane == 0, i1 - N_GROUPS,
                   jnp.where(lane == 1, i2 - N_GROUPS,
                             jnp.where(lane == 2, g_p * p1, jnp.where(lane == 3, g_p * (e21 * p1), 0.0))))
    rt_ref[...] = rt


def _proj_router(y, w, b, h, g, wr, br, tm):
    t, kdim = y.shape
    row = lambda i: (i, 0)
    fixed = lambda i: (0, 0)
    return pl.pallas_call(
        _proj_router_kernel,
        out_shape=(jax.ShapeDtypeStruct((t, D_MODEL), F32), jax.ShapeDtypeStruct((t * ROW_TILES, LANES), F32),
                   jax.ShapeDtypeStruct((t, LANES), F32)),
        grid=(t // tm,),
        in_specs=[pl.BlockSpec((tm, kdim), row), pl.BlockSpec((kdim, D_MODEL), fixed),
                  pl.BlockSpec((1, D_MODEL), fixed), pl.BlockSpec((tm, D_MODEL), row),
                  pl.BlockSpec((1, D_MODEL), fixed), pl.BlockSpec((D_MODEL, 2 * LANES), fixed),
                  pl.BlockSpec((1, LANES), fixed)],
        out_specs=(pl.BlockSpec((tm, D_MODEL), row), pl.BlockSpec((tm * ROW_TILES, LANES), row),
                   pl.BlockSpec((tm, LANES), row)),
        compiler_params=_cparams(("parallel",)),
        name="proj_router",
    )(y, w, b, h, g, wr, br)


def _rank_kernel(rt_ref, rk_ref, cnt_ref, base_s, strict_s):
    tt = rt_ref.shape[0]

    @pl.when(pl.program_id(0) == 0)
    def _():
        base_s[...] = jnp.zeros(base_s.shape, F32)
        r = lax.broadcasted_iota(jnp.int32, (tt, tt), 0)
        c = lax.broadcasted_iota(jnp.int32, (tt, tt), 1)
        strict_s[...] = jnp.where(c < r, 1.0, 0.0).astype(BF16)

    rt = rt_ref[...]
    lane = lax.broadcasted_iota(jnp.int32, rt.shape, 1)
    lane_f = lane.astype(F32)
    e0, e1 = rt[:, 0:1], rt[:, 1:2]
    oh0 = lane_f == e0
    oh1 = lane_f == e1
    oh = jnp.where(oh0 | oh1, 1.0, 0.0)
    before = _dot(strict_s[...], oh.astype(BF16)) + base_s[...]
    r0 = jnp.sum(jnp.where(oh0, before, 0.0), axis=-1, keepdims=True)
    r1 = jnp.sum(jnp.where(oh1, before, 0.0), axis=-1, keepdims=True)
    table = jnp.where(lane == 0, r0, jnp.where(lane == 1, r1, jnp.where(lane == 2, e0, jnp.where(lane == 3, e1, 0.0))))
    rk_ref[...] = jnp.transpose(table)[0:8, :].astype(jnp.int32)
    base_s[...] = base_s[...] + jnp.sum(oh, axis=0, keepdims=True)
    cnt_ref[...] = base_s[...]


def _rank(rt, tt):
    t = rt.shape[0]
    return pl.pallas_call(
        _rank_kernel,
        out_shape=(jax.ShapeDtypeStruct((8, t), jnp.int32), jax.ShapeDtypeStruct((1, LANES), F32)),
        grid=(t // tt,),
        in_specs=[pl.BlockSpec((tt, LANES), lambda i: (i, 0))],
        out_specs=(pl.BlockSpec((8, tt), lambda i: (0, i)), pl.BlockSpec((1, LANES), lambda i: (0, 0))),
        scratch_shapes=[pltpu.VMEM((1, LANES), F32), pltpu.VMEM((tt, tt), BF16)],
        compiler_params=_cparams(("arbitrary",)),
        name="expert_rank",
    )(rt)


DMA_GROUP = 8


def _row_copy(src, dst, sem):
    return pltpu.make_async_copy(src, dst, sem)


def _row_tile(r):
    return pl.ds(pl.multiple_of(r * ROW_TILES, ROW_TILES), ROW_TILES)


def _dispatch_kernel(dest_ref, u_ref, xin_hbm, xout_hbm, inv_ref, sem, *, tt):
    del xin_hbm
    i = pl.program_id(0)

    @pl.when(i == 0)
    def _():
        def fill(p, carry):
            inv_ref[p] = -1
            return carry
        lax.fori_loop(0, inv_ref.shape[0], fill, 0, unroll=8)

    def issue(g, carry):
        for jj in range(DMA_GROUP):
            j = g * DMA_GROUP + jj
            src = u_ref.at[_row_tile(j)]
            for k in range(TOP_K):
                d = dest_ref[k, j]
                _row_copy(src, xout_hbm.at[_row_tile(d)], sem).start(priority=k)
                inv_ref[d] = (i * tt + j) * TOP_K + k
        return carry

    lax.fori_loop(0, tt // DMA_GROUP, issue, 0)
    for k in range(TOP_K):
        _row_copy(u_ref, xout_hbm.at[pl.ds(0, tt * ROW_TILES)], sem).wait()


def _dispatch(dest, u2, xinit, tt):
    t = u2.shape[0] // ROW_TILES
    return pl.pallas_call(
        functools.partial(_dispatch_kernel, tt=tt),
        out_shape=(jax.ShapeDtypeStruct(xinit.shape, F32),
                   jax.ShapeDtypeStruct((xinit.shape[0] // ROW_TILES,), jnp.int32)),
        grid=(t // tt,),
        in_specs=[pl.BlockSpec((TOP_K, tt), lambda i: (0, i), memory_space=pltpu.SMEM),
                  pl.BlockSpec((tt * ROW_TILES, LANES), lambda i: (i, 0)), pl.BlockSpec(memory_space=pl.ANY)],
        out_specs=(pl.BlockSpec(memory_space=pl.ANY), pl.BlockSpec(memory_space=pltpu.SMEM)),
        scratch_shapes=[pltpu.SemaphoreType.DMA],
        input_output_aliases={2: 0},
        compiler_params=_cparams(("arbitrary",)),
        name="moe_dispatch",
    )(dest, u2, xinit)


def _ffn_kernel(be_ref, nu_ref, inv_ref, x_ref, wg_ref, wu_ref, wd_ref, o2_hbm, wg_s, wu_s, wd_s, x_s, y_s, sem, *, n_slots):
    b = pl.program_id(0)
    last = pl.num_programs(0) - 1
    n_used = nu_ref[0]
    block_rows = MOE_BM * ROW_TILES

    def drain():
        _row_copy(y_s.at[0], o2_hbm.at[pl.ds(0, block_rows)], sem).wait()

    def scatter(blk):
        s = blk % 2
        for j in range(MOE_BM):
            a = inv_ref[blk * MOE_BM + j]
            slot = jnp.where(a >= 0, a, n_slots + s * MOE_BM + j)
            _row_copy(y_s.at[s, _row_tile(j)], o2_hbm.at[_row_tile(slot)], sem).start(priority=j % 2)

    def compute():
        for c in range(ROW_TILES):
            x_s[:, c * LANES:(c + 1) * LANES] = x_ref[pl.ds(c, MOE_BM, stride=ROW_TILES), :].astype(BF16)
        x = x_s[...]
        a = _dot(x, wg_s[...])
        u = _dot(x, wu_s[...])
        y = _dot(((a * _sigmoid(a)) * u).astype(BF16), wd_s[...])
        for c in range(ROW_TILES):
            y_s[b % 2, pl.ds(c, MOE_BM, stride=ROW_TILES), :] = y[:, c * LANES:(c + 1) * LANES]

    @pl.when(b == 0)
    def _():
        y_s[...] = jnp.zeros(y_s.shape, F32)
        for s in range(2):
            _row_copy(y_s.at[s], o2_hbm.at[pl.ds((n_slots + s * MOE_BM) * ROW_TILES, block_rows)], sem).start()
        for s in range(2):
            drain()

    @pl.when((b >= 2) & (b - 2 < n_used))
    def _():
        drain()

    @pl.when((b == 0) | (be_ref[b] != be_ref[jnp.maximum(b - 1, 0)]))
    def _():
        wg_s[...] = wg_ref[...].astype(BF16)
        wu_s[...] = wu_ref[...].astype(BF16)
        wd_s[...] = wd_ref[...].astype(BF16)

    @pl.when(b == 0)
    def _():
        compute()

    @pl.when((b >= 1) & (b < n_used))
    def _():
        scatter(b - 1)
        compute()

    @pl.when((b >= 1) & (b >= n_used) & (b - 1 < n_used))
    def _():
        scatter(b - 1)

    @pl.when(b == last)
    def _():
        @pl.when((b >= 1) & (b - 1 < n_used))
        def _():
            drain()

        @pl.when(b < n_used)
        def _():
            scatter(b)
            drain()


def _ffn(block_expert, n_used, inv, xbuf, wg, wu, wd, layer, n_slots):
    n_blocks = xbuf.shape[0] // (MOE_BM * ROW_TILES)
    rows = lambda b, be, nu, inv: (jnp.minimum(b, nu[0] - 1), 0)
    wmap = lambda b, be, nu, inv: (layer, be[b], 0, 0)
    return pl.pallas_call(
        functools.partial(_ffn_kernel, n_slots=n_slots),
        out_shape=jax.ShapeDtypeStruct(((n_slots + 2 * MOE_BM) * ROW_TILES, LANES), F32),
        grid_spec=pltpu.PrefetchScalarGridSpec(
            num_scalar_prefetch=3,
            grid=(n_blocks,),
            in_specs=[pl.BlockSpec((MOE_BM * ROW_TILES, LANES), rows),
                      pl.BlockSpec((None, None, D_MODEL, D_FF), wmap),
                      pl.BlockSpec((None, None, D_MODEL, D_FF), wmap),
                      pl.BlockSpec((None, None, D_FF, D_MODEL), wmap)],
            out_specs=pl.BlockSpec(memory_space=pl.ANY),
            scratch_shapes=[pltpu.VMEM((D_MODEL, D_FF), BF16), pltpu.VMEM((D_MODEL, D_FF), BF16),
                            pltpu.VMEM((D_FF, D_MODEL), BF16), pltpu.VMEM((MOE_BM, D_MODEL), BF16),
                            pltpu.VMEM((2, MOE_BM * ROW_TILES, LANES), F32), pltpu.SemaphoreType.DMA]),
        compiler_params=_cparams(("arbitrary",)),
        name="moe_ffn",
    )(block_expert, n_used, inv, xbuf, wg, wu, wd)


def _combine_kernel(o2_ref, h_ref, rt_ref, g_ref, o_ref, *, final_norm):
    tt = h_ref.shape[0]
    rt = rt_ref[...]
    w0, w1 = rt[:, 2:3], rt[:, 3:4]
    for c in range(ROW_TILES):
        lanes = slice(c * LANES, (c + 1) * LANES)
        y0 = o2_ref[pl.ds(c, tt, stride=TOP_K * ROW_TILES), :]
        y1 = o2_ref[pl.ds(ROW_TILES + c, tt, stride=TOP_K * ROW_TILES), :]
        o_ref[:, lanes] = h_ref[:, lanes] + (y0 * w0 + y1 * w1)
    if final_norm:
        o_ref[...] = _rms(o_ref[...], g_ref[...])


def _combine(o2, h1, rt, g, tt, final_norm):
    t = h1.shape[0]
    return pl.pallas_call(
        functools.partial(_combine_kernel, final_norm=final_norm),
        out_shape=jax.ShapeDtypeStruct((t, D_MODEL), F32),
        grid=(t // tt,),
        in_specs=[pl.BlockSpec((tt * TOP_K * ROW_TILES, LANES), lambda i: (i, 0)),
                  pl.BlockSpec((tt, D_MODEL), lambda i: (i, 0)),
                  pl.BlockSpec((tt, LANES), lambda i: (i, 0)),
                  pl.BlockSpec((1, D_MODEL), lambda i: (0, 0))],
        out_specs=pl.BlockSpec((tt, D_MODEL), lambda i: (i, 0)),
        compiler_params=_cparams(("parallel",)),
        name="moe_combine",
    )(o2, h1, rt, g)


def _moe_rows(t):
    return ((t * TOP_K) // MOE_BM + N_EXPERTS) * MOE_BM


def _moe(h1, u2, rt, wg, wu, wd, layer, g_final, final_norm, xinit):
    rk, cnt = _rank(rt, 1024)
    counts = cnt[0, :N_EXPERTS].astype(jnp.int32)
    padded = (counts + MOE_BM - 1) // MOE_BM * MOE_BM
    pad_end = jnp.cumsum(padded)
    pad_start = pad_end - padded
    n_blocks = xinit.shape[0] // (MOE_BM * ROW_TILES)
    is_expert = rk[TOP_K:2 * TOP_K][None] == jnp.arange(N_EXPERTS, dtype=jnp.int32)[:, None, None]
    dest = jnp.sum(jnp.where(is_expert, pad_start[:, None, None], 0), axis=0) + rk[0:TOP_K]
    n_used = (pad_end[-1] // MOE_BM).astype(jnp.int32)
    blk = jnp.minimum(jnp.arange(n_blocks, dtype=jnp.int32), n_used - 1) * MOE_BM
    block_expert = jnp.minimum(jnp.sum(pad_end[None, :] <= blk[:, None], axis=1), N_EXPERTS - 1).astype(jnp.int32)
    xbuf, inv = _dispatch(dest, u2, xinit, 256)
    o2 = _ffn(block_expert, n_used.reshape(1), inv, xbuf, wg, wu, wd, layer, h1.shape[0] * TOP_K)
    return _combine(o2, h1, rt, g_final, 256, final_norm), xbuf


def _qkv_kernel(x_ref, g_ref, w_ref, b_ref, pos_ref, freq_ref, sp_ref, q_ref, kv_ref):
    u = _rms(x_ref[...], g_ref[...]).astype(BF16)
    ang = freq_ref[...] * pos_ref[...].astype(F32)
    spread = lambda v, m: sum(_dot_tn(t.astype(F32), m) for t in _split_terms(v, 3))
    cosv = jnp.cos(ang)
    sinv = jnp.sin(ang)
    c_coef = spread(cosv, sp_ref[0]) + sp_ref[3, 0:1, :]
    s_lo = spread(sinv, sp_ref[1])
    s_hi = spread(sinv, sp_ref[2])

    def rotate(z):
        return z * c_coef + pltpu.roll(z, LANES - ROT_DIM // 2, 1) * s_lo + pltpu.roll(z, ROT_DIM // 2, 1) * s_hi

    scale = SWA_HEAD_DIM ** -0.5
    for j in range(Q_W // LANES):
        sl = slice(j * LANES, (j + 1) * LANES)
        q_ref[:, sl] = (rotate(_dot(u, w_ref[:, sl]) + b_ref[:, sl]) * scale).astype(q_ref.dtype)
    for j in range(KV_OUT_W // LANES):
        sl = slice(Q_W + j * LANES, Q_W + (j + 1) * LANES)
        z = _dot(u, w_ref[:, sl]) + b_ref[:, sl]
        kv_ref[:, j * LANES:(j + 1) * LANES] = (rotate(z) if j % 2 == 0 else z).astype(kv_ref.dtype)


def _rot_tables():
    half = ROT_DIM // 2
    inv_freq = (ROPE_THETA ** (-jnp.arange(0, ROT_DIM, 2, dtype=F32) / ROT_DIM)).reshape(half, 1)
    d = np.arange(LANES) % SWA_HEAD_DIM
    f = np.arange(half)[:, None]
    sp = np.zeros((4, half, LANES), np.float32)
    sp[0] = (d[None, :] < ROT_DIM) & (d[None, :] % half == f)
    sp[1] = -((d[None, :] < half) & (d[None, :] == f)).astype(np.float32)
    sp[2] = (d[None, :] >= half) & (d[None, :] < ROT_DIM) & (d[None, :] - half == f)
    sp[3, 0] = d >= ROT_DIM
    return inv_freq, jnp.asarray(sp)


def _qkv(h, g, w, b, pos, tm):
    t = h.shape[0]
    row = lambda i: (i, 0)
    fixed = lambda i: (0, 0)
    wtot = Q_W + KV_OUT_W
    return pl.pallas_call(
        _qkv_kernel,
        out_shape=(jax.ShapeDtypeStruct((t, Q_W), BF16), jax.ShapeDtypeStruct((t, KV_OUT_W), BF16)),
        grid=(t // tm,),
        in_specs=[pl.BlockSpec((tm, D_MODEL), row), pl.BlockSpec((1, D_MODEL), fixed),
                  pl.BlockSpec((D_MODEL, wtot), fixed), pl.BlockSpec((1, wtot), fixed),
                  pl.BlockSpec((1, tm), lambda i: (0, i)), pl.BlockSpec((ROT_DIM // 2, 1), fixed),
                  pl.BlockSpec((4, ROT_DIM // 2, LANES), lambda i: (0, 0, 0))],
        out_specs=(pl.BlockSpec((tm, Q_W), row), pl.BlockSpec((tm, KV_OUT_W), row)),
        compiler_params=_cparams(("parallel",)),
        name="qkv_rotary",
    )(h, g, w, b, pos, *_rot_tables())


def _swa_kernel(q_ref, kvc_ref, kvp_ref, sink_ref, o_ref):
    neg = jnp.where(pl.program_id(1) > 0, 0.0, -jnp.inf).astype(F32)
    kj = lax.broadcasted_iota(jnp.int32, (WINDOW, WINDOW), 0)
    qi = lax.broadcasted_iota(jnp.int32, (WINDOW, WINDOW), 1)
    mask_cur = kj <= qi
    top = kj < SWA_HEAD_DIM
    lane = lax.broadcasted_iota(jnp.int32, (1, LANES), 1)
    keep_lo = jnp.where(lane < SWA_HEAD_DIM, 1.0, 0.0).astype(BF16)
    keep_hi = jnp.where(lane < SWA_HEAD_DIM, 0.0, 1.0).astype(BF16)
    zeros_half = jnp.zeros((SWA_HEAD_DIM, WINDOW), F32)

    def arranged(ref):
        k_nat, k_swp = ref[:, 0:KV_W], ref[:, 2 * KV_W:3 * KV_W]
        vt = jnp.transpose(ref[:, KV_W:2 * KV_W].astype(F32))
        vt_top = lambda g: jnp.concatenate([vt[g * SWA_HEAD_DIM:(g + 1) * SWA_HEAD_DIM], zeros_half], axis=0).astype(BF16)
        vt_bot = lambda g: jnp.concatenate([zeros_half, vt[g * SWA_HEAD_DIM:(g + 1) * SWA_HEAD_DIM]], axis=0).astype(BF16)
        return {(0, 0): (k_nat * keep_lo, vt_top(0)), (0, 1): (k_swp * keep_hi, vt_bot(0)),
                (1, 0): (k_swp * keep_lo, vt_top(1)), (1, 1): (k_nat * keep_hi, vt_bot(1))}

    cur = arranged(kvc_ref)
    prev = arranged(kvp_ref)
    heads = range(SWA_Q_HEADS)
    key = lambda h: (h // SWA_GROUP, h % 2)
    scores = []
    for h in heads:
        qp = q_ref[:, (h // 2) * LANES:(h // 2 + 1) * LANES]
        sc = _dot_nt(cur[key(h)][0], qp)
        sp = _dot_nt(prev[key(h)][0], qp)
        scores.append(jnp.where(mask_cur, sc, sp + neg))
    sinks = [sink_ref[:, h:h + 1] for h in heads]
    maxes = [jnp.maximum(jnp.max(scores[h], axis=0, keepdims=True), sinks[h]) for h in heads]
    probs = [jnp.exp(scores[h] - maxes[h]) for h in heads]
    denoms = [jnp.sum(probs[h], axis=0, keepdims=True) + jnp.exp(sinks[h] - maxes[h]) for h in heads]
    for pr in range(SWA_Q_HEADS // 2):
        num = None
        for h in (2 * pr, 2 * pr + 1):
            pc = jnp.where(mask_cur, probs[h], 0.0).astype(BF16)
            pp = jnp.where(mask_cur, 0.0, probs[h]).astype(BF16)
            part = _dot(cur[key(h)][1], pc) + _dot(prev[key(h)][1], pp)
            num = part if num is None else num + part
        den = jnp.where(top, denoms[2 * pr], denoms[2 * pr + 1])
        o_ref[:, pr * LANES:(pr + 1) * LANES] = jnp.transpose(num / den).astype(o_ref.dtype)


def _swa(q, kv, sinks, bsz, seq):
    nb = seq // WINDOW
    return pl.pallas_call(
        _swa_kernel,
        out_shape=jax.ShapeDtypeStruct((bsz * seq, Q_W), BF16),
        grid=(bsz, nb),
        in_specs=[pl.BlockSpec((WINDOW, Q_W), lambda b, n: (b * nb + n, 0)),
                  pl.BlockSpec((WINDOW, KV_OUT_W), lambda b, n: (b * nb + n, 0)),
                  pl.BlockSpec((WINDOW, KV_OUT_W), lambda b, n: (b * nb + jnp.maximum(n - 1, 0), 0)),
                  pl.BlockSpec((1, LANES), lambda b, n: (0, 0))],
        out_specs=pl.BlockSpec((WINDOW, Q_W), lambda b, n: (b * nb + n, 0)),
        compiler_params=_cparams(("parallel", "parallel")),
        name="swa_attention",
    )(q, kv, kv, sinks)


def _pack_in_proj(w_in):
    offs = np.cumsum((0,) + AB_SPLITS)
    part = lambda i: w_in[:, offs[i]:offs[i + 1]]
    w_main = jnp.concatenate([part(0), part(1), part(2), part(5), part(6), part(7), part(8)], axis=1).astype(BF16)
    small = jnp.concatenate([part(3), part(4), part(9)], axis=1)
    w_small = jnp.pad(small, ((0, 0), (0, LANES - small.shape[1]))).astype(BF16)
    return w_main, w_small


def _pack_qkv(w, b):
    hd = SWA_HEAD_DIM
    k0, k1 = slice(Q_W, Q_W + hd), slice(Q_W + hd, Q_W + 2 * hd)
    cols = lambda a: jnp.concatenate([a[..., :Q_W + 2 * KV_W], a[..., k1], a[..., k0]], axis=-1)
    return cols(w).astype(BF16), cols(b).reshape(1, -1).astype(F32)


def _lane_row(v, offset=0):
    return jnp.zeros((1, LANES), F32).at[0, offset:offset + v.shape[0]].set(v.astype(F32))


def kernel(x, positions, mix_norm, ffn_norm, final_norm, ab_w_in, ab_conv_w, ab_conv_b, ml_igate_b, ml_fgate_b, ml_head_norm, gla_w_lr_up, gla_gate_b, gla_head_norm, ab_w_out, swa_w_qkv, swa_b_qkv, swa_sinks, swa_w_o, swa_b_o, router_group_w, router_group_b, router_expert_w, router_expert_b, expert_w_gate, expert_w_up, expert_w_down):
    bsz, seq, d = x.shape
    t = bsz * seq
    h = x.reshape(t, d)
    row = lambda v: v.reshape(1, -1).astype(F32)

    def router_params(layer):
        wr = jnp.zeros((d, LANES), F32).at[:, :N_GROUPS].set(router_group_w[layer])
        wr = wr.at[:, N_GROUPS:N_GROUPS + N_EXPERTS].set(router_expert_w[layer])
        br = jnp.zeros((1, LANES), F32).at[0, :N_GROUPS].set(router_group_b[layer])
        br = br.at[0, N_GROUPS:N_GROUPS + N_EXPERTS].set(router_expert_b[layer])
        w_hi = wr.astype(BF16)
        w_lo = (wr - w_hi.astype(F32)).astype(BF16)
        return jnp.concatenate([w_hi, w_lo], axis=1), br

    def experts(layer):
        return expert_w_gate, expert_w_up, expert_w_down, layer

    w_main, w_small = _pack_in_proj(ab_w_in[0])
    zm, zs = _in_proj(h, row(mix_norm[0]), w_main, w_small, 512)
    lrup = jnp.zeros((LANES, GLA_QK_W), F32).at[SM_LR:SM_LR + GLA_LOWRANK].set(gla_w_lr_up[0]).astype(BF16)
    y = _mixer(zm, zs, ab_conv_w[0], row(ab_conv_b[0]), _lane_row(ml_igate_b[0]), _lane_row(ml_fgate_b[0]), lrup,
               row(gla_gate_b[0]), row(ml_head_norm[0]), row(gla_head_norm[0]), bsz, seq, 256)
    wr, br = router_params(0)
    h1, u2, rt = _proj_router(y, ab_w_out[0].astype(BF16), jnp.zeros((1, d), F32), h, row(ffn_norm[0]), wr, br, 512)
    h, xbuf = _moe(h1, u2, rt, *experts(0), row(final_norm), False, jnp.zeros((_moe_rows(t) * ROW_TILES, LANES), F32))

    w_qkv, b_qkv = _pack_qkv(swa_w_qkv[0], swa_b_qkv[0])
    q, kv = _qkv(h, row(mix_norm[1]), w_qkv, b_qkv, positions.reshape(1, t), 512)
    o = _swa(q, kv, _lane_row(swa_sinks[0]), bsz, seq)
    wr, br = router_params(1)
    h1, u2, rt = _proj_router(o, swa_w_o[0].astype(BF16), row(swa_b_o[0]), h, row(ffn_norm[1]), wr, br, 512)
    out, _ = _moe(h1, u2, rt, *experts(1), row(final_norm), True, xbuf)
    return out.reshape(bsz, seq, d)
```

```python
import functools

import jax
import jax.numpy as jnp
import numpy as np
from jax import lax
from jax.experimental import pallas as pl
from jax.experimental.pallas import tpu as pltpu

F32 = jnp.float32
BF16 = jnp.bfloat16
HIGHEST = lax.Precision.HIGHEST

D_MODEL = 1024
EPS = 1e-6
ML_HEADS = 4
ML_DK = 64
ML_DV = 128
GLA_HEADS = 4
GLA_DK = 64
GLA_DV = 128
CHUNK = 64
CONV_K = 4
GLA_LOWRANK = 16
GLA_TAU = 16.0
ML_QK_W = ML_HEADS * ML_DK
ML_V_W = ML_HEADS * ML_DV
GLA_QK_W = GLA_HEADS * GLA_DK
GLA_V_W = GLA_HEADS * GLA_DV
AB_SPLITS = (2 * ML_QK_W, ML_V_W, ML_V_W, ML_HEADS, ML_HEADS, GLA_QK_W, GLA_QK_W, GLA_V_W, GLA_V_W, GLA_LOWRANK)
OFF_MQK = 0
OFF_MV = OFF_MQK + 2 * ML_QK_W
OFF_MO = OFF_MV + ML_V_W
OFF_GQ = OFF_MO + ML_V_W
OFF_GK = OFF_GQ + GLA_QK_W
OFF_GV = OFF_GK + GLA_QK_W
OFF_GG = OFF_GV + GLA_V_W
Z_MAIN_W = OFF_GG + GLA_V_W
SM_I = 0
SM_F = ML_HEADS
SM_LR = 2 * ML_HEADS
LANES = 128
SWA_Q_HEADS = 16
SWA_KV_HEADS = 2
SWA_HEAD_DIM = 64
SWA_GROUP = SWA_Q_HEADS // SWA_KV_HEADS
WINDOW = 128
ROT_DIM = SWA_HEAD_DIM // 4
ROPE_THETA = 500000.0
Q_W = SWA_Q_HEADS * SWA_HEAD_DIM
KV_W = SWA_KV_HEADS * SWA_HEAD_DIM
KV_OUT_W = 3 * KV_W
N_GROUPS = 4
EXPERTS_PER_GROUP = 8
N_EXPERTS = N_GROUPS * EXPERTS_PER_GROUP
TOP_K = 2
D_FF = 512
MOE_BM = 512
ROW_TILES = D_MODEL // LANES

VMEM_LIMIT = 56 * 1024 * 1024


def _cparams(sem):
    return pltpu.CompilerParams(dimension_semantics=sem, vmem_limit_bytes=VMEM_LIMIT)


def _rms(x, g):
    return x * lax.rsqrt(jnp.mean(x * x, axis=-1, keepdims=True) + EPS) * g


def _log_sigmoid(x):
    return jnp.minimum(x, 0.0) - jnp.log1p(jnp.exp(-jnp.abs(x)))


def _sigmoid(x):
    return 1.0 / (1.0 + jnp.exp(-x))


def _dot(a, b):
    return jnp.dot(a, b, preferred_element_type=F32)


def _dot_nt(a, b):
    return lax.dot_general(a, b, (((1,), (1,)), ((), ())), preferred_element_type=F32)


def _dot_tn(a, b):
    return lax.dot_general(a, b, (((0,), (0,)), ((), ())), preferred_element_type=F32)


def _in_proj_kernel(x_ref, g_ref, w_ref, ws_ref, zm_ref, zs_ref, *, n_chunk):
    u = _rms(x_ref[...], g_ref[...]).astype(BF16)
    for n0 in range(0, w_ref.shape[1], n_chunk):
        zm_ref[:, n0:n0 + n_chunk] = _dot(u, w_ref[:, n0:n0 + n_chunk]).astype(zm_ref.dtype)
    zs_ref[...] = _dot(u, ws_ref[...])


def _in_proj(h, g, w_main, w_small, tm):
    t = h.shape[0]
    return pl.pallas_call(
        functools.partial(_in_proj_kernel, n_chunk=768),
        out_shape=(jax.ShapeDtypeStruct((t, Z_MAIN_W), BF16), jax.ShapeDtypeStruct((t, LANES), F32)),
        grid=(t // tm,),
        in_specs=[pl.BlockSpec((tm, D_MODEL), lambda i: (i, 0)),
                  pl.BlockSpec((1, D_MODEL), lambda i: (0, 0)),
                  pl.BlockSpec((D_MODEL, Z_MAIN_W), lambda i: (0, 0)),
                  pl.BlockSpec((D_MODEL, LANES), lambda i: (0, 0))],
        out_specs=(pl.BlockSpec((tm, Z_MAIN_W), lambda i: (i, 0)),
                   pl.BlockSpec((tm, LANES), lambda i: (i, 0))),
        compiler_params=_cparams(("parallel",)),
        name="in_proj",
    )(h, g, w_main, w_small)


def _split_terms(x, n):
    terms = []
    for _ in range(n):
        t = x.astype(BF16)
        terms.append(t)
        x = x - t.astype(F32)
    return terms


def _dot01_left(m01, x, n):
    return sum(_dot(m01, t) for t in _split_terms(x, n))


def _dot01_right(x, m01, n):
    return sum(_dot(t, m01) for t in _split_terms(x, n))


def _mixer_kernel(zm_ref, zs_ref, convw_ref, convb_ref, igb_ref, fgb_ref, lrup_ref, gateb_ref, mlnorm_ref,
                  glanorm_ref, y_ref, xpad, q_s, k_s, e_pad, hml_s, hgla_s, cn_s, m_s, st_s, *, tb):
    nc = tb // CHUNK

    @pl.when(pl.program_id(1) == 0)
    def _():
        xpad[0:8, :] = jnp.zeros((8, 2 * ML_QK_W), F32)
        cn_s[...] = jnp.zeros(cn_s.shape, F32)
        st_s[...] = jnp.zeros(st_s.shape, F32)
        m_s[...] = jnp.full(m_s.shape, -jnp.inf, F32)

    xpad[8:8 + tb, :] = zm_ref[:, OFF_MQK:OFF_MQK + 2 * ML_QK_W].astype(F32)
    conv = convb_ref[...] + convw_ref[3:4, :] * xpad[8:8 + tb, :]
    for j in range(CONV_K - 1):
        conv = conv + convw_ref[j:j + 1, :] * xpad[5 + j:5 + j + tb, :]
    xpad[0:8, :] = xpad[tb:tb + 8, :]
    qk = conv * _sigmoid(conv)
    q_s[...] = qk[:, :ML_QK_W]
    k_s[...] = qk[:, ML_QK_W:] * (ML_DK ** -0.5)

    r_t = lax.broadcasted_iota(jnp.int32, (tb, tb), 0)
    c_t = lax.broadcasted_iota(jnp.int32, (tb, tb), 1)
    tri_blk = jnp.where((r_t // CHUNK == c_t // CHUNK) & (c_t <= r_t), 1.0, 0.0).astype(BF16)
    r_e = lax.broadcasted_iota(jnp.int32, (LANES, 2 * LANES), 0)
    c_e = lax.broadcasted_iota(jnp.int32, (LANES, 2 * LANES), 1)
    spread_dk = jnp.where(c_e // ML_DK == r_e, 1.0, 0.0).astype(BF16)
    mean_dv = jnp.full((ML_DV, ML_DV), 1.0 / ML_DV, F32).astype(BF16)
    row_c = lax.broadcasted_iota(jnp.int32, (CHUNK, CHUNK), 0)
    col_c = lax.broadcasted_iota(jnp.int32, (CHUNK, CHUNK), 1)
    causal = col_c <= row_c
    lane_c = lax.broadcasted_iota(jnp.int32, (CHUNK, LANES), 1)
    ones_dv = jnp.ones((CHUNK, ML_DV), BF16)
    chunk_rows = lambda c: slice(c * CHUNK, (c + 1) * CHUNK)

    zs = zs_ref[...]
    ig = zs + igb_ref[...]
    lf = _log_sigmoid(pltpu.roll(zs, LANES - SM_F, 1) + fgb_ref[...])
    bc = _dot01_left(tri_blk, lf, 3)
    bc3 = bc.reshape(nc, CHUNK, LANES)
    g3 = bc3[:, CHUNK - 1:CHUNK, :]
    a3 = g3 - bc3 + ig.reshape(nc, CHUNK, LANES)
    amax3 = jnp.max(a3, axis=1, keepdims=True)
    wa = jnp.exp(a3 - amax3).reshape(tb, LANES)
    m_run = m_s[...]
    m_prev, s_old, s_in = [], [], []
    for c in range(nc):
        m_new = jnp.maximum(g3[c] + m_run, amax3[c])
        m_prev.append(m_run)
        s_old.append(jnp.exp(g3[c] + m_run - m_new))
        s_in.append(jnp.exp(amax3[c] - m_new))
        m_run = m_new
    m_s[...] = m_run
    e_nat = ig - bc
    e_pad[0:CHUNK, :] = jnp.full((CHUNK, LANES), -jnp.inf, F32)
    e_pad[CHUNK:CHUNK + tb, :] = e_nat
    pos = lax.broadcasted_iota(jnp.int32, (tb, LANES), 0) % CHUNK
    shift = 1
    while shift < CHUNK:
        shifted = e_pad[CHUNK - shift:CHUNK - shift + tb, :]
        e_pad[CHUNK:CHUNK + tb, :] = jnp.maximum(e_pad[CHUNK:CHUNK + tb, :], jnp.where(pos >= shift, shifted, -jnp.inf))
        shift *= 2
    m_intra = bc + e_pad[CHUNK:CHUNK + tb, :]
    il = jnp.concatenate([bc3[c] + m_prev[c] for c in range(nc)], axis=0)
    mt = jnp.maximum(il, m_intra)
    s_inter = jnp.exp(il - mt)
    exp_neg = jnp.exp(-mt)
    xn = bc - mt
    e_up = pltpu.roll(e_nat, ML_HEADS, 1)
    wa_x = _dot01_right(wa, spread_dk, 2)
    si_x = _dot01_right(s_inter, spread_dk, 2)
    q_all = q_s[...]
    k_all = k_s[...]
    kw_b = (k_all * wa_x).astype(BF16)
    qs_b = (q_all * si_x).astype(BF16)
    q_b = q_all.astype(BF16)
    k_b = k_all.astype(BF16)

    qkm, upd, vo = {}, {}, {}
    for c in range(nc):
        rows = chunk_rows(c)
        for h in range(ML_HEADS):
            dk = slice(h * ML_DK, (h + 1) * ML_DK)
            x_t = jnp.where(lane_c == h, xn[rows], jnp.where(lane_c == h + ML_HEADS, 1.0, 0.0))
            y_t = jnp.where(lane_c == h, 1.0, jnp.where(lane_c == h + ML_HEADS, e_up[rows], 0.0))
            d = lax.dot_general(x_t, y_t, (((1,), (1,)), ((), ())), precision=HIGHEST, preferred_element_type=F32)
            p = jnp.where(causal, jnp.exp(d), 0.0)
            qkm[c, h] = (_dot_nt(q_b[rows, dk], k_b[rows, dk]) * p).astype(BF16)
            vo[c, h] = jnp.concatenate([zm_ref[rows, OFF_MV + h * ML_DV:OFF_MV + (h + 1) * ML_DV], ones_dv], axis=1)
            upd[c, h] = _dot_tn(kw_b[rows, dk], vo[c, h])
    for h in range(ML_HEADS):
        dk = slice(h * ML_DK, (h + 1) * ML_DK)
        dv = slice(h * ML_DV, (h + 1) * ML_DV)
        cn = cn_s[h]
        for c in range(nc):
            rows = chunk_rows(c)
            res = _dot(qs_b[rows, dk], cn.astype(BF16)) + _dot(qkm[c, h], vo[c, h])
            num, den = res[:, :ML_DV], res[:, ML_DV:]
            hml_s[rows, dv] = num / jnp.maximum(jnp.abs(den), exp_neg[rows, h:h + 1])
            cn = s_old[c][:, h:h + 1] * cn + s_in[c][:, h:h + 1] * upd[c, h]
        cn_s[h] = cn

    la = _log_sigmoid(_dot(zs.astype(BF16), lrup_ref[...]) + gateb_ref[...]) * (1.0 / GLA_TAU)
    bcg = _dot01_left(tri_blk, la, 3)
    bcg3 = bcg.reshape(nc, CHUNK, GLA_QK_W)
    gg3 = bcg3[:, CHUNK - 1:CHUNK, :]
    gq = zm_ref[:, OFF_GQ:OFF_GQ + GLA_QK_W].astype(F32)
    gk = zm_ref[:, OFF_GK:OFF_GK + GLA_QK_W].astype(F32) * (GLA_DK ** -0.5)
    q_dec = (gq * jnp.exp(bcg)).astype(BF16)
    k_inv = (gk * jnp.exp(-bcg)).astype(BF16)
    k_end = (gk * jnp.exp(gg3 - bcg3).reshape(tb, GLA_QK_W)).astype(BF16)
    eg3 = jnp.exp(gg3)
    att, updg = {}, {}
    for c in range(nc):
        rows = chunk_rows(c)
        for h in range(GLA_HEADS):
            dk = slice(h * GLA_DK, (h + 1) * GLA_DK)
            vh = zm_ref[rows, OFF_GV + h * GLA_DV:OFF_GV + (h + 1) * GLA_DV]
            att[c, h] = jnp.where(causal, _dot_nt(q_dec[rows, dk], k_inv[rows, dk]), 0.0).astype(BF16)
            updg[c, h] = _dot_tn(vh, k_end[rows, dk])
    for h in range(GLA_HEADS):
        dk = slice(h * GLA_DK, (h + 1) * GLA_DK)
        st = st_s[h]
        for c in range(nc):
            rows = chunk_rows(c)
            vh = zm_ref[rows, OFF_GV + h * GLA_DV:OFF_GV + (h + 1) * GLA_DV]
            hgla_s[rows, h * GLA_DV:(h + 1) * GLA_DV] = _dot_nt(q_dec[rows, dk], st.astype(BF16)) + _dot(att[c, h], vh)
            st = st * eg3[c][:, dk] + updg[c, h]
        st_s[h] = st

    mean = lambda x: _dot01_right(x, mean_dv, 2)
    for h in range(ML_HEADS):
        sl = slice(h * ML_DV, (h + 1) * ML_DV)
        hh = hml_s[:, sl]
        d = hh - mean(hh)
        hn = d * lax.rsqrt(mean(d * d) + EPS)
        og = zm_ref[:, OFF_MO + h * ML_DV:OFF_MO + (h + 1) * ML_DV].astype(F32)
        y_ref[:, sl] = (hn * mlnorm_ref[:, sl] * _sigmoid(og)).astype(y_ref.dtype)
    for h in range(GLA_HEADS):
        sl = slice(h * GLA_DV, (h + 1) * GLA_DV)
        o = hgla_s[:, sl]
        on = o * lax.rsqrt(mean(o * o) + EPS)
        gg = zm_ref[:, OFF_GG + h * GLA_DV:OFF_GG + (h + 1) * GLA_DV].astype(F32)
        y_ref[:, ML_V_W + h * GLA_DV:ML_V_W + (h + 1) * GLA_DV] = (on * glanorm_ref[:, sl] * (gg * _sigmoid(gg))).astype(y_ref.dtype)


def _mixer(zm, zs, convw, convb, igb, fgb, lrup, gateb, mlnorm, glanorm, bsz, seq, tb):
    nt = seq // tb
    const = lambda shape: pl.BlockSpec(shape, lambda b, i: (0,) * len(shape))
    return pl.pallas_call(
        functools.partial(_mixer_kernel, tb=tb),
        out_shape=jax.ShapeDtypeStruct((bsz * seq, ML_V_W + GLA_V_W), BF16),
        grid=(bsz, nt),
        in_specs=[pl.BlockSpec((tb, Z_MAIN_W), lambda b, i: (b * nt + i, 0)),
                  pl.BlockSpec((tb, LANES), lambda b, i: (b * nt + i, 0)),
                  const((CONV_K, 2 * ML_QK_W)), const((1, 2 * ML_QK_W)), const((1, LANES)), const((1, LANES)),
                  const((LANES, GLA_QK_W)), const((1, GLA_QK_W)), const((1, ML_V_W)), const((1, GLA_V_W))],
        out_specs=pl.BlockSpec((tb, ML_V_W + GLA_V_W), lambda b, i: (b * nt + i, 0)),
        scratch_shapes=[pltpu.VMEM((tb + 8, 2 * ML_QK_W), F32),
                        pltpu.VMEM((tb, ML_QK_W), F32), pltpu.VMEM((tb, ML_QK_W), F32),
                        pltpu.VMEM((tb + CHUNK, LANES), F32),
                        pltpu.VMEM((tb, ML_V_W), F32), pltpu.VMEM((tb, GLA_V_W), F32),
                        pltpu.VMEM((ML_HEADS, ML_DK, ML_DV + LANES), F32),
                        pltpu.VMEM((1, LANES), F32), pltpu.VMEM((GLA_HEADS, GLA_DV, GLA_DK), F32)],
        compiler_params=_cparams(("parallel", "arbitrary")),
        name="mlstm_gla",
    )(zm, zs, convw, convb, igb, fgb, lrup, gateb, mlnorm, glanorm)


def _proj_router_kernel(y_ref, w_ref, b_ref, h_ref, g_ref, wr_ref, br_ref, h1_ref, u2_ref, rt_ref):
    h1 = h_ref[...] + (_dot(y_ref[...], w_ref[...]) + b_ref[...])
    h1_ref[...] = h1
    u2 = _rms(h1, g_ref[...])
    for c in range(ROW_TILES):
        u2_ref[pl.ds(c, u2.shape[0], stride=ROW_TILES), :] = u2[:, c * LANES:(c + 1) * LANES]
    u_hi = u2.astype(BF16)
    u_lo = (u2 - u_hi.astype(F32)).astype(BF16)
    part = _dot(u_hi, wr_ref[...])
    logits = part[:, :LANES] + (part[:, LANES:] + _dot(u_lo, wr_ref[:, :LANES])) + br_ref[...]
    lane = lax.broadcasted_iota(jnp.int32, logits.shape, 1)
    lane_f = lane.astype(F32)
    big = float(LANES)
    gl = jnp.where(lane < N_GROUPS, logits, -jnp.inf)
    g_max = jnp.max(gl, axis=-1, keepdims=True)
    g_idx = jnp.min(jnp.where(gl == g_max, lane_f, big), axis=-1, keepdims=True)
    g_p = 1.0 / jnp.sum(jnp.exp(gl - g_max), axis=-1, keepdims=True)
    e_grp = ((lane - N_GROUPS) // EXPERTS_PER_GROUP).astype(F32)
    in_grp = (lane >= N_GROUPS) & (lane < N_GROUPS + N_EXPERTS) & (e_grp == g_idx)
    el = jnp.where(in_grp, logits, -jnp.inf)
    t1 = jnp.max(el, axis=-1, keepdims=True)
    i1 = jnp.min(jnp.where(el == t1, lane_f, big), axis=-1, keepdims=True)
    el2 = jnp.where(lane_f == i1, -jnp.inf, el)
    t2 = jnp.max(el2, axis=-1, keepdims=True)
    i2 = jnp.min(jnp.where(el2 == t2, lane_f, big), axis=-1, keepdims=True)
    e21 = jnp.exp(t2 - t1)
    p1 = 1.0 / (1.0 + e21)
    rt = jnp.where(lane == 0, i1 - N_GROUPS,
                   jnp.where(lane == 1, i2 - N_GROUPS,
                             jnp.where(lane == 2, g_p * p1, jnp.where(lane == 3, g_p * (e21 * p1), 0.0))))
    rt_ref[...] = rt


def _proj_router(y, w, b, h, g, wr, br, tm):
    t, kdim = y.shape
    row = lambda i: (i, 0)
    fixed = lambda i: (0, 0)
    return pl.pallas_call(
        _proj_router_kernel,
        out_shape=(jax.ShapeDtypeStruct((t, D_MODEL), F32), jax.ShapeDtypeStruct((t * ROW_TILES, LANES), F32),
                   jax.ShapeDtypeStruct((t, LANES), F32)),
        grid=(t // tm,),
        in_specs=[pl.BlockSpec((tm, kdim), row), pl.BlockSpec((kdim, D_MODEL), fixed),
                  pl.BlockSpec((1, D_MODEL), fixed), pl.BlockSpec((tm, D_MODEL), row),
                  pl.BlockSpec((1, D_MODEL), fixed), pl.BlockSpec((D_MODEL, 2 * LANES), fixed),
                  pl.BlockSpec((1, LANES), fixed)],
        out_specs=(pl.BlockSpec((tm, D_MODEL), row), pl.BlockSpec((tm * ROW_TILES, LANES), row),
                   pl.BlockSpec((tm, LANES), row)),
        compiler_params=_cparams(("parallel",)),
        name="proj_router",
    )(y, w, b, h, g, wr, br)


def _rank_kernel(rt_ref, rk_ref, cnt_ref, base_s, strict_s):
    tt = rt_ref.shape[0]

    @pl.when(pl.program_id(0) == 0)
    def _():
        base_s[...] = jnp.zeros(base_s.shape, F32)
        r = lax.broadcasted_iota(jnp.int32, (tt, tt), 0)
        c = lax.broadcasted_iota(jnp.int32, (tt, tt), 1)
        strict_s[...] = jnp.where(c < r, 1.0, 0.0).astype(BF16)

    rt = rt_ref[...]
    lane = lax.broadcasted_iota(jnp.int32, rt.shape, 1)
    lane_f = lane.astype(F32)
    e0, e1 = rt[:, 0:1], rt[:, 1:2]
    oh0 = lane_f == e0
    oh1 = lane_f == e1
    oh = jnp.where(oh0 | oh1, 1.0, 0.0)
    before = _dot(strict_s[...], oh.astype(BF16)) + base_s[...]
    r0 = jnp.sum(jnp.where(oh0, before, 0.0), axis=-1, keepdims=True)
    r1 = jnp.sum(jnp.where(oh1, before, 0.0), axis=-1, keepdims=True)
    table = jnp.where(lane == 0, r0, jnp.where(lane == 1, r1, jnp.where(lane == 2, e0, jnp.where(lane == 3, e1, 0.0))))
    rk_ref[...] = jnp.transpose(table)[0:8, :].astype(jnp.int32)
    base_s[...] = base_s[...] + jnp.sum(oh, axis=0, keepdims=True)
    cnt_ref[...] = base_s[...]


def _rank(rt, tt):
    t = rt.shape[0]
    return pl.pallas_call(
        _rank_kernel,
        out_shape=(jax.ShapeDtypeStruct((8, t), jnp.int32), jax.ShapeDtypeStruct((1, LANES), F32)),
        grid=(t // tt,),
        in_specs=[pl.BlockSpec((tt, LANES), lambda i: (i, 0))],
        out_specs=(pl.BlockSpec((8, tt), lambda i: (0, i)), pl.BlockSpec((1, LANES), lambda i: (0, 0))),
        scratch_shapes=[pltpu.VMEM((1, LANES), F32), pltpu.VMEM((tt, tt), BF16)],
        compiler_params=_cparams(("arbitrary",)),
        name="expert_rank",
    )(rt)


DMA_GROUP = 8


def _row_copy(src, dst, sem):
    return pltpu.make_async_copy(src, dst, sem)


def _row_tile(r):
    return pl.ds(pl.multiple_of(r * ROW_TILES, ROW_TILES), ROW_TILES)


def _dispatch_kernel(dest_ref, zflag_ref, u_ref, xout_hbm, inv_ref, fill_s, zero_s, sem, *, tt):
    i = pl.program_id(0)
    block_rows = MOE_BM * ROW_TILES

    @pl.when(i == 0)
    def _():
        fill_s[...] = jnp.full(fill_s.shape, -1, jnp.int32)
        fill = pltpu.make_async_copy(fill_s, inv_ref, sem)
        fill.start()
        fill.wait()
        zero_s[...] = jnp.zeros(zero_s.shape, F32)
        zero_block = lambda blk: _row_copy(zero_s, xout_hbm.at[pl.ds(blk * block_rows, block_rows)], sem)
        for blk in range(zflag_ref.shape[0]):
            @pl.when(zflag_ref[blk] != 0)
            def _():
                zero_block(blk).start()
        for blk in range(zflag_ref.shape[0]):
            @pl.when(zflag_ref[blk] != 0)
            def _():
                zero_block(blk).wait()

    def issue(g, carry):
        for jj in range(DMA_GROUP):
            j = g * DMA_GROUP + jj
            src = u_ref.at[_row_tile(j)]
            for k in range(TOP_K):
                d = dest_ref[k, j]
                _row_copy(src, xout_hbm.at[_row_tile(d)], sem).start(priority=k)
                inv_ref[d] = (i * tt + j) * TOP_K + k
        return carry

    lax.fori_loop(0, tt // DMA_GROUP, issue, 0)
    for k in range(TOP_K):
        _row_copy(u_ref, xout_hbm.at[pl.ds(0, tt * ROW_TILES)], sem).wait()


def _dispatch(dest, zflag, u2, tt):
    t = u2.shape[0] // ROW_TILES
    n_rows = zflag.shape[0] * MOE_BM
    return pl.pallas_call(
        functools.partial(_dispatch_kernel, tt=tt),
        out_shape=(jax.ShapeDtypeStruct((n_rows * ROW_TILES, LANES), F32), jax.ShapeDtypeStruct((n_rows,), jnp.int32)),
        grid=(t // tt,),
        in_specs=[pl.BlockSpec((TOP_K, tt), lambda i: (0, i), memory_space=pltpu.SMEM),
                  pl.BlockSpec(memory_space=pltpu.SMEM),
                  pl.BlockSpec((tt * ROW_TILES, LANES), lambda i: (i, 0))],
        out_specs=(pl.BlockSpec(memory_space=pl.ANY), pl.BlockSpec(memory_space=pltpu.SMEM)),
        scratch_shapes=[pltpu.VMEM((n_rows,), jnp.int32), pltpu.VMEM((MOE_BM * ROW_TILES, LANES), F32),
                        pltpu.SemaphoreType.DMA],
        compiler_params=_cparams(("arbitrary",)),
        name="moe_dispatch",
    )(dest, zflag, u2)


def _ffn_kernel(be_ref, nu_ref, inv_ref, x_ref, wg_ref, wu_ref, wd_ref, o2_hbm, wg_s, wu_s, wd_s, x_s, y_s, sem, *, n_slots):
    b = pl.program_id(0)
    last = pl.num_programs(0) - 1
    n_used = nu_ref[0]
    block_rows = MOE_BM * ROW_TILES

    def drain():
        _row_copy(y_s.at[0], o2_hbm.at[pl.ds(0, block_rows)], sem).wait()

    def scatter(blk):
        s = blk % 2
        for j in range(MOE_BM):
            a = inv_ref[blk * MOE_BM + j]
            slot = jnp.where(a >= 0, a, n_slots + s * MOE_BM + j)
            _row_copy(y_s.at[s, _row_tile(j)], o2_hbm.at[_row_tile(slot)], sem).start(priority=j % 2)

    def compute():
        for c in range(ROW_TILES):
            x_s[:, c * LANES:(c + 1) * LANES] = x_ref[pl.ds(c, MOE_BM, stride=ROW_TILES), :].astype(BF16)
        x = x_s[...]
        a = _dot(x, wg_s[...])
        u = _dot(x, wu_s[...])
        y = _dot(((a * _sigmoid(a)) * u).astype(BF16), wd_s[...])
        for c in range(ROW_TILES):
            y_s[b % 2, pl.ds(c, MOE_BM, stride=ROW_TILES), :] = y[:, c * LANES:(c + 1) * LANES]

    @pl.when(b == 0)
    def _():
        y_s[...] = jnp.zeros(y_s.shape, F32)
        for s in range(2):
            _row_copy(y_s.at[s], o2_hbm.at[pl.ds((n_slots + s * MOE_BM) * ROW_TILES, block_rows)], sem).start()
        for s in range(2):
            drain()

    @pl.when((b >= 2) & (b - 2 < n_used))
    def _():
        drain()

    @pl.when((b == 0) | (be_ref[b] != be_ref[jnp.maximum(b - 1, 0)]))
    def _():
        wg_s[...] = wg_ref[...].astype(BF16)
        wu_s[...] = wu_ref[...].astype(BF16)
        wd_s[...] = wd_ref[...].astype(BF16)

    @pl.when(b == 0)
    def _():
        compute()

    @pl.when((b >= 1) & (b < n_used))
    def _():
        scatter(b - 1)
        compute()

    @pl.when((b >= 1) & (b >= n_used) & (b - 1 < n_used))
    def _():
        scatter(b - 1)

    @pl.when(b == last)
    def _():
        @pl.when((b >= 1) & (b - 1 < n_used))
        def _():
            drain()

        @pl.when(b < n_used)
        def _():
            scatter(b)
            drain()


def _ffn(block_expert, n_used, inv, xbuf, wg, wu, wd, layer, n_slots):
    n_blocks = xbuf.shape[0] // (MOE_BM * ROW_TILES)
    rows = lambda b, be, nu, inv: (jnp.minimum(b, nu[0] - 1), 0)
    wmap = lambda b, be, nu, inv: (layer, be[b], 0, 0)
    return pl.pallas_call(
        functools.partial(_ffn_kernel, n_slots=n_slots),
        out_shape=jax.ShapeDtypeStruct(((n_slots + 2 * MOE_BM) * ROW_TILES, LANES), F32),
        grid_spec=pltpu.PrefetchScalarGridSpec(
            num_scalar_prefetch=3,
            grid=(n_blocks,),
            in_specs=[pl.BlockSpec((MOE_BM * ROW_TILES, LANES), rows),
                      pl.BlockSpec((None, None, D_MODEL, D_FF), wmap),
                      pl.BlockSpec((None, None, D_MODEL, D_FF), wmap),
                      pl.BlockSpec((None, None, D_FF, D_MODEL), wmap)],
            out_specs=pl.BlockSpec(memory_space=pl.ANY),
            scratch_shapes=[pltpu.VMEM((D_MODEL, D_FF), BF16), pltpu.VMEM((D_MODEL, D_FF), BF16),
                            pltpu.VMEM((D_FF, D_MODEL), BF16), pltpu.VMEM((MOE_BM, D_MODEL), BF16),
                            pltpu.VMEM((2, MOE_BM * ROW_TILES, LANES), F32), pltpu.SemaphoreType.DMA]),
        compiler_params=_cparams(("arbitrary",)),
        name="moe_ffn",
    )(block_expert, n_used, inv, xbuf, wg, wu, wd)


def _combine_kernel(o2_ref, h_ref, rt_ref, g_ref, o_ref, *, final_norm):
    tt = h_ref.shape[0]
    rt = rt_ref[...]
    w0, w1 = rt[:, 2:3], rt[:, 3:4]
    for c in range(ROW_TILES):
        lanes = slice(c * LANES, (c + 1) * LANES)
        y0 = o2_ref[pl.ds(c, tt, stride=TOP_K * ROW_TILES), :]
        y1 = o2_ref[pl.ds(ROW_TILES + c, tt, stride=TOP_K * ROW_TILES), :]
        o_ref[:, lanes] = h_ref[:, lanes] + (y0 * w0 + y1 * w1)
    if final_norm:
        o_ref[...] = _rms(o_ref[...], g_ref[...])


def _combine(o2, h1, rt, g, tt, final_norm):
    t = h1.shape[0]
    return pl.pallas_call(
        functools.partial(_combine_kernel, final_norm=final_norm),
        out_shape=jax.ShapeDtypeStruct((t, D_MODEL), F32),
        grid=(t // tt,),
        in_specs=[pl.BlockSpec((tt * TOP_K * ROW_TILES, LANES), lambda i: (i, 0)),
                  pl.BlockSpec((tt, D_MODEL), lambda i: (i, 0)),
                  pl.BlockSpec((tt, LANES), lambda i: (i, 0)),
                  pl.BlockSpec((1, D_MODEL), lambda i: (0, 0))],
        out_specs=pl.BlockSpec((tt, D_MODEL), lambda i: (i, 0)),
        compiler_params=_cparams(("parallel",)),
        name="moe_combine",
    )(o2, h1, rt, g)


def _moe(h1, u2, rt, wg, wu, wd, layer, g_final, final_norm):
    t = h1.shape[0]
    rk, cnt = _rank(rt, 1024)
    counts = cnt[0, :N_EXPERTS].astype(jnp.int32)
    padded = (counts + MOE_BM - 1) // MOE_BM * MOE_BM
    pad_end = jnp.cumsum(padded)
    pad_start = pad_end - padded
    n_blocks = (t * TOP_K) // MOE_BM + N_EXPERTS
    is_expert = rk[TOP_K:2 * TOP_K][None] == jnp.arange(N_EXPERTS, dtype=jnp.int32)[:, None, None]
    dest = jnp.sum(jnp.where(is_expert, pad_start[:, None, None], 0), axis=0) + rk[0:TOP_K]
    n_used = (pad_end[-1] // MOE_BM).astype(jnp.int32)
    blocks = jnp.arange(n_blocks, dtype=jnp.int32)
    blk = jnp.minimum(blocks, n_used - 1) * MOE_BM
    block_expert = jnp.minimum(jnp.sum(pad_end[None, :] <= blk[:, None], axis=1), N_EXPERTS - 1).astype(jnp.int32)
    closes_expert = jnp.any(((blocks[:, None] + 1) * MOE_BM == pad_end[None, :]) & (padded[None, :] > 0), axis=1)
    zflag = ((blocks >= n_used) | closes_expert).astype(jnp.int32)
    xbuf, inv = _dispatch(dest, zflag, u2, 256)
    o2 = _ffn(block_expert, n_used.reshape(1), inv, xbuf, wg, wu, wd, layer, t * TOP_K)
    return _combine(o2, h1, rt, g_final, 256, final_norm)


def _qkv_kernel(x_ref, g_ref, w_ref, b_ref, pos_ref, freq_ref, sp_ref, q_ref, kv_ref):
    u = _rms(x_ref[...], g_ref[...]).astype(BF16)
    ang = freq_ref[...] * pos_ref[...].astype(F32)
    spread = lambda v, m: sum(_dot_tn(t.astype(F32), m) for t in _split_terms(v, 3))
    cosv = jnp.cos(ang)
    sinv = jnp.sin(ang)
    c_coef = spread(cosv, sp_ref[0]) + sp_ref[3, 0:1, :]
    s_lo = spread(sinv, sp_ref[1])
    s_hi = spread(sinv, sp_ref[2])

    def rotate(z):
        return z * c_coef + pltpu.roll(z, LANES - ROT_DIM // 2, 1) * s_lo + pltpu.roll(z, ROT_DIM // 2, 1) * s_hi

    scale = SWA_HEAD_DIM ** -0.5
    for j in range(Q_W // LANES):
        sl = slice(j * LANES, (j + 1) * LANES)
        q_ref[:, sl] = (rotate(_dot(u, w_ref[:, sl]) + b_ref[:, sl]) * scale).astype(q_ref.dtype)
    for j in range(KV_OUT_W // LANES):
        sl = slice(Q_W + j * LANES, Q_W + (j + 1) * LANES)
        z = _dot(u, w_ref[:, sl]) + b_ref[:, sl]
        kv_ref[:, j * LANES:(j + 1) * LANES] = (rotate(z) if j % 2 == 0 else z).astype(kv_ref.dtype)


def _rot_tables():
    half = ROT_DIM // 2
    inv_freq = (ROPE_THETA ** (-jnp.arange(0, ROT_DIM, 2, dtype=F32) / ROT_DIM)).reshape(half, 1)
    d = np.arange(LANES) % SWA_HEAD_DIM
    f = np.arange(half)[:, None]
    sp = np.zeros((4, half, LANES), np.float32)
    sp[0] = (d[None, :] < ROT_DIM) & (d[None, :] % half == f)
    sp[1] = -((d[None, :] < half) & (d[None, :] == f)).astype(np.float32)
    sp[2] = (d[None, :] >= half) & (d[None, :] < ROT_DIM) & (d[None, :] - half == f)
    sp[3, 0] = d >= ROT_DIM
    return inv_freq, jnp.asarray(sp)


def _qkv(h, g, w, b, pos, tm):
    t = h.shape[0]
    row = lambda i: (i, 0)
    fixed = lambda i: (0, 0)
    wtot = Q_W + KV_OUT_W
    return pl.pallas_call(
        _qkv_kernel,
        out_shape=(jax.ShapeDtypeStruct((t, Q_W), BF16), jax.ShapeDtypeStruct((t, KV_OUT_W), BF16)),
        grid=(t // tm,),
        in_specs=[pl.BlockSpec((tm, D_MODEL), row), pl.BlockSpec((1, D_MODEL), fixed),
                  pl.BlockSpec((D_MODEL, wtot), fixed), pl.BlockSpec((1, wtot), fixed),
                  pl.BlockSpec((1, tm), lambda i: (0, i)), pl.BlockSpec((ROT_DIM // 2, 1), fixed),
                  pl.BlockSpec((4, ROT_DIM // 2, LANES), lambda i: (0, 0, 0))],
        out_specs=(pl.BlockSpec((tm, Q_W), row), pl.BlockSpec((tm, KV_OUT_W), row)),
        compiler_params=_cparams(("parallel",)),
        name="qkv_rotary",
    )(h, g, w, b, pos, *_rot_tables())


def _swa_kernel(q_ref, kvc_ref, kvp_ref, sink_ref, o_ref):
    neg = jnp.where(pl.program_id(1) > 0, 0.0, -jnp.inf).astype(F32)
    kj = lax.broadcasted_iota(jnp.int32, (WINDOW, WINDOW), 0)
    qi = lax.broadcasted_iota(jnp.int32, (WINDOW, WINDOW), 1)
    mask_cur = kj <= qi
    top = kj < SWA_HEAD_DIM
    lane = lax.broadcasted_iota(jnp.int32, (1, LANES), 1)
    keep_lo = jnp.where(lane < SWA_HEAD_DIM, 1.0, 0.0).astype(BF16)
    keep_hi = jnp.where(lane < SWA_HEAD_DIM, 0.0, 1.0).astype(BF16)
    zeros_half = jnp.zeros((SWA_HEAD_DIM, WINDOW), F32)

    def arranged(ref):
        k_nat, k_swp = ref[:, 0:KV_W], ref[:, 2 * KV_W:3 * KV_W]
        vt = jnp.transpose(ref[:, KV_W:2 * KV_W].astype(F32))
        vt_top = lambda g: jnp.concatenate([vt[g * SWA_HEAD_DIM:(g + 1) * SWA_HEAD_DIM], zeros_half], axis=0).astype(BF16)
        vt_bot = lambda g: jnp.concatenate([zeros_half, vt[g * SWA_HEAD_DIM:(g + 1) * SWA_HEAD_DIM]], axis=0).astype(BF16)
        return {(0, 0): (k_nat * keep_lo, vt_top(0)), (0, 1): (k_swp * keep_hi, vt_bot(0)),
                (1, 0): (k_swp * keep_lo, vt_top(1)), (1, 1): (k_nat * keep_hi, vt_bot(1))}

    cur = arranged(kvc_ref)
    prev = arranged(kvp_ref)
    heads = range(SWA_Q_HEADS)
    key = lambda h: (h // SWA_GROUP, h % 2)
    scores = []
    for h in heads:
        qp = q_ref[:, (h // 2) * LANES:(h // 2 + 1) * LANES]
        sc = _dot_nt(cur[key(h)][0], qp)
        sp = _dot_nt(prev[key(h)][0], qp)
        scores.append(jnp.where(mask_cur, sc, sp + neg))
    sinks = [sink_ref[:, h:h + 1] for h in heads]
    maxes = [jnp.maximum(jnp.max(scores[h], axis=0, keepdims=True), sinks[h]) for h in heads]
    probs = [jnp.exp(scores[h] - maxes[h]) for h in heads]
    denoms = [jnp.sum(probs[h], axis=0, keepdims=True) + jnp.exp(sinks[h] - maxes[h]) for h in heads]
    for pr in range(SWA_Q_HEADS // 2):
        num = None
        for h in (2 * pr, 2 * pr + 1):
            pc = jnp.where(mask_cur, probs[h], 0.0).astype(BF16)
            pp = jnp.where(mask_cur, 0.0, probs[h]).astype(BF16)
            part = _dot(cur[key(h)][1], pc) + _dot(prev[key(h)][1], pp)
            num = part if num is None else num + part
        den = jnp.where(top, denoms[2 * pr], denoms[2 * pr + 1])
        o_ref[:, pr * LANES:(pr + 1) * LANES] = jnp.transpose(num / den).astype(o_ref.dtype)


def _swa(q, kv, sinks, bsz, seq):
    nb = seq // WINDOW
    return pl.pallas_call(
        _swa_kernel,
        out_shape=jax.ShapeDtypeStruct((bsz * seq, Q_W), BF16),
        grid=(bsz, nb),
        in_specs=[pl.BlockSpec((WINDOW, Q_W), lambda b, n: (b * nb + n, 0)),
                  pl.BlockSpec((WINDOW, KV_OUT_W), lambda b, n: (b * nb + n, 0)),
                  pl.BlockSpec((WINDOW, KV_OUT_W), lambda b, n: (b * nb + jnp.maximum(n - 1, 0), 0)),
                  pl.BlockSpec((1, LANES), lambda b, n: (0, 0))],
        out_specs=pl.BlockSpec((WINDOW, Q_W), lambda b, n: (b * nb + n, 0)),
        compiler_params=_cparams(("parallel", "parallel")),
        name="swa_attention",
    )(q, kv, kv, sinks)


def _pack_in_proj(w_in):
    offs = np.cumsum((0,) + AB_SPLITS)
    part = lambda i: w_in[:, offs[i]:offs[i + 1]]
    w_main = jnp.concatenate([part(0), part(1), part(2), part(5), part(6), part(7), part(8)], axis=1).astype(BF16)
    small = jnp.concatenate([part(3), part(4), part(9)], axis=1)
    w_small = jnp.pad(small, ((0, 0), (0, LANES - small.shape[1]))).astype(BF16)
    return w_main, w_small


def _pack_qkv(w, b):
    hd = SWA_HEAD_DIM
    k0, k1 = slice(Q_W, Q_W + hd), slice(Q_W + hd, Q_W + 2 * hd)
    cols = lambda a: jnp.concatenate([a[..., :Q_W + 2 * KV_W], a[..., k1], a[..., k0]], axis=-1)
    return cols(w).astype(BF16), cols(b).reshape(1, -1).astype(F32)


def _lane_row(v, offset=0):
    return jnp.zeros((1, LANES), F32).at[0, offset:offset + v.shape[0]].set(v.astype(F32))


def kernel(x, positions, mix_norm, ffn_norm, final_norm, ab_w_in, ab_conv_w, ab_conv_b, ml_igate_b, ml_fgate_b, ml_head_norm, gla_w_lr_up, gla_gate_b, gla_head_norm, ab_w_out, swa_w_qkv, swa_b_qkv, swa_sinks, swa_w_o, swa_b_o, router_group_w, router_group_b, router_expert_w, router_expert_b, expert_w_gate, expert_w_up, expert_w_down):
    bsz, seq, d = x.shape
    t = bsz * seq
    h = x.reshape(t, d)
    row = lambda v: v.reshape(1, -1).astype(F32)

    def router_params(layer):
        wr = jnp.zeros((d, LANES), F32).at[:, :N_GROUPS].set(router_group_w[layer])
        wr = wr.at[:, N_GROUPS:N_GROUPS + N_EXPERTS].set(router_expert_w[layer])
        br = jnp.zeros((1, LANES), F32).at[0, :N_GROUPS].set(router_group_b[layer])
        br = br.at[0, N_GROUPS:N_GROUPS + N_EXPERTS].set(router_expert_b[layer])
        w_hi = wr.astype(BF16)
        w_lo = (wr - w_hi.astype(F32)).astype(BF16)
        return jnp.concatenate([w_hi, w_lo], axis=1), br

    def experts(layer):
        return expert_w_gate, expert_w_up, expert_w_down, layer

    w_main, w_small = _pack_in_proj(ab_w_in[0])
    zm, zs = _in_proj(h, row(mix_norm[0]), w_main, w_small, 512)
    lrup = jnp.zeros((LANES, GLA_QK_W), F32).at[SM_LR:SM_LR + GLA_LOWRANK].set(gla_w_lr_up[0]).astype(BF16)
    y = _mixer(zm, zs, ab_conv_w[0], row(ab_conv_b[0]), _lane_row(ml_igate_b[0]), _lane_row(ml_fgate_b[0]), lrup,
               row(gla_gate_b[0]), row(ml_head_norm[0]), row(gla_head_norm[0]), bsz, seq, 256)
    wr, br = router_params(0)
    h1, u2, rt = _proj_router(y, ab_w_out[0].astype(BF16), jnp.zeros((1, d), F32), h, row(ffn_norm[0]), wr, br, 512)
    h = _moe(h1, u2, rt, *experts(0), row(final_norm), False)

    w_qkv, b_qkv = _pack_qkv(swa_w_qkv[0], swa_b_qkv[0])
    q, kv = _qkv(h, row(mix_norm[1]), w_qkv, b_qkv, positions.reshape(1, t), 512)
    o = _swa(q, kv, _lane_row(swa_sinks[0]), bsz, seq)
    wr, br = router_params(1)
    h1, u2, rt = _proj_router(o, swa_w_o[0].astype(BF16), row(swa_b_o[0]), h, row(ffn_norm[1]), wr, br, 512)
    out = _moe(h1, u2, rt, *experts(1), row(final_norm), True)
    return out.reshape(bsz, seq, d)
```

```python
import functools

import jax
import jax.numpy as jnp
import numpy as np
from jax import lax
from jax.experimental import pallas as pl
from jax.experimental.pallas import tpu as pltpu

F32 = jnp.float32
BF16 = jnp.bfloat16
HIGHEST = lax.Precision.HIGHEST

D_MODEL = 1024
EPS = 1e-6
ML_HEADS = 4
ML_DK = 64
ML_DV = 128
GLA_HEADS = 4
GLA_DK = 64
GLA_DV = 128
CHUNK = 64
CONV_K = 4
GLA_LOWRANK = 16
GLA_TAU = 16.0
ML_QK_W = ML_HEADS * ML_DK
ML_V_W = ML_HEADS * ML_DV
GLA_QK_W = GLA_HEADS * GLA_DK
GLA_V_W = GLA_HEADS * GLA_DV
AB_SPLITS = (2 * ML_QK_W, ML_V_W, ML_V_W, ML_HEADS, ML_HEADS, GLA_QK_W, GLA_QK_W, GLA_V_W, GLA_V_W, GLA_LOWRANK)
OFF_MQK = 0
OFF_MV = OFF_MQK + 2 * ML_QK_W
OFF_MO = OFF_MV + ML_V_W
OFF_GQ = OFF_MO + ML_V_W
OFF_GK = OFF_GQ + GLA_QK_W
OFF_GV = OFF_GK + GLA_QK_W
OFF_GG = OFF_GV + GLA_V_W
Z_MAIN_W = OFF_GG + GLA_V_W
SM_I = 0
SM_F = ML_HEADS
SM_LR = 2 * ML_HEADS
LANES = 128
SWA_Q_HEADS = 16
SWA_KV_HEADS = 2
SWA_HEAD_DIM = 64
SWA_GROUP = SWA_Q_HEADS // SWA_KV_HEADS
WINDOW = 128
ROT_DIM = SWA_HEAD_DIM // 4
ROPE_THETA = 500000.0
Q_W = SWA_Q_HEADS * SWA_HEAD_DIM
KV_W = SWA_KV_HEADS * SWA_HEAD_DIM
KV_OUT_W = 3 * KV_W
N_GROUPS = 4
EXPERTS_PER_GROUP = 8
N_EXPERTS = N_GROUPS * EXPERTS_PER_GROUP
TOP_K = 2
D_FF = 512
MOE_BM = 512
ROW_TILES = D_MODEL // LANES

VMEM_LIMIT = 56 * 1024 * 1024


def _cparams(sem):
    return pltpu.CompilerParams(dimension_semantics=sem, vmem_limit_bytes=VMEM_LIMIT)


def _rms(x, g):
    return x * lax.rsqrt(jnp.mean(x * x, axis=-1, keepdims=True) + EPS) * g


def _log_sigmoid(x):
    return jnp.minimum(x, 0.0) - jnp.log1p(jnp.exp(-jnp.abs(x)))


def _sigmoid(x):
    return 1.0 / (1.0 + jnp.exp(-x))


def _dot(a, b):
    return jnp.dot(a, b, preferred_element_type=F32)


def _dot_nt(a, b):
    return lax.dot_general(a, b, (((1,), (1,)), ((), ())), preferred_element_type=F32)


def _dot_tn(a, b):
    return lax.dot_general(a, b, (((0,), (0,)), ((), ())), preferred_element_type=F32)


IN_PROJ_CHUNK = 768
def _split_terms(x, n):
    terms = []
    for _ in range(n):
        t = x.astype(BF16)
        terms.append(t)
        x = x - t.astype(F32)
    return terms


def _dot01_left(m01, x, n):
    return sum(_dot(m01, t) for t in _split_terms(x, n))


def _dot01_right(x, m01, n):
    return sum(_dot(t, m01) for t in _split_terms(x, n))


def _mixer_kernel(xp_ref, xn_ref, g_ref, wm_ref, ws_ref, convw_ref, convb_ref, igb_ref, fgb_ref, lrup_ref, gateb_ref,
                  mlnorm_ref, glanorm_ref, y_ref, zm_s, zs_s, xpad, q_s, k_s, e_pad, hml_s, hgla_s, cn_s, m_s, st_s,
                  *, tb):
    step = pl.program_id(0) * pl.num_programs(1) + pl.program_id(1)

    def project(x, slot):
        u = _rms(x, g_ref[...]).astype(BF16)
        for n0 in range(0, Z_MAIN_W, IN_PROJ_CHUNK):
            zm_s[slot, :, n0:n0 + IN_PROJ_CHUNK] = _dot(u, wm_ref[:, n0:n0 + IN_PROJ_CHUNK]).astype(BF16)
        zs_s[slot] = _dot(u, ws_ref[...])

    @pl.when(step == 0)
    def _():
        project(xp_ref[0:tb, :], 0)

    @pl.when(pl.program_id(1) == 0)
    def _():
        xpad[0:8, :] = jnp.zeros((8, 2 * ML_QK_W), F32)
        cn_s[...] = jnp.zeros(cn_s.shape, F32)
        st_s[...] = jnp.zeros(st_s.shape, F32)
        m_s[...] = jnp.full(m_s.shape, -jnp.inf, F32)

    params = (convw_ref, convb_ref, igb_ref, fgb_ref, lrup_ref, gateb_ref, mlnorm_ref, glanorm_ref)
    scratch = (xpad, q_s, k_s, e_pad, hml_s, hgla_s, cn_s, m_s, st_s)
    project(xp_ref[tb:2 * tb, :], 1)
    _mix_block(zm_s.at[0], zs_s.at[0], *params, y_ref.at[0:tb], *scratch, tb=tb)
    project(xn_ref[...], 0)
    _mix_block(zm_s.at[1], zs_s.at[1], *params, y_ref.at[tb:2 * tb], *scratch, tb=tb)


def _mix_block(zm_ref, zs_ref, convw_ref, convb_ref, igb_ref, fgb_ref, lrup_ref, gateb_ref, mlnorm_ref,
               glanorm_ref, y_ref, xpad, q_s, k_s, e_pad, hml_s, hgla_s, cn_s, m_s, st_s, *, tb):
    nc = tb // CHUNK

    xpad[8:8 + tb, :] = zm_ref[:, OFF_MQK:OFF_MQK + 2 * ML_QK_W].astype(F32)
    conv = convb_ref[...] + convw_ref[3:4, :] * xpad[8:8 + tb, :]
    for j in range(CONV_K - 1):
        conv = conv + convw_ref[j:j + 1, :] * xpad[5 + j:5 + j + tb, :]
    xpad[0:8, :] = xpad[tb:tb + 8, :]
    qk = conv * _sigmoid(conv)
    q_s[...] = qk[:, :ML_QK_W]
    k_s[...] = qk[:, ML_QK_W:] * (ML_DK ** -0.5)

    r_t = lax.broadcasted_iota(jnp.int32, (tb, tb), 0)
    c_t = lax.broadcasted_iota(jnp.int32, (tb, tb), 1)
    tri_blk = jnp.where((r_t // CHUNK == c_t // CHUNK) & (c_t <= r_t), 1.0, 0.0).astype(BF16)
    r_e = lax.broadcasted_iota(jnp.int32, (LANES, 2 * LANES), 0)
    c_e = lax.broadcasted_iota(jnp.int32, (LANES, 2 * LANES), 1)
    spread_dk = jnp.where(c_e // ML_DK == r_e, 1.0, 0.0).astype(BF16)
    mean_dv = jnp.full((ML_DV, ML_DV), 1.0 / ML_DV, F32).astype(BF16)
    row_c = lax.broadcasted_iota(jnp.int32, (CHUNK, CHUNK), 0)
    col_c = lax.broadcasted_iota(jnp.int32, (CHUNK, CHUNK), 1)
    causal = col_c <= row_c
    lane_c = lax.broadcasted_iota(jnp.int32, (CHUNK, LANES), 1)
    ones_dv = jnp.ones((CHUNK, ML_DV), BF16)
    chunk_rows = lambda c: slice(c * CHUNK, (c + 1) * CHUNK)

    zs = zs_ref[...]
    ig = zs + igb_ref[...]
    lf = _log_sigmoid(pltpu.roll(zs, LANES - SM_F, 1) + fgb_ref[...])
    bc = _dot01_left(tri_blk, lf, 3)
    bc3 = bc.reshape(nc, CHUNK, LANES)
    g3 = bc3[:, CHUNK - 1:CHUNK, :]
    a3 = g3 - bc3 + ig.reshape(nc, CHUNK, LANES)
    amax3 = jnp.max(a3, axis=1, keepdims=True)
    wa = jnp.exp(a3 - amax3).reshape(tb, LANES)
    m_run = m_s[...]
    m_prev, s_old, s_in = [], [], []
    for c in range(nc):
        m_new = jnp.maximum(g3[c] + m_run, amax3[c])
        m_prev.append(m_run)
        s_old.append(jnp.exp(g3[c] + m_run - m_new))
        s_in.append(jnp.exp(amax3[c] - m_new))
        m_run = m_new
    m_s[...] = m_run
    e_nat = ig - bc
    e_pad[0:CHUNK, :] = jnp.full((CHUNK, LANES), -jnp.inf, F32)
    e_pad[CHUNK:CHUNK + tb, :] = e_nat
    pos = lax.broadcasted_iota(jnp.int32, (tb, LANES), 0) % CHUNK
    shift = 1
    while shift < CHUNK:
        shifted = e_pad[CHUNK - shift:CHUNK - shift + tb, :]
        e_pad[CHUNK:CHUNK + tb, :] = jnp.maximum(e_pad[CHUNK:CHUNK + tb, :], jnp.where(pos >= shift, shifted, -jnp.inf))
        shift *= 2
    m_intra = bc + e_pad[CHUNK:CHUNK + tb, :]
    il = jnp.concatenate([bc3[c] + m_prev[c] for c in range(nc)], axis=0)
    mt = jnp.maximum(il, m_intra)
    s_inter = jnp.exp(il - mt)
    exp_neg = jnp.exp(-mt)
    xn = bc - mt
    e_up = pltpu.roll(e_nat, ML_HEADS, 1)
    wa_x = _dot01_right(wa, spread_dk, 2)
    si_x = _dot01_right(s_inter, spread_dk, 2)
    q_all = q_s[...]
    k_all = k_s[...]
    kw_b = (k_all * wa_x).astype(BF16)
    qs_b = (q_all * si_x).astype(BF16)
    q_b = q_all.astype(BF16)
    k_b = k_all.astype(BF16)

    qkm, upd, vo = {}, {}, {}
    for c in range(nc):
        rows = chunk_rows(c)
        for h in range(ML_HEADS):
            dk = slice(h * ML_DK, (h + 1) * ML_DK)
            x_t = jnp.where(lane_c == h, xn[rows], jnp.where(lane_c == h + ML_HEADS, 1.0, 0.0))
            y_t = jnp.where(lane_c == h, 1.0, jnp.where(lane_c == h + ML_HEADS, e_up[rows], 0.0))
            d = lax.dot_general(x_t, y_t, (((1,), (1,)), ((), ())), precision=HIGHEST, preferred_element_type=F32)
            p = jnp.where(causal, jnp.exp(d), 0.0)
            qkm[c, h] = (_dot_nt(q_b[rows, dk], k_b[rows, dk]) * p).astype(BF16)
            vo[c, h] = jnp.concatenate([zm_ref[rows, OFF_MV + h * ML_DV:OFF_MV + (h + 1) * ML_DV], ones_dv], axis=1)
            upd[c, h] = _dot_tn(kw_b[rows, dk], vo[c, h])
    for h in range(ML_HEADS):
        dk = slice(h * ML_DK, (h + 1) * ML_DK)
        dv = slice(h * ML_DV, (h + 1) * ML_DV)
        cn = cn_s[h]
        for c in range(nc):
            rows = chunk_rows(c)
            res = _dot(qs_b[rows, dk], cn.astype(BF16)) + _dot(qkm[c, h], vo[c, h])
            num, den = res[:, :ML_DV], res[:, ML_DV:]
            hml_s[rows, dv] = num / jnp.maximum(jnp.abs(den), exp_neg[rows, h:h + 1])
            cn = s_old[c][:, h:h + 1] * cn + s_in[c][:, h:h + 1] * upd[c, h]
        cn_s[h] = cn

    la = _log_sigmoid(_dot(zs.astype(BF16), lrup_ref[...]) + gateb_ref[...]) * (1.0 / GLA_TAU)
    bcg = _dot01_left(tri_blk, la, 3)
    bcg3 = bcg.reshape(nc, CHUNK, GLA_QK_W)
    gg3 = bcg3[:, CHUNK - 1:CHUNK, :]
    gq = zm_ref[:, OFF_GQ:OFF_GQ + GLA_QK_W].astype(F32)
    gk = zm_ref[:, OFF_GK:OFF_GK + GLA_QK_W].astype(F32) * (GLA_DK ** -0.5)
    q_dec = (gq * jnp.exp(bcg)).astype(BF16)
    k_inv = (gk * jnp.exp(-bcg)).astype(BF16)
    k_end = (gk * jnp.exp(gg3 - bcg3).reshape(tb, GLA_QK_W)).astype(BF16)
    eg3 = jnp.exp(gg3)
    att, updg = {}, {}
    for c in range(nc):
        rows = chunk_rows(c)
        for h in range(GLA_HEADS):
            dk = slice(h * GLA_DK, (h + 1) * GLA_DK)
            vh = zm_ref[rows, OFF_GV + h * GLA_DV:OFF_GV + (h + 1) * GLA_DV]
            att[c, h] = jnp.where(causal, _dot_nt(q_dec[rows, dk], k_inv[rows, dk]), 0.0).astype(BF16)
            updg[c, h] = _dot_tn(vh, k_end[rows, dk])
    for h in range(GLA_HEADS):
        dk = slice(h * GLA_DK, (h + 1) * GLA_DK)
        st = st_s[h]
        for c in range(nc):
            rows = chunk_rows(c)
            vh = zm_ref[rows, OFF_GV + h * GLA_DV:OFF_GV + (h + 1) * GLA_DV]
            hgla_s[rows, h * GLA_DV:(h + 1) * GLA_DV] = _dot_nt(q_dec[rows, dk], st.astype(BF16)) + _dot(att[c, h], vh)
            st = st * eg3[c][:, dk] + updg[c, h]
        st_s[h] = st

    mean = lambda x: _dot01_right(x, mean_dv, 2)
    for h in range(ML_HEADS):
        sl = slice(h * ML_DV, (h + 1) * ML_DV)
        hh = hml_s[:, sl]
        d = hh - mean(hh)
        hn = d * lax.rsqrt(mean(d * d) + EPS)
        og = zm_ref[:, OFF_MO + h * ML_DV:OFF_MO + (h + 1) * ML_DV].astype(F32)
        y_ref[:, sl] = (hn * mlnorm_ref[:, sl] * _sigmoid(og)).astype(y_ref.dtype)
    for h in range(GLA_HEADS):
        sl = slice(h * GLA_DV, (h + 1) * GLA_DV)
        o = hgla_s[:, sl]
        on = o * lax.rsqrt(mean(o * o) + EPS)
        gg = zm_ref[:, OFF_GG + h * GLA_DV:OFF_GG + (h + 1) * GLA_DV].astype(F32)
        y_ref[:, ML_V_W + h * GLA_DV:ML_V_W + (h + 1) * GLA_DV] = (on * glanorm_ref[:, sl] * (gg * _sigmoid(gg))).astype(y_ref.dtype)


def _mixer(h, g, w_main, w_small, convw, convb, igb, fgb, lrup, gateb, mlnorm, glanorm, bsz, seq, tb):
    nt = seq // (2 * tb)
    last_block = bsz * seq // tb - 1
    const = lambda shape: pl.BlockSpec(shape, lambda b, i: (0,) * len(shape))
    return pl.pallas_call(
        functools.partial(_mixer_kernel, tb=tb),
        out_shape=jax.ShapeDtypeStruct((bsz * seq, ML_V_W + GLA_V_W), BF16),
        grid=(bsz, nt),
        in_specs=[pl.BlockSpec((2 * tb, D_MODEL), lambda b, i: (b * nt + i, 0)),
                  pl.BlockSpec((tb, D_MODEL), lambda b, i: (jnp.minimum(2 * (b * nt + i) + 2, last_block), 0)),
                  const((1, D_MODEL)), const((D_MODEL, Z_MAIN_W)), const((D_MODEL, LANES)),
                  const((CONV_K, 2 * ML_QK_W)), const((1, 2 * ML_QK_W)), const((1, LANES)), const((1, LANES)),
                  const((LANES, GLA_QK_W)), const((1, GLA_QK_W)), const((1, ML_V_W)), const((1, GLA_V_W))],
        out_specs=pl.BlockSpec((2 * tb, ML_V_W + GLA_V_W), lambda b, i: (b * nt + i, 0)),
        scratch_shapes=[pltpu.VMEM((2, tb, Z_MAIN_W), BF16), pltpu.VMEM((2, tb, LANES), F32),
                        pltpu.VMEM((tb + 8, 2 * ML_QK_W), F32),
                        pltpu.VMEM((tb, ML_QK_W), F32), pltpu.VMEM((tb, ML_QK_W), F32),
                        pltpu.VMEM((tb + CHUNK, LANES), F32),
                        pltpu.VMEM((tb, ML_V_W), F32), pltpu.VMEM((tb, GLA_V_W), F32),
                        pltpu.VMEM((ML_HEADS, ML_DK, ML_DV + LANES), F32),
                        pltpu.VMEM((1, LANES), F32), pltpu.VMEM((GLA_HEADS, GLA_DV, GLA_DK), F32)],
        compiler_params=_cparams(("arbitrary", "arbitrary")),
        name="mlstm_gla",
    )(h, h, g, w_main, w_small, convw, convb, igb, fgb, lrup, gateb, mlnorm, glanorm)


def _proj_router_kernel(y_ref, w_ref, b_ref, h_ref, g_ref, wr_ref, br_ref, h1_ref, u2_ref, rt_ref):
    h1 = h_ref[...] + (_dot(y_ref[...], w_ref[...]) + b_ref[...])
    h1_ref[...] = h1
    u2 = _rms(h1, g_ref[...])
    for c in range(ROW_TILES):
        u2_ref[pl.ds(c, u2.shape[0], stride=ROW_TILES), :] = u2[:, c * LANES:(c + 1) * LANES]
    u_hi = u2.astype(BF16)
    u_lo = (u2 - u_hi.astype(F32)).astype(BF16)
    part = _dot(u_hi, wr_ref[...])
    logits = part[:, :LANES] + (part[:, LANES:] + _dot(u_lo, wr_ref[:, :LANES])) + br_ref[...]
    lane = lax.broadcasted_iota(jnp.int32, logits.shape, 1)
    lane_f = lane.astype(F32)
    big = float(LANES)
    gl = jnp.where(lane < N_GROUPS, logits, -jnp.inf)
    g_max = jnp.max(gl, axis=-1, keepdims=True)
    g_idx = jnp.min(jnp.where(gl == g_max, lane_f, big), axis=-1, keepdims=True)
    g_p = 1.0 / jnp.sum(jnp.exp(gl - g_max), axis=-1, keepdims=True)
    e_grp = ((lane - N_GROUPS) // EXPERTS_PER_GROUP).astype(F32)
    in_grp = (lane >= N_GROUPS) & (lane < N_GROUPS + N_EXPERTS) & (e_grp == g_idx)
    el = jnp.where(in_grp, logits, -jnp.inf)
    t1 = jnp.max(el, axis=-1, keepdims=True)
    i1 = jnp.min(jnp.where(el == t1, lane_f, big), axis=-1, keepdims=True)
    el2 = jnp.where(lane_f == i1, -jnp.inf, el)
    t2 = jnp.max(el2, axis=-1, keepdims=True)
    i2 = jnp.min(jnp.where(el2 == t2, lane_f, big), axis=-1, keepdims=True)
    e21 = jnp.exp(t2 - t1)
    p1 = 1.0 / (1.0 + e21)
    rt = jnp.where(lane == 0, i1 - N_GROUPS,
                   jnp.where(lane == 1, i2 - N_GROUPS,
                             jnp.where(lane == 2, g_p * p1, jnp.where(lane == 3, g_p * (e21 * p1), 0.0))))
    rt_ref[...] = rt


def _proj_router(y, w, b, h, g, wr, br, tm):
    t, kdim = y.shape
    row = lambda i: (i, 0)
    fixed = lambda i: (0, 0)
    return pl.pallas_call(
        _proj_router_kernel,
        out_shape=(jax.ShapeDtypeStruct((t, D_MODEL), F32), jax.ShapeDtypeStruct((t * ROW_TILES, LANES), F32),
                   jax.ShapeDtypeStruct((t, LANES), F32)),
        grid=(t // tm,),
        in_specs=[pl.BlockSpec((tm, kdim), row), pl.BlockSpec((kdim, D_MODEL), fixed),
                  pl.BlockSpec((1, D_MODEL), fixed), pl.BlockSpec((tm, D_MODEL), row),
                  pl.BlockSpec((1, D_MODEL), fixed), pl.BlockSpec((D_MODEL, 2 * LANES), fixed),
                  pl.BlockSpec((1, LANES), fixed)],
        out_specs=(pl.BlockSpec((tm, D_MODEL), row), pl.BlockSpec((tm * ROW_TILES, LANES), row),
                   pl.BlockSpec((tm, LANES), row)),
        compiler_params=_cparams(("parallel",)),
        name="proj_router",
    )(y, w, b, h, g, wr, br)


def _rank_kernel(rt_ref, rk_ref, cnt_ref, base_s, strict_s):
    tt = rt_ref.shape[0]

    @pl.when(pl.program_id(0) == 0)
    def _():
        base_s[...] = jnp.zeros(base_s.shape, F32)
        r = lax.broadcasted_iota(jnp.int32, (tt, tt), 0)
        c = lax.broadcasted_iota(jnp.int32, (tt, tt), 1)
        strict_s[...] = jnp.where(c < r, 1.0, 0.0).astype(BF16)

    rt = rt_ref[...]
    lane = lax.broadcasted_iota(jnp.int32, rt.shape, 1)
    lane_f = lane.astype(F32)
    e0, e1 = rt[:, 0:1], rt[:, 1:2]
    oh0 = lane_f == e0
    oh1 = lane_f == e1
    oh = jnp.where(oh0 | oh1, 1.0, 0.0)
    before = _dot(strict_s[...], oh.astype(BF16)) + base_s[...]
    r0 = jnp.sum(jnp.where(oh0, before, 0.0), axis=-1, keepdims=True)
    r1 = jnp.sum(jnp.where(oh1, before, 0.0), axis=-1, keepdims=True)
    table = jnp.where(lane == 0, r0, jnp.where(lane == 1, r1, jnp.where(lane == 2, e0, jnp.where(lane == 3, e1, 0.0))))
    rk_ref[...] = jnp.transpose(table)[0:8, :].astype(jnp.int32)
    base_s[...] = base_s[...] + jnp.sum(oh, axis=0, keepdims=True)
    cnt_ref[...] = base_s[...]


def _rank(rt, tt):
    t = rt.shape[0]
    return pl.pallas_call(
        _rank_kernel,
        out_shape=(jax.ShapeDtypeStruct((8, t), jnp.int32), jax.ShapeDtypeStruct((1, LANES), F32)),
        grid=(t // tt,),
        in_specs=[pl.BlockSpec((tt, LANES), lambda i: (i, 0))],
        out_specs=(pl.BlockSpec((8, tt), lambda i: (0, i)), pl.BlockSpec((1, LANES), lambda i: (0, 0))),
        scratch_shapes=[pltpu.VMEM((1, LANES), F32), pltpu.VMEM((tt, tt), BF16)],
        compiler_params=_cparams(("arbitrary",)),
        name="expert_rank",
    )(rt)


DMA_GROUP = 8


def _row_copy(src, dst, sem):
    return pltpu.make_async_copy(src, dst, sem)


def _row_tile(r):
    return pl.ds(pl.multiple_of(r * ROW_TILES, ROW_TILES), ROW_TILES)


def _dispatch_kernel(dest_ref, zflag_ref, u_ref, xout_hbm, inv_ref, fill_s, zero_s, sem, *, tt):
    i = pl.program_id(0)
    block_rows = MOE_BM * ROW_TILES

    @pl.when(i == 0)
    def _():
        fill_s[...] = jnp.full(fill_s.shape, -1, jnp.int32)
        fill = pltpu.make_async_copy(fill_s, inv_ref, sem)
        fill.start()
        fill.wait()
        zero_s[...] = jnp.zeros(zero_s.shape, F32)
        zero_block = lambda blk: _row_copy(zero_s, xout_hbm.at[pl.ds(blk * block_rows, block_rows)], sem)
        for blk in range(zflag_ref.shape[0]):
            @pl.when(zflag_ref[blk] != 0)
            def _():
                zero_block(blk).start()
        for blk in range(zflag_ref.shape[0]):
            @pl.when(zflag_ref[blk] != 0)
            def _():
                zero_block(blk).wait()

    def issue(g, carry):
        for jj in range(DMA_GROUP):
            j = g * DMA_GROUP + jj
            src = u_ref.at[_row_tile(j)]
            for k in range(TOP_K):
                d = dest_ref[k, j]
                _row_copy(src, xout_hbm.at[_row_tile(d)], sem).start(priority=k)
                inv_ref[d] = (i * tt + j) * TOP_K + k
        return carry

    lax.fori_loop(0, tt // DMA_GROUP, issue, 0)
    for k in range(TOP_K):
        _row_copy(u_ref, xout_hbm.at[pl.ds(0, tt * ROW_TILES)], sem).wait()


def _dispatch(dest, zflag, u2, tt):
    t = u2.shape[0] // ROW_TILES
    n_rows = zflag.shape[0] * MOE_BM
    return pl.pallas_call(
        functools.partial(_dispatch_kernel, tt=tt),
        out_shape=(jax.ShapeDtypeStruct((n_rows * ROW_TILES, LANES), F32), jax.ShapeDtypeStruct((n_rows,), jnp.int32)),
        grid=(t // tt,),
        in_specs=[pl.BlockSpec((TOP_K, tt), lambda i: (0, i), memory_space=pltpu.SMEM),
                  pl.BlockSpec(memory_space=pltpu.SMEM),
                  pl.BlockSpec((tt * ROW_TILES, LANES), lambda i: (i, 0))],
        out_specs=(pl.BlockSpec(memory_space=pl.ANY), pl.BlockSpec(memory_space=pltpu.SMEM)),
        scratch_shapes=[pltpu.VMEM((n_rows,), jnp.int32), pltpu.VMEM((MOE_BM * ROW_TILES, LANES), F32),
                        pltpu.SemaphoreType.DMA],
        compiler_params=_cparams(("arbitrary",)),
        name="moe_dispatch",
    )(dest, zflag, u2)


def _ffn_kernel(be_ref, nu_ref, inv_ref, x_ref, wg_ref, wu_ref, wd_ref, o2_hbm, wg_s, wu_s, wd_s, x_s, y_s, sem, *, n_slots):
    b = pl.program_id(0)
    last = pl.num_programs(0) - 1
    n_used = nu_ref[0]
    block_rows = MOE_BM * ROW_TILES

    def drain():
        _row_copy(y_s.at[0], o2_hbm.at[pl.ds(0, block_rows)], sem).wait()

    def scatter(blk):
        s = blk % 2
        for j in range(MOE_BM):
            a = inv_ref[blk * MOE_BM + j]
            slot = jnp.where(a >= 0, a, n_slots + s * MOE_BM + j)
            _row_copy(y_s.at[s, _row_tile(j)], o2_hbm.at[_row_tile(slot)], sem).start(priority=j % 2)

    def compute():
        for c in range(ROW_TILES):
            x_s[:, c * LANES:(c + 1) * LANES] = x_ref[pl.ds(c, MOE_BM, stride=ROW_TILES), :].astype(BF16)
        x = x_s[...]
        a = _dot(x, wg_s[...])
        u = _dot(x, wu_s[...])
        y = _dot(((a * _sigmoid(a)) * u).astype(BF16), wd_s[...])
        for c in range(ROW_TILES):
            y_s[b % 2, pl.ds(c, MOE_BM, stride=ROW_TILES), :] = y[:, c * LANES:(c + 1) * LANES]

    @pl.when(b == 0)
    def _():
        y_s[...] = jnp.zeros(y_s.shape, F32)
        for s in range(2):
            _row_copy(y_s.at[s], o2_hbm.at[pl.ds((n_slots + s * MOE_BM) * ROW_TILES, block_rows)], sem).start()
        for s in range(2):
            drain()

    @pl.when((b >= 2) & (b - 2 < n_used))
    def _():
        drain()

    @pl.when((b == 0) | (be_ref[b] != be_ref[jnp.maximum(b - 1, 0)]))
    def _():
        wg_s[...] = wg_ref[...].astype(BF16)
        wu_s[...] = wu_ref[...].astype(BF16)
        wd_s[...] = wd_ref[...].astype(BF16)

    @pl.when(b == 0)
    def _():
        compute()

    @pl.when((b >= 1) & (b < n_used))
    def _():
        scatter(b - 1)
        compute()

    @pl.when((b >= 1) & (b >= n_used) & (b - 1 < n_used))
    def _():
        scatter(b - 1)

    @pl.when(b == last)
    def _():
        @pl.when((b >= 1) & (b - 1 < n_used))
        def _():
            drain()

        @pl.when(b < n_used)
        def _():
            scatter(b)
            drain()


def _ffn(block_expert, n_used, inv, xbuf, wg, wu, wd, layer, n_slots):
    n_blocks = xbuf.shape[0] // (MOE_BM * ROW_TILES)
    rows = lambda b, be, nu, inv: (jnp.minimum(b, nu[0] - 1), 0)
    wmap = lambda b, be, nu, inv: (layer, be[b], 0, 0)
    return pl.pallas_call(
        functools.partial(_ffn_kernel, n_slots=n_slots),
        out_shape=jax.ShapeDtypeStruct(((n_slots + 2 * MOE_BM) * ROW_TILES, LANES), F32),
        grid_spec=pltpu.PrefetchScalarGridSpec(
            num_scalar_prefetch=3,
            grid=(n_blocks,),
            in_specs=[pl.BlockSpec((MOE_BM * ROW_TILES, LANES), rows),
                      pl.BlockSpec((None, None, D_MODEL, D_FF), wmap),
                      pl.BlockSpec((None, None, D_MODEL, D_FF), wmap),
                      pl.BlockSpec((None, None, D_FF, D_MODEL), wmap)],
            out_specs=pl.BlockSpec(memory_space=pl.ANY),
            scratch_shapes=[pltpu.VMEM((D_MODEL, D_FF), BF16), pltpu.VMEM((D_MODEL, D_FF), BF16),
                            pltpu.VMEM((D_FF, D_MODEL), BF16), pltpu.VMEM((MOE_BM, D_MODEL), BF16),
                            pltpu.VMEM((2, MOE_BM * ROW_TILES, LANES), F32), pltpu.SemaphoreType.DMA]),
        compiler_params=_cparams(("arbitrary",)),
        name="moe_ffn",
    )(block_expert, n_used, inv, xbuf, wg, wu, wd)


def _combine_kernel(o2_ref, h_ref, rt_ref, g_ref, o_ref, *, final_norm):
    tt = h_ref.shape[0]
    rt = rt_ref[...]
    w0, w1 = rt[:, 2:3], rt[:, 3:4]
    for c in range(ROW_TILES):
        lanes = slice(c * LANES, (c + 1) * LANES)
        y0 = o2_ref[pl.ds(c, tt, stride=TOP_K * ROW_TILES), :]
        y1 = o2_ref[pl.ds(ROW_TILES + c, tt, stride=TOP_K * ROW_TILES), :]
        o_ref[:, lanes] = h_ref[:, lanes] + (y0 * w0 + y1 * w1)
    if final_norm:
        o_ref[...] = _rms(o_ref[...], g_ref[...])


def _combine(o2, h1, rt, g, tt, final_norm):
    t = h1.shape[0]
    return pl.pallas_call(
        functools.partial(_combine_kernel, final_norm=final_norm),
        out_shape=jax.ShapeDtypeStruct((t, D_MODEL), F32),
        grid=(t // tt,),
        in_specs=[pl.BlockSpec((tt * TOP_K * ROW_TILES, LANES), lambda i: (i, 0)),
                  pl.BlockSpec((tt, D_MODEL), lambda i: (i, 0)),
                  pl.BlockSpec((tt, LANES), lambda i: (i, 0)),
                  pl.BlockSpec((1, D_MODEL), lambda i: (0, 0))],
        out_specs=pl.BlockSpec((tt, D_MODEL), lambda i: (i, 0)),
        compiler_params=_cparams(("parallel",)),
        name="moe_combine",
    )(o2, h1, rt, g)


def _moe(h1, u2, rt, wg, wu, wd, layer, g_final, final_norm):
    t = h1.shape[0]
    rk, cnt = _rank(rt, 1024)
    counts = cnt[0, :N_EXPERTS].astype(jnp.int32)
    padded = (counts + MOE_BM - 1) // MOE_BM * MOE_BM
    pad_end = jnp.cumsum(padded)
    pad_start = pad_end - padded
    n_blocks = (t * TOP_K) // MOE_BM + N_EXPERTS
    is_expert = rk[TOP_K:2 * TOP_K][None] == jnp.arange(N_EXPERTS, dtype=jnp.int32)[:, None, None]
    dest = jnp.sum(jnp.where(is_expert, pad_start[:, None, None], 0), axis=0) + rk[0:TOP_K]
    n_used = (pad_end[-1] // MOE_BM).astype(jnp.int32)
    blocks = jnp.arange(n_blocks, dtype=jnp.int32)
    blk = jnp.minimum(blocks, n_used - 1) * MOE_BM
    block_expert = jnp.minimum(jnp.sum(pad_end[None, :] <= blk[:, None], axis=1), N_EXPERTS - 1).astype(jnp.int32)
    closes_expert = jnp.any(((blocks[:, None] + 1) * MOE_BM == pad_end[None, :]) & (padded[None, :] > 0), axis=1)
    zflag = ((blocks >= n_used) | closes_expert).astype(jnp.int32)
    xbuf, inv = _dispatch(dest, zflag, u2, 1024)
    o2 = _ffn(block_expert, n_used.reshape(1), inv, xbuf, wg, wu, wd, layer, t * TOP_K)
    return _combine(o2, h1, rt, g_final, 512, final_norm)


def _qkv_kernel(x_ref, g_ref, w_ref, b_ref, pos_ref, freq_ref, sp_ref, q_ref, kv_ref):
    u = _rms(x_ref[...], g_ref[...]).astype(BF16)
    ang = freq_ref[...] * pos_ref[...].astype(F32)
    spread = lambda v, m: sum(_dot_tn(t.astype(F32), m) for t in _split_terms(v, 3))
    cosv = jnp.cos(ang)
    sinv = jnp.sin(ang)
    c_coef = spread(cosv, sp_ref[0]) + sp_ref[3, 0:1, :]
    s_lo = spread(sinv, sp_ref[1])
    s_hi = spread(sinv, sp_ref[2])

    def rotate(z):
        return z * c_coef + pltpu.roll(z, LANES - ROT_DIM // 2, 1) * s_lo + pltpu.roll(z, ROT_DIM // 2, 1) * s_hi

    scale = SWA_HEAD_DIM ** -0.5
    for j in range(Q_W // LANES):
        sl = slice(j * LANES, (j + 1) * LANES)
        q_ref[:, sl] = (rotate(_dot(u, w_ref[:, sl]) + b_ref[:, sl]) * scale).astype(q_ref.dtype)
    for j in range(KV_OUT_W // LANES):
        sl = slice(Q_W + j * LANES, Q_W + (j + 1) * LANES)
        z = _dot(u, w_ref[:, sl]) + b_ref[:, sl]
        kv_ref[:, j * LANES:(j + 1) * LANES] = (rotate(z) if j % 2 == 0 else z).astype(kv_ref.dtype)


def _rot_tables():
    half = ROT_DIM // 2
    inv_freq = (ROPE_THETA ** (-jnp.arange(0, ROT_DIM, 2, dtype=F32) / ROT_DIM)).reshape(half, 1)
    d = np.arange(LANES) % SWA_HEAD_DIM
    f = np.arange(half)[:, None]
    sp = np.zeros((4, half, LANES), np.float32)
    sp[0] = (d[None, :] < ROT_DIM) & (d[None, :] % half == f)
    sp[1] = -((d[None, :] < half) & (d[None, :] == f)).astype(np.float32)
    sp[2] = (d[None, :] >= half) & (d[None, :] < ROT_DIM) & (d[None, :] - half == f)
    sp[3, 0] = d >= ROT_DIM
    return inv_freq, jnp.asarray(sp)


def _qkv(h, g, w, b, pos, tm):
    t = h.shape[0]
    row = lambda i: (i, 0)
    fixed = lambda i: (0, 0)
    wtot = Q_W + KV_OUT_W
    return pl.pallas_call(
        _qkv_kernel,
        out_shape=(jax.ShapeDtypeStruct((t, Q_W), BF16), jax.ShapeDtypeStruct((t, KV_OUT_W), BF16)),
        grid=(t // tm,),
        in_specs=[pl.BlockSpec((tm, D_MODEL), row), pl.BlockSpec((1, D_MODEL), fixed),
                  pl.BlockSpec((D_MODEL, wtot), fixed), pl.BlockSpec((1, wtot), fixed),
                  pl.BlockSpec((1, tm), lambda i: (0, i)), pl.BlockSpec((ROT_DIM // 2, 1), fixed),
                  pl.BlockSpec((4, ROT_DIM // 2, LANES), lambda i: (0, 0, 0))],
        out_specs=(pl.BlockSpec((tm, Q_W), row), pl.BlockSpec((tm, KV_OUT_W), row)),
        compiler_params=_cparams(("parallel",)),
        name="qkv_rotary",
    )(h, g, w, b, pos, *_rot_tables())


def _swa_kernel(q_ref, kvc_ref, kvp_ref, sink_ref, o_ref):
    neg = jnp.where(pl.program_id(1) > 0, 0.0, -jnp.inf).astype(F32)
    kj = lax.broadcasted_iota(jnp.int32, (WINDOW, WINDOW), 0)
    qi = lax.broadcasted_iota(jnp.int32, (WINDOW, WINDOW), 1)
    mask_cur = kj <= qi
    top = kj < SWA_HEAD_DIM
    lane = lax.broadcasted_iota(jnp.int32, (1, LANES), 1)
    keep_lo = jnp.where(lane < SWA_HEAD_DIM, 1.0, 0.0).astype(BF16)
    keep_hi = jnp.where(lane < SWA_HEAD_DIM, 0.0, 1.0).astype(BF16)
    zeros_half = jnp.zeros((SWA_HEAD_DIM, WINDOW), F32)

    def arranged(ref):
        k_nat, k_swp = ref[:, 0:KV_W], ref[:, 2 * KV_W:3 * KV_W]
        vt = jnp.transpose(ref[:, KV_W:2 * KV_W].astype(F32))
        vt_top = lambda g: jnp.concatenate([vt[g * SWA_HEAD_DIM:(g + 1) * SWA_HEAD_DIM], zeros_half], axis=0).astype(BF16)
        vt_bot = lambda g: jnp.concatenate([zeros_half, vt[g * SWA_HEAD_DIM:(g + 1) * SWA_HEAD_DIM]], axis=0).astype(BF16)
        return {(0, 0): (k_nat * keep_lo, vt_top(0)), (0, 1): (k_swp * keep_hi, vt_bot(0)),
                (1, 0): (k_swp * keep_lo, vt_top(1)), (1, 1): (k_nat * keep_hi, vt_bot(1))}

    cur = arranged(kvc_ref)
    prev = arranged(kvp_ref)
    heads = range(SWA_Q_HEADS)
    key = lambda h: (h // SWA_GROUP, h % 2)
    scores = []
    for h in heads:
        qp = q_ref[:, (h // 2) * LANES:(h // 2 + 1) * LANES]
        sc = _dot_nt(cur[key(h)][0], qp)
        sp = _dot_nt(prev[key(h)][0], qp)
        scores.append(jnp.where(mask_cur, sc, sp + neg))
    sinks = [sink_ref[:, h:h + 1] for h in heads]
    maxes = [jnp.maximum(jnp.max(scores[h], axis=0, keepdims=True), sinks[h]) for h in heads]
    probs = [jnp.exp(scores[h] - maxes[h]) for h in heads]
    denoms = [jnp.sum(probs[h], axis=0, keepdims=True) + jnp.exp(sinks[h] - maxes[h]) for h in heads]
    for pr in range(SWA_Q_HEADS // 2):
        num = None
        for h in (2 * pr, 2 * pr + 1):
            pc = jnp.where(mask_cur, probs[h], 0.0).astype(BF16)
            pp = jnp.where(mask_cur, 0.0, probs[h]).astype(BF16)
            part = _dot(cur[key(h)][1], pc) + _dot(prev[key(h)][1], pp)
            num = part if num is None else num + part
        den = jnp.where(top, denoms[2 * pr], denoms[2 * pr + 1])
        o_ref[:, pr * LANES:(pr + 1) * LANES] = jnp.transpose(num / den).astype(o_ref.dtype)


def _swa(q, kv, sinks, bsz, seq):
    nb = seq // WINDOW
    return pl.pallas_call(
        _swa_kernel,
        out_shape=jax.ShapeDtypeStruct((bsz * seq, Q_W), BF16),
        grid=(bsz, nb),
        in_specs=[pl.BlockSpec((WINDOW, Q_W), lambda b, n: (b * nb + n, 0)),
                  pl.BlockSpec((WINDOW, KV_OUT_W), lambda b, n: (b * nb + n, 0)),
                  pl.BlockSpec((WINDOW, KV_OUT_W), lambda b, n: (b * nb + jnp.maximum(n - 1, 0), 0)),
                  pl.BlockSpec((1, LANES), lambda b, n: (0, 0))],
        out_specs=pl.BlockSpec((WINDOW, Q_W), lambda b, n: (b * nb + n, 0)),
        compiler_params=_cparams(("parallel", "parallel")),
        name="swa_attention",
    )(q, kv, kv, sinks)


def _pack_in_proj(w_in):
    offs = np.cumsum((0,) + AB_SPLITS)
    part = lambda i: w_in[:, offs[i]:offs[i + 1]]
    w_main = jnp.concatenate([part(0), part(1), part(2), part(5), part(6), part(7), part(8)], axis=1).astype(BF16)
    small = jnp.concatenate([part(3), part(4), part(9)], axis=1)
    w_small = jnp.pad(small, ((0, 0), (0, LANES - small.shape[1]))).astype(BF16)
    return w_main, w_small


def _pack_qkv(w, b):
    hd = SWA_HEAD_DIM
    k0, k1 = slice(Q_W, Q_W + hd), slice(Q_W + hd, Q_W + 2 * hd)
    cols = lambda a: jnp.concatenate([a[..., :Q_W + 2 * KV_W], a[..., k1], a[..., k0]], axis=-1)
    return cols(w).astype(BF16), cols(b).reshape(1, -1).astype(F32)


def _lane_row(v, offset=0):
    return jnp.zeros((1, LANES), F32).at[0, offset:offset + v.shape[0]].set(v.astype(F32))


def kernel(x, positions, mix_norm, ffn_norm, final_norm, ab_w_in, ab_conv_w, ab_conv_b, ml_igate_b, ml_fgate_b, ml_head_norm, gla_w_lr_up, gla_gate_b, gla_head_norm, ab_w_out, swa_w_qkv, swa_b_qkv, swa_sinks, swa_w_o, swa_b_o, router_group_w, router_group_b, router_expert_w, router_expert_b, expert_w_gate, expert_w_up, expert_w_down):
    bsz, seq, d = x.shape
    t = bsz * seq
    h = x.reshape(t, d)
    row = lambda v: v.reshape(1, -1).astype(F32)

    def router_params(layer):
        wr = jnp.zeros((d, LANES), F32).at[:, :N_GROUPS].set(router_group_w[layer])
        wr = wr.at[:, N_GROUPS:N_GROUPS + N_EXPERTS].set(router_expert_w[layer])
        br = jnp.zeros((1, LANES), F32).at[0, :N_GROUPS].set(router_group_b[layer])
        br = br.at[0, N_GROUPS:N_GROUPS + N_EXPERTS].set(router_expert_b[layer])
        w_hi = wr.astype(BF16)
        w_lo = (wr - w_hi.astype(F32)).astype(BF16)
        return jnp.concatenate([w_hi, w_lo], axis=1), br

    def experts(layer):
        return expert_w_gate, expert_w_up, expert_w_down, layer

    w_main, w_small = _pack_in_proj(ab_w_in[0])
    lrup = jnp.zeros((LANES, GLA_QK_W), F32).at[SM_LR:SM_LR + GLA_LOWRANK].set(gla_w_lr_up[0]).astype(BF16)
    y = _mixer(h, row(mix_norm[0]), w_main, w_small, ab_conv_w[0], row(ab_conv_b[0]), _lane_row(ml_igate_b[0]),
               _lane_row(ml_fgate_b[0]), lrup, row(gla_gate_b[0]), row(ml_head_norm[0]), row(gla_head_norm[0]),
               bsz, seq, 256)
    wr, br = router_params(0)
    h1, u2, rt = _proj_router(y, ab_w_out[0].astype(BF16), jnp.zeros((1, d), F32), h, row(ffn_norm[0]), wr, br, 512)
    h = _moe(h1, u2, rt, *experts(0), row(final_norm), False)

    w_qkv, b_qkv = _pack_qkv(swa_w_qkv[0], swa_b_qkv[0])
    q, kv = _qkv(h, row(mix_norm[1]), w_qkv, b_qkv, positions.reshape(1, t), 512)
    o = _swa(q, kv, _lane_row(swa_sinks[0]), bsz, seq)
    wr, br = router_params(1)
    h1, u2, rt = _proj_router(o, swa_w_o[0].astype(BF16), row(swa_b_o[0]), h, row(ffn_norm[1]), wr, br, 512)
    out = _moe(h1, u2, rt, *experts(1), row(final_norm), True)
    return out.reshape(bsz, seq, d)
```

```python
import functools

import jax
import jax.numpy as jnp
import numpy as np
from jax import lax
from jax.experimental import pallas as pl
from jax.experimental.pallas import tpu as pltpu

F32 = jnp.float32
BF16 = jnp.bfloat16
HIGHEST = lax.Precision.HIGHEST

D_MODEL = 1024
EPS = 1e-6
ML_HEADS = 4
ML_DK = 64
ML_DV = 128
GLA_HEADS = 4
GLA_DK = 64
GLA_DV = 128
CHUNK = 64
CONV_K = 4
GLA_LOWRANK = 16
GLA_TAU = 16.0
ML_QK_W = ML_HEADS * ML_DK
ML_V_W = ML_HEADS * ML_DV
GLA_QK_W = GLA_HEADS * GLA_DK
GLA_V_W = GLA_HEADS * GLA_DV
AB_SPLITS = (2 * ML_QK_W, ML_V_W, ML_V_W, ML_HEADS, ML_HEADS, GLA_QK_W, GLA_QK_W, GLA_V_W, GLA_V_W, GLA_LOWRANK)
OFF_MQK = 0
OFF_MV = OFF_MQK + 2 * ML_QK_W
OFF_MO = OFF_MV + ML_V_W
OFF_GQ = OFF_MO + ML_V_W
OFF_GK = OFF_GQ + GLA_QK_W
OFF_GV = OFF_GK + GLA_QK_W
OFF_GG = OFF_GV + GLA_V_W
Z_MAIN_W = OFF_GG + GLA_V_W
SM_I = 0
SM_F = ML_HEADS
SM_LR = 2 * ML_HEADS
LANES = 128
SWA_Q_HEADS = 16
SWA_KV_HEADS = 2
SWA_HEAD_DIM = 64
SWA_GROUP = SWA_Q_HEADS // SWA_KV_HEADS
WINDOW = 128
ROT_DIM = SWA_HEAD_DIM // 4
ROPE_THETA = 500000.0
Q_W = SWA_Q_HEADS * SWA_HEAD_DIM
KV_W = SWA_KV_HEADS * SWA_HEAD_DIM
KV_OUT_W = 3 * KV_W
N_GROUPS = 4
EXPERTS_PER_GROUP = 8
N_EXPERTS = N_GROUPS * EXPERTS_PER_GROUP
TOP_K = 2
D_FF = 512
MOE_BM = 512
ROW_TILES = D_MODEL // LANES

VMEM_LIMIT = 56 * 1024 * 1024


def _cparams(sem):
    return pltpu.CompilerParams(dimension_semantics=sem, vmem_limit_bytes=VMEM_LIMIT)


def _rms(x, g):
    return x * lax.rsqrt(jnp.mean(x * x, axis=-1, keepdims=True) + EPS) * g


def _log_sigmoid(x):
    return jnp.minimum(x, 0.0) - jnp.log1p(jnp.exp(-jnp.abs(x)))


def _sigmoid(x):
    return 1.0 / (1.0 + jnp.exp(-x))


def _dot(a, b):
    return jnp.dot(a, b, preferred_element_type=F32)


def _dot_nt(a, b):
    return lax.dot_general(a, b, (((1,), (1,)), ((), ())), preferred_element_type=F32)


def _dot_tn(a, b):
    return lax.dot_general(a, b, (((0,), (0,)), ((), ())), preferred_element_type=F32)


IN_PROJ_CHUNK = 768
def _split_terms(x, n):
    terms = []
    for _ in range(n):
        t = x.astype(BF16)
        terms.append(t)
        x = x - t.astype(F32)
    return terms


def _dot01_left(m01, x, n):
    return sum(_dot(m01, t) for t in _split_terms(x, n))


def _dot01_right(x, m01, n):
    return sum(_dot(t, m01) for t in _split_terms(x, n))


def _mixer_kernel(xp_ref, xn_ref, g_ref, wm_ref, ws_ref, convw_ref, convb_ref, igb_ref, fgb_ref, lrup_ref, gateb_ref,
                  mlnorm_ref, glanorm_ref, y_ref, zm_s, zs_s, xpad, q_s, k_s, e_pad, hml_s, hgla_s, cn_s, m_s, st_s,
                  *, tb):
    step = pl.program_id(0) * pl.num_programs(1) + pl.program_id(1)

    def project(x, slot):
        u = _rms(x, g_ref[...]).astype(BF16)
        for n0 in range(0, Z_MAIN_W, IN_PROJ_CHUNK):
            zm_s[slot, :, n0:n0 + IN_PROJ_CHUNK] = _dot(u, wm_ref[:, n0:n0 + IN_PROJ_CHUNK]).astype(BF16)
        zs_s[slot] = _dot(u, ws_ref[...])

    @pl.when(step == 0)
    def _():
        project(xp_ref[0:tb, :], 0)

    @pl.when(pl.program_id(1) == 0)
    def _():
        xpad[0:8, :] = jnp.zeros((8, 2 * ML_QK_W), F32)
        cn_s[...] = jnp.zeros(cn_s.shape, F32)
        st_s[...] = jnp.zeros(st_s.shape, F32)
        m_s[...] = jnp.full(m_s.shape, -jnp.inf, F32)

    params = (convw_ref, convb_ref, igb_ref, fgb_ref, lrup_ref, gateb_ref, mlnorm_ref, glanorm_ref)
    scratch = (xpad, q_s, k_s, e_pad, hml_s, hgla_s, cn_s, m_s, st_s)
    project(xp_ref[tb:2 * tb, :], 1)
    _mix_block(zm_s.at[0], zs_s.at[0], *params, y_ref.at[0:tb], *scratch, tb=tb)
    project(xn_ref[...], 0)
    _mix_block(zm_s.at[1], zs_s.at[1], *params, y_ref.at[tb:2 * tb], *scratch, tb=tb)


def _mix_block(zm_ref, zs_ref, convw_ref, convb_ref, igb_ref, fgb_ref, lrup_ref, gateb_ref, mlnorm_ref,
               glanorm_ref, y_ref, xpad, q_s, k_s, e_pad, hml_s, hgla_s, cn_s, m_s, st_s, *, tb):
    nc = tb // CHUNK

    xpad[8:8 + tb, :] = zm_ref[:, OFF_MQK:OFF_MQK + 2 * ML_QK_W].astype(F32)
    conv = convb_ref[...] + convw_ref[3:4, :] * xpad[8:8 + tb, :]
    for j in range(CONV_K - 1):
        conv = conv + convw_ref[j:j + 1, :] * xpad[5 + j:5 + j + tb, :]
    xpad[0:8, :] = xpad[tb:tb + 8, :]
    qk = conv * _sigmoid(conv)
    q_s[...] = qk[:, :ML_QK_W]
    k_s[...] = qk[:, ML_QK_W:] * (ML_DK ** -0.5)

    r_t = lax.broadcasted_iota(jnp.int32, (tb, tb), 0)
    c_t = lax.broadcasted_iota(jnp.int32, (tb, tb), 1)
    tri_blk = jnp.where((r_t // CHUNK == c_t // CHUNK) & (c_t <= r_t), 1.0, 0.0).astype(BF16)
    r_e = lax.broadcasted_iota(jnp.int32, (LANES, 2 * LANES), 0)
    c_e = lax.broadcasted_iota(jnp.int32, (LANES, 2 * LANES), 1)
    spread_dk = jnp.where(c_e // ML_DK == r_e, 1.0, 0.0).astype(BF16)
    mean_dv = jnp.full((ML_DV, ML_DV), 1.0 / ML_DV, F32).astype(BF16)
    row_c = lax.broadcasted_iota(jnp.int32, (CHUNK, CHUNK), 0)
    col_c = lax.broadcasted_iota(jnp.int32, (CHUNK, CHUNK), 1)
    causal = col_c <= row_c
    lane_c = lax.broadcasted_iota(jnp.int32, (CHUNK, LANES), 1)
    ones_dv = jnp.ones((CHUNK, ML_DV), BF16)
    chunk_rows = lambda c: slice(c * CHUNK, (c + 1) * CHUNK)

    zs = zs_ref[...]
    ig = zs + igb_ref[...]
    lf = _log_sigmoid(pltpu.roll(zs, LANES - SM_F, 1) + fgb_ref[...])
    bc = _dot01_left(tri_blk, lf, 3)
    bc3 = bc.reshape(nc, CHUNK, LANES)
    g3 = bc3[:, CHUNK - 1:CHUNK, :]
    a3 = g3 - bc3 + ig.reshape(nc, CHUNK, LANES)
    amax3 = jnp.max(a3, axis=1, keepdims=True)
    wa = jnp.exp(a3 - amax3).reshape(tb, LANES)
    m_run = m_s[...]
    m_prev, s_old, s_in = [], [], []
    for c in range(nc):
        m_new = jnp.maximum(g3[c] + m_run, amax3[c])
        m_prev.append(m_run)
        s_old.append(jnp.exp(g3[c] + m_run - m_new))
        s_in.append(jnp.exp(amax3[c] - m_new))
        m_run = m_new
    m_s[...] = m_run
    e_nat = ig - bc
    e_pad[0:CHUNK, :] = jnp.full((CHUNK, LANES), -jnp.inf, F32)
    e_pad[CHUNK:CHUNK + tb, :] = e_nat
    pos = lax.broadcasted_iota(jnp.int32, (tb, LANES), 0) % CHUNK
    shift = 1
    while shift < CHUNK:
        shifted = e_pad[CHUNK - shift:CHUNK - shift + tb, :]
        e_pad[CHUNK:CHUNK + tb, :] = jnp.maximum(e_pad[CHUNK:CHUNK + tb, :], jnp.where(pos >= shift, shifted, -jnp.inf))
        shift *= 2
    m_intra = bc + e_pad[CHUNK:CHUNK + tb, :]
    il = jnp.concatenate([bc3[c] + m_prev[c] for c in range(nc)], axis=0)
    mt = jnp.maximum(il, m_intra)
    s_inter = jnp.exp(il - mt)
    exp_neg = jnp.exp(-mt)
    xn = bc - mt
    e_up = pltpu.roll(e_nat, ML_HEADS, 1)
    wa_x = _dot01_right(wa, spread_dk, 2)
    si_x = _dot01_right(s_inter, spread_dk, 2)
    q_all = q_s[...]
    k_all = k_s[...]
    kw_b = (k_all * wa_x).astype(BF16)
    qs_b = (q_all * si_x).astype(BF16)
    q_b = q_all.astype(BF16)
    k_b = k_all.astype(BF16)

    qkm, upd, vo = {}, {}, {}
    for c in range(nc):
        rows = chunk_rows(c)
        for h in range(ML_HEADS):
            dk = slice(h * ML_DK, (h + 1) * ML_DK)
            x_t = jnp.where(lane_c == h, xn[rows], jnp.where(lane_c == h + ML_HEADS, 1.0, 0.0))
            y_t = jnp.where(lane_c == h, 1.0, jnp.where(lane_c == h + ML_HEADS, e_up[rows], 0.0))
            d = lax.dot_general(x_t, y_t, (((1,), (1,)), ((), ())), precision=HIGHEST, preferred_element_type=F32)
            p = jnp.where(causal, jnp.exp(d), 0.0)
            qkm[c, h] = (_dot_nt(q_b[rows, dk], k_b[rows, dk]) * p).astype(BF16)
            vo[c, h] = jnp.concatenate([zm_ref[rows, OFF_MV + h * ML_DV:OFF_MV + (h + 1) * ML_DV], ones_dv], axis=1)
            upd[c, h] = _dot_tn(kw_b[rows, dk], vo[c, h])
    for h in range(ML_HEADS):
        dk = slice(h * ML_DK, (h + 1) * ML_DK)
        dv = slice(h * ML_DV, (h + 1) * ML_DV)
        cn = cn_s[h]
        for c in range(nc):
            rows = chunk_rows(c)
            res = _dot(qs_b[rows, dk], cn.astype(BF16)) + _dot(qkm[c, h], vo[c, h])
            num, den = res[:, :ML_DV], res[:, ML_DV:]
            hml_s[rows, dv] = num / jnp.maximum(jnp.abs(den), exp_neg[rows, h:h + 1])
            cn = s_old[c][:, h:h + 1] * cn + s_in[c][:, h:h + 1] * upd[c, h]
        cn_s[h] = cn

    la = _log_sigmoid(_dot(zs.astype(BF16), lrup_ref[...]) + gateb_ref[...]) * (1.0 / GLA_TAU)
    bcg = _dot01_left(tri_blk, la, 3)
    bcg3 = bcg.reshape(nc, CHUNK, GLA_QK_W)
    gg3 = bcg3[:, CHUNK - 1:CHUNK, :]
    gq = zm_ref[:, OFF_GQ:OFF_GQ + GLA_QK_W].astype(F32)
    gk = zm_ref[:, OFF_GK:OFF_GK + GLA_QK_W].astype(F32) * (GLA_DK ** -0.5)
    q_dec = (gq * jnp.exp(bcg)).astype(BF16)
    k_inv = (gk * jnp.exp(-bcg)).astype(BF16)
    k_end = (gk * jnp.exp(gg3 - bcg3).reshape(tb, GLA_QK_W)).astype(BF16)
    eg3 = jnp.exp(gg3)
    att, updg = {}, {}
    for c in range(nc):
        rows = chunk_rows(c)
        for h in range(GLA_HEADS):
            dk = slice(h * GLA_DK, (h + 1) * GLA_DK)
            vh = zm_ref[rows, OFF_GV + h * GLA_DV:OFF_GV + (h + 1) * GLA_DV]
            att[c, h] = jnp.where(causal, _dot_nt(q_dec[rows, dk], k_inv[rows, dk]), 0.0).astype(BF16)
            updg[c, h] = _dot_tn(vh, k_end[rows, dk])
    for h in range(GLA_HEADS):
        dk = slice(h * GLA_DK, (h + 1) * GLA_DK)
        st = st_s[h]
        for c in range(nc):
            rows = chunk_rows(c)
            vh = zm_ref[rows, OFF_GV + h * GLA_DV:OFF_GV + (h + 1) * GLA_DV]
            hgla_s[rows, h * GLA_DV:(h + 1) * GLA_DV] = _dot_nt(q_dec[rows, dk], st.astype(BF16)) + _dot(att[c, h], vh)
            st = st * eg3[c][:, dk] + updg[c, h]
        st_s[h] = st

    mean = lambda x: _dot01_right(x, mean_dv, 2)
    for h in range(ML_HEADS):
        sl = slice(h * ML_DV, (h + 1) * ML_DV)
        hh = hml_s[:, sl]
        d = hh - mean(hh)
        hn = d * lax.rsqrt(mean(d * d) + EPS)
        og = zm_ref[:, OFF_MO + h * ML_DV:OFF_MO + (h + 1) * ML_DV].astype(F32)
        y_ref[:, sl] = (hn * mlnorm_ref[:, sl] * _sigmoid(og)).astype(y_ref.dtype)
    for h in range(GLA_HEADS):
        sl = slice(h * GLA_DV, (h + 1) * GLA_DV)
        o = hgla_s[:, sl]
        on = o * lax.rsqrt(mean(o * o) + EPS)
        gg = zm_ref[:, OFF_GG + h * GLA_DV:OFF_GG + (h + 1) * GLA_DV].astype(F32)
        y_ref[:, ML_V_W + h * GLA_DV:ML_V_W + (h + 1) * GLA_DV] = (on * glanorm_ref[:, sl] * (gg * _sigmoid(gg))).astype(y_ref.dtype)


def _mixer(h, g, w_main, w_small, convw, convb, igb, fgb, lrup, gateb, mlnorm, glanorm, bsz, seq, tb):
    nt = seq // (2 * tb)
    last_block = bsz * seq // tb - 1
    const = lambda shape: pl.BlockSpec(shape, lambda b, i: (0,) * len(shape))
    return pl.pallas_call(
        functools.partial(_mixer_kernel, tb=tb),
        out_shape=jax.ShapeDtypeStruct((bsz * seq, ML_V_W + GLA_V_W), BF16),
        grid=(bsz, nt),
        in_specs=[pl.BlockSpec((2 * tb, D_MODEL), lambda b, i: (b * nt + i, 0)),
                  pl.BlockSpec((tb, D_MODEL), lambda b, i: (jnp.minimum(2 * (b * nt + i) + 2, last_block), 0)),
                  const((1, D_MODEL)), const((D_MODEL, Z_MAIN_W)), const((D_MODEL, LANES)),
                  const((CONV_K, 2 * ML_QK_W)), const((1, 2 * ML_QK_W)), const((1, LANES)), const((1, LANES)),
                  const((LANES, GLA_QK_W)), const((1, GLA_QK_W)), const((1, ML_V_W)), const((1, GLA_V_W))],
        out_specs=pl.BlockSpec((2 * tb, ML_V_W + GLA_V_W), lambda b, i: (b * nt + i, 0)),
        scratch_shapes=[pltpu.VMEM((2, tb, Z_MAIN_W), BF16), pltpu.VMEM((2, tb, LANES), F32),
                        pltpu.VMEM((tb + 8, 2 * ML_QK_W), F32),
                        pltpu.VMEM((tb, ML_QK_W), F32), pltpu.VMEM((tb, ML_QK_W), F32),
                        pltpu.VMEM((tb + CHUNK, LANES), F32),
                        pltpu.VMEM((tb, ML_V_W), F32), pltpu.VMEM((tb, GLA_V_W), F32),
                        pltpu.VMEM((ML_HEADS, ML_DK, ML_DV + LANES), F32),
                        pltpu.VMEM((1, LANES), F32), pltpu.VMEM((GLA_HEADS, GLA_DV, GLA_DK), F32)],
        compiler_params=_cparams(("arbitrary", "arbitrary")),
        name="mlstm_gla",
    )(h, h, g, w_main, w_small, convw, convb, igb, fgb, lrup, gateb, mlnorm, glanorm)


def _proj_router_kernel(y_ref, w_ref, b_ref, h_ref, g_ref, wr_ref, br_ref, h1_ref, u2_ref, rt_ref):
    h1 = h_ref[...] + (_dot(y_ref[...], w_ref[...]) + b_ref[...])
    h1_ref[...] = h1
    u2 = _rms(h1, g_ref[...])
    for c in range(ROW_TILES):
        u2_ref[pl.ds(c, u2.shape[0], stride=ROW_TILES), :] = u2[:, c * LANES:(c + 1) * LANES]
    u_hi = u2.astype(BF16)
    u_lo = (u2 - u_hi.astype(F32)).astype(BF16)
    part = _dot(u_hi, wr_ref[...])
    logits = part[:, :LANES] + (part[:, LANES:] + _dot(u_lo, wr_ref[:, :LANES])) + br_ref[...]
    lane = lax.broadcasted_iota(jnp.int32, logits.shape, 1)
    lane_f = lane.astype(F32)
    big = float(LANES)
    gl = jnp.where(lane < N_GROUPS, logits, -jnp.inf)
    g_max = jnp.max(gl, axis=-1, keepdims=True)
    g_idx = jnp.min(jnp.where(gl == g_max, lane_f, big), axis=-1, keepdims=True)
    g_p = 1.0 / jnp.sum(jnp.exp(gl - g_max), axis=-1, keepdims=True)
    e_grp = ((lane - N_GROUPS) // EXPERTS_PER_GROUP).astype(F32)
    in_grp = (lane >= N_GROUPS) & (lane < N_GROUPS + N_EXPERTS) & (e_grp == g_idx)
    el = jnp.where(in_grp, logits, -jnp.inf)
    t1 = jnp.max(el, axis=-1, keepdims=True)
    i1 = jnp.min(jnp.where(el == t1, lane_f, big), axis=-1, keepdims=True)
    el2 = jnp.where(lane_f == i1, -jnp.inf, el)
    t2 = jnp.max(el2, axis=-1, keepdims=True)
    i2 = jnp.min(jnp.where(el2 == t2, lane_f, big), axis=-1, keepdims=True)
    e21 = jnp.exp(t2 - t1)
    p1 = 1.0 / (1.0 + e21)
    rt = jnp.where(lane == 0, i1 - N_GROUPS,
                   jnp.where(lane == 1, i2 - N_GROUPS,
                             jnp.where(lane == 2, g_p * p1, jnp.where(lane == 3, g_p * (e21 * p1), 0.0))))
    rt_ref[...] = rt


def _proj_router(y, w, b, h, g, wr, br, tm):
    t, kdim = y.shape
    row = lambda i: (i, 0)
    fixed = lambda i: (0, 0)
    return pl.pallas_call(
        _proj_router_kernel,
        out_shape=(jax.ShapeDtypeStruct((t, D_MODEL), F32), jax.ShapeDtypeStruct((t * ROW_TILES, LANES), F32),
                   jax.ShapeDtypeStruct((t, LANES), F32)),
        grid=(t // tm,),
        in_specs=[pl.BlockSpec((tm, kdim), row), pl.BlockSpec((kdim, D_MODEL), fixed),
                  pl.BlockSpec((1, D_MODEL), fixed), pl.BlockSpec((tm, D_MODEL), row),
                  pl.BlockSpec((1, D_MODEL), fixed), pl.BlockSpec((D_MODEL, 2 * LANES), fixed),
                  pl.BlockSpec((1, LANES), fixed)],
        out_specs=(pl.BlockSpec((tm, D_MODEL), row), pl.BlockSpec((tm * ROW_TILES, LANES), row),
                   pl.BlockSpec((tm, LANES), row)),
        compiler_params=_cparams(("parallel",)),
        name="proj_router",
    )(y, w, b, h, g, wr, br)


def _rank_kernel(rt_ref, rk_ref, cnt_ref, base_s, strict_s):
    tt = rt_ref.shape[0]

    @pl.when(pl.program_id(0) == 0)
    def _():
        base_s[...] = jnp.zeros(base_s.shape, F32)
        r = lax.broadcasted_iota(jnp.int32, (tt, tt), 0)
        c = lax.broadcasted_iota(jnp.int32, (tt, tt), 1)
        strict_s[...] = jnp.where(c < r, 1.0, 0.0).astype(BF16)

    rt = rt_ref[...]
    lane = lax.broadcasted_iota(jnp.int32, rt.shape, 1)
    lane_f = lane.astype(F32)
    e0, e1 = rt[:, 0:1], rt[:, 1:2]
    oh0 = lane_f == e0
    oh1 = lane_f == e1
    oh = jnp.where(oh0 | oh1, 1.0, 0.0)
    before = _dot(strict_s[...], oh.astype(BF16)) + base_s[...]
    r0 = jnp.sum(jnp.where(oh0, before, 0.0), axis=-1, keepdims=True)
    r1 = jnp.sum(jnp.where(oh1, before, 0.0), axis=-1, keepdims=True)
    table = jnp.where(lane == 0, r0, jnp.where(lane == 1, r1, jnp.where(lane == 2, e0, jnp.where(lane == 3, e1, 0.0))))
    rk_ref[...] = jnp.transpose(table)[0:8, :].astype(jnp.int32)
    base_s[...] = base_s[...] + jnp.sum(oh, axis=0, keepdims=True)
    cnt_ref[...] = base_s[...]


def _rank(rt, tt):
    t = rt.shape[0]
    return pl.pallas_call(
        _rank_kernel,
        out_shape=(jax.ShapeDtypeStruct((8, t), jnp.int32), jax.ShapeDtypeStruct((1, LANES), F32)),
        grid=(t // tt,),
        in_specs=[pl.BlockSpec((tt, LANES), lambda i: (i, 0))],
        out_specs=(pl.BlockSpec((8, tt), lambda i: (0, i)), pl.BlockSpec((1, LANES), lambda i: (0, 0))),
        scratch_shapes=[pltpu.VMEM((1, LANES), F32), pltpu.VMEM((tt, tt), BF16)],
        compiler_params=_cparams(("arbitrary",)),
        name="expert_rank",
    )(rt)


DMA_GROUP = 8


def _row_copy(src, dst, sem):
    return pltpu.make_async_copy(src, dst, sem)


def _row_tile(r):
    return pl.ds(pl.multiple_of(r * ROW_TILES, ROW_TILES), ROW_TILES)


def _dispatch_kernel(dest_ref, zflag_ref, u_ref, xout_hbm, inv_ref, fill_s, zero_s, sem, *, tt):
    i = pl.program_id(0)
    block_rows = MOE_BM * ROW_TILES

    @pl.when(i == 0)
    def _():
        fill_s[...] = jnp.full(fill_s.shape, -1, jnp.int32)
        fill = pltpu.make_async_copy(fill_s, inv_ref, sem)
        fill.start()
        fill.wait()
        zero_s[...] = jnp.zeros(zero_s.shape, F32)
        zero_block = lambda blk: _row_copy(zero_s, xout_hbm.at[pl.ds(blk * block_rows, block_rows)], sem)
        for blk in range(zflag_ref.shape[0]):
            @pl.when(zflag_ref[blk] != 0)
            def _():
                zero_block(blk).start()
        for blk in range(zflag_ref.shape[0]):
            @pl.when(zflag_ref[blk] != 0)
            def _():
                zero_block(blk).wait()

    def issue(g, carry):
        for jj in range(DMA_GROUP):
            j = g * DMA_GROUP + jj
            src = u_ref.at[_row_tile(j)]
            for k in range(TOP_K):
                d = dest_ref[k, j]
                _row_copy(src, xout_hbm.at[_row_tile(d)], sem).start(priority=k)
                inv_ref[d] = (i * tt + j) * TOP_K + k
        return carry

    lax.fori_loop(0, tt // DMA_GROUP, issue, 0)
    for k in range(TOP_K):
        _row_copy(u_ref, xout_hbm.at[pl.ds(0, tt * ROW_TILES)], sem).wait()


def _dispatch(dest, zflag, u2, tt):
    t = u2.shape[0] // ROW_TILES
    n_rows = zflag.shape[0] * MOE_BM
    return pl.pallas_call(
        functools.partial(_dispatch_kernel, tt=tt),
        out_shape=(jax.ShapeDtypeStruct((n_rows * ROW_TILES, LANES), F32), jax.ShapeDtypeStruct((n_rows,), jnp.int32)),
        grid=(t // tt,),
        in_specs=[pl.BlockSpec((TOP_K, tt), lambda i: (0, i), memory_space=pltpu.SMEM),
                  pl.BlockSpec(memory_space=pltpu.SMEM),
                  pl.BlockSpec((tt * ROW_TILES, LANES), lambda i: (i, 0))],
        out_specs=(pl.BlockSpec(memory_space=pl.ANY), pl.BlockSpec(memory_space=pltpu.SMEM)),
        scratch_shapes=[pltpu.VMEM((n_rows,), jnp.int32), pltpu.VMEM((MOE_BM * ROW_TILES, LANES), F32),
                        pltpu.SemaphoreType.DMA],
        compiler_params=_cparams(("arbitrary",)),
        name="moe_dispatch",
    )(dest, zflag, u2)


def _ffn_kernel(be_ref, nu_ref, inv_ref, x_ref, wg_ref, wu_ref, wd_ref, o2_hbm, wg_s, wu_s, wd_s, x_s, y_s, sem, *, n_slots):
    b = pl.program_id(0)
    last = pl.num_programs(0) - 1
    n_used = nu_ref[0]
    block_rows = MOE_BM * ROW_TILES

    def drain():
        _row_copy(y_s.at[0], o2_hbm.at[pl.ds(0, block_rows)], sem).wait()

    def scatter(blk):
        s = blk % 2
        for j in range(MOE_BM):
            a = inv_ref[blk * MOE_BM + j]
            slot = jnp.where(a >= 0, a, n_slots + s * MOE_BM + j)
            _row_copy(y_s.at[s, _row_tile(j)], o2_hbm.at[_row_tile(slot)], sem).start(priority=j % 2)

    def compute():
        for c in range(ROW_TILES):
            x_s[:, c * LANES:(c + 1) * LANES] = x_ref[pl.ds(c, MOE_BM, stride=ROW_TILES), :].astype(BF16)
        x = x_s[...]
        a = _dot(x, wg_s[...])
        u = _dot(x, wu_s[...])
        y = _dot(((a * _sigmoid(a)) * u).astype(BF16), wd_s[...])
        for c in range(ROW_TILES):
            y_s[b % 2, pl.ds(c, MOE_BM, stride=ROW_TILES), :] = y[:, c * LANES:(c + 1) * LANES]

    @pl.when(b == 0)
    def _():
        y_s[...] = jnp.zeros(y_s.shape, F32)
        for s in range(2):
            _row_copy(y_s.at[s], o2_hbm.at[pl.ds((n_slots + s * MOE_BM) * ROW_TILES, block_rows)], sem).start()
        for s in range(2):
            drain()

    @pl.when((b >= 2) & (b - 2 < n_used))
    def _():
        drain()

    @pl.when((b == 0) | (be_ref[b] != be_ref[jnp.maximum(b - 1, 0)]))
    def _():
        wg_s[...] = wg_ref[...].astype(BF16)
        wu_s[...] = wu_ref[...].astype(BF16)
        wd_s[...] = wd_ref[...].astype(BF16)

    @pl.when(b == 0)
    def _():
        compute()

    @pl.when((b >= 1) & (b < n_used))
    def _():
        scatter(b - 1)
        compute()

    @pl.when((b >= 1) & (b >= n_used) & (b - 1 < n_used))
    def _():
        scatter(b - 1)

    @pl.when(b == last)
    def _():
        @pl.when((b >= 1) & (b - 1 < n_used))
        def _():
            drain()

        @pl.when(b < n_used)
        def _():
            scatter(b)
            drain()


def _ffn(block_expert, n_used, inv, xbuf, wg, wu, wd, layer, n_slots):
    n_blocks = xbuf.shape[0] // (MOE_BM * ROW_TILES)
    rows = lambda b, be, nu, inv: (jnp.minimum(b, nu[0] - 1), 0)
    wmap = lambda b, be, nu, inv: (layer, be[b], 0, 0)
    return pl.pallas_call(
        functools.partial(_ffn_kernel, n_slots=n_slots),
        out_shape=jax.ShapeDtypeStruct(((n_slots + 2 * MOE_BM) * ROW_TILES, LANES), F32),
        grid_spec=pltpu.PrefetchScalarGridSpec(
            num_scalar_prefetch=3,
            grid=(n_blocks,),
            in_specs=[pl.BlockSpec((MOE_BM * ROW_TILES, LANES), rows),
                      pl.BlockSpec((None, None, D_MODEL, D_FF), wmap),
                      pl.BlockSpec((None, None, D_MODEL, D_FF), wmap),
                      pl.BlockSpec((None, None, D_FF, D_MODEL), wmap)],
            out_specs=pl.BlockSpec(memory_space=pl.ANY),
            scratch_shapes=[pltpu.VMEM((D_MODEL, D_FF), BF16), pltpu.VMEM((D_MODEL, D_FF), BF16),
                            pltpu.VMEM((D_FF, D_MODEL), BF16), pltpu.VMEM((MOE_BM, D_MODEL), BF16),
                            pltpu.VMEM((2, MOE_BM * ROW_TILES, LANES), F32), pltpu.SemaphoreType.DMA]),
        compiler_params=_cparams(("arbitrary",)),
        name="moe_ffn",
    )(block_expert, n_used, inv, xbuf, wg, wu, wd)


def _combine_kernel(o2_ref, h_ref, rt_ref, g_ref, o_ref, *, final_norm):
    tt = h_ref.shape[0]
    rt = rt_ref[...]
    w0, w1 = rt[:, 2:3], rt[:, 3:4]
    for c in range(ROW_TILES):
        lanes = slice(c * LANES, (c + 1) * LANES)
        y0 = o2_ref[pl.ds(c, tt, stride=TOP_K * ROW_TILES), :]
        y1 = o2_ref[pl.ds(ROW_TILES + c, tt, stride=TOP_K * ROW_TILES), :]
        o_ref[:, lanes] = h_ref[:, lanes] + (y0 * w0 + y1 * w1)
    if final_norm:
        o_ref[...] = _rms(o_ref[...], g_ref[...])


def _combine(o2, h1, rt, g, tt, final_norm):
    t = h1.shape[0]
    return pl.pallas_call(
        functools.partial(_combine_kernel, final_norm=final_norm),
        out_shape=jax.ShapeDtypeStruct((t, D_MODEL), F32),
        grid=(t // tt,),
        in_specs=[pl.BlockSpec((tt * TOP_K * ROW_TILES, LANES), lambda i: (i, 0)),
                  pl.BlockSpec((tt, D_MODEL), lambda i: (i, 0)),
                  pl.BlockSpec((tt, LANES), lambda i: (i, 0)),
                  pl.BlockSpec((1, D_MODEL), lambda i: (0, 0))],
        out_specs=pl.BlockSpec((tt, D_MODEL), lambda i: (i, 0)),
        compiler_params=_cparams(("parallel",)),
        name="moe_combine",
    )(o2, h1, rt, g)


def _moe(h1, u2, rt, wg, wu, wd, layer, g_final, final_norm):
    t = h1.shape[0]
    rk, cnt = _rank(rt, 1024)
    counts = cnt[0, :N_EXPERTS].astype(jnp.int32)
    padded = (counts + MOE_BM - 1) // MOE_BM * MOE_BM
    pad_end = jnp.cumsum(padded)
    pad_start = pad_end - padded
    n_blocks = (t * TOP_K) // MOE_BM + N_EXPERTS
    is_expert = rk[TOP_K:2 * TOP_K][None] == jnp.arange(N_EXPERTS, dtype=jnp.int32)[:, None, None]
    dest = jnp.sum(jnp.where(is_expert, pad_start[:, None, None], 0), axis=0) + rk[0:TOP_K]
    n_used = (pad_end[-1] // MOE_BM).astype(jnp.int32)
    blocks = jnp.arange(n_blocks, dtype=jnp.int32)
    blk = jnp.minimum(blocks, n_used - 1) * MOE_BM
    block_expert = jnp.minimum(jnp.sum(pad_end[None, :] <= blk[:, None], axis=1), N_EXPERTS - 1).astype(jnp.int32)
    closes_expert = jnp.any(((blocks[:, None] + 1) * MOE_BM == pad_end[None, :]) & (padded[None, :] > 0), axis=1)
    zflag = ((blocks >= n_used) | closes_expert).astype(jnp.int32)
    xbuf, inv = _dispatch(dest, zflag, u2, 1024)
    o2 = _ffn(block_expert, n_used.reshape(1), inv, xbuf, wg, wu, wd, layer, t * TOP_K)
    return _combine(o2, h1, rt, g_final, 512, final_norm)


def _attn_kernel(x_ref, g_ref, w_ref, b_ref, pos_ref, freq_ref, sp_ref, sink_ref, o_ref, q_s, kv_s, *, tm, tiles_per_seq):
    seq_start = pl.program_id(0) % tiles_per_seq == 0

    @pl.when(seq_start)
    def _():
        kv_s[0:WINDOW, :] = jnp.zeros((WINDOW, KV_OUT_W), kv_s.dtype)

    _project_qkv(x_ref, g_ref, w_ref, b_ref, pos_ref, freq_ref, sp_ref, q_s, kv_s.at[pl.ds(WINDOW, tm)])
    for j in range(tm // WINDOW):
        has_prev = jnp.logical_not(seq_start) if j == 0 else True
        _swa_block(q_s.at[pl.ds(j * WINDOW, WINDOW)], kv_s.at[pl.ds((j + 1) * WINDOW, WINDOW)],
                   kv_s.at[pl.ds(j * WINDOW, WINDOW)], sink_ref, o_ref.at[pl.ds(j * WINDOW, WINDOW)], has_prev)
    kv_s[0:WINDOW, :] = kv_s[tm:tm + WINDOW, :]


def _project_qkv(x_ref, g_ref, w_ref, b_ref, pos_ref, freq_ref, sp_ref, q_ref, kv_ref):
    u = _rms(x_ref[...], g_ref[...]).astype(BF16)
    ang = freq_ref[...] * pos_ref[...].astype(F32)
    spread = lambda v, m: sum(_dot_tn(t.astype(F32), m) for t in _split_terms(v, 3))
    cosv = jnp.cos(ang)
    sinv = jnp.sin(ang)
    c_coef = spread(cosv, sp_ref[0]) + sp_ref[3, 0:1, :]
    s_lo = spread(sinv, sp_ref[1])
    s_hi = spread(sinv, sp_ref[2])

    def rotate(z):
        return z * c_coef + pltpu.roll(z, LANES - ROT_DIM // 2, 1) * s_lo + pltpu.roll(z, ROT_DIM // 2, 1) * s_hi

    scale = SWA_HEAD_DIM ** -0.5
    for j in range(Q_W // LANES):
        sl = slice(j * LANES, (j + 1) * LANES)
        q_ref[:, sl] = (rotate(_dot(u, w_ref[:, sl]) + b_ref[:, sl]) * scale).astype(q_ref.dtype)
    for j in range(KV_OUT_W // LANES):
        sl = slice(Q_W + j * LANES, Q_W + (j + 1) * LANES)
        z = _dot(u, w_ref[:, sl]) + b_ref[:, sl]
        kv_ref[:, j * LANES:(j + 1) * LANES] = (rotate(z) if j % 2 == 0 else z).astype(kv_ref.dtype)


def _rot_tables():
    half = ROT_DIM // 2
    inv_freq = (ROPE_THETA ** (-jnp.arange(0, ROT_DIM, 2, dtype=F32) / ROT_DIM)).reshape(half, 1)
    d = np.arange(LANES) % SWA_HEAD_DIM
    f = np.arange(half)[:, None]
    sp = np.zeros((4, half, LANES), np.float32)
    sp[0] = (d[None, :] < ROT_DIM) & (d[None, :] % half == f)
    sp[1] = -((d[None, :] < half) & (d[None, :] == f)).astype(np.float32)
    sp[2] = (d[None, :] >= half) & (d[None, :] < ROT_DIM) & (d[None, :] - half == f)
    sp[3, 0] = d >= ROT_DIM
    return inv_freq, jnp.asarray(sp)


def _attn(h, g, w, b, pos, sinks, tm, seq):
    t = h.shape[0]
    row = lambda i: (i, 0)
    fixed = lambda i: (0, 0)
    wtot = Q_W + KV_OUT_W
    return pl.pallas_call(
        functools.partial(_attn_kernel, tm=tm, tiles_per_seq=seq // tm),
        out_shape=jax.ShapeDtypeStruct((t, Q_W), BF16),
        grid=(t // tm,),
        in_specs=[pl.BlockSpec((tm, D_MODEL), row), pl.BlockSpec((1, D_MODEL), fixed),
                  pl.BlockSpec((D_MODEL, wtot), fixed), pl.BlockSpec((1, wtot), fixed),
                  pl.BlockSpec((1, tm), lambda i: (0, i)), pl.BlockSpec((ROT_DIM // 2, 1), fixed),
                  pl.BlockSpec((4, ROT_DIM // 2, LANES), lambda i: (0, 0, 0)), pl.BlockSpec((1, LANES), fixed)],
        out_specs=pl.BlockSpec((tm, Q_W), row),
        scratch_shapes=[pltpu.VMEM((tm, Q_W), BF16), pltpu.VMEM((tm + WINDOW, KV_OUT_W), BF16)],
        compiler_params=_cparams(("arbitrary",)),
        name="qkv_swa",
    )(h, g, w, b, pos, *_rot_tables(), sinks)


def _swa_block(q_ref, kvc_ref, kvp_ref, sink_ref, o_ref, has_prev):
    neg = jnp.where(has_prev, 0.0, -jnp.inf).astype(F32)
    kj = lax.broadcasted_iota(jnp.int32, (WINDOW, WINDOW), 0)
    qi = lax.broadcasted_iota(jnp.int32, (WINDOW, WINDOW), 1)
    mask_cur = kj <= qi
    top = kj < SWA_HEAD_DIM
    lane = lax.broadcasted_iota(jnp.int32, (1, LANES), 1)
    keep_lo = jnp.where(lane < SWA_HEAD_DIM, 1.0, 0.0).astype(BF16)
    keep_hi = jnp.where(lane < SWA_HEAD_DIM, 0.0, 1.0).astype(BF16)
    zeros_half = jnp.zeros((SWA_HEAD_DIM, WINDOW), F32)

    def arranged(ref):
        k_nat, k_swp = ref[:, 0:KV_W], ref[:, 2 * KV_W:3 * KV_W]
        vt = jnp.transpose(ref[:, KV_W:2 * KV_W].astype(F32))
        vt_top = lambda g: jnp.concatenate([vt[g * SWA_HEAD_DIM:(g + 1) * SWA_HEAD_DIM], zeros_half], axis=0).astype(BF16)
        vt_bot = lambda g: jnp.concatenate([zeros_half, vt[g * SWA_HEAD_DIM:(g + 1) * SWA_HEAD_DIM]], axis=0).astype(BF16)
        return {(0, 0): (k_nat * keep_lo, vt_top(0)), (0, 1): (k_swp * keep_hi, vt_bot(0)),
                (1, 0): (k_swp * keep_lo, vt_top(1)), (1, 1): (k_nat * keep_hi, vt_bot(1))}

    cur = arranged(kvc_ref)
    prev = arranged(kvp_ref)
    heads = range(SWA_Q_HEADS)
    key = lambda h: (h // SWA_GROUP, h % 2)
    scores = []
    for h in heads:
        qp = q_ref[:, (h // 2) * LANES:(h // 2 + 1) * LANES]
        sc = _dot_nt(cur[key(h)][0], qp)
        sp = _dot_nt(prev[key(h)][0], qp)
        scores.append(jnp.where(mask_cur, sc, sp + neg))
    sinks = [sink_ref[:, h:h + 1] for h in heads]
    maxes = [jnp.maximum(jnp.max(scores[h], axis=0, keepdims=True), sinks[h]) for h in heads]
    probs = [jnp.exp(scores[h] - maxes[h]) for h in heads]
    denoms = [jnp.sum(probs[h], axis=0, keepdims=True) + jnp.exp(sinks[h] - maxes[h]) for h in heads]
    for pr in range(SWA_Q_HEADS // 2):
        num = None
        for h in (2 * pr, 2 * pr + 1):
            pc = jnp.where(mask_cur, probs[h], 0.0).astype(BF16)
            pp = jnp.where(mask_cur, 0.0, probs[h]).astype(BF16)
            part = _dot(cur[key(h)][1], pc) + _dot(prev[key(h)][1], pp)
            num = part if num is None else num + part
        den = jnp.where(top, denoms[2 * pr], denoms[2 * pr + 1])
        o_ref[:, pr * LANES:(pr + 1) * LANES] = jnp.transpose(num / den).astype(o_ref.dtype)


def _pack_in_proj(w_in):
    offs = np.cumsum((0,) + AB_SPLITS)
    part = lambda i: w_in[:, offs[i]:offs[i + 1]]
    w_main = jnp.concatenate([part(0), part(1), part(2), part(5), part(6), part(7), part(8)], axis=1).astype(BF16)
    small = jnp.concatenate([part(3), part(4), part(9)], axis=1)
    w_small = jnp.pad(small, ((0, 0), (0, LANES - small.shape[1]))).astype(BF16)
    return w_main, w_small


def _pack_qkv(w, b):
    hd = SWA_HEAD_DIM
    k0, k1 = slice(Q_W, Q_W + hd), slice(Q_W + hd, Q_W + 2 * hd)
    cols = lambda a: jnp.concatenate([a[..., :Q_W + 2 * KV_W], a[..., k1], a[..., k0]], axis=-1)
    return cols(w).astype(BF16), cols(b).reshape(1, -1).astype(F32)


def _lane_row(v, offset=0):
    return jnp.zeros((1, LANES), F32).at[0, offset:offset + v.shape[0]].set(v.astype(F32))


def kernel(x, positions, mix_norm, ffn_norm, final_norm, ab_w_in, ab_conv_w, ab_conv_b, ml_igate_b, ml_fgate_b, ml_head_norm, gla_w_lr_up, gla_gate_b, gla_head_norm, ab_w_out, swa_w_qkv, swa_b_qkv, swa_sinks, swa_w_o, swa_b_o, router_group_w, router_group_b, router_expert_w, router_expert_b, expert_w_gate, expert_w_up, expert_w_down):
    bsz, seq, d = x.shape
    t = bsz * seq
    h = x.reshape(t, d)
    row = lambda v: v.reshape(1, -1).astype(F32)

    def router_params(layer):
        wr = jnp.zeros((d, LANES), F32).at[:, :N_GROUPS].set(router_group_w[layer])
        wr = wr.at[:, N_GROUPS:N_GROUPS + N_EXPERTS].set(router_expert_w[layer])
        br = jnp.zeros((1, LANES), F32).at[0, :N_GROUPS].set(router_group_b[layer])
        br = br.at[0, N_GROUPS:N_GROUPS + N_EXPERTS].set(router_expert_b[layer])
        w_hi = wr.astype(BF16)
        w_lo = (wr - w_hi.astype(F32)).astype(BF16)
        return jnp.concatenate([w_hi, w_lo], axis=1), br

    def experts(layer):
        return expert_w_gate, expert_w_up, expert_w_down, layer

    w_main, w_small = _pack_in_proj(ab_w_in[0])
    lrup = jnp.zeros((LANES, GLA_QK_W), F32).at[SM_LR:SM_LR + GLA_LOWRANK].set(gla_w_lr_up[0]).astype(BF16)
    y = _mixer(h, row(mix_norm[0]), w_main, w_small, ab_conv_w[0], row(ab_conv_b[0]), _lane_row(ml_igate_b[0]),
               _lane_row(ml_fgate_b[0]), lrup, row(gla_gate_b[0]), row(ml_head_norm[0]), row(gla_head_norm[0]),
               bsz, seq, 256)
    wr, br = router_params(0)
    h1, u2, rt = _proj_router(y, ab_w_out[0].astype(BF16), jnp.zeros((1, d), F32), h, row(ffn_norm[0]), wr, br, 512)
    h = _moe(h1, u2, rt, *experts(0), row(final_norm), False)

    w_qkv, b_qkv = _pack_qkv(swa_w_qkv[0], swa_b_qkv[0])
    o = _attn(h, row(mix_norm[1]), w_qkv, b_qkv, positions.reshape(1, t), _lane_row(swa_sinks[0]), 512, seq)
    wr, br = router_params(1)
    h1, u2, rt = _proj_router(o, swa_w_o[0].astype(BF16), row(swa_b_o[0]), h, row(ffn_norm[1]), wr, br, 512)
    out = _moe(h1, u2, rt, *experts(1), row(final_norm), True)
    return out.reshape(bsz, seq, d)
```

```python
import functools

import jax
import jax.numpy as jnp
import numpy as np
from jax import lax
from jax.experimental import pallas as pl
from jax.experimental.pallas import tpu as pltpu

F32 = jnp.float32
BF16 = jnp.bfloat16
HIGHEST = lax.Precision.HIGHEST

D_MODEL = 1024
EPS = 1e-6
ML_HEADS = 4
ML_DK = 64
ML_DV = 128
GLA_HEADS = 4
GLA_DK = 64
GLA_DV = 128
CHUNK = 64
CONV_K = 4
GLA_LOWRANK = 16
GLA_TAU = 16.0
ML_QK_W = ML_HEADS * ML_DK
ML_V_W = ML_HEADS * ML_DV
GLA_QK_W = GLA_HEADS * GLA_DK
GLA_V_W = GLA_HEADS * GLA_DV
AB_SPLITS = (2 * ML_QK_W, ML_V_W, ML_V_W, ML_HEADS, ML_HEADS, GLA_QK_W, GLA_QK_W, GLA_V_W, GLA_V_W, GLA_LOWRANK)
OFF_MQK = 0
OFF_MV = OFF_MQK + 2 * ML_QK_W
OFF_MO = OFF_MV + ML_V_W
OFF_GQ = OFF_MO + ML_V_W
OFF_GK = OFF_GQ + GLA_QK_W
OFF_GV = OFF_GK + GLA_QK_W
OFF_GG = OFF_GV + GLA_V_W
Z_MAIN_W = OFF_GG + GLA_V_W
SM_I = 0
SM_F = ML_HEADS
SM_LR = 2 * ML_HEADS
LANES = 128
SWA_Q_HEADS = 16
SWA_KV_HEADS = 2
SWA_HEAD_DIM = 64
SWA_GROUP = SWA_Q_HEADS // SWA_KV_HEADS
WINDOW = 128
ROT_DIM = SWA_HEAD_DIM // 4
ROPE_THETA = 500000.0
Q_W = SWA_Q_HEADS * SWA_HEAD_DIM
KV_W = SWA_KV_HEADS * SWA_HEAD_DIM
KV_OUT_W = 3 * KV_W
N_GROUPS = 4
EXPERTS_PER_GROUP = 8
N_EXPERTS = N_GROUPS * EXPERTS_PER_GROUP
TOP_K = 2
D_FF = 512
MOE_BM = 512
ROW_TILES = D_MODEL // LANES

VMEM_LIMIT = 56 * 1024 * 1024


def _cparams(sem):
    return pltpu.CompilerParams(dimension_semantics=sem, vmem_limit_bytes=VMEM_LIMIT)


def _rms(x, g):
    return x * lax.rsqrt(jnp.mean(x * x, axis=-1, keepdims=True) + EPS) * g


def _log_sigmoid(x):
    return jnp.minimum(x, 0.0) - jnp.log1p(jnp.exp(-jnp.abs(x)))


def _sigmoid(x):
    return 1.0 / (1.0 + jnp.exp(-x))


def _dot(a, b):
    return jnp.dot(a, b, preferred_element_type=F32)


def _dot_nt(a, b):
    return lax.dot_general(a, b, (((1,), (1,)), ((), ())), preferred_element_type=F32)


def _dot_tn(a, b):
    return lax.dot_general(a, b, (((0,), (0,)), ((), ())), preferred_element_type=F32)


IN_PROJ_CHUNK = 768
def _split_terms(x, n):
    terms = []
    for _ in range(n):
        t = x.astype(BF16)
        terms.append(t)
        x = x - t.astype(F32)
    return terms


def _dot01_left(m01, x, n):
    return sum(_dot(m01, t) for t in _split_terms(x, n))


def _dot01_right(x, m01, n):
    return sum(_dot(t, m01) for t in _split_terms(x, n))


def _mixer_kernel(xp_ref, xn_ref, g_ref, wm_ref, ws_ref, convw_ref, convb_ref, igb_ref, fgb_ref, lrup_ref, gateb_ref,
                  mlnorm_ref, glanorm_ref, y_ref, zm_s, zs_s, xpad, q_s, k_s, e_pad, hml_s, hgla_s, cn_s, m_s, st_s,
                  *, tb):
    step = pl.program_id(0) * pl.num_programs(1) + pl.program_id(1)

    def project(x, slot):
        u = _rms(x, g_ref[...]).astype(BF16)
        for n0 in range(0, Z_MAIN_W, IN_PROJ_CHUNK):
            zm_s[slot, :, n0:n0 + IN_PROJ_CHUNK] = _dot(u, wm_ref[:, n0:n0 + IN_PROJ_CHUNK]).astype(BF16)
        zs_s[slot] = _dot(u, ws_ref[...])

    @pl.when(step == 0)
    def _():
        project(xp_ref[0:tb, :], 0)

    @pl.when(pl.program_id(1) == 0)
    def _():
        xpad[0:8, :] = jnp.zeros((8, 2 * ML_QK_W), F32)
        cn_s[...] = jnp.zeros(cn_s.shape, F32)
        st_s[...] = jnp.zeros(st_s.shape, F32)
        m_s[...] = jnp.full(m_s.shape, -jnp.inf, F32)

    params = (convw_ref, convb_ref, igb_ref, fgb_ref, lrup_ref, gateb_ref, mlnorm_ref, glanorm_ref)
    scratch = (xpad, q_s, k_s, e_pad, hml_s, hgla_s, cn_s, m_s, st_s)
    project(xp_ref[tb:2 * tb, :], 1)
    _mix_block(zm_s.at[0], zs_s.at[0], *params, y_ref.at[0:tb], *scratch, tb=tb)
    project(xn_ref[...], 0)
    _mix_block(zm_s.at[1], zs_s.at[1], *params, y_ref.at[tb:2 * tb], *scratch, tb=tb)


def _mix_block(zm_ref, zs_ref, convw_ref, convb_ref, igb_ref, fgb_ref, lrup_ref, gateb_ref, mlnorm_ref,
               glanorm_ref, y_ref, xpad, q_s, k_s, e_pad, hml_s, hgla_s, cn_s, m_s, st_s, *, tb):
    nc = tb // CHUNK

    xpad[8:8 + tb, :] = zm_ref[:, OFF_MQK:OFF_MQK + 2 * ML_QK_W].astype(F32)
    conv = convb_ref[...] + convw_ref[3:4, :] * xpad[8:8 + tb, :]
    for j in range(CONV_K - 1):
        conv = conv + convw_ref[j:j + 1, :] * xpad[5 + j:5 + j + tb, :]
    xpad[0:8, :] = xpad[tb:tb + 8, :]
    qk = conv * _sigmoid(conv)
    q_s[...] = qk[:, :ML_QK_W]
    k_s[...] = qk[:, ML_QK_W:] * (ML_DK ** -0.5)

    r_t = lax.broadcasted_iota(jnp.int32, (tb, tb), 0)
    c_t = lax.broadcasted_iota(jnp.int32, (tb, tb), 1)
    tri_blk = jnp.where((r_t // CHUNK == c_t // CHUNK) & (c_t <= r_t), 1.0, 0.0).astype(BF16)
    r_e = lax.broadcasted_iota(jnp.int32, (LANES, 2 * LANES), 0)
    c_e = lax.broadcasted_iota(jnp.int32, (LANES, 2 * LANES), 1)
    spread_dk = jnp.where(c_e // ML_DK == r_e, 1.0, 0.0).astype(BF16)
    mean_dv = jnp.full((ML_DV, ML_DV), 1.0 / ML_DV, F32).astype(BF16)
    row_c = lax.broadcasted_iota(jnp.int32, (CHUNK, CHUNK), 0)
    col_c = lax.broadcasted_iota(jnp.int32, (CHUNK, CHUNK), 1)
    causal = col_c <= row_c
    lane_c = lax.broadcasted_iota(jnp.int32, (CHUNK, LANES), 1)
    ones_dv = jnp.ones((CHUNK, ML_DV), BF16)
    chunk_rows = lambda c: slice(c * CHUNK, (c + 1) * CHUNK)

    zs = zs_ref[...]
    ig = zs + igb_ref[...]
    lf = _log_sigmoid(pltpu.roll(zs, LANES - SM_F, 1) + fgb_ref[...])
    bc = _dot01_left(tri_blk, lf, 3)
    bc3 = bc.reshape(nc, CHUNK, LANES)
    g3 = bc3[:, CHUNK - 1:CHUNK, :]
    a3 = g3 - bc3 + ig.reshape(nc, CHUNK, LANES)
    amax3 = jnp.max(a3, axis=1, keepdims=True)
    wa = jnp.exp(a3 - amax3).reshape(tb, LANES)
    m_run = m_s[...]
    m_prev, s_old, s_in = [], [], []
    for c in range(nc):
        m_new = jnp.maximum(g3[c] + m_run, amax3[c])
        m_prev.append(m_run)
        s_old.append(jnp.exp(g3[c] + m_run - m_new))
        s_in.append(jnp.exp(amax3[c] - m_new))
        m_run = m_new
    m_s[...] = m_run
    e_nat = ig - bc
    e_pad[0:CHUNK, :] = jnp.full((CHUNK, LANES), -jnp.inf, F32)
    e_pad[CHUNK:CHUNK + tb, :] = e_nat
    pos = lax.broadcasted_iota(jnp.int32, (tb, LANES), 0) % CHUNK
    shift = 1
    while shift < CHUNK:
        shifted = e_pad[CHUNK - shift:CHUNK - shift + tb, :]
        e_pad[CHUNK:CHUNK + tb, :] = jnp.maximum(e_pad[CHUNK:CHUNK + tb, :], jnp.where(pos >= shift, shifted, -jnp.inf))
        shift *= 2
    m_intra = bc + e_pad[CHUNK:CHUNK + tb, :]
    il = jnp.concatenate([bc3[c] + m_prev[c] for c in range(nc)], axis=0)
    mt = jnp.maximum(il, m_intra)
    s_inter = jnp.exp(il - mt)
    exp_neg = jnp.exp(-mt)
    xn = bc - mt
    e_up = pltpu.roll(e_nat, ML_HEADS, 1)
    wa_x = _dot01_right(wa, spread_dk, 2)
    si_x = _dot01_right(s_inter, spread_dk, 2)
    q_all = q_s[...]
    k_all = k_s[...]
    kw_b = (k_all * wa_x).astype(BF16)
    qs_b = (q_all * si_x).astype(BF16)
    q_b = q_all.astype(BF16)
    k_b = k_all.astype(BF16)

    qkm, upd, vo = {}, {}, {}
    for c in range(nc):
        rows = chunk_rows(c)
        for h in range(ML_HEADS):
            dk = slice(h * ML_DK, (h + 1) * ML_DK)
            x_t = jnp.where(lane_c == h, xn[rows], jnp.where(lane_c == h + ML_HEADS, 1.0, 0.0))
            y_t = jnp.where(lane_c == h, 1.0, jnp.where(lane_c == h + ML_HEADS, e_up[rows], 0.0))
            d = lax.dot_general(x_t, y_t, (((1,), (1,)), ((), ())), precision=HIGHEST, preferred_element_type=F32)
            p = jnp.where(causal, jnp.exp(d), 0.0)
            qkm[c, h] = (_dot_nt(q_b[rows, dk], k_b[rows, dk]) * p).astype(BF16)
            vo[c, h] = jnp.concatenate([zm_ref[rows, OFF_MV + h * ML_DV:OFF_MV + (h + 1) * ML_DV], ones_dv], axis=1)
            upd[c, h] = _dot_tn(kw_b[rows, dk], vo[c, h])
    for h in range(ML_HEADS):
        dk = slice(h * ML_DK, (h + 1) * ML_DK)
        dv = slice(h * ML_DV, (h + 1) * ML_DV)
        cn = cn_s[h]
        for c in range(nc):
            rows = chunk_rows(c)
            res = _dot(qs_b[rows, dk], cn.astype(BF16)) + _dot(qkm[c, h], vo[c, h])
            num, den = res[:, :ML_DV], res[:, ML_DV:]
            hml_s[rows, dv] = num / jnp.maximum(jnp.abs(den), exp_neg[rows, h:h + 1])
            cn = s_old[c][:, h:h + 1] * cn + s_in[c][:, h:h + 1] * upd[c, h]
        cn_s[h] = cn

    la = _log_sigmoid(_dot(zs.astype(BF16), lrup_ref[...]) + gateb_ref[...]) * (1.0 / GLA_TAU)
    bcg = _dot01_left(tri_blk, la, 3)
    bcg3 = bcg.reshape(nc, CHUNK, GLA_QK_W)
    gg3 = bcg3[:, CHUNK - 1:CHUNK, :]
    gq = zm_ref[:, OFF_GQ:OFF_GQ + GLA_QK_W].astype(F32)
    gk = zm_ref[:, OFF_GK:OFF_GK + GLA_QK_W].astype(F32) * (GLA_DK ** -0.5)
    q_dec = (gq * jnp.exp(bcg)).astype(BF16)
    k_inv = (gk * jnp.exp(-bcg)).astype(BF16)
    k_end = (gk * jnp.exp(gg3 - bcg3).reshape(tb, GLA_QK_W)).astype(BF16)
    eg3 = jnp.exp(gg3)
    att, updg = {}, {}
    for c in range(nc):
        rows = chunk_rows(c)
        for h in range(GLA_HEADS):
            dk = slice(h * GLA_DK, (h + 1) * GLA_DK)
            vh = zm_ref[rows, OFF_GV + h * GLA_DV:OFF_GV + (h + 1) * GLA_DV]
            att[c, h] = jnp.where(causal, _dot_nt(q_dec[rows, dk], k_inv[rows, dk]), 0.0).astype(BF16)
            updg[c, h] = _dot_tn(vh, k_end[rows, dk])
    for h in range(GLA_HEADS):
        dk = slice(h * GLA_DK, (h + 1) * GLA_DK)
        st = st_s[h]
        for c in range(nc):
            rows = chunk_rows(c)
            vh = zm_ref[rows, OFF_GV + h * GLA_DV:OFF_GV + (h + 1) * GLA_DV]
            hgla_s[rows, h * GLA_DV:(h + 1) * GLA_DV] = _dot_nt(q_dec[rows, dk], st.astype(BF16)) + _dot(att[c, h], vh)
            st = st * eg3[c][:, dk] + updg[c, h]
        st_s[h] = st

    mean = lambda x: _dot01_right(x, mean_dv, 2)
    for h in range(ML_HEADS):
        sl = slice(h * ML_DV, (h + 1) * ML_DV)
        hh = hml_s[:, sl]
        d = hh - mean(hh)
        hn = d * lax.rsqrt(mean(d * d) + EPS)
        og = zm_ref[:, OFF_MO + h * ML_DV:OFF_MO + (h + 1) * ML_DV].astype(F32)
        y_ref[:, sl] = (hn * mlnorm_ref[:, sl] * _sigmoid(og)).astype(y_ref.dtype)
    for h in range(GLA_HEADS):
        sl = slice(h * GLA_DV, (h + 1) * GLA_DV)
        o = hgla_s[:, sl]
        on = o * lax.rsqrt(mean(o * o) + EPS)
        gg = zm_ref[:, OFF_GG + h * GLA_DV:OFF_GG + (h + 1) * GLA_DV].astype(F32)
        y_ref[:, ML_V_W + h * GLA_DV:ML_V_W + (h + 1) * GLA_DV] = (on * glanorm_ref[:, sl] * (gg * _sigmoid(gg))).astype(y_ref.dtype)


def _mixer(h, g, w_main, w_small, convw, convb, igb, fgb, lrup, gateb, mlnorm, glanorm, bsz, seq, tb):
    nt = seq // (2 * tb)
    last_block = bsz * seq // tb - 1
    const = lambda shape: pl.BlockSpec(shape, lambda b, i: (0,) * len(shape))
    return pl.pallas_call(
        functools.partial(_mixer_kernel, tb=tb),
        out_shape=jax.ShapeDtypeStruct((bsz * seq, ML_V_W + GLA_V_W), BF16),
        grid=(bsz, nt),
        in_specs=[pl.BlockSpec((2 * tb, D_MODEL), lambda b, i: (b * nt + i, 0)),
                  pl.BlockSpec((tb, D_MODEL), lambda b, i: (jnp.minimum(2 * (b * nt + i) + 2, last_block), 0)),
                  const((1, D_MODEL)), const((D_MODEL, Z_MAIN_W)), const((D_MODEL, LANES)),
                  const((CONV_K, 2 * ML_QK_W)), const((1, 2 * ML_QK_W)), const((1, LANES)), const((1, LANES)),
                  const((LANES, GLA_QK_W)), const((1, GLA_QK_W)), const((1, ML_V_W)), const((1, GLA_V_W))],
        out_specs=pl.BlockSpec((2 * tb, ML_V_W + GLA_V_W), lambda b, i: (b * nt + i, 0)),
        scratch_shapes=[pltpu.VMEM((2, tb, Z_MAIN_W), BF16), pltpu.VMEM((2, tb, LANES), F32),
                        pltpu.VMEM((tb + 8, 2 * ML_QK_W), F32),
                        pltpu.VMEM((tb, ML_QK_W), F32), pltpu.VMEM((tb, ML_QK_W), F32),
                        pltpu.VMEM((tb + CHUNK, LANES), F32),
                        pltpu.VMEM((tb, ML_V_W), F32), pltpu.VMEM((tb, GLA_V_W), F32),
                        pltpu.VMEM((ML_HEADS, ML_DK, ML_DV + LANES), F32),
                        pltpu.VMEM((1, LANES), F32), pltpu.VMEM((GLA_HEADS, GLA_DV, GLA_DK), F32)],
        compiler_params=_cparams(("arbitrary", "arbitrary")),
        name="mlstm_gla",
    )(h, h, g, w_main, w_small, convw, convb, igb, fgb, lrup, gateb, mlnorm, glanorm)


def _proj_router_kernel(y_ref, w_ref, b_ref, h_ref, g_ref, wr_ref, br_ref, h1_ref, u2_ref, rt_ref, rk_ref, cnt_ref,
                        base_s, strict_s):
    @pl.when(pl.program_id(0) == 0)
    def _():
        _rank_init(base_s, strict_s)

    h1 = h_ref[...] + (_dot(y_ref[...], w_ref[...]) + b_ref[...])
    h1_ref[...] = h1
    u2 = _rms(h1, g_ref[...])
    for c in range(ROW_TILES):
        u2_ref[pl.ds(c, u2.shape[0], stride=ROW_TILES), :] = u2[:, c * LANES:(c + 1) * LANES]
    u_hi = u2.astype(BF16)
    u_lo = (u2 - u_hi.astype(F32)).astype(BF16)
    part = _dot(u_hi, wr_ref[...])
    logits = part[:, :LANES] + (part[:, LANES:] + _dot(u_lo, wr_ref[:, :LANES])) + br_ref[...]
    lane = lax.broadcasted_iota(jnp.int32, logits.shape, 1)
    lane_f = lane.astype(F32)
    big = float(LANES)
    gl = jnp.where(lane < N_GROUPS, logits, -jnp.inf)
    g_max = jnp.max(gl, axis=-1, keepdims=True)
    g_idx = jnp.min(jnp.where(gl == g_max, lane_f, big), axis=-1, keepdims=True)
    g_p = 1.0 / jnp.sum(jnp.exp(gl - g_max), axis=-1, keepdims=True)
    e_grp = ((lane - N_GROUPS) // EXPERTS_PER_GROUP).astype(F32)
    in_grp = (lane >= N_GROUPS) & (lane < N_GROUPS + N_EXPERTS) & (e_grp == g_idx)
    el = jnp.where(in_grp, logits, -jnp.inf)
    t1 = jnp.max(el, axis=-1, keepdims=True)
    i1 = jnp.min(jnp.where(el == t1, lane_f, big), axis=-1, keepdims=True)
    el2 = jnp.where(lane_f == i1, -jnp.inf, el)
    t2 = jnp.max(el2, axis=-1, keepdims=True)
    i2 = jnp.min(jnp.where(el2 == t2, lane_f, big), axis=-1, keepdims=True)
    e21 = jnp.exp(t2 - t1)
    p1 = 1.0 / (1.0 + e21)
    rt = jnp.where(lane == 0, i1 - N_GROUPS,
                   jnp.where(lane == 1, i2 - N_GROUPS,
                             jnp.where(lane == 2, g_p * p1, jnp.where(lane == 3, g_p * (e21 * p1), 0.0))))
    rt_ref[...] = rt
    _rank_rows(rt, rk_ref, cnt_ref, base_s, strict_s)


def _proj_router(y, w, b, h, g, wr, br, tm):
    t, kdim = y.shape
    row = lambda i: (i, 0)
    fixed = lambda i: (0, 0)
    return pl.pallas_call(
        _proj_router_kernel,
        out_shape=(jax.ShapeDtypeStruct((t, D_MODEL), F32), jax.ShapeDtypeStruct((t * ROW_TILES, LANES), F32),
                   jax.ShapeDtypeStruct((t, LANES), F32), jax.ShapeDtypeStruct((8, t), jnp.int32),
                   jax.ShapeDtypeStruct((1, LANES), F32)),
        grid=(t // tm,),
        in_specs=[pl.BlockSpec((tm, kdim), row), pl.BlockSpec((kdim, D_MODEL), fixed),
                  pl.BlockSpec((1, D_MODEL), fixed), pl.BlockSpec((tm, D_MODEL), row),
                  pl.BlockSpec((1, D_MODEL), fixed), pl.BlockSpec((D_MODEL, 2 * LANES), fixed),
                  pl.BlockSpec((1, LANES), fixed)],
        out_specs=(pl.BlockSpec((tm, D_MODEL), row), pl.BlockSpec((tm * ROW_TILES, LANES), row),
                   pl.BlockSpec((tm, LANES), row), pl.BlockSpec((8, tm), lambda i: (0, i)),
                   pl.BlockSpec((1, LANES), fixed)),
        scratch_shapes=[pltpu.VMEM((1, LANES), F32), pltpu.VMEM((tm, tm), BF16)],
        compiler_params=_cparams(("arbitrary",)),
        name="proj_router",
    )(y, w, b, h, g, wr, br)


def _rank_init(base_s, strict_s):
    tt = strict_s.shape[0]
    base_s[...] = jnp.zeros(base_s.shape, F32)
    r = lax.broadcasted_iota(jnp.int32, (tt, tt), 0)
    c = lax.broadcasted_iota(jnp.int32, (tt, tt), 1)
    strict_s[...] = jnp.where(c < r, 1.0, 0.0).astype(BF16)


def _rank_rows(rt, rk_ref, cnt_ref, base_s, strict_s):
    lane = lax.broadcasted_iota(jnp.int32, rt.shape, 1)
    lane_f = lane.astype(F32)
    e0, e1 = rt[:, 0:1], rt[:, 1:2]
    oh0 = lane_f == e0
    oh1 = lane_f == e1
    oh = jnp.where(oh0 | oh1, 1.0, 0.0)
    before = _dot(strict_s[...], oh.astype(BF16)) + base_s[...]
    r0 = jnp.sum(jnp.where(oh0, before, 0.0), axis=-1, keepdims=True)
    r1 = jnp.sum(jnp.where(oh1, before, 0.0), axis=-1, keepdims=True)
    table = jnp.where(lane == 0, r0, jnp.where(lane == 1, r1, jnp.where(lane == 2, e0, jnp.where(lane == 3, e1, 0.0))))
    rk_ref[...] = jnp.transpose(table)[0:8, :].astype(jnp.int32)
    base_s[...] = base_s[...] + jnp.sum(oh, axis=0, keepdims=True)
    cnt_ref[...] = base_s[...]


DMA_GROUP = 8


def _row_copy(src, dst, sem):
    return pltpu.make_async_copy(src, dst, sem)


def _row_tile(r):
    return pl.ds(pl.multiple_of(r * ROW_TILES, ROW_TILES), ROW_TILES)


def _dispatch_kernel(dest_ref, zflag_ref, u_ref, xout_hbm, inv_ref, fill_s, zero_s, sem, *, tt):
    i = pl.program_id(0)
    block_rows = MOE_BM * ROW_TILES

    @pl.when(i == 0)
    def _():
        fill_s[...] = jnp.full(fill_s.shape, -1, jnp.int32)
        fill = pltpu.make_async_copy(fill_s, inv_ref, sem)
        fill.start()
        fill.wait()
        zero_s[...] = jnp.zeros(zero_s.shape, F32)
        zero_block = lambda blk: _row_copy(zero_s, xout_hbm.at[pl.ds(blk * block_rows, block_rows)], sem)
        for blk in range(zflag_ref.shape[0]):
            @pl.when(zflag_ref[blk] != 0)
            def _():
                zero_block(blk).start()
        for blk in range(zflag_ref.shape[0]):
            @pl.when(zflag_ref[blk] != 0)
            def _():
                zero_block(blk).wait()

    def issue(g, carry):
        for jj in range(DMA_GROUP):
            j = g * DMA_GROUP + jj
            src = u_ref.at[_row_tile(j)]
            for k in range(TOP_K):
                d = dest_ref[k, j]
                _row_copy(src, xout_hbm.at[_row_tile(d)], sem).start(priority=k)
                inv_ref[d] = (i * tt + j) * TOP_K + k
        return carry

    lax.fori_loop(0, tt // DMA_GROUP, issue, 0)
    for k in range(TOP_K):
        _row_copy(u_ref, xout_hbm.at[pl.ds(0, tt * ROW_TILES)], sem).wait()


def _dispatch(dest, zflag, u2, tt):
    t = u2.shape[0] // ROW_TILES
    n_rows = zflag.shape[0] * MOE_BM
    return pl.pallas_call(
        functools.partial(_dispatch_kernel, tt=tt),
        out_shape=(jax.ShapeDtypeStruct((n_rows * ROW_TILES, LANES), F32), jax.ShapeDtypeStruct((n_rows,), jnp.int32)),
        grid=(t // tt,),
        in_specs=[pl.BlockSpec((TOP_K, tt), lambda i: (0, i), memory_space=pltpu.SMEM),
                  pl.BlockSpec(memory_space=pltpu.SMEM),
                  pl.BlockSpec((tt * ROW_TILES, LANES), lambda i: (i, 0))],
        out_specs=(pl.BlockSpec(memory_space=pl.ANY), pl.BlockSpec(memory_space=pltpu.SMEM)),
        scratch_shapes=[pltpu.VMEM((n_rows,), jnp.int32), pltpu.VMEM((MOE_BM * ROW_TILES, LANES), F32),
                        pltpu.SemaphoreType.DMA],
        compiler_params=_cparams(("arbitrary",)),
        name="moe_dispatch",
    )(dest, zflag, u2)


def _ffn_kernel(be_ref, nu_ref, inv_ref, x_ref, wg_ref, wu_ref, wd_ref, o2_hbm, wg_s, wu_s, wd_s, x_s, y_s, sem, *, n_slots):
    b = pl.program_id(0)
    last = pl.num_programs(0) - 1
    n_used = nu_ref[0]
    block_rows = MOE_BM * ROW_TILES

    def drain():
        _row_copy(y_s.at[0], o2_hbm.at[pl.ds(0, block_rows)], sem).wait()

    def scatter(blk):
        s = blk % 2
        for j in range(MOE_BM):
            a = inv_ref[blk * MOE_BM + j]
            slot = jnp.where(a >= 0, a, n_slots + s * MOE_BM + j)
            _row_copy(y_s.at[s, _row_tile(j)], o2_hbm.at[_row_tile(slot)], sem).start(priority=j % 2)

    def compute():
        for c in range(ROW_TILES):
            x_s[:, c * LANES:(c + 1) * LANES] = x_ref[pl.ds(c, MOE_BM, stride=ROW_TILES), :].astype(BF16)
        x = x_s[...]
        a = _dot(x, wg_s[...])
        u = _dot(x, wu_s[...])
        y = _dot(((a * _sigmoid(a)) * u).astype(BF16), wd_s[...])
        for c in range(ROW_TILES):
            y_s[b % 2, pl.ds(c, MOE_BM, stride=ROW_TILES), :] = y[:, c * LANES:(c + 1) * LANES]

    @pl.when(b == 0)
    def _():
        y_s[...] = jnp.zeros(y_s.shape, F32)
        for s in range(2):
            _row_copy(y_s.at[s], o2_hbm.at[pl.ds((n_slots + s * MOE_BM) * ROW_TILES, block_rows)], sem).start()
        for s in range(2):
            drain()

    @pl.when((b >= 2) & (b - 2 < n_used))
    def _():
        drain()

    @pl.when((b == 0) | (be_ref[b] != be_ref[jnp.maximum(b - 1, 0)]))
    def _():
        wg_s[...] = wg_ref[...].astype(BF16)
        wu_s[...] = wu_ref[...].astype(BF16)
        wd_s[...] = wd_ref[...].astype(BF16)

    @pl.when(b == 0)
    def _():
        compute()

    @pl.when((b >= 1) & (b < n_used))
    def _():
        scatter(b - 1)
        compute()

    @pl.when((b >= 1) & (b >= n_used) & (b - 1 < n_used))
    def _():
        scatter(b - 1)

    @pl.when(b == last)
    def _():
        @pl.when((b >= 1) & (b - 1 < n_used))
        def _():
            drain()

        @pl.when(b < n_used)
        def _():
            scatter(b)
            drain()


def _ffn(block_expert, n_used, inv, xbuf, wg, wu, wd, layer, n_slots):
    n_blocks = xbuf.shape[0] // (MOE_BM * ROW_TILES)
    rows = lambda b, be, nu, inv: (jnp.minimum(b, nu[0] - 1), 0)
    wmap = lambda b, be, nu, inv: (layer, be[b], 0, 0)
    return pl.pallas_call(
        functools.partial(_ffn_kernel, n_slots=n_slots),
        out_shape=jax.ShapeDtypeStruct(((n_slots + 2 * MOE_BM) * ROW_TILES, LANES), F32),
        grid_spec=pltpu.PrefetchScalarGridSpec(
            num_scalar_prefetch=3,
            grid=(n_blocks,),
            in_specs=[pl.BlockSpec((MOE_BM * ROW_TILES, LANES), rows),
                      pl.BlockSpec((None, None, D_MODEL, D_FF), wmap),
                      pl.BlockSpec((None, None, D_MODEL, D_FF), wmap),
                      pl.BlockSpec((None, None, D_FF, D_MODEL), wmap)],
            out_specs=pl.BlockSpec(memory_space=pl.ANY),
            scratch_shapes=[pltpu.VMEM((D_MODEL, D_FF), BF16), pltpu.VMEM((D_MODEL, D_FF), BF16),
                            pltpu.VMEM((D_FF, D_MODEL), BF16), pltpu.VMEM((MOE_BM, D_MODEL), BF16),
                            pltpu.VMEM((2, MOE_BM * ROW_TILES, LANES), F32), pltpu.SemaphoreType.DMA]),
        compiler_params=_cparams(("arbitrary",)),
        name="moe_ffn",
    )(block_expert, n_used, inv, xbuf, wg, wu, wd)


def _combine_rows(o2_ref, h_ref, rt_ref, o_ref):
    tt = h_ref.shape[0]
    rt = rt_ref[...]
    w0, w1 = rt[:, 2:3], rt[:, 3:4]
    for c in range(ROW_TILES):
        lanes = slice(c * LANES, (c + 1) * LANES)
        y0 = o2_ref[pl.ds(c, tt, stride=TOP_K * ROW_TILES), :]
        y1 = o2_ref[pl.ds(ROW_TILES + c, tt, stride=TOP_K * ROW_TILES), :]
        o_ref[:, lanes] = h_ref[:, lanes] + (y0 * w0 + y1 * w1)


def _combine_kernel(o2_ref, h_ref, rt_ref, g_ref, o_ref):
    _combine_rows(o2_ref, h_ref, rt_ref, o_ref)
    o_ref[...] = _rms(o_ref[...], g_ref[...])


def _combine_norm(o2, h1, rt, g, tt):
    t = h1.shape[0]
    return pl.pallas_call(
        _combine_kernel,
        out_shape=jax.ShapeDtypeStruct((t, D_MODEL), F32),
        grid=(t // tt,),
        in_specs=[pl.BlockSpec((tt * TOP_K * ROW_TILES, LANES), lambda i: (i, 0)),
                  pl.BlockSpec((tt, D_MODEL), lambda i: (i, 0)),
                  pl.BlockSpec((tt, LANES), lambda i: (i, 0)),
                  pl.BlockSpec((1, D_MODEL), lambda i: (0, 0))],
        out_specs=pl.BlockSpec((tt, D_MODEL), lambda i: (i, 0)),
        compiler_params=_cparams(("parallel",)),
        name="moe_combine",
    )(o2, h1, rt, g)


def _moe_experts(u2, rk, cnt, wg, wu, wd, layer):
    t = rk.shape[1]
    counts = cnt[0, :N_EXPERTS].astype(jnp.int32)
    padded = (counts + MOE_BM - 1) // MOE_BM * MOE_BM
    pad_end = jnp.cumsum(padded)
    pad_start = pad_end - padded
    n_blocks = (t * TOP_K) // MOE_BM + N_EXPERTS
    is_expert = rk[TOP_K:2 * TOP_K][None] == jnp.arange(N_EXPERTS, dtype=jnp.int32)[:, None, None]
    dest = jnp.sum(jnp.where(is_expert, pad_start[:, None, None], 0), axis=0) + rk[0:TOP_K]
    n_used = (pad_end[-1] // MOE_BM).astype(jnp.int32)
    blocks = jnp.arange(n_blocks, dtype=jnp.int32)
    blk = jnp.minimum(blocks, n_used - 1) * MOE_BM
    block_expert = jnp.minimum(jnp.sum(pad_end[None, :] <= blk[:, None], axis=1), N_EXPERTS - 1).astype(jnp.int32)
    closes_expert = jnp.any(((blocks[:, None] + 1) * MOE_BM == pad_end[None, :]) & (padded[None, :] > 0), axis=1)
    zflag = ((blocks >= n_used) | closes_expert).astype(jnp.int32)
    xbuf, inv = _dispatch(dest, zflag, u2, 1024)
    return _ffn(block_expert, n_used.reshape(1), inv, xbuf, wg, wu, wd, layer, t * TOP_K)


def _attn_kernel(o2_ref, h1_ref, rt_ref, g_ref, w_ref, b_ref, pos_ref, freq_ref, sp_ref, sink_ref, o_ref, h_ref, q_s,
                 kv_s, *, tm, tiles_per_seq):
    seq_start = pl.program_id(0) % tiles_per_seq == 0

    @pl.when(seq_start)
    def _():
        kv_s[0:WINDOW, :] = jnp.zeros((WINDOW, KV_OUT_W), kv_s.dtype)

    _combine_rows(o2_ref, h1_ref, rt_ref, h_ref)
    _project_qkv(h_ref, g_ref, w_ref, b_ref, pos_ref, freq_ref, sp_ref, q_s, kv_s.at[pl.ds(WINDOW, tm)])
    for j in range(tm // WINDOW):
        has_prev = jnp.logical_not(seq_start) if j == 0 else True
        _swa_block(q_s.at[pl.ds(j * WINDOW, WINDOW)], kv_s.at[pl.ds((j + 1) * WINDOW, WINDOW)],
                   kv_s.at[pl.ds(j * WINDOW, WINDOW)], sink_ref, o_ref.at[pl.ds(j * WINDOW, WINDOW)], has_prev)
    kv_s[0:WINDOW, :] = kv_s[tm:tm + WINDOW, :]


def _project_qkv(x_ref, g_ref, w_ref, b_ref, pos_ref, freq_ref, sp_ref, q_ref, kv_ref):
    u = _rms(x_ref[...], g_ref[...]).astype(BF16)
    ang = freq_ref[...] * pos_ref[...].astype(F32)
    spread = lambda v, m: sum(_dot_tn(t.astype(F32), m) for t in _split_terms(v, 3))
    cosv = jnp.cos(ang)
    sinv = jnp.sin(ang)
    c_coef = spread(cosv, sp_ref[0]) + sp_ref[3, 0:1, :]
    s_lo = spread(sinv, sp_ref[1])
    s_hi = spread(sinv, sp_ref[2])

    def rotate(z):
        return z * c_coef + pltpu.roll(z, LANES - ROT_DIM // 2, 1) * s_lo + pltpu.roll(z, ROT_DIM // 2, 1) * s_hi

    scale = SWA_HEAD_DIM ** -0.5
    for j in range(Q_W // LANES):
        sl = slice(j * LANES, (j + 1) * LANES)
        q_ref[:, sl] = (rotate(_dot(u, w_ref[:, sl]) + b_ref[:, sl]) * scale).astype(q_ref.dtype)
    for j in range(KV_OUT_W // LANES):
        sl = slice(Q_W + j * LANES, Q_W + (j + 1) * LANES)
        z = _dot(u, w_ref[:, sl]) + b_ref[:, sl]
        kv_ref[:, j * LANES:(j + 1) * LANES] = (rotate(z) if j % 2 == 0 else z).astype(kv_ref.dtype)


def _rot_tables():
    half = ROT_DIM // 2
    inv_freq = (ROPE_THETA ** (-jnp.arange(0, ROT_DIM, 2, dtype=F32) / ROT_DIM)).reshape(half, 1)
    d = np.arange(LANES) % SWA_HEAD_DIM
    f = np.arange(half)[:, None]
    sp = np.zeros((4, half, LANES), np.float32)
    sp[0] = (d[None, :] < ROT_DIM) & (d[None, :] % half == f)
    sp[1] = -((d[None, :] < half) & (d[None, :] == f)).astype(np.float32)
    sp[2] = (d[None, :] >= half) & (d[None, :] < ROT_DIM) & (d[None, :] - half == f)
    sp[3, 0] = d >= ROT_DIM
    return inv_freq, jnp.asarray(sp)


def _attn(o2, h1, rt, g, w, b, pos, sinks, tm, seq):
    t = h1.shape[0]
    row = lambda i: (i, 0)
    fixed = lambda i: (0, 0)
    wtot = Q_W + KV_OUT_W
    return pl.pallas_call(
        functools.partial(_attn_kernel, tm=tm, tiles_per_seq=seq // tm),
        out_shape=(jax.ShapeDtypeStruct((t, Q_W), BF16), jax.ShapeDtypeStruct((t, D_MODEL), F32)),
        grid=(t // tm,),
        in_specs=[pl.BlockSpec((tm * TOP_K * ROW_TILES, LANES), row), pl.BlockSpec((tm, D_MODEL), row),
                  pl.BlockSpec((tm, LANES), row), pl.BlockSpec((1, D_MODEL), fixed),
                  pl.BlockSpec((D_MODEL, wtot), fixed), pl.BlockSpec((1, wtot), fixed),
                  pl.BlockSpec((1, tm), lambda i: (0, i)), pl.BlockSpec((ROT_DIM // 2, 1), fixed),
                  pl.BlockSpec((4, ROT_DIM // 2, LANES), lambda i: (0, 0, 0)), pl.BlockSpec((1, LANES), fixed)],
        out_specs=(pl.BlockSpec((tm, Q_W), row), pl.BlockSpec((tm, D_MODEL), row)),
        scratch_shapes=[pltpu.VMEM((tm, Q_W), BF16), pltpu.VMEM((tm + WINDOW, KV_OUT_W), BF16)],
        compiler_params=_cparams(("arbitrary",)),
        name="combine_qkv_swa",
    )(o2, h1, rt, g, w, b, pos, *_rot_tables(), sinks)


def _swa_block(q_ref, kvc_ref, kvp_ref, sink_ref, o_ref, has_prev):
    neg = jnp.where(has_prev, 0.0, -jnp.inf).astype(F32)
    kj = lax.broadcasted_iota(jnp.int32, (WINDOW, WINDOW), 0)
    qi = lax.broadcasted_iota(jnp.int32, (WINDOW, WINDOW), 1)
    mask_cur = kj <= qi
    top = kj < SWA_HEAD_DIM
    lane = lax.broadcasted_iota(jnp.int32, (1, LANES), 1)
    keep_lo = jnp.where(lane < SWA_HEAD_DIM, 1.0, 0.0).astype(BF16)
    keep_hi = jnp.where(lane < SWA_HEAD_DIM, 0.0, 1.0).astype(BF16)
    zeros_half = jnp.zeros((SWA_HEAD_DIM, WINDOW), F32)

    def arranged(ref):
        k_nat, k_swp = ref[:, 0:KV_W], ref[:, 2 * KV_W:3 * KV_W]
        vt = jnp.transpose(ref[:, KV_W:2 * KV_W].astype(F32))
        vt_top = lambda g: jnp.concatenate([vt[g * SWA_HEAD_DIM:(g + 1) * SWA_HEAD_DIM], zeros_half], axis=0).astype(BF16)
        vt_bot = lambda g: jnp.concatenate([zeros_half, vt[g * SWA_HEAD_DIM:(g + 1) * SWA_HEAD_DIM]], axis=0).astype(BF16)
        return {(0, 0): (k_nat * keep_lo, vt_top(0)), (0, 1): (k_swp * keep_hi, vt_bot(0)),
                (1, 0): (k_swp * keep_lo, vt_top(1)), (1, 1): (k_nat * keep_hi, vt_bot(1))}

    cur = arranged(kvc_ref)
    prev = arranged(kvp_ref)
    heads = range(SWA_Q_HEADS)
    key = lambda h: (h // SWA_GROUP, h % 2)
    scores = []
    for h in heads:
        qp = q_ref[:, (h // 2) * LANES:(h // 2 + 1) * LANES]
        sc = _dot_nt(cur[key(h)][0], qp)
        sp = _dot_nt(prev[key(h)][0], qp)
        scores.append(jnp.where(mask_cur, sc, sp + neg))
    sinks = [sink_ref[:, h:h + 1] for h in heads]
    maxes = [jnp.maximum(jnp.max(scores[h], axis=0, keepdims=True), sinks[h]) for h in heads]
    probs = [jnp.exp(scores[h] - maxes[h]) for h in heads]
    denoms = [jnp.sum(probs[h], axis=0, keepdims=True) + jnp.exp(sinks[h] - maxes[h]) for h in heads]
    for pr in range(SWA_Q_HEADS // 2):
        num = None
        for h in (2 * pr, 2 * pr + 1):
            pc = jnp.where(mask_cur, probs[h], 0.0).astype(BF16)
            pp = jnp.where(mask_cur, 0.0, probs[h]).astype(BF16)
            part = _dot(cur[key(h)][1], pc) + _dot(prev[key(h)][1], pp)
            num = part if num is None else num + part
        den = jnp.where(top, denoms[2 * pr], denoms[2 * pr + 1])
        o_ref[:, pr * LANES:(pr + 1) * LANES] = jnp.transpose(num / den).astype(o_ref.dtype)


def _pack_in_proj(w_in):
    offs = np.cumsum((0,) + AB_SPLITS)
    part = lambda i: w_in[:, offs[i]:offs[i + 1]]
    w_main = jnp.concatenate([part(0), part(1), part(2), part(5), part(6), part(7), part(8)], axis=1).astype(BF16)
    small = jnp.concatenate([part(3), part(4), part(9)], axis=1)
    w_small = jnp.pad(small, ((0, 0), (0, LANES - small.shape[1]))).astype(BF16)
    return w_main, w_small


def _pack_qkv(w, b):
    hd = SWA_HEAD_DIM
    k0, k1 = slice(Q_W, Q_W + hd), slice(Q_W + hd, Q_W + 2 * hd)
    cols = lambda a: jnp.concatenate([a[..., :Q_W + 2 * KV_W], a[..., k1], a[..., k0]], axis=-1)
    return cols(w).astype(BF16), cols(b).reshape(1, -1).astype(F32)


def _lane_row(v, offset=0):
    return jnp.zeros((1, LANES), F32).at[0, offset:offset + v.shape[0]].set(v.astype(F32))


def kernel(x, positions, mix_norm, ffn_norm, final_norm, ab_w_in, ab_conv_w, ab_conv_b, ml_igate_b, ml_fgate_b, ml_head_norm, gla_w_lr_up, gla_gate_b, gla_head_norm, ab_w_out, swa_w_qkv, swa_b_qkv, swa_sinks, swa_w_o, swa_b_o, router_group_w, router_group_b, router_expert_w, router_expert_b, expert_w_gate, expert_w_up, expert_w_down):
    bsz, seq, d = x.shape
    t = bsz * seq
    h = x.reshape(t, d)
    row = lambda v: v.reshape(1, -1).astype(F32)

    def router_params(layer):
        wr = jnp.zeros((d, LANES), F32).at[:, :N_GROUPS].set(router_group_w[layer])
        wr = wr.at[:, N_GROUPS:N_GROUPS + N_EXPERTS].set(router_expert_w[layer])
        br = jnp.zeros((1, LANES), F32).at[0, :N_GROUPS].set(router_group_b[layer])
        br = br.at[0, N_GROUPS:N_GROUPS + N_EXPERTS].set(router_expert_b[layer])
        w_hi = wr.astype(BF16)
        w_lo = (wr - w_hi.astype(F32)).astype(BF16)
        return jnp.concatenate([w_hi, w_lo], axis=1), br

    def experts(layer):
        return expert_w_gate, expert_w_up, expert_w_down, layer

    w_main, w_small = _pack_in_proj(ab_w_in[0])
    lrup = jnp.zeros((LANES, GLA_QK_W), F32).at[SM_LR:SM_LR + GLA_LOWRANK].set(gla_w_lr_up[0]).astype(BF16)
    y = _mixer(h, row(mix_norm[0]), w_main, w_small, ab_conv_w[0], row(ab_conv_b[0]), _lane_row(ml_igate_b[0]),
               _lane_row(ml_fgate_b[0]), lrup, row(gla_gate_b[0]), row(ml_head_norm[0]), row(gla_head_norm[0]),
               bsz, seq, 256)
    wr, br = router_params(0)
    h1, u2, rt, rk, cnt = _proj_router(y, ab_w_out[0].astype(BF16), jnp.zeros((1, d), F32), h, row(ffn_norm[0]), wr, br, 512)
    o2 = _moe_experts(u2, rk, cnt, *experts(0))

    w_qkv, b_qkv = _pack_qkv(swa_w_qkv[0], swa_b_qkv[0])
    o, h = _attn(o2, h1, rt, row(mix_norm[1]), w_qkv, b_qkv, positions.reshape(1, t), _lane_row(swa_sinks[0]), 512, seq)
    wr, br = router_params(1)
    h1, u2, rt, rk, cnt = _proj_router(o, swa_w_o[0].astype(BF16), row(swa_b_o[0]), h, row(ffn_norm[1]), wr, br, 512)
    out = _combine_norm(_moe_experts(u2, rk, cnt, *experts(1)), h1, rt, row(final_norm), 512)
    return out.reshape(bsz, seq, d)
```

```python
import functools

import jax
import jax.numpy as jnp
import numpy as np
from jax import lax
from jax.experimental import pallas as pl
from jax.experimental.pallas import tpu as pltpu

F32 = jnp.float32
BF16 = jnp.bfloat16
HIGHEST = lax.Precision.HIGHEST

D_MODEL = 1024
EPS = 1e-6
ML_HEADS = 4
ML_DK = 64
ML_DV = 128
GLA_HEADS = 4
GLA_DK = 64
GLA_DV = 128
CHUNK = 64
CONV_K = 4
GLA_LOWRANK = 16
GLA_TAU = 16.0
ML_QK_W = ML_HEADS * ML_DK
ML_V_W = ML_HEADS * ML_DV
GLA_QK_W = GLA_HEADS * GLA_DK
GLA_V_W = GLA_HEADS * GLA_DV
AB_SPLITS = (2 * ML_QK_W, ML_V_W, ML_V_W, ML_HEADS, ML_HEADS, GLA_QK_W, GLA_QK_W, GLA_V_W, GLA_V_W, GLA_LOWRANK)
OFF_MQK = 0
OFF_MV = OFF_MQK + 2 * ML_QK_W
OFF_MO = OFF_MV + ML_V_W
OFF_GQ = OFF_MO + ML_V_W
OFF_GK = OFF_GQ + GLA_QK_W
OFF_GV = OFF_GK + GLA_QK_W
OFF_GG = OFF_GV + GLA_V_W
Z_MAIN_W = OFF_GG + GLA_V_W
SM_I = 0
SM_F = ML_HEADS
SM_LR = 2 * ML_HEADS
LANES = 128
SWA_Q_HEADS = 16
SWA_KV_HEADS = 2
SWA_HEAD_DIM = 64
SWA_GROUP = SWA_Q_HEADS // SWA_KV_HEADS
WINDOW = 128
ROT_DIM = SWA_HEAD_DIM // 4
ROPE_THETA = 500000.0
Q_W = SWA_Q_HEADS * SWA_HEAD_DIM
KV_W = SWA_KV_HEADS * SWA_HEAD_DIM
KV_OUT_W = 3 * KV_W
N_GROUPS = 4
EXPERTS_PER_GROUP = 8
N_EXPERTS = N_GROUPS * EXPERTS_PER_GROUP
TOP_K = 2
D_FF = 512
MOE_BM = 512
ROW_TILES = D_MODEL // LANES

VMEM_LIMIT = 56 * 1024 * 1024


def _cparams(sem):
    return pltpu.CompilerParams(dimension_semantics=sem, vmem_limit_bytes=VMEM_LIMIT)


def _rms(x, g):
    return x * lax.rsqrt(jnp.mean(x * x, axis=-1, keepdims=True) + EPS) * g


def _log_sigmoid(x):
    return jnp.minimum(x, 0.0) - jnp.log1p(jnp.exp(-jnp.abs(x)))


def _sigmoid(x):
    return 1.0 / (1.0 + jnp.exp(-x))


def _dot(a, b):
    return jnp.dot(a, b, preferred_element_type=F32)


def _dot_nt(a, b):
    return lax.dot_general(a, b, (((1,), (1,)), ((), ())), preferred_element_type=F32)


def _dot_tn(a, b):
    return lax.dot_general(a, b, (((0,), (0,)), ((), ())), preferred_element_type=F32)


IN_PROJ_CHUNK = 768
def _split_terms(x, n):
    terms = []
    for _ in range(n):
        t = x.astype(BF16)
        terms.append(t)
        x = x - t.astype(F32)
    return terms


def _dot01_left(m01, x, n):
    return sum(_dot(m01, t) for t in _split_terms(x, n))


def _dot01_right(x, m01, n):
    return sum(_dot(t, m01) for t in _split_terms(x, n))


def _mixer_kernel(xp_ref, xn_ref, g_ref, wm_ref, ws_ref, convw_ref, convb_ref, igb_ref, fgb_ref, lrup_ref, gateb_ref,
                  mlnorm_ref, glanorm_ref, y_ref, zm_s, zs_s, xpad, q_s, k_s, e_pad, hml_s, hgla_s, cn_s, m_s, st_s,
                  *, tb):
    step = pl.program_id(0) * pl.num_programs(1) + pl.program_id(1)

    def project(x, slot):
        u = _rms(x, g_ref[...]).astype(BF16)
        for n0 in range(0, Z_MAIN_W, IN_PROJ_CHUNK):
            zm_s[slot, :, n0:n0 + IN_PROJ_CHUNK] = _dot(u, wm_ref[:, n0:n0 + IN_PROJ_CHUNK]).astype(BF16)
        zs_s[slot] = _dot(u, ws_ref[...])

    @pl.when(step == 0)
    def _():
        project(xp_ref[0:tb, :], 0)

    @pl.when(pl.program_id(1) == 0)
    def _():
        xpad[0:8, :] = jnp.zeros((8, 2 * ML_QK_W), F32)
        cn_s[...] = jnp.zeros(cn_s.shape, F32)
        st_s[...] = jnp.zeros(st_s.shape, F32)
        m_s[...] = jnp.full(m_s.shape, -jnp.inf, F32)

    params = (convw_ref, convb_ref, igb_ref, fgb_ref, lrup_ref, gateb_ref, mlnorm_ref, glanorm_ref)
    scratch = (xpad, q_s, k_s, e_pad, hml_s, hgla_s, cn_s, m_s, st_s)
    project(xp_ref[tb:2 * tb, :], 1)
    _mix_block(zm_s.at[0], zs_s.at[0], *params, y_ref.at[0:tb], *scratch, tb=tb)
    project(xn_ref[...], 0)
    _mix_block(zm_s.at[1], zs_s.at[1], *params, y_ref.at[tb:2 * tb], *scratch, tb=tb)


def _mix_block(zm_ref, zs_ref, convw_ref, convb_ref, igb_ref, fgb_ref, lrup_ref, gateb_ref, mlnorm_ref,
               glanorm_ref, y_ref, xpad, q_s, k_s, e_pad, hml_s, hgla_s, cn_s, m_s, st_s, *, tb):
    nc = tb // CHUNK

    xpad[8:8 + tb, :] = zm_ref[:, OFF_MQK:OFF_MQK + 2 * ML_QK_W].astype(F32)
    conv = convb_ref[...] + convw_ref[3:4, :] * xpad[8:8 + tb, :]
    for j in range(CONV_K - 1):
        conv = conv + convw_ref[j:j + 1, :] * xpad[5 + j:5 + j + tb, :]
    xpad[0:8, :] = xpad[tb:tb + 8, :]
    qk = conv * _sigmoid(conv)
    q_s[...] = qk[:, :ML_QK_W]
    k_s[...] = qk[:, ML_QK_W:] * (ML_DK ** -0.5)

    r_t = lax.broadcasted_iota(jnp.int32, (tb, tb), 0)
    c_t = lax.broadcasted_iota(jnp.int32, (tb, tb), 1)
    tri_blk = jnp.where((r_t // CHUNK == c_t // CHUNK) & (c_t <= r_t), 1.0, 0.0).astype(BF16)
    r_e = lax.broadcasted_iota(jnp.int32, (LANES, 2 * LANES), 0)
    c_e = lax.broadcasted_iota(jnp.int32, (LANES, 2 * LANES), 1)
    spread_dk = jnp.where(c_e // ML_DK == r_e, 1.0, 0.0).astype(BF16)
    mean_dv = jnp.full((ML_DV, ML_DV), 1.0 / ML_DV, F32).astype(BF16)
    row_c = lax.broadcasted_iota(jnp.int32, (CHUNK, CHUNK), 0)
    col_c = lax.broadcasted_iota(jnp.int32, (CHUNK, CHUNK), 1)
    causal = col_c <= row_c
    lane_c = lax.broadcasted_iota(jnp.int32, (CHUNK, LANES), 1)
    ones_dv = jnp.ones((CHUNK, ML_DV), BF16)
    chunk_rows = lambda c: slice(c * CHUNK, (c + 1) * CHUNK)

    zs = zs_ref[...]
    ig = zs + igb_ref[...]
    lf = _log_sigmoid(pltpu.roll(zs, LANES - SM_F, 1) + fgb_ref[...])
    bc = _dot01_left(tri_blk, lf, 3)
    bc3 = bc.reshape(nc, CHUNK, LANES)
    g3 = bc3[:, CHUNK - 1:CHUNK, :]
    a3 = g3 - bc3 + ig.reshape(nc, CHUNK, LANES)
    amax3 = jnp.max(a3, axis=1, keepdims=True)
    wa = jnp.exp(a3 - amax3).reshape(tb, LANES)
    m_run = m_s[...]
    m_prev, s_old, s_in = [], [], []
    for c in range(nc):
        m_new = jnp.maximum(g3[c] + m_run, amax3[c])
        m_prev.append(m_run)
        s_old.append(jnp.exp(g3[c] + m_run - m_new))
        s_in.append(jnp.exp(amax3[c] - m_new))
        m_run = m_new
    m_s[...] = m_run
    e_nat = ig - bc
    e_pad[0:CHUNK, :] = jnp.full((CHUNK, LANES), -jnp.inf, F32)
    e_pad[CHUNK:CHUNK + tb, :] = e_nat
    pos = lax.broadcasted_iota(jnp.int32, (tb, LANES), 0) % CHUNK
    shift = 1
    while shift < CHUNK:
        shifted = e_pad[CHUNK - shift:CHUNK - shift + tb, :]
        e_pad[CHUNK:CHUNK + tb, :] = jnp.maximum(e_pad[CHUNK:CHUNK + tb, :], jnp.where(pos >= shift, shifted, -jnp.inf))
        shift *= 2
    m_intra = bc + e_pad[CHUNK:CHUNK + tb, :]
    il = jnp.concatenate([bc3[c] + m_prev[c] for c in range(nc)], axis=0)
    mt = jnp.maximum(il, m_intra)
    s_inter = jnp.exp(il - mt)
    exp_neg = jnp.exp(-mt)
    xn = bc - mt
    e_up = pltpu.roll(e_nat, ML_HEADS, 1)
    wa_x = _dot01_right(wa, spread_dk, 2)
    si_x = _dot01_right(s_inter, spread_dk, 2)
    q_all = q_s[...]
    k_all = k_s[...]
    kw_b = (k_all * wa_x).astype(BF16)
    qs_b = (q_all * si_x).astype(BF16)
    q_b = q_all.astype(BF16)
    k_b = k_all.astype(BF16)

    qkm, upd, vo = {}, {}, {}
    for c in range(nc):
        rows = chunk_rows(c)
        for h in range(ML_HEADS):
            dk = slice(h * ML_DK, (h + 1) * ML_DK)
            x_t = jnp.where(lane_c == h, xn[rows], jnp.where(lane_c == h + ML_HEADS, 1.0, 0.0))
            y_t = jnp.where(lane_c == h, 1.0, jnp.where(lane_c == h + ML_HEADS, e_up[rows], 0.0))
            d = lax.dot_general(x_t, y_t, (((1,), (1,)), ((), ())), precision=HIGHEST, preferred_element_type=F32)
            p = jnp.where(causal, jnp.exp(d), 0.0)
            qkm[c, h] = (_dot_nt(q_b[rows, dk], k_b[rows, dk]) * p).astype(BF16)
            vo[c, h] = jnp.concatenate([zm_ref[rows, OFF_MV + h * ML_DV:OFF_MV + (h + 1) * ML_DV], ones_dv], axis=1)
            upd[c, h] = _dot_tn(kw_b[rows, dk], vo[c, h])
    for h in range(ML_HEADS):
        dk = slice(h * ML_DK, (h + 1) * ML_DK)
        dv = slice(h * ML_DV, (h + 1) * ML_DV)
        cn = cn_s[h]
        for c in range(nc):
            rows = chunk_rows(c)
            res = _dot(qs_b[rows, dk], cn.astype(BF16)) + _dot(qkm[c, h], vo[c, h])
            num, den = res[:, :ML_DV], res[:, ML_DV:]
            hml_s[rows, dv] = num / jnp.maximum(jnp.abs(den), exp_neg[rows, h:h + 1])
            cn = s_old[c][:, h:h + 1] * cn + s_in[c][:, h:h + 1] * upd[c, h]
        cn_s[h] = cn

    la = _log_sigmoid(_dot(zs.astype(BF16), lrup_ref[...]) + gateb_ref[...]) * (1.0 / GLA_TAU)
    bcg = _dot01_left(tri_blk, la, 3)
    bcg3 = bcg.reshape(nc, CHUNK, GLA_QK_W)
    gg3 = bcg3[:, CHUNK - 1:CHUNK, :]
    gq = zm_ref[:, OFF_GQ:OFF_GQ + GLA_QK_W].astype(F32)
    gk = zm_ref[:, OFF_GK:OFF_GK + GLA_QK_W].astype(F32) * (GLA_DK ** -0.5)
    q_dec = (gq * jnp.exp(bcg)).astype(BF16)
    k_inv = (gk * jnp.exp(-bcg)).astype(BF16)
    k_end = (gk * jnp.exp(gg3 - bcg3).reshape(tb, GLA_QK_W)).astype(BF16)
    eg3 = jnp.exp(gg3)
    att, updg = {}, {}
    for c in range(nc):
        rows = chunk_rows(c)
        for h in range(GLA_HEADS):
            dk = slice(h * GLA_DK, (h + 1) * GLA_DK)
            vh = zm_ref[rows, OFF_GV + h * GLA_DV:OFF_GV + (h + 1) * GLA_DV]
            att[c, h] = jnp.where(causal, _dot_nt(q_dec[rows, dk], k_inv[rows, dk]), 0.0).astype(BF16)
            updg[c, h] = _dot_tn(vh, k_end[rows, dk])
    for h in range(GLA_HEADS):
        dk = slice(h * GLA_DK, (h + 1) * GLA_DK)
        st = st_s[h]
        for c in range(nc):
            rows = chunk_rows(c)
            vh = zm_ref[rows, OFF_GV + h * GLA_DV:OFF_GV + (h + 1) * GLA_DV]
            hgla_s[rows, h * GLA_DV:(h + 1) * GLA_DV] = _dot_nt(q_dec[rows, dk], st.astype(BF16)) + _dot(att[c, h], vh)
            st = st * eg3[c][:, dk] + updg[c, h]
        st_s[h] = st

    mean = lambda x: _dot01_right(x, mean_dv, 2)
    for h in range(ML_HEADS):
        sl = slice(h * ML_DV, (h + 1) * ML_DV)
        hh = hml_s[:, sl]
        d = hh - mean(hh)
        hn = d * lax.rsqrt(mean(d * d) + EPS)
        og = zm_ref[:, OFF_MO + h * ML_DV:OFF_MO + (h + 1) * ML_DV].astype(F32)
        y_ref[:, sl] = (hn * mlnorm_ref[:, sl] * _sigmoid(og)).astype(y_ref.dtype)
    for h in range(GLA_HEADS):
        sl = slice(h * GLA_DV, (h + 1) * GLA_DV)
        o = hgla_s[:, sl]
        on = o * lax.rsqrt(mean(o * o) + EPS)
        gg = zm_ref[:, OFF_GG + h * GLA_DV:OFF_GG + (h + 1) * GLA_DV].astype(F32)
        y_ref[:, ML_V_W + h * GLA_DV:ML_V_W + (h + 1) * GLA_DV] = (on * glanorm_ref[:, sl] * (gg * _sigmoid(gg))).astype(y_ref.dtype)


def _mixer(h, g, w_main, w_small, convw, convb, igb, fgb, lrup, gateb, mlnorm, glanorm, bsz, seq, tb):
    nt = seq // (2 * tb)
    last_block = bsz * seq // tb - 1
    const = lambda shape: pl.BlockSpec(shape, lambda b, i: (0,) * len(shape))
    return pl.pallas_call(
        functools.partial(_mixer_kernel, tb=tb),
        out_shape=jax.ShapeDtypeStruct((bsz * seq, ML_V_W + GLA_V_W), BF16),
        grid=(bsz, nt),
        in_specs=[pl.BlockSpec((2 * tb, D_MODEL), lambda b, i: (b * nt + i, 0)),
                  pl.BlockSpec((tb, D_MODEL), lambda b, i: (jnp.minimum(2 * (b * nt + i) + 2, last_block), 0)),
                  const((1, D_MODEL)), const((D_MODEL, Z_MAIN_W)), const((D_MODEL, LANES)),
                  const((CONV_K, 2 * ML_QK_W)), const((1, 2 * ML_QK_W)), const((1, LANES)), const((1, LANES)),
                  const((LANES, GLA_QK_W)), const((1, GLA_QK_W)), const((1, ML_V_W)), const((1, GLA_V_W))],
        out_specs=pl.BlockSpec((2 * tb, ML_V_W + GLA_V_W), lambda b, i: (b * nt + i, 0)),
        scratch_shapes=[pltpu.VMEM((2, tb, Z_MAIN_W), BF16), pltpu.VMEM((2, tb, LANES), F32),
                        pltpu.VMEM((tb + 8, 2 * ML_QK_W), F32),
                        pltpu.VMEM((tb, ML_QK_W), F32), pltpu.VMEM((tb, ML_QK_W), F32),
                        pltpu.VMEM((tb + CHUNK, LANES), F32),
                        pltpu.VMEM((tb, ML_V_W), F32), pltpu.VMEM((tb, GLA_V_W), F32),
                        pltpu.VMEM((ML_HEADS, ML_DK, ML_DV + LANES), F32),
                        pltpu.VMEM((1, LANES), F32), pltpu.VMEM((GLA_HEADS, GLA_DV, GLA_DK), F32)],
        compiler_params=_cparams(("arbitrary", "arbitrary")),
        name="mlstm_gla",
    )(h, h, g, w_main, w_small, convw, convb, igb, fgb, lrup, gateb, mlnorm, glanorm)


def _proj_router_kernel(y_ref, w_ref, b_ref, h_ref, g_ref, wr_ref, br_ref, h1_ref, u2_ref, rt_ref, rk_ref, cnt_ref,
                        base_s, strict_s, rtp_s):
    step = pl.program_id(0)

    @pl.when(step == 0)
    def _():
        _rank_init(base_s, strict_s)
        rtp_s[...] = jnp.zeros(rtp_s.shape, F32)

    _rank_rows(rtp_s[...], step > 0, rk_ref.at[jnp.maximum(step - 1, 0)], cnt_ref, base_s, strict_s)

    h1 = h_ref[...] + (_dot(y_ref[...], w_ref[...]) + b_ref[...])
    h1_ref[...] = h1
    u2 = _rms(h1, g_ref[...])
    for c in range(ROW_TILES):
        u2_ref[pl.ds(c, u2.shape[0], stride=ROW_TILES), :] = u2[:, c * LANES:(c + 1) * LANES]
    u_hi = u2.astype(BF16)
    u_lo = (u2 - u_hi.astype(F32)).astype(BF16)
    part = _dot(u_hi, wr_ref[...])
    logits = part[:, :LANES] + (part[:, LANES:] + _dot(u_lo, wr_ref[:, :LANES])) + br_ref[...]
    lane = lax.broadcasted_iota(jnp.int32, logits.shape, 1)
    lane_f = lane.astype(F32)
    big = float(LANES)
    gl = jnp.where(lane < N_GROUPS, logits, -jnp.inf)
    g_max = jnp.max(gl, axis=-1, keepdims=True)
    g_idx = jnp.min(jnp.where(gl == g_max, lane_f, big), axis=-1, keepdims=True)
    g_p = 1.0 / jnp.sum(jnp.exp(gl - g_max), axis=-1, keepdims=True)
    e_grp = ((lane - N_GROUPS) // EXPERTS_PER_GROUP).astype(F32)
    in_grp = (lane >= N_GROUPS) & (lane < N_GROUPS + N_EXPERTS) & (e_grp == g_idx)
    el = jnp.where(in_grp, logits, -jnp.inf)
    t1 = jnp.max(el, axis=-1, keepdims=True)
    i1 = jnp.min(jnp.where(el == t1, lane_f, big), axis=-1, keepdims=True)
    el2 = jnp.where(lane_f == i1, -jnp.inf, el)
    t2 = jnp.max(el2, axis=-1, keepdims=True)
    i2 = jnp.min(jnp.where(el2 == t2, lane_f, big), axis=-1, keepdims=True)
    e21 = jnp.exp(t2 - t1)
    p1 = 1.0 / (1.0 + e21)
    rt = jnp.where(lane == 0, i1 - N_GROUPS,
                   jnp.where(lane == 1, i2 - N_GROUPS,
                             jnp.where(lane == 2, g_p * p1, jnp.where(lane == 3, g_p * (e21 * p1), 0.0))))
    rt_ref[...] = rt
    rtp_s[...] = rt

    @pl.when(step == pl.num_programs(0) - 1)
    def _():
        _rank_rows(rt, True, rk_ref.at[step], cnt_ref, base_s, strict_s)


def _proj_router(y, w, b, h, g, wr, br, tm):
    t, kdim = y.shape
    row = lambda i: (i, 0)
    fixed = lambda i: (0, 0)
    h1, u2, rt, rk, cnt = pl.pallas_call(
        _proj_router_kernel,
        out_shape=(jax.ShapeDtypeStruct((t, D_MODEL), F32), jax.ShapeDtypeStruct((t * ROW_TILES, LANES), F32),
                   jax.ShapeDtypeStruct((t, LANES), F32), jax.ShapeDtypeStruct((t // tm, 8, tm), jnp.int32),
                   jax.ShapeDtypeStruct((1, LANES), F32)),
        grid=(t // tm,),
        in_specs=[pl.BlockSpec((tm, kdim), row), pl.BlockSpec((kdim, D_MODEL), fixed),
                  pl.BlockSpec((1, D_MODEL), fixed), pl.BlockSpec((tm, D_MODEL), row),
                  pl.BlockSpec((1, D_MODEL), fixed), pl.BlockSpec((D_MODEL, 2 * LANES), fixed),
                  pl.BlockSpec((1, LANES), fixed)],
        out_specs=(pl.BlockSpec((tm, D_MODEL), row), pl.BlockSpec((tm * ROW_TILES, LANES), row),
                   pl.BlockSpec((tm, LANES), row), pl.BlockSpec((t // tm, 8, tm), lambda i: (0, 0, 0)),
                   pl.BlockSpec((1, LANES), fixed)),
        scratch_shapes=[pltpu.VMEM((1, LANES), F32), pltpu.VMEM((tm, tm), BF16), pltpu.VMEM((tm, LANES), F32)],
        compiler_params=_cparams(("arbitrary",)),
        name="proj_router",
    )(y, w, b, h, g, wr, br)
    return h1, u2, rt, rk.transpose(1, 0, 2).reshape(8, t), cnt


def _rank_init(base_s, strict_s):
    tt = strict_s.shape[0]
    base_s[...] = jnp.zeros(base_s.shape, F32)
    r = lax.broadcasted_iota(jnp.int32, (tt, tt), 0)
    c = lax.broadcasted_iota(jnp.int32, (tt, tt), 1)
    strict_s[...] = jnp.where(c < r, 1.0, 0.0).astype(BF16)


def _rank_rows(rt, valid, rk_ref, cnt_ref, base_s, strict_s):
    lane = lax.broadcasted_iota(jnp.int32, rt.shape, 1)
    lane_f = lane.astype(F32)
    e0, e1 = rt[:, 0:1], rt[:, 1:2]
    oh0 = (lane_f == e0) & valid
    oh1 = (lane_f == e1) & valid
    oh = jnp.where(oh0 | oh1, 1.0, 0.0)
    before = _dot(strict_s[...], oh.astype(BF16)) + base_s[...]
    r0 = jnp.sum(jnp.where(oh0, before, 0.0), axis=-1, keepdims=True)
    r1 = jnp.sum(jnp.where(oh1, before, 0.0), axis=-1, keepdims=True)
    table = jnp.where(lane == 0, r0, jnp.where(lane == 1, r1, jnp.where(lane == 2, e0, jnp.where(lane == 3, e1, 0.0))))
    rk_ref[...] = jnp.transpose(table)[0:8, :].astype(jnp.int32)
    base_s[...] = base_s[...] + jnp.sum(oh, axis=0, keepdims=True)
    cnt_ref[...] = base_s[...]


DMA_GROUP = 8


def _row_copy(src, dst, sem):
    return pltpu.make_async_copy(src, dst, sem)


def _row_tile(r):
    return pl.ds(pl.multiple_of(r * ROW_TILES, ROW_TILES), ROW_TILES)


def _dispatch_kernel(dest_ref, zflag_ref, fill_ref, u_ref, xout_hbm, inv_ref, zero_s, sem, *, tt):
    i = pl.program_id(0)
    block_rows = MOE_BM * ROW_TILES

    @pl.when(i == 0)
    def _():
        fill = pltpu.make_async_copy(fill_ref, inv_ref, sem)
        fill.start()
        fill.wait()
        zero_s[...] = jnp.zeros(zero_s.shape, F32)
        zero_block = lambda blk: _row_copy(zero_s, xout_hbm.at[pl.ds(blk * block_rows, block_rows)], sem)
        for blk in range(zflag_ref.shape[0]):
            @pl.when(zflag_ref[blk] != 0)
            def _():
                zero_block(blk).start()
        for blk in range(zflag_ref.shape[0]):
            @pl.when(zflag_ref[blk] != 0)
            def _():
                zero_block(blk).wait()

    def issue(g, carry):
        for jj in range(DMA_GROUP):
            j = g * DMA_GROUP + jj
            src = u_ref.at[_row_tile(j)]
            for k in range(TOP_K):
                d = dest_ref[k, j]
                _row_copy(src, xout_hbm.at[_row_tile(d)], sem).start(priority=k)
                inv_ref[d] = (i * tt + j) * TOP_K + k
        return carry

    lax.fori_loop(0, tt // DMA_GROUP, issue, 0)
    for k in range(TOP_K):
        _row_copy(u_ref, xout_hbm.at[pl.ds(0, tt * ROW_TILES)], sem).wait()


def _dispatch(dest, zflag, u2, tt):
    t = u2.shape[0] // ROW_TILES
    n_rows = zflag.shape[0] * MOE_BM
    row = jnp.arange(n_rows, dtype=jnp.int32)
    fill = t * TOP_K + (row // MOE_BM % 2) * MOE_BM + row % MOE_BM
    return pl.pallas_call(
        functools.partial(_dispatch_kernel, tt=tt),
        out_shape=(jax.ShapeDtypeStruct((n_rows * ROW_TILES, LANES), F32), jax.ShapeDtypeStruct((n_rows,), jnp.int32)),
        grid=(t // tt,),
        in_specs=[pl.BlockSpec((TOP_K, tt), lambda i: (0, i), memory_space=pltpu.SMEM),
                  pl.BlockSpec(memory_space=pltpu.SMEM),
                  pl.BlockSpec(memory_space=pltpu.VMEM),
                  pl.BlockSpec((tt * ROW_TILES, LANES), lambda i: (i, 0))],
        out_specs=(pl.BlockSpec(memory_space=pl.ANY), pl.BlockSpec(memory_space=pltpu.SMEM)),
        scratch_shapes=[pltpu.VMEM((MOE_BM * ROW_TILES, LANES), F32), pltpu.SemaphoreType.DMA],
        compiler_params=_cparams(("arbitrary",)),
        name="moe_dispatch",
    )(dest, zflag, fill, u2)


def _ffn_kernel(be_ref, nu_ref, inv_ref, x_ref, wg_ref, wu_ref, wd_ref, o2_hbm, wg_s, wu_s, wd_s, x_s, y_s, sem, *, n_slots):
    b = pl.program_id(0)
    last = pl.num_programs(0) - 1
    n_used = nu_ref[0]
    block_rows = MOE_BM * ROW_TILES

    def drain():
        _row_copy(y_s.at[0], o2_hbm.at[pl.ds(0, block_rows)], sem).wait()

    def scatter(blk):
        s = blk % 2
        for j in range(MOE_BM):
            slot = inv_ref[blk * MOE_BM + j]
            _row_copy(y_s.at[s, _row_tile(j)], o2_hbm.at[_row_tile(slot)], sem).start(priority=j % 2)

    def compute():
        for c in range(ROW_TILES):
            x_s[:, c * LANES:(c + 1) * LANES] = x_ref[pl.ds(c, MOE_BM, stride=ROW_TILES), :].astype(BF16)
        x = x_s[...]
        a = _dot(x, wg_s[...])
        u = _dot(x, wu_s[...])
        y = _dot(((a * _sigmoid(a)) * u).astype(BF16), wd_s[...])
        for c in range(ROW_TILES):
            y_s[b % 2, pl.ds(c, MOE_BM, stride=ROW_TILES), :] = y[:, c * LANES:(c + 1) * LANES]

    @pl.when(b == 0)
    def _():
        y_s[...] = jnp.zeros(y_s.shape, F32)
        for s in range(2):
            _row_copy(y_s.at[s], o2_hbm.at[pl.ds((n_slots + s * MOE_BM) * ROW_TILES, block_rows)], sem).start()
        for s in range(2):
            drain()

    @pl.when((b >= 2) & (b - 2 < n_used))
    def _():
        drain()

    @pl.when((b == 0) | (be_ref[b] != be_ref[jnp.maximum(b - 1, 0)]))
    def _():
        wg_s[...] = wg_ref[...].astype(BF16)
        wu_s[...] = wu_ref[...].astype(BF16)
        wd_s[...] = wd_ref[...].astype(BF16)

    @pl.when(b == 0)
    def _():
        compute()

    @pl.when((b >= 1) & (b < n_used))
    def _():
        scatter(b - 1)
        compute()

    @pl.when((b >= 1) & (b >= n_used) & (b - 1 < n_used))
    def _():
        scatter(b - 1)

    @pl.when(b == last)
    def _():
        @pl.when((b >= 1) & (b - 1 < n_used))
        def _():
            drain()

        @pl.when(b < n_used)
        def _():
            scatter(b)
            drain()


def _ffn(block_expert, n_used, inv, xbuf, wg, wu, wd, layer, n_slots):
    n_blocks = xbuf.shape[0] // (MOE_BM * ROW_TILES)
    rows = lambda b, be, nu, inv: (jnp.minimum(b, nu[0] - 1), 0)
    wmap = lambda b, be, nu, inv: (layer, be[b], 0, 0)
    return pl.pallas_call(
        functools.partial(_ffn_kernel, n_slots=n_slots),
        out_shape=jax.ShapeDtypeStruct(((n_slots + 2 * MOE_BM) * ROW_TILES, LANES), F32),
        grid_spec=pltpu.PrefetchScalarGridSpec(
            num_scalar_prefetch=3,
            grid=(n_blocks,),
            in_specs=[pl.BlockSpec((MOE_BM * ROW_TILES, LANES), rows),
                      pl.BlockSpec((None, None, D_MODEL, D_FF), wmap),
                      pl.BlockSpec((None, None, D_MODEL, D_FF), wmap),
                      pl.BlockSpec((None, None, D_FF, D_MODEL), wmap)],
            out_specs=pl.BlockSpec(memory_space=pl.ANY),
            scratch_shapes=[pltpu.VMEM((D_MODEL, D_FF), BF16), pltpu.VMEM((D_MODEL, D_FF), BF16),
                            pltpu.VMEM((D_FF, D_MODEL), BF16), pltpu.VMEM((MOE_BM, D_MODEL), BF16),
                            pltpu.VMEM((2, MOE_BM * ROW_TILES, LANES), F32), pltpu.SemaphoreType.DMA]),
        compiler_params=_cparams(("arbitrary",)),
        name="moe_ffn",
    )(block_expert, n_used, inv, xbuf, wg, wu, wd)


def _combine_rows(o2_ref, h_ref, rt_ref, o_ref):
    tt = h_ref.shape[0]
    rt = rt_ref[...]
    w0, w1 = rt[:, 2:3], rt[:, 3:4]
    for c in range(ROW_TILES):
        lanes = slice(c * LANES, (c + 1) * LANES)
        y0 = o2_ref[pl.ds(c, tt, stride=TOP_K * ROW_TILES), :]
        y1 = o2_ref[pl.ds(ROW_TILES + c, tt, stride=TOP_K * ROW_TILES), :]
        o_ref[:, lanes] = h_ref[:, lanes] + (y0 * w0 + y1 * w1)


def _combine_kernel(o2_ref, h_ref, rt_ref, g_ref, o_ref):
    _combine_rows(o2_ref, h_ref, rt_ref, o_ref)
    o_ref[...] = _rms(o_ref[...], g_ref[...])


def _combine_norm(o2, h1, rt, g, tt):
    t = h1.shape[0]
    return pl.pallas_call(
        _combine_kernel,
        out_shape=jax.ShapeDtypeStruct((t, D_MODEL), F32),
        grid=(t // tt,),
        in_specs=[pl.BlockSpec((tt * TOP_K * ROW_TILES, LANES), lambda i: (i, 0)),
                  pl.BlockSpec((tt, D_MODEL), lambda i: (i, 0)),
                  pl.BlockSpec((tt, LANES), lambda i: (i, 0)),
                  pl.BlockSpec((1, D_MODEL), lambda i: (0, 0))],
        out_specs=pl.BlockSpec((tt, D_MODEL), lambda i: (i, 0)),
        compiler_params=_cparams(("parallel",)),
        name="moe_combine",
    )(o2, h1, rt, g)


def _moe_experts(u2, rk, cnt, wg, wu, wd, layer):
    t = rk.shape[1]
    counts = cnt[0, :N_EXPERTS].astype(jnp.int32)
    padded = (counts + MOE_BM - 1) // MOE_BM * MOE_BM
    pad_end = jnp.cumsum(padded)
    pad_start = pad_end - padded
    n_blocks = (t * TOP_K) // MOE_BM + N_EXPERTS
    is_expert = rk[TOP_K:2 * TOP_K][None] == jnp.arange(N_EXPERTS, dtype=jnp.int32)[:, None, None]
    dest = jnp.sum(jnp.where(is_expert, pad_start[:, None, None], 0), axis=0) + rk[0:TOP_K]
    n_used = (pad_end[-1] // MOE_BM).astype(jnp.int32)
    blocks = jnp.arange(n_blocks, dtype=jnp.int32)
    blk = jnp.minimum(blocks, n_used - 1) * MOE_BM
    block_expert = jnp.minimum(jnp.sum(pad_end[None, :] <= blk[:, None], axis=1), N_EXPERTS - 1).astype(jnp.int32)
    closes_expert = jnp.any(((blocks[:, None] + 1) * MOE_BM == pad_end[None, :]) & (padded[None, :] > 0), axis=1)
    zflag = ((blocks >= n_used) | closes_expert).astype(jnp.int32)
    xbuf, inv = _dispatch(dest, zflag, u2, 1024)
    return _ffn(block_expert, n_used.reshape(1), inv, xbuf, wg, wu, wd, layer, t * TOP_K)


def _attn_kernel(o2_ref, h1_ref, rt_ref, g_ref, w_ref, b_ref, pos_ref, freq_ref, sp_ref, sink_ref, o_ref, h_ref, q_s,
                 kv_s, *, tm, tiles_per_seq):
    seq_start = pl.program_id(0) % tiles_per_seq == 0

    @pl.when(seq_start)
    def _():
        kv_s[0:WINDOW, :] = jnp.zeros((WINDOW, KV_OUT_W), kv_s.dtype)

    _combine_rows(o2_ref, h1_ref, rt_ref, h_ref)
    _project_qkv(h_ref, g_ref, w_ref, b_ref, pos_ref, freq_ref, sp_ref, q_s, kv_s.at[pl.ds(WINDOW, tm)])
    for j in range(tm // WINDOW):
        has_prev = jnp.logical_not(seq_start) if j == 0 else True
        _swa_block(q_s.at[pl.ds(j * WINDOW, WINDOW)], kv_s.at[pl.ds((j + 1) * WINDOW, WINDOW)],
                   kv_s.at[pl.ds(j * WINDOW, WINDOW)], sink_ref, o_ref.at[pl.ds(j * WINDOW, WINDOW)], has_prev)
    kv_s[0:WINDOW, :] = kv_s[tm:tm + WINDOW, :]


def _project_qkv(x_ref, g_ref, w_ref, b_ref, pos_ref, freq_ref, sp_ref, q_ref, kv_ref):
    u = _rms(x_ref[...], g_ref[...]).astype(BF16)
    ang = freq_ref[...] * pos_ref[...].astype(F32)
    spread = lambda v, m: sum(_dot_tn(t.astype(F32), m) for t in _split_terms(v, 3))
    cosv = jnp.cos(ang)
    sinv = jnp.sin(ang)
    c_coef = spread(cosv, sp_ref[0]) + sp_ref[3, 0:1, :]
    s_lo = spread(sinv, sp_ref[1])
    s_hi = spread(sinv, sp_ref[2])

    def rotate(z):
        return z * c_coef + pltpu.roll(z, LANES - ROT_DIM // 2, 1) * s_lo + pltpu.roll(z, ROT_DIM // 2, 1) * s_hi

    scale = SWA_HEAD_DIM ** -0.5
    for j in range(Q_W // LANES):
        sl = slice(j * LANES, (j + 1) * LANES)
        q_ref[:, sl] = (rotate(_dot(u, w_ref[:, sl]) + b_ref[:, sl]) * scale).astype(q_ref.dtype)
    for j in range(KV_OUT_W // LANES):
        sl = slice(Q_W + j * LANES, Q_W + (j + 1) * LANES)
        z = _dot(u, w_ref[:, sl]) + b_ref[:, sl]
        kv_ref[:, j * LANES:(j + 1) * LANES] = (rotate(z) if j % 2 == 0 else z).astype(kv_ref.dtype)


def _rot_tables():
    half = ROT_DIM // 2
    inv_freq = (ROPE_THETA ** (-jnp.arange(0, ROT_DIM, 2, dtype=F32) / ROT_DIM)).reshape(half, 1)
    d = np.arange(LANES) % SWA_HEAD_DIM
    f = np.arange(half)[:, None]
    sp = np.zeros((4, half, LANES), np.float32)
    sp[0] = (d[None, :] < ROT_DIM) & (d[None, :] % half == f)
    sp[1] = -((d[None, :] < half) & (d[None, :] == f)).astype(np.float32)
    sp[2] = (d[None, :] >= half) & (d[None, :] < ROT_DIM) & (d[None, :] - half == f)
    sp[3, 0] = d >= ROT_DIM
    return inv_freq, jnp.asarray(sp)


def _attn(o2, h1, rt, g, w, b, pos, sinks, tm, seq):
    t = h1.shape[0]
    row = lambda i: (i, 0)
    fixed = lambda i: (0, 0)
    wtot = Q_W + KV_OUT_W
    return pl.pallas_call(
        functools.partial(_attn_kernel, tm=tm, tiles_per_seq=seq // tm),
        out_shape=(jax.ShapeDtypeStruct((t, Q_W), BF16), jax.ShapeDtypeStruct((t, D_MODEL), F32)),
        grid=(t // tm,),
        in_specs=[pl.BlockSpec((tm * TOP_K * ROW_TILES, LANES), row), pl.BlockSpec((tm, D_MODEL), row),
                  pl.BlockSpec((tm, LANES), row), pl.BlockSpec((1, D_MODEL), fixed),
                  pl.BlockSpec((D_MODEL, wtot), fixed), pl.BlockSpec((1, wtot), fixed),
                  pl.BlockSpec((1, tm), lambda i: (0, i)), pl.BlockSpec((ROT_DIM // 2, 1), fixed),
                  pl.BlockSpec((4, ROT_DIM // 2, LANES), lambda i: (0, 0, 0)), pl.BlockSpec((1, LANES), fixed)],
        out_specs=(pl.BlockSpec((tm, Q_W), row), pl.BlockSpec((tm, D_MODEL), row)),
        scratch_shapes=[pltpu.VMEM((tm, Q_W), BF16), pltpu.VMEM((tm + WINDOW, KV_OUT_W), BF16)],
        compiler_params=_cparams(("arbitrary",)),
        name="combine_qkv_swa",
    )(o2, h1, rt, g, w, b, pos, *_rot_tables(), sinks)


def _swa_block(q_ref, kvc_ref, kvp_ref, sink_ref, o_ref, has_prev):
    neg = jnp.where(has_prev, 0.0, -jnp.inf).astype(F32)
    kj = lax.broadcasted_iota(jnp.int32, (WINDOW, WINDOW), 0)
    qi = lax.broadcasted_iota(jnp.int32, (WINDOW, WINDOW), 1)
    mask_cur = kj <= qi
    top = kj < SWA_HEAD_DIM
    lane = lax.broadcasted_iota(jnp.int32, (1, LANES), 1)
    keep_lo = jnp.where(lane < SWA_HEAD_DIM, 1.0, 0.0).astype(BF16)
    keep_hi = jnp.where(lane < SWA_HEAD_DIM, 0.0, 1.0).astype(BF16)
    zeros_half = jnp.zeros((SWA_HEAD_DIM, WINDOW), F32)

    def arranged(ref):
        k_nat, k_swp = ref[:, 0:KV_W], ref[:, 2 * KV_W:3 * KV_W]
        vt = jnp.transpose(ref[:, KV_W:2 * KV_W].astype(F32))
        vt_top = lambda g: jnp.concatenate([vt[g * SWA_HEAD_DIM:(g + 1) * SWA_HEAD_DIM], zeros_half], axis=0).astype(BF16)
        vt_bot = lambda g: jnp.concatenate([zeros_half, vt[g * SWA_HEAD_DIM:(g + 1) * SWA_HEAD_DIM]], axis=0).astype(BF16)
        return {(0, 0): (k_nat * keep_lo, vt_top(0)), (0, 1): (k_swp * keep_hi, vt_bot(0)),
                (1, 0): (k_swp * keep_lo, vt_top(1)), (1, 1): (k_nat * keep_hi, vt_bot(1))}

    cur = arranged(kvc_ref)
    prev = arranged(kvp_ref)
    heads = range(SWA_Q_HEADS)
    key = lambda h: (h // SWA_GROUP, h % 2)
    scores = []
    for h in heads:
        qp = q_ref[:, (h // 2) * LANES:(h // 2 + 1) * LANES]
        sc = _dot_nt(cur[key(h)][0], qp)
        sp = _dot_nt(prev[key(h)][0], qp)
        scores.append(jnp.where(mask_cur, sc, sp + neg))
    sinks = [sink_ref[:, h:h + 1] for h in heads]
    maxes = [jnp.maximum(jnp.max(scores[h], axis=0, keepdims=True), sinks[h]) for h in heads]
    probs = [jnp.exp(scores[h] - maxes[h]) for h in heads]
    denoms = [jnp.sum(probs[h], axis=0, keepdims=True) + jnp.exp(sinks[h] - maxes[h]) for h in heads]
    for pr in range(SWA_Q_HEADS // 2):
        num = None
        for h in (2 * pr, 2 * pr + 1):
            pc = jnp.where(mask_cur, probs[h], 0.0).astype(BF16)
            pp = jnp.where(mask_cur, 0.0, probs[h]).astype(BF16)
            part = _dot(cur[key(h)][1], pc) + _dot(prev[key(h)][1], pp)
            num = part if num is None else num + part
        den = jnp.where(top, denoms[2 * pr], denoms[2 * pr + 1])
        o_ref[:, pr * LANES:(pr + 1) * LANES] = jnp.transpose(num / den).astype(o_ref.dtype)


def _pack_in_proj(w_in):
    offs = np.cumsum((0,) + AB_SPLITS)
    part = lambda i: w_in[:, offs[i]:offs[i + 1]]
    w_main = jnp.concatenate([part(0), part(1), part(2), part(5), part(6), part(7), part(8)], axis=1).astype(BF16)
    small = jnp.concatenate([part(3), part(4), part(9)], axis=1)
    w_small = jnp.pad(small, ((0, 0), (0, LANES - small.shape[1]))).astype(BF16)
    return w_main, w_small


def _pack_qkv(w, b):
    hd = SWA_HEAD_DIM
    k0, k1 = slice(Q_W, Q_W + hd), slice(Q_W + hd, Q_W + 2 * hd)
    cols = lambda a: jnp.concatenate([a[..., :Q_W + 2 * KV_W], a[..., k1], a[..., k0]], axis=-1)
    return cols(w).astype(BF16), cols(b).reshape(1, -1).astype(F32)


def _lane_row(v, offset=0):
    return jnp.zeros((1, LANES), F32).at[0, offset:offset + v.shape[0]].set(v.astype(F32))


def kernel(x, positions, mix_norm, ffn_norm, final_norm, ab_w_in, ab_conv_w, ab_conv_b, ml_igate_b, ml_fgate_b, ml_head_norm, gla_w_lr_up, gla_gate_b, gla_head_norm, ab_w_out, swa_w_qkv, swa_b_qkv, swa_sinks, swa_w_o, swa_b_o, router_group_w, router_group_b, router_expert_w, router_expert_b, expert_w_gate, expert_w_up, expert_w_down):
    bsz, seq, d = x.shape
    t = bsz * seq
    h = x.reshape(t, d)
    row = lambda v: v.reshape(1, -1).astype(F32)

    def router_params(layer):
        wr = jnp.zeros((d, LANES), F32).at[:, :N_GROUPS].set(router_group_w[layer])
        wr = wr.at[:, N_GROUPS:N_GROUPS + N_EXPERTS].set(router_expert_w[layer])
        br = jnp.zeros((1, LANES), F32).at[0, :N_GROUPS].set(router_group_b[layer])
        br = br.at[0, N_GROUPS:N_GROUPS + N_EXPERTS].set(router_expert_b[layer])
        w_hi = wr.astype(BF16)
        w_lo = (wr - w_hi.astype(F32)).astype(BF16)
        return jnp.concatenate([w_hi, w_lo], axis=1), br

    def experts(layer):
        return expert_w_gate, expert_w_up, expert_w_down, layer

    w_main, w_small = _pack_in_proj(ab_w_in[0])
    lrup = jnp.zeros((LANES, GLA_QK_W), F32).at[SM_LR:SM_LR + GLA_LOWRANK].set(gla_w_lr_up[0]).astype(BF16)
    y = _mixer(h, row(mix_norm[0]), w_main, w_small, ab_conv_w[0], row(ab_conv_b[0]), _lane_row(ml_igate_b[0]),
               _lane_row(ml_fgate_b[0]), lrup, row(gla_gate_b[0]), row(ml_head_norm[0]), row(gla_head_norm[0]),
               bsz, seq, 256)
    wr, br = router_params(0)
    h1, u2, rt, rk, cnt = _proj_router(y, ab_w_out[0].astype(BF16), jnp.zeros((1, d), F32), h, row(ffn_norm[0]), wr, br, 512)
    o2 = _moe_experts(u2, rk, cnt, *experts(0))

    w_qkv, b_qkv = _pack_qkv(swa_w_qkv[0], swa_b_qkv[0])
    o, h = _attn(o2, h1, rt, row(mix_norm[1]), w_qkv, b_qkv, positions.reshape(1, t), _lane_row(swa_sinks[0]), 512, seq)
    wr, br = router_params(1)
    h1, u2, rt, rk, cnt = _proj_router(o, swa_w_o[0].astype(BF16), row(swa_b_o[0]), h, row(ffn_norm[1]), wr, br, 512)
    out = _combine_norm(_moe_experts(u2, rk, cnt, *experts(1)), h1, rt, row(final_norm), 512)
    return out.reshape(bsz, seq, d)
```

```python
import functools

import jax
import jax.numpy as jnp
import numpy as np
from jax import lax
from jax.experimental import pallas as pl
from jax.experimental.pallas import tpu as pltpu

F32 = jnp.float32
BF16 = jnp.bfloat16
HIGHEST = lax.Precision.HIGHEST

D_MODEL = 1024
EPS = 1e-6
ML_HEADS = 4
ML_DK = 64
ML_DV = 128
GLA_HEADS = 4
GLA_DK = 64
GLA_DV = 128
CHUNK = 64
CONV_K = 4
GLA_LOWRANK = 16
GLA_TAU = 16.0
ML_QK_W = ML_HEADS * ML_DK
ML_V_W = ML_HEADS * ML_DV
GLA_QK_W = GLA_HEADS * GLA_DK
GLA_V_W = GLA_HEADS * GLA_DV
AB_SPLITS = (2 * ML_QK_W, ML_V_W, ML_V_W, ML_HEADS, ML_HEADS, GLA_QK_W, GLA_QK_W, GLA_V_W, GLA_V_W, GLA_LOWRANK)
OFF_MQK = 0
OFF_MV = OFF_MQK + 2 * ML_QK_W
OFF_MO = OFF_MV + ML_V_W
OFF_GQ = OFF_MO + ML_V_W
OFF_GK = OFF_GQ + GLA_QK_W
OFF_GV = OFF_GK + GLA_QK_W
OFF_GG = OFF_GV + GLA_V_W
Z_MAIN_W = OFF_GG + GLA_V_W
SM_I = 0
SM_F = ML_HEADS
SM_LR = 2 * ML_HEADS
LANES = 128
SWA_Q_HEADS = 16
SWA_KV_HEADS = 2
SWA_HEAD_DIM = 64
SWA_GROUP = SWA_Q_HEADS // SWA_KV_HEADS
WINDOW = 128
ROT_DIM = SWA_HEAD_DIM // 4
ROPE_THETA = 500000.0
Q_W = SWA_Q_HEADS * SWA_HEAD_DIM
KV_W = SWA_KV_HEADS * SWA_HEAD_DIM
KV_OUT_W = 3 * KV_W
N_GROUPS = 4
EXPERTS_PER_GROUP = 8
N_EXPERTS = N_GROUPS * EXPERTS_PER_GROUP
TOP_K = 2
D_FF = 512
MOE_BM = 512
ROW_TILES = D_MODEL // LANES

VMEM_LIMIT = 56 * 1024 * 1024


def _cparams(sem):
    return pltpu.CompilerParams(dimension_semantics=sem, vmem_limit_bytes=VMEM_LIMIT)


def _rms(x, g):
    return x * lax.rsqrt(jnp.mean(x * x, axis=-1, keepdims=True) + EPS) * g


def _log_sigmoid(x):
    return jnp.minimum(x, 0.0) - jnp.log1p(jnp.exp(-jnp.abs(x)))


def _sigmoid(x):
    return 1.0 / (1.0 + jnp.exp(-x))


def _dot(a, b):
    return jnp.dot(a, b, preferred_element_type=F32)


def _dot_nt(a, b):
    return lax.dot_general(a, b, (((1,), (1,)), ((), ())), preferred_element_type=F32)


def _dot_tn(a, b):
    return lax.dot_general(a, b, (((0,), (0,)), ((), ())), preferred_element_type=F32)


IN_PROJ_CHUNK = 768
def _split_terms(x, n):
    terms = []
    for _ in range(n):
        t = x.astype(BF16)
        terms.append(t)
        x = x - t.astype(F32)
    return terms


def _dot01_left(m01, x, n):
    return sum(_dot(m01, t) for t in _split_terms(x, n))


def _dot01_right(x, m01, n):
    return sum(_dot(t, m01) for t in _split_terms(x, n))


def _mixer_kernel(xp_ref, xn_ref, g_ref, wm_ref, ws_ref, convw_ref, convb_ref, igb_ref, fgb_ref, lrup_ref, gateb_ref,
                  mlnorm_ref, glanorm_ref, y_ref, zm_s, zs_s, xpad, q_s, k_s, e_pad, hml_s, hgla_s, cn_s, m_s, st_s,
                  *, tb):
    step = pl.program_id(0) * pl.num_programs(1) + pl.program_id(1)

    def project(x, slot):
        u = _rms(x, g_ref[...]).astype(BF16)
        for n0 in range(0, Z_MAIN_W, IN_PROJ_CHUNK):
            zm_s[slot, :, n0:n0 + IN_PROJ_CHUNK] = _dot(u, wm_ref[:, n0:n0 + IN_PROJ_CHUNK]).astype(BF16)
        zs_s[slot] = _dot(u, ws_ref[...])

    @pl.when(step == 0)
    def _():
        project(xp_ref[0:tb, :], 0)

    @pl.when(pl.program_id(1) == 0)
    def _():
        xpad[0:8, :] = jnp.zeros((8, 2 * ML_QK_W), F32)
        cn_s[...] = jnp.zeros(cn_s.shape, F32)
        st_s[...] = jnp.zeros(st_s.shape, F32)
        m_s[...] = jnp.full(m_s.shape, -jnp.inf, F32)

    params = (convw_ref, convb_ref, igb_ref, fgb_ref, lrup_ref, gateb_ref, mlnorm_ref, glanorm_ref)
    scratch = (xpad, q_s, k_s, e_pad, hml_s, hgla_s, cn_s, m_s, st_s)
    project(xp_ref[tb:2 * tb, :], 1)
    _mix_block(zm_s.at[0], zs_s.at[0], *params, y_ref.at[0:tb], *scratch, tb=tb)
    project(xn_ref[...], 0)
    _mix_block(zm_s.at[1], zs_s.at[1], *params, y_ref.at[tb:2 * tb], *scratch, tb=tb)


def _mix_block(zm_ref, zs_ref, convw_ref, convb_ref, igb_ref, fgb_ref, lrup_ref, gateb_ref, mlnorm_ref,
               glanorm_ref, y_ref, xpad, q_s, k_s, e_pad, hml_s, hgla_s, cn_s, m_s, st_s, *, tb):
    nc = tb // CHUNK

    xpad[8:8 + tb, :] = zm_ref[:, OFF_MQK:OFF_MQK + 2 * ML_QK_W].astype(F32)
    conv = convb_ref[...] + convw_ref[3:4, :] * xpad[8:8 + tb, :]
    for j in range(CONV_K - 1):
        conv = conv + convw_ref[j:j + 1, :] * xpad[5 + j:5 + j + tb, :]
    xpad[0:8, :] = xpad[tb:tb + 8, :]
    qk = conv * _sigmoid(conv)
    q_s[...] = qk[:, :ML_QK_W]
    k_s[...] = qk[:, ML_QK_W:] * (ML_DK ** -0.5)

    r_t = lax.broadcasted_iota(jnp.int32, (tb, tb), 0)
    c_t = lax.broadcasted_iota(jnp.int32, (tb, tb), 1)
    tri_blk = jnp.where((r_t // CHUNK == c_t // CHUNK) & (c_t <= r_t), 1.0, 0.0).astype(BF16)
    r_e = lax.broadcasted_iota(jnp.int32, (LANES, 2 * LANES), 0)
    c_e = lax.broadcasted_iota(jnp.int32, (LANES, 2 * LANES), 1)
    spread_dk = jnp.where(c_e // ML_DK == r_e, 1.0, 0.0).astype(BF16)
    mean_dv = jnp.full((ML_DV, ML_DV), 1.0 / ML_DV, F32).astype(BF16)
    row_c = lax.broadcasted_iota(jnp.int32, (CHUNK, CHUNK), 0)
    col_c = lax.broadcasted_iota(jnp.int32, (CHUNK, CHUNK), 1)
    causal = col_c <= row_c
    lane_c = lax.broadcasted_iota(jnp.int32, (CHUNK, LANES), 1)
    ones_dv = jnp.ones((CHUNK, ML_DV), BF16)
    chunk_rows = lambda c: slice(c * CHUNK, (c + 1) * CHUNK)

    zs = zs_ref[...]
    ig = zs + igb_ref[...]
    lf = _log_sigmoid(pltpu.roll(zs, LANES - SM_F, 1) + fgb_ref[...])
    bc = _dot01_left(tri_blk, lf, 3)
    bc3 = bc.reshape(nc, CHUNK, LANES)
    g3 = bc3[:, CHUNK - 1:CHUNK, :]
    a3 = g3 - bc3 + ig.reshape(nc, CHUNK, LANES)
    amax3 = jnp.max(a3, axis=1, keepdims=True)
    wa = jnp.exp(a3 - amax3).reshape(tb, LANES)
    m_run = m_s[...]
    m_prev, s_old, s_in = [], [], []
    for c in range(nc):
        m_new = jnp.maximum(g3[c] + m_run, amax3[c])
        m_prev.append(m_run)
        s_old.append(jnp.exp(g3[c] + m_run - m_new))
        s_in.append(jnp.exp(amax3[c] - m_new))
        m_run = m_new
    m_s[...] = m_run
    e_nat = ig - bc
    e_pad[0:CHUNK, :] = jnp.full((CHUNK, LANES), -jnp.inf, F32)
    e_pad[CHUNK:CHUNK + tb, :] = e_nat
    pos = lax.broadcasted_iota(jnp.int32, (tb, LANES), 0) % CHUNK
    shift = 1
    while shift < CHUNK:
        shifted = e_pad[CHUNK - shift:CHUNK - shift + tb, :]
        e_pad[CHUNK:CHUNK + tb, :] = jnp.maximum(e_pad[CHUNK:CHUNK + tb, :], jnp.where(pos >= shift, shifted, -jnp.inf))
        shift *= 2
    m_intra = bc + e_pad[CHUNK:CHUNK + tb, :]
    il = jnp.concatenate([bc3[c] + m_prev[c] for c in range(nc)], axis=0)
    mt = jnp.maximum(il, m_intra)
    s_inter = jnp.exp(il - mt)
    exp_neg = jnp.exp(-mt)
    xn = bc - mt
    e_up = pltpu.roll(e_nat, ML_HEADS, 1)
    wa_x = _dot01_right(wa, spread_dk, 2)
    si_x = _dot01_right(s_inter, spread_dk, 2)
    q_all = q_s[...]
    k_all = k_s[...]
    kw_b = (k_all * wa_x).astype(BF16)
    qs_b = (q_all * si_x).astype(BF16)
    q_b = q_all.astype(BF16)
    k_b = k_all.astype(BF16)

    qkm, upd, vo = {}, {}, {}
    for c in range(nc):
        rows = chunk_rows(c)
        for h in range(ML_HEADS):
            dk = slice(h * ML_DK, (h + 1) * ML_DK)
            x_t = jnp.where(lane_c == h, xn[rows], jnp.where(lane_c == h + ML_HEADS, 1.0, 0.0))
            y_t = jnp.where(lane_c == h, 1.0, jnp.where(lane_c == h + ML_HEADS, e_up[rows], 0.0))
            d = lax.dot_general(x_t, y_t, (((1,), (1,)), ((), ())), precision=HIGHEST, preferred_element_type=F32)
            p = jnp.where(causal, jnp.exp(d), 0.0)
            qkm[c, h] = (_dot_nt(q_b[rows, dk], k_b[rows, dk]) * p).astype(BF16)
            vo[c, h] = jnp.concatenate([zm_ref[rows, OFF_MV + h * ML_DV:OFF_MV + (h + 1) * ML_DV], ones_dv], axis=1)
            upd[c, h] = _dot_tn(kw_b[rows, dk], vo[c, h])
    for h in range(ML_HEADS):
        dk = slice(h * ML_DK, (h + 1) * ML_DK)
        dv = slice(h * ML_DV, (h + 1) * ML_DV)
        cn = cn_s[h]
        for c in range(nc):
            rows = chunk_rows(c)
            res = _dot(qs_b[rows, dk], cn.astype(BF16)) + _dot(qkm[c, h], vo[c, h])
            num, den = res[:, :ML_DV], res[:, ML_DV:]
            hml_s[rows, dv] = num / jnp.maximum(jnp.abs(den), exp_neg[rows, h:h + 1])
            cn = s_old[c][:, h:h + 1] * cn + s_in[c][:, h:h + 1] * upd[c, h]
        cn_s[h] = cn

    la = _log_sigmoid(_dot(zs.astype(BF16), lrup_ref[...]) + gateb_ref[...]) * (1.0 / GLA_TAU)
    bcg = _dot01_left(tri_blk, la, 3)
    bcg3 = bcg.reshape(nc, CHUNK, GLA_QK_W)
    gg3 = bcg3[:, CHUNK - 1:CHUNK, :]
    gq = zm_ref[:, OFF_GQ:OFF_GQ + GLA_QK_W].astype(F32)
    gk = zm_ref[:, OFF_GK:OFF_GK + GLA_QK_W].astype(F32) * (GLA_DK ** -0.5)
    q_dec = (gq * jnp.exp(bcg)).astype(BF16)
    k_inv = (gk * jnp.exp(-bcg)).astype(BF16)
    k_end = (gk * jnp.exp(gg3 - bcg3).reshape(tb, GLA_QK_W)).astype(BF16)
    eg3 = jnp.exp(gg3)
    att, updg = {}, {}
    for c in range(nc):
        rows = chunk_rows(c)
        for h in range(GLA_HEADS):
            dk = slice(h * GLA_DK, (h + 1) * GLA_DK)
            vh = zm_ref[rows, OFF_GV + h * GLA_DV:OFF_GV + (h + 1) * GLA_DV]
            att[c, h] = jnp.where(causal, _dot_nt(q_dec[rows, dk], k_inv[rows, dk]), 0.0).astype(BF16)
            updg[c, h] = _dot_tn(vh, k_end[rows, dk])
    for h in range(GLA_HEADS):
        dk = slice(h * GLA_DK, (h + 1) * GLA_DK)
        st = st_s[h]
        for c in range(nc):
            rows = chunk_rows(c)
            vh = zm_ref[rows, OFF_GV + h * GLA_DV:OFF_GV + (h + 1) * GLA_DV]
            hgla_s[rows, h * GLA_DV:(h + 1) * GLA_DV] = _dot_nt(q_dec[rows, dk], st.astype(BF16)) + _dot(att[c, h], vh)
            st = st * eg3[c][:, dk] + updg[c, h]
        st_s[h] = st

    mean = lambda x: _dot01_right(x, mean_dv, 2)
    for h in range(ML_HEADS):
        sl = slice(h * ML_DV, (h + 1) * ML_DV)
        hh = hml_s[:, sl]
        d = hh - mean(hh)
        hn = d * lax.rsqrt(mean(d * d) + EPS)
        og = zm_ref[:, OFF_MO + h * ML_DV:OFF_MO + (h + 1) * ML_DV].astype(F32)
        y_ref[:, sl] = (hn * mlnorm_ref[:, sl] * _sigmoid(og)).astype(y_ref.dtype)
    for h in range(GLA_HEADS):
        sl = slice(h * GLA_DV, (h + 1) * GLA_DV)
        o = hgla_s[:, sl]
        on = o * lax.rsqrt(mean(o * o) + EPS)
        gg = zm_ref[:, OFF_GG + h * GLA_DV:OFF_GG + (h + 1) * GLA_DV].astype(F32)
        y_ref[:, ML_V_W + h * GLA_DV:ML_V_W + (h + 1) * GLA_DV] = (on * glanorm_ref[:, sl] * (gg * _sigmoid(gg))).astype(y_ref.dtype)


def _mixer(h, g, w_main, w_small, convw, convb, igb, fgb, lrup, gateb, mlnorm, glanorm, bsz, seq, tb):
    nt = seq // (2 * tb)
    last_block = bsz * seq // tb - 1
    const = lambda shape: pl.BlockSpec(shape, lambda b, i: (0,) * len(shape))
    return pl.pallas_call(
        functools.partial(_mixer_kernel, tb=tb),
        out_shape=jax.ShapeDtypeStruct((bsz * seq, ML_V_W + GLA_V_W), BF16),
        grid=(bsz, nt),
        in_specs=[pl.BlockSpec((2 * tb, D_MODEL), lambda b, i: (b * nt + i, 0)),
                  pl.BlockSpec((tb, D_MODEL), lambda b, i: (jnp.minimum(2 * (b * nt + i) + 2, last_block), 0)),
                  const((1, D_MODEL)), const((D_MODEL, Z_MAIN_W)), const((D_MODEL, LANES)),
                  const((CONV_K, 2 * ML_QK_W)), const((1, 2 * ML_QK_W)), const((1, LANES)), const((1, LANES)),
                  const((LANES, GLA_QK_W)), const((1, GLA_QK_W)), const((1, ML_V_W)), const((1, GLA_V_W))],
        out_specs=pl.BlockSpec((2 * tb, ML_V_W + GLA_V_W), lambda b, i: (b * nt + i, 0)),
        scratch_shapes=[pltpu.VMEM((2, tb, Z_MAIN_W), BF16), pltpu.VMEM((2, tb, LANES), F32),
                        pltpu.VMEM((tb + 8, 2 * ML_QK_W), F32),
                        pltpu.VMEM((tb, ML_QK_W), F32), pltpu.VMEM((tb, ML_QK_W), F32),
                        pltpu.VMEM((tb + CHUNK, LANES), F32),
                        pltpu.VMEM((tb, ML_V_W), F32), pltpu.VMEM((tb, GLA_V_W), F32),
                        pltpu.VMEM((ML_HEADS, ML_DK, ML_DV + LANES), F32),
                        pltpu.VMEM((1, LANES), F32), pltpu.VMEM((GLA_HEADS, GLA_DV, GLA_DK), F32)],
        compiler_params=_cparams(("arbitrary", "arbitrary")),
        name="mlstm_gla",
    )(h, h, g, w_main, w_small, convw, convb, igb, fgb, lrup, gateb, mlnorm, glanorm)


def _proj_router_kernel(y_ref, w_ref, b_ref, h_ref, g_ref, wr_ref, br_ref, h1_ref, u2_ref, rt_ref, rk_ref, cnt_ref,
                        base_s, strict_s, rtp_s):
    step = pl.program_id(0)

    @pl.when(step == 0)
    def _():
        _rank_init(base_s, strict_s)
        rtp_s[...] = jnp.zeros(rtp_s.shape, F32)

    _rank_rows(rtp_s[...], step > 0, rk_ref.at[jnp.maximum(step - 1, 0)], cnt_ref, base_s, strict_s)

    h1 = h_ref[...] + (_dot(y_ref[...], w_ref[...]) + b_ref[...])
    h1_ref[...] = h1
    u2 = _rms(h1, g_ref[...])
    for c in range(ROW_TILES):
        u2_ref[pl.ds(c, u2.shape[0], stride=ROW_TILES), :] = u2[:, c * LANES:(c + 1) * LANES]
    u_hi = u2.astype(BF16)
    u_lo = (u2 - u_hi.astype(F32)).astype(BF16)
    part = _dot(u_hi, wr_ref[...])
    logits = part[:, :LANES] + (part[:, LANES:] + _dot(u_lo, wr_ref[:, :LANES])) + br_ref[...]
    lane = lax.broadcasted_iota(jnp.int32, logits.shape, 1)
    lane_f = lane.astype(F32)
    big = float(LANES)
    gl = jnp.where(lane < N_GROUPS, logits, -jnp.inf)
    g_max = jnp.max(gl, axis=-1, keepdims=True)
    g_idx = jnp.min(jnp.where(gl == g_max, lane_f, big), axis=-1, keepdims=True)
    g_p = 1.0 / jnp.sum(jnp.exp(gl - g_max), axis=-1, keepdims=True)
    e_grp = ((lane - N_GROUPS) // EXPERTS_PER_GROUP).astype(F32)
    in_grp = (lane >= N_GROUPS) & (lane < N_GROUPS + N_EXPERTS) & (e_grp == g_idx)
    el = jnp.where(in_grp, logits, -jnp.inf)
    t1 = jnp.max(el, axis=-1, keepdims=True)
    i1 = jnp.min(jnp.where(el == t1, lane_f, big), axis=-1, keepdims=True)
    el2 = jnp.where(lane_f == i1, -jnp.inf, el)
    t2 = jnp.max(el2, axis=-1, keepdims=True)
    i2 = jnp.min(jnp.where(el2 == t2, lane_f, big), axis=-1, keepdims=True)
    e21 = jnp.exp(t2 - t1)
    p1 = 1.0 / (1.0 + e21)
    rt = jnp.where(lane == 0, i1 - N_GROUPS,
                   jnp.where(lane == 1, i2 - N_GROUPS,
                             jnp.where(lane == 2, g_p * p1, jnp.where(lane == 3, g_p * (e21 * p1), 0.0))))
    rt_ref[...] = rt
    rtp_s[...] = rt

    @pl.when(step == pl.num_programs(0) - 1)
    def _():
        _rank_rows(rt, True, rk_ref.at[step], cnt_ref, base_s, strict_s)


def _proj_router(y, w, b, h, g, wr, br, tm):
    t, kdim = y.shape
    row = lambda i: (i, 0)
    fixed = lambda i: (0, 0)
    h1, u2, rt, rk, cnt = pl.pallas_call(
        _proj_router_kernel,
        out_shape=(jax.ShapeDtypeStruct((t, D_MODEL), F32), jax.ShapeDtypeStruct((t * ROW_TILES, LANES), F32),
                   jax.ShapeDtypeStruct((t, LANES), F32), jax.ShapeDtypeStruct((t // tm, 8, tm), jnp.int32),
                   jax.ShapeDtypeStruct((1, LANES), F32)),
        grid=(t // tm,),
        in_specs=[pl.BlockSpec((tm, kdim), row), pl.BlockSpec((kdim, D_MODEL), fixed),
                  pl.BlockSpec((1, D_MODEL), fixed), pl.BlockSpec((tm, D_MODEL), row),
                  pl.BlockSpec((1, D_MODEL), fixed), pl.BlockSpec((D_MODEL, 2 * LANES), fixed),
                  pl.BlockSpec((1, LANES), fixed)],
        out_specs=(pl.BlockSpec((tm, D_MODEL), row), pl.BlockSpec((tm * ROW_TILES, LANES), row),
                   pl.BlockSpec((tm, LANES), row), pl.BlockSpec((t // tm, 8, tm), lambda i: (0, 0, 0)),
                   pl.BlockSpec((1, LANES), fixed)),
        scratch_shapes=[pltpu.VMEM((1, LANES), F32), pltpu.VMEM((tm, tm), BF16), pltpu.VMEM((tm, LANES), F32)],
        compiler_params=_cparams(("arbitrary",)),
        name="proj_router",
    )(y, w, b, h, g, wr, br)
    return h1, u2, rt, rk.transpose(1, 0, 2).reshape(8, t), cnt


def _rank_init(base_s, strict_s):
    tt = strict_s.shape[0]
    base_s[...] = jnp.zeros(base_s.shape, F32)
    r = lax.broadcasted_iota(jnp.int32, (tt, tt), 0)
    c = lax.broadcasted_iota(jnp.int32, (tt, tt), 1)
    strict_s[...] = jnp.where(c < r, 1.0, 0.0).astype(BF16)


def _rank_rows(rt, valid, rk_ref, cnt_ref, base_s, strict_s):
    lane = lax.broadcasted_iota(jnp.int32, rt.shape, 1)
    lane_f = lane.astype(F32)
    e0, e1 = rt[:, 0:1], rt[:, 1:2]
    oh0 = (lane_f == e0) & valid
    oh1 = (lane_f == e1) & valid
    oh = jnp.where(oh0 | oh1, 1.0, 0.0)
    before = _dot(strict_s[...], oh.astype(BF16)) + base_s[...]
    r0 = jnp.sum(jnp.where(oh0, before, 0.0), axis=-1, keepdims=True)
    r1 = jnp.sum(jnp.where(oh1, before, 0.0), axis=-1, keepdims=True)
    table = jnp.where(lane == 0, r0, jnp.where(lane == 1, r1, jnp.where(lane == 2, e0, jnp.where(lane == 3, e1, 0.0))))
    rk_ref[...] = jnp.transpose(table)[0:8, :].astype(jnp.int32)
    base_s[...] = base_s[...] + jnp.sum(oh, axis=0, keepdims=True)
    cnt_ref[...] = base_s[...]


DMA_GROUP = 8


def _row_copy(src, dst, sem):
    return pltpu.make_async_copy(src, dst, sem)


def _row_tile(r):
    return pl.ds(pl.multiple_of(r * ROW_TILES, ROW_TILES), ROW_TILES)


def _dispatch_kernel(dest_ref, zflag_ref, fill_ref, u_ref, xout_hbm, inv_ref, zero_s, sem, *, tt):
    i = pl.program_id(0)
    block_rows = MOE_BM * ROW_TILES

    @pl.when(i == 0)
    def _():
        fill = pltpu.make_async_copy(fill_ref, inv_ref, sem)
        fill.start()
        fill.wait()
        zero_s[...] = jnp.zeros(zero_s.shape, F32)
        zero_block = lambda blk: _row_copy(zero_s, xout_hbm.at[pl.ds(blk * block_rows, block_rows)], sem)
        for blk in range(zflag_ref.shape[0]):
            @pl.when(zflag_ref[blk] != 0)
            def _():
                zero_block(blk).start()
        for blk in range(zflag_ref.shape[0]):
            @pl.when(zflag_ref[blk] != 0)
            def _():
                zero_block(blk).wait()

    def issue(g, carry):
        for jj in range(DMA_GROUP):
            j = g * DMA_GROUP + jj
            src = u_ref.at[_row_tile(j)]
            for k in range(TOP_K):
                d = dest_ref[k, j]
                _row_copy(src, xout_hbm.at[_row_tile(d)], sem).start(priority=k)
                inv_ref[d] = (i * tt + j) * TOP_K + k
        return carry

    lax.fori_loop(0, tt // DMA_GROUP, issue, 0)
    for k in range(TOP_K):
        _row_copy(u_ref, xout_hbm.at[pl.ds(0, tt * ROW_TILES)], sem).wait()


def _dispatch(dest, zflag, u2, tt):
    t = u2.shape[0] // ROW_TILES
    n_rows = zflag.shape[0] * MOE_BM
    row = jnp.arange(n_rows, dtype=jnp.int32)
    fill = t * TOP_K + (row // MOE_BM % 2) * MOE_BM + row % MOE_BM
    return pl.pallas_call(
        functools.partial(_dispatch_kernel, tt=tt),
        out_shape=(jax.ShapeDtypeStruct((n_rows * ROW_TILES, LANES), F32), jax.ShapeDtypeStruct((n_rows,), jnp.int32)),
        grid=(t // tt,),
        in_specs=[pl.BlockSpec((TOP_K, tt), lambda i: (0, i), memory_space=pltpu.SMEM),
                  pl.BlockSpec(memory_space=pltpu.SMEM),
                  pl.BlockSpec(memory_space=pltpu.VMEM),
                  pl.BlockSpec((tt * ROW_TILES, LANES), lambda i: (i, 0))],
        out_specs=(pl.BlockSpec(memory_space=pl.ANY), pl.BlockSpec(memory_space=pltpu.SMEM)),
        scratch_shapes=[pltpu.VMEM((MOE_BM * ROW_TILES, LANES), F32), pltpu.SemaphoreType.DMA],
        compiler_params=_cparams(("arbitrary",)),
        name="moe_dispatch",
    )(dest, zflag, fill, u2)


def _ffn_kernel(be_ref, nu_ref, inv_ref, par_ref, nxt_ref, x_ref, wg_hbm, wu_hbm, wd_hbm, o2_hbm, wg_f, wu_f, wd_f,
                wg_s, wu_s, wd_s, x_s, y_s, sem, wsem, *, n_slots, layer):
    b = pl.program_id(0)

    def weight_copies(e, s):
        return (pltpu.make_async_copy(wg_hbm.at[layer, e], wg_f.at[s], wsem.at[s]),
                pltpu.make_async_copy(wu_hbm.at[layer, e], wu_f.at[s], wsem.at[s]),
                pltpu.make_async_copy(wd_hbm.at[layer, e], wd_f.at[s], wsem.at[s]))

    last = pl.num_programs(0) - 1
    n_used = nu_ref[0]
    block_rows = MOE_BM * ROW_TILES

    def drain():
        _row_copy(y_s.at[0], o2_hbm.at[pl.ds(0, block_rows)], sem).wait()

    def scatter(blk):
        s = blk % 2
        for j in range(MOE_BM):
            slot = inv_ref[blk * MOE_BM + j]
            _row_copy(y_s.at[s, _row_tile(j)], o2_hbm.at[_row_tile(slot)], sem).start(priority=j % 2)

    def compute():
        for c in range(ROW_TILES):
            x_s[:, c * LANES:(c + 1) * LANES] = x_ref[pl.ds(c, MOE_BM, stride=ROW_TILES), :].astype(BF16)
        x = x_s[...]
        a = _dot(x, wg_s[...])
        u = _dot(x, wu_s[...])
        y = _dot(((a * _sigmoid(a)) * u).astype(BF16), wd_s[...])
        for c in range(ROW_TILES):
            y_s[b % 2, pl.ds(c, MOE_BM, stride=ROW_TILES), :] = y[:, c * LANES:(c + 1) * LANES]

    @pl.when(b == 0)
    def _():
        y_s[...] = jnp.zeros(y_s.shape, F32)
        for s in range(2):
            _row_copy(y_s.at[s], o2_hbm.at[pl.ds((n_slots + s * MOE_BM) * ROW_TILES, block_rows)], sem).start()
        for s in range(2):
            drain()

    @pl.when((b >= 2) & (b - 2 < n_used))
    def _():
        drain()

    @pl.when(b == 0)
    def _():
        for c in weight_copies(be_ref[0], 0):
            c.start()

    @pl.when((b == 0) | (be_ref[b] != be_ref[jnp.maximum(b - 1, 0)]))
    def _():
        s = par_ref[b]
        for c in weight_copies(0, s):
            c.wait()
        wg_s[...] = wg_f[s].astype(BF16)
        wu_s[...] = wu_f[s].astype(BF16)
        wd_s[...] = wd_f[s].astype(BF16)

        @pl.when(nxt_ref[b] >= 0)
        def _():
            for c in weight_copies(nxt_ref[b], 1 - s):
                c.start()

    @pl.when(b == 0)
    def _():
        compute()

    @pl.when((b >= 1) & (b < n_used))
    def _():
        scatter(b - 1)
        compute()

    @pl.when((b >= 1) & (b >= n_used) & (b - 1 < n_used))
    def _():
        scatter(b - 1)

    @pl.when(b == last)
    def _():
        @pl.when((b >= 1) & (b - 1 < n_used))
        def _():
            drain()

        @pl.when(b < n_used)
        def _():
            scatter(b)
            drain()


def _ffn(block_expert, n_used, inv, xbuf, wg, wu, wd, layer, n_slots):
    n_blocks = xbuf.shape[0] // (MOE_BM * ROW_TILES)
    starts = jnp.concatenate([jnp.ones((1,), jnp.int32), (block_expert[1:] != block_expert[:-1]).astype(jnp.int32)])
    parity = (jnp.cumsum(starts) - 1) % 2
    later = jnp.where(block_expert[None, :] > block_expert[:, None], block_expert[None, :], N_EXPERTS)
    nxt = jnp.min(later, axis=1)
    nxt = jnp.where(nxt == N_EXPERTS, -1, nxt).astype(jnp.int32)
    rows = lambda b, be, nu, inv, par, nx: (jnp.minimum(b, nu[0] - 1), 0)
    hbm = pl.BlockSpec(memory_space=pl.ANY)
    return pl.pallas_call(
        functools.partial(_ffn_kernel, n_slots=n_slots, layer=layer),
        out_shape=jax.ShapeDtypeStruct(((n_slots + 2 * MOE_BM) * ROW_TILES, LANES), F32),
        grid_spec=pltpu.PrefetchScalarGridSpec(
            num_scalar_prefetch=5,
            grid=(n_blocks,),
            in_specs=[pl.BlockSpec((MOE_BM * ROW_TILES, LANES), rows), hbm, hbm, hbm],
            out_specs=pl.BlockSpec(memory_space=pl.ANY),
            scratch_shapes=[pltpu.VMEM((2, D_MODEL, D_FF), F32), pltpu.VMEM((2, D_MODEL, D_FF), F32),
                            pltpu.VMEM((2, D_FF, D_MODEL), F32),
                            pltpu.VMEM((D_MODEL, D_FF), BF16), pltpu.VMEM((D_MODEL, D_FF), BF16),
                            pltpu.VMEM((D_FF, D_MODEL), BF16), pltpu.VMEM((MOE_BM, D_MODEL), BF16),
                            pltpu.VMEM((2, MOE_BM * ROW_TILES, LANES), F32), pltpu.SemaphoreType.DMA,
                            pltpu.SemaphoreType.DMA((2,))]),
        compiler_params=_cparams(("arbitrary",)),
        name="moe_ffn",
    )(block_expert, n_used, inv, parity.astype(jnp.int32), nxt, xbuf, wg, wu, wd)


def _combine_rows(o2_ref, h_ref, rt_ref, o_ref):
    tt = h_ref.shape[0]
    rt = rt_ref[...]
    w0, w1 = rt[:, 2:3], rt[:, 3:4]
    for c in range(ROW_TILES):
        lanes = slice(c * LANES, (c + 1) * LANES)
        y0 = o2_ref[pl.ds(c, tt, stride=TOP_K * ROW_TILES), :]
        y1 = o2_ref[pl.ds(ROW_TILES + c, tt, stride=TOP_K * ROW_TILES), :]
        o_ref[:, lanes] = h_ref[:, lanes] + (y0 * w0 + y1 * w1)


def _combine_kernel(o2_ref, h_ref, rt_ref, g_ref, o_ref):
    _combine_rows(o2_ref, h_ref, rt_ref, o_ref)
    o_ref[...] = _rms(o_ref[...], g_ref[...])


def _combine_norm(o2, h1, rt, g, tt):
    t = h1.shape[0]
    return pl.pallas_call(
        _combine_kernel,
        out_shape=jax.ShapeDtypeStruct((t, D_MODEL), F32),
        grid=(t // tt,),
        in_specs=[pl.BlockSpec((tt * TOP_K * ROW_TILES, LANES), lambda i: (i, 0)),
                  pl.BlockSpec((tt, D_MODEL), lambda i: (i, 0)),
                  pl.BlockSpec((tt, LANES), lambda i: (i, 0)),
                  pl.BlockSpec((1, D_MODEL), lambda i: (0, 0))],
        out_specs=pl.BlockSpec((tt, D_MODEL), lambda i: (i, 0)),
        compiler_params=_cparams(("parallel",)),
        name="moe_combine",
    )(o2, h1, rt, g)


def _moe_experts(u2, rk, cnt, wg, wu, wd, layer):
    t = rk.shape[1]
    counts = cnt[0, :N_EXPERTS].astype(jnp.int32)
    padded = (counts + MOE_BM - 1) // MOE_BM * MOE_BM
    pad_end = jnp.cumsum(padded)
    pad_start = pad_end - padded
    n_blocks = (t * TOP_K) // MOE_BM + N_EXPERTS
    is_expert = rk[TOP_K:2 * TOP_K][None] == jnp.arange(N_EXPERTS, dtype=jnp.int32)[:, None, None]
    dest = jnp.sum(jnp.where(is_expert, pad_start[:, None, None], 0), axis=0) + rk[0:TOP_K]
    n_used = (pad_end[-1] // MOE_BM).astype(jnp.int32)
    blocks = jnp.arange(n_blocks, dtype=jnp.int32)
    blk = jnp.minimum(blocks, n_used - 1) * MOE_BM
    block_expert = jnp.minimum(jnp.sum(pad_end[None, :] <= blk[:, None], axis=1), N_EXPERTS - 1).astype(jnp.int32)
    closes_expert = jnp.any(((blocks[:, None] + 1) * MOE_BM == pad_end[None, :]) & (padded[None, :] > 0), axis=1)
    zflag = ((blocks >= n_used) | closes_expert).astype(jnp.int32)
    xbuf, inv = _dispatch(dest, zflag, u2, 1024)
    return _ffn(block_expert, n_used.reshape(1), inv, xbuf, wg, wu, wd, layer, t * TOP_K)


def _attn_kernel(o2_ref, h1_ref, rt_ref, g_ref, w_ref, b_ref, pos_ref, freq_ref, sp_ref, sink_ref, o_ref, h_ref, q_s,
                 kv_s, *, tm, tiles_per_seq):
    seq_start = pl.program_id(0) % tiles_per_seq == 0

    @pl.when(seq_start)
    def _():
        kv_s[0:WINDOW, :] = jnp.zeros((WINDOW, KV_OUT_W), kv_s.dtype)

    _combine_rows(o2_ref, h1_ref, rt_ref, h_ref)
    _project_qkv(h_ref, g_ref, w_ref, b_ref, pos_ref, freq_ref, sp_ref, q_s, kv_s.at[pl.ds(WINDOW, tm)])
    for j in range(tm // WINDOW):
        has_prev = jnp.logical_not(seq_start) if j == 0 else True
        _swa_block(q_s.at[pl.ds(j * WINDOW, WINDOW)], kv_s.at[pl.ds((j + 1) * WINDOW, WINDOW)],
                   kv_s.at[pl.ds(j * WINDOW, WINDOW)], sink_ref, o_ref.at[pl.ds(j * WINDOW, WINDOW)], has_prev)
    kv_s[0:WINDOW, :] = kv_s[tm:tm + WINDOW, :]


def _project_qkv(x_ref, g_ref, w_ref, b_ref, pos_ref, freq_ref, sp_ref, q_ref, kv_ref):
    u = _rms(x_ref[...], g_ref[...]).astype(BF16)
    ang = freq_ref[...] * pos_ref[...].astype(F32)
    spread = lambda v, m: sum(_dot_tn(t.astype(F32), m) for t in _split_terms(v, 3))
    cosv = jnp.cos(ang)
    sinv = jnp.sin(ang)
    c_coef = spread(cosv, sp_ref[0]) + sp_ref[3, 0:1, :]
    s_lo = spread(sinv, sp_ref[1])
    s_hi = spread(sinv, sp_ref[2])

    def rotate(z):
        return z * c_coef + pltpu.roll(z, LANES - ROT_DIM // 2, 1) * s_lo + pltpu.roll(z, ROT_DIM // 2, 1) * s_hi

    scale = SWA_HEAD_DIM ** -0.5
    for j in range(Q_W // LANES):
        sl = slice(j * LANES, (j + 1) * LANES)
        q_ref[:, sl] = (rotate(_dot(u, w_ref[:, sl]) + b_ref[:, sl]) * scale).astype(q_ref.dtype)
    for j in range(KV_OUT_W // LANES):
        sl = slice(Q_W + j * LANES, Q_W + (j + 1) * LANES)
        z = _dot(u, w_ref[:, sl]) + b_ref[:, sl]
        kv_ref[:, j * LANES:(j + 1) * LANES] = (rotate(z) if j % 2 == 0 else z).astype(kv_ref.dtype)


def _rot_tables():
    half = ROT_DIM // 2
    inv_freq = (ROPE_THETA ** (-jnp.arange(0, ROT_DIM, 2, dtype=F32) / ROT_DIM)).reshape(half, 1)
    d = np.arange(LANES) % SWA_HEAD_DIM
    f = np.arange(half)[:, None]
    sp = np.zeros((4, half, LANES), np.float32)
    sp[0] = (d[None, :] < ROT_DIM) & (d[None, :] % half == f)
    sp[1] = -((d[None, :] < half) & (d[None, :] == f)).astype(np.float32)
    sp[2] = (d[None, :] >= half) & (d[None, :] < ROT_DIM) & (d[None, :] - half == f)
    sp[3, 0] = d >= ROT_DIM
    return inv_freq, jnp.asarray(sp)


def _attn(o2, h1, rt, g, w, b, pos, sinks, tm, seq):
    t = h1.shape[0]
    row = lambda i: (i, 0)
    fixed = lambda i: (0, 0)
    wtot = Q_W + KV_OUT_W
    return pl.pallas_call(
        functools.partial(_attn_kernel, tm=tm, tiles_per_seq=seq // tm),
        out_shape=(jax.ShapeDtypeStruct((t, Q_W), BF16), jax.ShapeDtypeStruct((t, D_MODEL), F32)),
        grid=(t // tm,),
        in_specs=[pl.BlockSpec((tm * TOP_K * ROW_TILES, LANES), row), pl.BlockSpec((tm, D_MODEL), row),
                  pl.BlockSpec((tm, LANES), row), pl.BlockSpec((1, D_MODEL), fixed),
                  pl.BlockSpec((D_MODEL, wtot), fixed), pl.BlockSpec((1, wtot), fixed),
                  pl.BlockSpec((1, tm), lambda i: (0, i)), pl.BlockSpec((ROT_DIM // 2, 1), fixed),
                  pl.BlockSpec((4, ROT_DIM // 2, LANES), lambda i: (0, 0, 0)), pl.BlockSpec((1, LANES), fixed)],
        out_specs=(pl.BlockSpec((tm, Q_W), row), pl.BlockSpec((tm, D_MODEL), row)),
        scratch_shapes=[pltpu.VMEM((tm, Q_W), BF16), pltpu.VMEM((tm + WINDOW, KV_OUT_W), BF16)],
        compiler_params=_cparams(("arbitrary",)),
        name="combine_qkv_swa",
    )(o2, h1, rt, g, w, b, pos, *_rot_tables(), sinks)


def _swa_block(q_ref, kvc_ref, kvp_ref, sink_ref, o_ref, has_prev):
    neg = jnp.where(has_prev, 0.0, -jnp.inf).astype(F32)
    kj = lax.broadcasted_iota(jnp.int32, (WINDOW, WINDOW), 0)
    qi = lax.broadcasted_iota(jnp.int32, (WINDOW, WINDOW), 1)
    mask_cur = kj <= qi
    top = kj < SWA_HEAD_DIM
    lane = lax.broadcasted_iota(jnp.int32, (1, LANES), 1)
    keep_lo = jnp.where(lane < SWA_HEAD_DIM, 1.0, 0.0).astype(BF16)
    keep_hi = jnp.where(lane < SWA_HEAD_DIM, 0.0, 1.0).astype(BF16)
    zeros_half = jnp.zeros((SWA_HEAD_DIM, WINDOW), F32)

    def arranged(ref):
        k_nat, k_swp = ref[:, 0:KV_W], ref[:, 2 * KV_W:3 * KV_W]
        vt = jnp.transpose(ref[:, KV_W:2 * KV_W].astype(F32))
        vt_top = lambda g: jnp.concatenate([vt[g * SWA_HEAD_DIM:(g + 1) * SWA_HEAD_DIM], zeros_half], axis=0).astype(BF16)
        vt_bot = lambda g: jnp.concatenate([zeros_half, vt[g * SWA_HEAD_DIM:(g + 1) * SWA_HEAD_DIM]], axis=0).astype(BF16)
        return {(0, 0): (k_nat * keep_lo, vt_top(0)), (0, 1): (k_swp * keep_hi, vt_bot(0)),
                (1, 0): (k_swp * keep_lo, vt_top(1)), (1, 1): (k_nat * keep_hi, vt_bot(1))}

    cur = arranged(kvc_ref)
    prev = arranged(kvp_ref)
    heads = range(SWA_Q_HEADS)
    key = lambda h: (h // SWA_GROUP, h % 2)
    scores = []
    for h in heads:
        qp = q_ref[:, (h // 2) * LANES:(h // 2 + 1) * LANES]
        sc = _dot_nt(cur[key(h)][0], qp)
        sp = _dot_nt(prev[key(h)][0], qp)
        scores.append(jnp.where(mask_cur, sc, sp + neg))
    sinks = [sink_ref[:, h:h + 1] for h in heads]
    maxes = [jnp.maximum(jnp.max(scores[h], axis=0, keepdims=True), sinks[h]) for h in heads]
    probs = [jnp.exp(scores[h] - maxes[h]) for h in heads]
    denoms = [jnp.sum(probs[h], axis=0, keepdims=True) + jnp.exp(sinks[h] - maxes[h]) for h in heads]
    for pr in range(SWA_Q_HEADS // 2):
        num = None
        for h in (2 * pr, 2 * pr + 1):
            pc = jnp.where(mask_cur, probs[h], 0.0).astype(BF16)
            pp = jnp.where(mask_cur, 0.0, probs[h]).astype(BF16)
            part = _dot(cur[key(h)][1], pc) + _dot(prev[key(h)][1], pp)
            num = part if num is None else num + part
        den = jnp.where(top, denoms[2 * pr], denoms[2 * pr + 1])
        o_ref[:, pr * LANES:(pr + 1) * LANES] = jnp.transpose(num / den).astype(o_ref.dtype)


def _pack_in_proj(w_in):
    offs = np.cumsum((0,) + AB_SPLITS)
    part = lambda i: w_in[:, offs[i]:offs[i + 1]]
    w_main = jnp.concatenate([part(0), part(1), part(2), part(5), part(6), part(7), part(8)], axis=1).astype(BF16)
    small = jnp.concatenate([part(3), part(4), part(9)], axis=1)
    w_small = jnp.pad(small, ((0, 0), (0, LANES - small.shape[1]))).astype(BF16)
    return w_main, w_small


def _pack_qkv(w, b):
    hd = SWA_HEAD_DIM
    k0, k1 = slice(Q_W, Q_W + hd), slice(Q_W + hd, Q_W + 2 * hd)
    cols = lambda a: jnp.concatenate([a[..., :Q_W + 2 * KV_W], a[..., k1], a[..., k0]], axis=-1)
    return cols(w).astype(BF16), cols(b).reshape(1, -1).astype(F32)


def _lane_row(v, offset=0):
    return jnp.zeros((1, LANES), F32).at[0, offset:offset + v.shape[0]].set(v.astype(F32))


def kernel(x, positions, mix_norm, ffn_norm, final_norm, ab_w_in, ab_conv_w, ab_conv_b, ml_igate_b, ml_fgate_b, ml_head_norm, gla_w_lr_up, gla_gate_b, gla_head_norm, ab_w_out, swa_w_qkv, swa_b_qkv, swa_sinks, swa_w_o, swa_b_o, router_group_w, router_group_b, router_expert_w, router_expert_b, expert_w_gate, expert_w_up, expert_w_down):
    bsz, seq, d = x.shape
    t = bsz * seq
    h = x.reshape(t, d)
    row = lambda v: v.reshape(1, -1).astype(F32)

    def router_params(layer):
        wr = jnp.zeros((d, LANES), F32).at[:, :N_GROUPS].set(router_group_w[layer])
        wr = wr.at[:, N_GROUPS:N_GROUPS + N_EXPERTS].set(router_expert_w[layer])
        br = jnp.zeros((1, LANES), F32).at[0, :N_GROUPS].set(router_group_b[layer])
        br = br.at[0, N_GROUPS:N_GROUPS + N_EXPERTS].set(router_expert_b[layer])
        w_hi = wr.astype(BF16)
        w_lo = (wr - w_hi.astype(F32)).astype(BF16)
        return jnp.concatenate([w_hi, w_lo], axis=1), br

    def experts(layer):
        return expert_w_gate, expert_w_up, expert_w_down, layer

    w_main, w_small = _pack_in_proj(ab_w_in[0])
    lrup = jnp.zeros((LANES, GLA_QK_W), F32).at[SM_LR:SM_LR + GLA_LOWRANK].set(gla_w_lr_up[0]).astype(BF16)
    y = _mixer(h, row(mix_norm[0]), w_main, w_small, ab_conv_w[0], row(ab_conv_b[0]), _lane_row(ml_igate_b[0]),
               _lane_row(ml_fgate_b[0]), lrup, row(gla_gate_b[0]), row(ml_head_norm[0]), row(gla_head_norm[0]),
               bsz, seq, 256)
    wr, br = router_params(0)
    h1, u2, rt, rk, cnt = _proj_router(y, ab_w_out[0].astype(BF16), jnp.zeros((1, d), F32), h, row(ffn_norm[0]), wr, br, 512)
    o2 = _moe_experts(u2, rk, cnt, *experts(0))

    w_qkv, b_qkv = _pack_qkv(swa_w_qkv[0], swa_b_qkv[0])
    o, h = _attn(o2, h1, rt, row(mix_norm[1]), w_qkv, b_qkv, positions.reshape(1, t), _lane_row(swa_sinks[0]), 512, seq)
    wr, br = router_params(1)
    h1, u2, rt, rk, cnt = _proj_router(o, swa_w_o[0].astype(BF16), row(swa_b_o[0]), h, row(ffn_norm[1]), wr, br, 512)
    out = _combine_norm(_moe_experts(u2, rk, cnt, *experts(1)), h1, rt, row(final_norm), 512)
    return out.reshape(bsz, seq, d)
```

```python
import functools

import jax
import jax.numpy as jnp
import numpy as np
from jax import lax
from jax.experimental import pallas as pl
from jax.experimental.pallas import tpu as pltpu

F32 = jnp.float32
BF16 = jnp.bfloat16

D_MODEL = 1024
EPS = 1e-6
ML_HEADS = 4
ML_DK = 64
ML_DV = 128
GLA_HEADS = 4
GLA_DK = 64
GLA_DV = 128
CHUNK = 64
CONV_K = 4
GLA_LOWRANK = 16
GLA_TAU = 16.0
ML_QK_W = ML_HEADS * ML_DK
ML_V_W = ML_HEADS * ML_DV
GLA_QK_W = GLA_HEADS * GLA_DK
GLA_V_W = GLA_HEADS * GLA_DV
AB_SPLITS = (2 * ML_QK_W, ML_V_W, ML_V_W, ML_HEADS, ML_HEADS, GLA_QK_W, GLA_QK_W, GLA_V_W, GLA_V_W, GLA_LOWRANK)
OFF_MQK = 0
OFF_MV = OFF_MQK + 2 * ML_QK_W
OFF_MO = OFF_MV + ML_V_W
OFF_GQ = OFF_MO + ML_V_W
OFF_GK = OFF_GQ + GLA_QK_W
OFF_GV = OFF_GK + GLA_QK_W
OFF_GG = OFF_GV + GLA_V_W
Z_MAIN_W = OFF_GG + GLA_V_W
SM_I = 0
SM_F = ML_HEADS
SM_LR = 2 * ML_HEADS
LANES = 128
SWA_Q_HEADS = 16
SWA_KV_HEADS = 2
SWA_HEAD_DIM = 64
SWA_GROUP = SWA_Q_HEADS // SWA_KV_HEADS
WINDOW = 128
ROT_DIM = SWA_HEAD_DIM // 4
ROPE_THETA = 500000.0
Q_W = SWA_Q_HEADS * SWA_HEAD_DIM
KV_W = SWA_KV_HEADS * SWA_HEAD_DIM
KV_OUT_W = 3 * KV_W
N_GROUPS = 4
EXPERTS_PER_GROUP = 8
N_EXPERTS = N_GROUPS * EXPERTS_PER_GROUP
TOP_K = 2
D_FF = 512
MOE_BM = 512
ROW_TILES = D_MODEL // LANES

VMEM_LIMIT = 56 * 1024 * 1024


def _cparams(sem):
    return pltpu.CompilerParams(dimension_semantics=sem, vmem_limit_bytes=VMEM_LIMIT)


def _rms(x, g):
    return x * lax.rsqrt(jnp.mean(x * x, axis=-1, keepdims=True) + EPS) * g


def _log_sigmoid(x):
    return jnp.minimum(x, 0.0) - jnp.log1p(jnp.exp(-jnp.abs(x)))


def _sigmoid(x):
    return 1.0 / (1.0 + jnp.exp(-x))


def _dot(a, b):
    return jnp.dot(a, b, preferred_element_type=F32)


def _dot_nt(a, b):
    return lax.dot_general(a, b, (((1,), (1,)), ((), ())), preferred_element_type=F32)


def _dot_tn(a, b):
    return lax.dot_general(a, b, (((0,), (0,)), ((), ())), preferred_element_type=F32)


IN_PROJ_CHUNK = 768
LANE_GROUP = 8
def _split_terms(x, n):
    terms = []
    for _ in range(n):
        t = x.astype(BF16)
        terms.append(t)
        x = x - t.astype(F32)
    return terms


def _dot01_left(m01, x, n):
    return sum(_dot(m01, t) for t in _split_terms(x, n))


def _dot01_right(x, m01, n):
    return sum(_dot(t, m01) for t in _split_terms(x, n))


def _mixer_kernel(xp_ref, xn_ref, g_ref, wm_ref, ws_ref, convw_ref, convb_ref, igb_ref, fgb_ref, lrup_ref, gateb_ref,
                  mlnorm_ref, glanorm_ref, y_ref, zm_s, zs_s, xpad, q_s, k_s, e_pad, hml_s, hgla_s, cn_s, m_s, st_s,
                  *, tb):
    step = pl.program_id(0) * pl.num_programs(1) + pl.program_id(1)

    def project(x, slot):
        u = _rms(x, g_ref[...]).astype(BF16)
        for n0 in range(0, Z_MAIN_W, IN_PROJ_CHUNK):
            zm_s[slot, :, n0:n0 + IN_PROJ_CHUNK] = _dot(u, wm_ref[:, n0:n0 + IN_PROJ_CHUNK]).astype(BF16)
        zs_s[slot] = _dot(u, ws_ref[...])

    @pl.when(step == 0)
    def _():
        project(xp_ref[0:tb, :], 0)

    @pl.when(pl.program_id(1) == 0)
    def _():
        xpad[0:8, :] = jnp.zeros((8, 2 * ML_QK_W), F32)
        cn_s[...] = jnp.zeros(cn_s.shape, F32)
        st_s[...] = jnp.zeros(st_s.shape, F32)
        m_s[...] = jnp.full(m_s.shape, -jnp.inf, F32)

    params = (convw_ref, convb_ref, igb_ref, fgb_ref, lrup_ref, gateb_ref, mlnorm_ref, glanorm_ref)
    scratch = (xpad, q_s, k_s, e_pad, hml_s, hgla_s, cn_s, m_s, st_s)
    project(xp_ref[tb:2 * tb, :], 1)
    _mix_block(zm_s.at[0], zs_s.at[0], *params, y_ref.at[0:tb], *scratch, tb=tb)
    project(xn_ref[...], 0)
    _mix_block(zm_s.at[1], zs_s.at[1], *params, y_ref.at[tb:2 * tb], *scratch, tb=tb)


def _mix_block(zm_ref, zs_ref, convw_ref, convb_ref, igb_ref, fgb_ref, lrup_ref, gateb_ref, mlnorm_ref,
               glanorm_ref, y_ref, xpad, q_s, k_s, e_pad, hml_s, hgla_s, cn_s, m_s, st_s, *, tb):
    nc = tb // CHUNK

    xpad[8:8 + tb, :] = zm_ref[:, OFF_MQK:OFF_MQK + 2 * ML_QK_W].astype(F32)
    conv = convb_ref[...] + convw_ref[3:4, :] * xpad[8:8 + tb, :]
    for j in range(CONV_K - 1):
        conv = conv + convw_ref[j:j + 1, :] * xpad[5 + j:5 + j + tb, :]
    xpad[0:8, :] = xpad[tb:tb + 8, :]
    qk = conv * _sigmoid(conv)
    q_s[...] = qk[:, :ML_QK_W]
    k_s[...] = qk[:, ML_QK_W:] * (ML_DK ** -0.5)

    r_t = lax.broadcasted_iota(jnp.int32, (tb, tb), 0)
    c_t = lax.broadcasted_iota(jnp.int32, (tb, tb), 1)
    tri_blk = jnp.where((r_t // CHUNK == c_t // CHUNK) & (c_t <= r_t), 1.0, 0.0).astype(BF16)
    r_e = lax.broadcasted_iota(jnp.int32, (LANES, 2 * LANES), 0)
    c_e = lax.broadcasted_iota(jnp.int32, (LANES, 2 * LANES), 1)
    spread_dk = jnp.where(c_e // ML_DK == r_e, 1.0, 0.0).astype(BF16)
    mean_dv = jnp.full((ML_DV, ML_DV), 1.0 / ML_DV, F32).astype(BF16)
    row_c = lax.broadcasted_iota(jnp.int32, (CHUNK, CHUNK), 0)
    col_c = lax.broadcasted_iota(jnp.int32, (CHUNK, CHUNK), 1)
    causal = col_c <= row_c
    lane_c = lax.broadcasted_iota(jnp.int32, (CHUNK, LANES), 1)
    ones_dv = jnp.ones((CHUNK, ML_DV), BF16)
    chunk_rows = lambda c: slice(c * CHUNK, (c + 1) * CHUNK)

    zs = zs_ref[...]
    ig = zs + igb_ref[...]
    lf = _log_sigmoid(pltpu.roll(zs, LANES - SM_F, 1) + fgb_ref[...])
    bc = _dot01_left(tri_blk, lf, 3)
    bc3 = bc.reshape(nc, CHUNK, LANES)
    g3 = bc3[:, CHUNK - 1:CHUNK, :]
    a3 = g3 - bc3 + ig.reshape(nc, CHUNK, LANES)
    amax3 = jnp.max(a3, axis=1, keepdims=True)
    wa = jnp.exp(a3 - amax3).reshape(tb, LANES)
    m_run = m_s[...]
    m_prev, s_old, s_in = [], [], []
    for c in range(nc):
        m_new = jnp.maximum(g3[c] + m_run, amax3[c])
        m_prev.append(m_run)
        s_old.append(jnp.exp(g3[c] + m_run - m_new))
        s_in.append(jnp.exp(amax3[c] - m_new))
        m_run = m_new
    m_s[...] = m_run
    e_nat = ig - bc
    e_pad[0:CHUNK, :] = jnp.full((CHUNK, LANES), -jnp.inf, F32)
    e_pad[CHUNK:CHUNK + tb, :] = e_nat
    pos = lax.broadcasted_iota(jnp.int32, (tb, LANES), 0) % CHUNK
    shift = 1
    while shift < CHUNK:
        shifted = e_pad[CHUNK - shift:CHUNK - shift + tb, :]
        e_pad[CHUNK:CHUNK + tb, :] = jnp.maximum(e_pad[CHUNK:CHUNK + tb, :], jnp.where(pos >= shift, shifted, -jnp.inf))
        shift *= 2
    m_intra = bc + e_pad[CHUNK:CHUNK + tb, :]
    il = jnp.concatenate([bc3[c] + m_prev[c] for c in range(nc)], axis=0)
    mt = jnp.maximum(il, m_intra)
    s_inter = jnp.exp(il - mt)
    exp_neg = jnp.exp(-mt)
    xn = bc - mt
    lane_t = lax.broadcasted_iota(jnp.int32, (tb, LANES), 1)

    def lane_terms(v):
        rest, terms = jnp.where(lane_t < ML_HEADS, v, 0.0), []
        for _ in range(3):
            terms.append(rest.astype(BF16).astype(F32))
            rest = rest - terms[-1]
        return terms

    in_group = lambda lanes, g: (lanes >= g * LANE_GROUP) & (lanes < g * LANE_GROUP + ML_HEADS)
    xa, xb, xc = lane_terms(xn)
    ea, eb, ec = lane_terms(e_nat)
    x_all = (xa + pltpu.roll(xb, LANE_GROUP, 1) + pltpu.roll(xc, 2 * LANE_GROUP, 1)
             + jnp.where(in_group(lane_t, 3) | in_group(lane_t, 4) | in_group(lane_t, 5), 1.0, 0.0)).astype(BF16)
    ye_all = pltpu.roll(ea, 3 * LANE_GROUP, 1) + pltpu.roll(eb, 4 * LANE_GROUP, 1) + pltpu.roll(ec, 5 * LANE_GROUP, 1)
    head_lanes = lambda h, groups: functools.reduce(jnp.logical_or, [lane_c == g * LANE_GROUP + h for g in groups])
    wa_x = _dot01_right(wa, spread_dk, 2)
    si_x = _dot01_right(s_inter, spread_dk, 2)
    q_all = q_s[...]
    k_all = k_s[...]
    kw_b = (k_all * wa_x).astype(BF16)
    qs_b = (q_all * si_x).astype(BF16)
    q_b = q_all.astype(BF16)
    k_b = k_all.astype(BF16)

    qkm, upd, vo = {}, {}, {}
    for c in range(nc):
        rows = chunk_rows(c)
        for h in range(ML_HEADS):
            dk = slice(h * ML_DK, (h + 1) * ML_DK)
            y_t = jnp.where(head_lanes(h, (0, 1, 2)), 1.0, jnp.where(head_lanes(h, (3, 4, 5)), ye_all[rows], 0.0))
            d = _dot_nt(x_all[rows], y_t.astype(BF16))
            p = jnp.where(causal, jnp.exp(d), 0.0)
            qkm[c, h] = (_dot_nt(q_b[rows, dk], k_b[rows, dk]) * p).astype(BF16)
            vo[c, h] = jnp.concatenate([zm_ref[rows, OFF_MV + h * ML_DV:OFF_MV + (h + 1) * ML_DV], ones_dv], axis=1)
            upd[c, h] = _dot_tn(kw_b[rows, dk], vo[c, h])
    for h in range(ML_HEADS):
        dk = slice(h * ML_DK, (h + 1) * ML_DK)
        dv = slice(h * ML_DV, (h + 1) * ML_DV)
        cn = cn_s[h]
        for c in range(nc):
            rows = chunk_rows(c)
            res = _dot(qs_b[rows, dk], cn.astype(BF16)) + _dot(qkm[c, h], vo[c, h])
            num, den = res[:, :ML_DV], res[:, ML_DV:]
            hml_s[rows, dv] = num / jnp.maximum(jnp.abs(den), exp_neg[rows, h:h + 1])
            cn = s_old[c][:, h:h + 1] * cn + s_in[c][:, h:h + 1] * upd[c, h]
        cn_s[h] = cn

    la = _log_sigmoid(_dot(zs.astype(BF16), lrup_ref[...]) + gateb_ref[...]) * (1.0 / GLA_TAU)
    bcg = _dot01_left(tri_blk, la, 3)
    bcg3 = bcg.reshape(nc, CHUNK, GLA_QK_W)
    gg3 = bcg3[:, CHUNK - 1:CHUNK, :]
    gq = zm_ref[:, OFF_GQ:OFF_GQ + GLA_QK_W].astype(F32)
    gk = zm_ref[:, OFF_GK:OFF_GK + GLA_QK_W].astype(F32) * (GLA_DK ** -0.5)
    q_dec = (gq * jnp.exp(bcg)).astype(BF16)
    k_inv = (gk * jnp.exp(-bcg)).astype(BF16)
    k_end = (gk * jnp.exp(gg3 - bcg3).reshape(tb, GLA_QK_W)).astype(BF16)
    eg3 = jnp.exp(gg3)
    att, updg = {}, {}
    for c in range(nc):
        rows = chunk_rows(c)
        for h in range(GLA_HEADS):
            dk = slice(h * GLA_DK, (h + 1) * GLA_DK)
            vh = zm_ref[rows, OFF_GV + h * GLA_DV:OFF_GV + (h + 1) * GLA_DV]
            att[c, h] = jnp.where(causal, _dot_nt(q_dec[rows, dk], k_inv[rows, dk]), 0.0).astype(BF16)
            updg[c, h] = _dot_tn(vh, k_end[rows, dk])
    for h in range(GLA_HEADS):
        dk = slice(h * GLA_DK, (h + 1) * GLA_DK)
        st = st_s[h]
        for c in range(nc):
            rows = chunk_rows(c)
            vh = zm_ref[rows, OFF_GV + h * GLA_DV:OFF_GV + (h + 1) * GLA_DV]
            hgla_s[rows, h * GLA_DV:(h + 1) * GLA_DV] = _dot_nt(q_dec[rows, dk], st.astype(BF16)) + _dot(att[c, h], vh)
            st = st * eg3[c][:, dk] + updg[c, h]
        st_s[h] = st

    mean = lambda x: _dot01_right(x, mean_dv, 2)
    for h in range(ML_HEADS):
        sl = slice(h * ML_DV, (h + 1) * ML_DV)
        hh = hml_s[:, sl]
        d = hh - mean(hh)
        hn = d * lax.rsqrt(mean(d * d) + EPS)
        og = zm_ref[:, OFF_MO + h * ML_DV:OFF_MO + (h + 1) * ML_DV].astype(F32)
        y_ref[:, sl] = (hn * mlnorm_ref[:, sl] * _sigmoid(og)).astype(y_ref.dtype)
    for h in range(GLA_HEADS):
        sl = slice(h * GLA_DV, (h + 1) * GLA_DV)
        o = hgla_s[:, sl]
        on = o * lax.rsqrt(mean(o * o) + EPS)
        gg = zm_ref[:, OFF_GG + h * GLA_DV:OFF_GG + (h + 1) * GLA_DV].astype(F32)
        y_ref[:, ML_V_W + h * GLA_DV:ML_V_W + (h + 1) * GLA_DV] = (on * glanorm_ref[:, sl] * (gg * _sigmoid(gg))).astype(y_ref.dtype)


def _mixer(h, g, w_main, w_small, convw, convb, igb, fgb, lrup, gateb, mlnorm, glanorm, bsz, seq, tb):
    nt = seq // (2 * tb)
    last_block = bsz * seq // tb - 1
    const = lambda shape: pl.BlockSpec(shape, lambda b, i: (0,) * len(shape))
    return pl.pallas_call(
        functools.partial(_mixer_kernel, tb=tb),
        out_shape=jax.ShapeDtypeStruct((bsz * seq, ML_V_W + GLA_V_W), BF16),
        grid=(bsz, nt),
        in_specs=[pl.BlockSpec((2 * tb, D_MODEL), lambda b, i: (b * nt + i, 0)),
                  pl.BlockSpec((tb, D_MODEL), lambda b, i: (jnp.minimum(2 * (b * nt + i) + 2, last_block), 0)),
                  const((1, D_MODEL)), const((D_MODEL, Z_MAIN_W)), const((D_MODEL, LANES)),
                  const((CONV_K, 2 * ML_QK_W)), const((1, 2 * ML_QK_W)), const((1, LANES)), const((1, LANES)),
                  const((LANES, GLA_QK_W)), const((1, GLA_QK_W)), const((1, ML_V_W)), const((1, GLA_V_W))],
        out_specs=pl.BlockSpec((2 * tb, ML_V_W + GLA_V_W), lambda b, i: (b * nt + i, 0)),
        scratch_shapes=[pltpu.VMEM((2, tb, Z_MAIN_W), BF16), pltpu.VMEM((2, tb, LANES), F32),
                        pltpu.VMEM((tb + 8, 2 * ML_QK_W), F32),
                        pltpu.VMEM((tb, ML_QK_W), F32), pltpu.VMEM((tb, ML_QK_W), F32),
                        pltpu.VMEM((tb + CHUNK, LANES), F32),
                        pltpu.VMEM((tb, ML_V_W), F32), pltpu.VMEM((tb, GLA_V_W), F32),
                        pltpu.VMEM((ML_HEADS, ML_DK, ML_DV + LANES), F32),
                        pltpu.VMEM((1, LANES), F32), pltpu.VMEM((GLA_HEADS, GLA_DV, GLA_DK), F32)],
        compiler_params=_cparams(("arbitrary", "arbitrary")),
        name="mlstm_gla",
    )(h, h, g, w_main, w_small, convw, convb, igb, fgb, lrup, gateb, mlnorm, glanorm)


def _proj_router_kernel(y_ref, w_ref, b_ref, h_ref, g_ref, wr_ref, br_ref, h1_ref, u2_ref, rt_ref, rk_ref, cnt_ref,
                        base_s, strict_s, rtp_s):
    step = pl.program_id(0)

    @pl.when(step == 0)
    def _():
        _rank_init(base_s, strict_s)
        rtp_s[...] = jnp.zeros(rtp_s.shape, F32)

    _rank_rows(rtp_s[...], step > 0, rk_ref.at[jnp.maximum(step - 1, 0)], cnt_ref, base_s, strict_s)

    h1 = h_ref[...] + (_dot(y_ref[...], w_ref[...]) + b_ref[...])
    h1_ref[...] = h1
    u2 = _rms(h1, g_ref[...])
    for c in range(ROW_TILES):
        u2_ref[pl.ds(c, u2.shape[0], stride=ROW_TILES), :] = u2[:, c * LANES:(c + 1) * LANES]
    u_hi = u2.astype(BF16)
    u_lo = (u2 - u_hi.astype(F32)).astype(BF16)
    part = _dot(u_hi, wr_ref[...])
    logits = part[:, :LANES] + (part[:, LANES:] + _dot(u_lo, wr_ref[:, :LANES])) + br_ref[...]
    lane = lax.broadcasted_iota(jnp.int32, logits.shape, 1)
    lane_f = lane.astype(F32)
    big = float(LANES)
    gl = jnp.where(lane < N_GROUPS, logits, -jnp.inf)
    g_max = jnp.max(gl, axis=-1, keepdims=True)
    g_idx = jnp.min(jnp.where(gl == g_max, lane_f, big), axis=-1, keepdims=True)
    g_p = 1.0 / jnp.sum(jnp.exp(gl - g_max), axis=-1, keepdims=True)
    e_grp = ((lane - N_GROUPS) // EXPERTS_PER_GROUP).astype(F32)
    in_grp = (lane >= N_GROUPS) & (lane < N_GROUPS + N_EXPERTS) & (e_grp == g_idx)
    el = jnp.where(in_grp, logits, -jnp.inf)
    t1 = jnp.max(el, axis=-1, keepdims=True)
    i1 = jnp.min(jnp.where(el == t1, lane_f, big), axis=-1, keepdims=True)
    el2 = jnp.where(lane_f == i1, -jnp.inf, el)
    t2 = jnp.max(el2, axis=-1, keepdims=True)
    i2 = jnp.min(jnp.where(el2 == t2, lane_f, big), axis=-1, keepdims=True)
    e21 = jnp.exp(t2 - t1)
    p1 = 1.0 / (1.0 + e21)
    rt = jnp.where(lane == 0, i1 - N_GROUPS,
                   jnp.where(lane == 1, i2 - N_GROUPS,
                             jnp.where(lane == 2, g_p * p1, jnp.where(lane == 3, g_p * (e21 * p1), 0.0))))
    rt_ref[...] = rt
    rtp_s[...] = rt

    @pl.when(step == pl.num_programs(0) - 1)
    def _():
        _rank_rows(rt, True, rk_ref.at[step], cnt_ref, base_s, strict_s)


def _proj_router(y, w, b, h, g, wr, br, tm):
    t, kdim = y.shape
    row = lambda i: (i, 0)
    fixed = lambda i: (0, 0)
    h1, u2, rt, rk, cnt = pl.pallas_call(
        _proj_router_kernel,
        out_shape=(jax.ShapeDtypeStruct((t, D_MODEL), F32), jax.ShapeDtypeStruct((t * ROW_TILES, LANES), F32),
                   jax.ShapeDtypeStruct((t, LANES), F32), jax.ShapeDtypeStruct((t // tm, 8, tm), jnp.int32),
                   jax.ShapeDtypeStruct((1, LANES), F32)),
        grid=(t // tm,),
        in_specs=[pl.BlockSpec((tm, kdim), row), pl.BlockSpec((kdim, D_MODEL), fixed),
                  pl.BlockSpec((1, D_MODEL), fixed), pl.BlockSpec((tm, D_MODEL), row),
                  pl.BlockSpec((1, D_MODEL), fixed), pl.BlockSpec((D_MODEL, 2 * LANES), fixed),
                  pl.BlockSpec((1, LANES), fixed)],
        out_specs=(pl.BlockSpec((tm, D_MODEL), row), pl.BlockSpec((tm * ROW_TILES, LANES), row),
                   pl.BlockSpec((tm, LANES), row), pl.BlockSpec((t // tm, 8, tm), lambda i: (0, 0, 0)),
                   pl.BlockSpec((1, LANES), fixed)),
        scratch_shapes=[pltpu.VMEM((1, LANES), F32), pltpu.VMEM((tm, tm), BF16), pltpu.VMEM((tm, LANES), F32)],
        compiler_params=_cparams(("arbitrary",)),
        name="proj_router",
    )(y, w, b, h, g, wr, br)
    return h1, u2, rt, rk.transpose(1, 0, 2).reshape(8, t), cnt


def _rank_init(base_s, strict_s):
    tt = strict_s.shape[0]
    base_s[...] = jnp.zeros(base_s.shape, F32)
    r = lax.broadcasted_iota(jnp.int32, (tt, tt), 0)
    c = lax.broadcasted_iota(jnp.int32, (tt, tt), 1)
    strict_s[...] = jnp.where(c < r, 1.0, 0.0).astype(BF16)


def _rank_rows(rt, valid, rk_ref, cnt_ref, base_s, strict_s):
    lane = lax.broadcasted_iota(jnp.int32, rt.shape, 1)
    lane_f = lane.astype(F32)
    e0, e1 = rt[:, 0:1], rt[:, 1:2]
    oh0 = (lane_f == e0) & valid
    oh1 = (lane_f == e1) & valid
    oh = jnp.where(oh0 | oh1, 1.0, 0.0)
    before = _dot(strict_s[...], oh.astype(BF16)) + base_s[...]
    r0 = jnp.sum(jnp.where(oh0, before, 0.0), axis=-1, keepdims=True)
    r1 = jnp.sum(jnp.where(oh1, before, 0.0), axis=-1, keepdims=True)
    table = jnp.where(lane == 0, r0, jnp.where(lane == 1, r1, jnp.where(lane == 2, e0, jnp.where(lane == 3, e1, 0.0))))
    rk_ref[...] = jnp.transpose(table)[0:8, :].astype(jnp.int32)
    base_s[...] = base_s[...] + jnp.sum(oh, axis=0, keepdims=True)
    cnt_ref[...] = base_s[...]


DMA_GROUP = 8


def _row_copy(src, dst, sem):
    return pltpu.make_async_copy(src, dst, sem)


def _row_tile(r):
    return pl.ds(pl.multiple_of(r * ROW_TILES, ROW_TILES), ROW_TILES)


def _dispatch_kernel(dest_ref, zflag_ref, fill_ref, u_ref, xout_hbm, inv_ref, zero_s, sem, *, tt):
    i = pl.program_id(0)
    block_rows = MOE_BM * ROW_TILES

    @pl.when(i == 0)
    def _():
        fill = pltpu.make_async_copy(fill_ref, inv_ref, sem)
        fill.start()
        fill.wait()
        zero_s[...] = jnp.zeros(zero_s.shape, F32)
        zero_block = lambda blk: _row_copy(zero_s, xout_hbm.at[pl.ds(blk * block_rows, block_rows)], sem)
        for blk in range(zflag_ref.shape[0]):
            @pl.when(zflag_ref[blk] != 0)
            def _():
                zero_block(blk).start()
        for blk in range(zflag_ref.shape[0]):
            @pl.when(zflag_ref[blk] != 0)
            def _():
                zero_block(blk).wait()

    def issue(g, carry):
        for jj in range(DMA_GROUP):
            j = g * DMA_GROUP + jj
            src = u_ref.at[_row_tile(j)]
            for k in range(TOP_K):
                d = dest_ref[k, j]
                _row_copy(src, xout_hbm.at[_row_tile(d)], sem).start(priority=k)
                inv_ref[d] = (i * tt + j) * TOP_K + k
        return carry

    lax.fori_loop(0, tt // DMA_GROUP, issue, 0)
    for k in range(TOP_K):
        _row_copy(u_ref, xout_hbm.at[pl.ds(0, tt * ROW_TILES)], sem).wait()


def _dispatch(dest, zflag, u2, tt):
    t = u2.shape[0] // ROW_TILES
    n_rows = zflag.shape[0] * MOE_BM
    row = jnp.arange(n_rows, dtype=jnp.int32)
    fill = t * TOP_K + (row // MOE_BM % 2) * MOE_BM + row % MOE_BM
    return pl.pallas_call(
        functools.partial(_dispatch_kernel, tt=tt),
        out_shape=(jax.ShapeDtypeStruct((n_rows * ROW_TILES, LANES), F32), jax.ShapeDtypeStruct((n_rows,), jnp.int32)),
        grid=(t // tt,),
        in_specs=[pl.BlockSpec((TOP_K, tt), lambda i: (0, i), memory_space=pltpu.SMEM),
                  pl.BlockSpec(memory_space=pltpu.SMEM),
                  pl.BlockSpec(memory_space=pltpu.VMEM),
                  pl.BlockSpec((tt * ROW_TILES, LANES), lambda i: (i, 0))],
        out_specs=(pl.BlockSpec(memory_space=pl.ANY), pl.BlockSpec(memory_space=pltpu.SMEM)),
        scratch_shapes=[pltpu.VMEM((MOE_BM * ROW_TILES, LANES), F32), pltpu.SemaphoreType.DMA],
        compiler_params=_cparams(("arbitrary",)),
        name="moe_dispatch",
    )(dest, zflag, fill, u2)


def _ffn_kernel(be_ref, nu_ref, inv_ref, par_ref, nxt_ref, x_ref, wg_hbm, wu_hbm, wd_hbm, o2_hbm, wg_f, wu_f, wd_f,
                wg_s, wu_s, wd_s, x_s, y_s, sem, wsem, *, n_slots, layer):
    b = pl.program_id(0)

    def weight_copies(e, s):
        return (pltpu.make_async_copy(wg_hbm.at[layer, e], wg_f.at[s], wsem.at[s]),
                pltpu.make_async_copy(wu_hbm.at[layer, e], wu_f.at[s], wsem.at[s]),
                pltpu.make_async_copy(wd_hbm.at[layer, e], wd_f.at[s], wsem.at[s]))

    last = pl.num_programs(0) - 1
    n_used = nu_ref[0]
    block_rows = MOE_BM * ROW_TILES

    def drain():
        _row_copy(y_s.at[0], o2_hbm.at[pl.ds(0, block_rows)], sem).wait()

    def scatter(blk):
        s = blk % 2
        for j in range(MOE_BM):
            slot = inv_ref[blk * MOE_BM + j]
            _row_copy(y_s.at[s, _row_tile(j)], o2_hbm.at[_row_tile(slot)], sem).start(priority=j % 2)

    def compute():
        for c in range(ROW_TILES):
            x_s[:, c * LANES:(c + 1) * LANES] = x_ref[pl.ds(c, MOE_BM, stride=ROW_TILES), :].astype(BF16)
        x = x_s[...]
        a = _dot(x, wg_s[...])
        u = _dot(x, wu_s[...])
        y = _dot(((a * _sigmoid(a)) * u).astype(BF16), wd_s[...])
        for c in range(ROW_TILES):
            y_s[b % 2, pl.ds(c, MOE_BM, stride=ROW_TILES), :] = y[:, c * LANES:(c + 1) * LANES]

    @pl.when(b == 0)
    def _():
        y_s[...] = jnp.zeros(y_s.shape, F32)
        for s in range(2):
            _row_copy(y_s.at[s], o2_hbm.at[pl.ds((n_slots + s * MOE_BM) * ROW_TILES, block_rows)], sem).start()
        for s in range(2):
            drain()

    @pl.when((b >= 2) & (b - 2 < n_used))
    def _():
        drain()

    @pl.when(b == 0)
    def _():
        for c in weight_copies(be_ref[0], 0):
            c.start()

    @pl.when((b == 0) | (be_ref[b] != be_ref[jnp.maximum(b - 1, 0)]))
    def _():
        s = par_ref[b]
        for c in weight_copies(0, s):
            c.wait()
        wg_s[...] = wg_f[s].astype(BF16)
        wu_s[...] = wu_f[s].astype(BF16)
        wd_s[...] = wd_f[s].astype(BF16)

        @pl.when(nxt_ref[b] >= 0)
        def _():
            for c in weight_copies(nxt_ref[b], 1 - s):
                c.start()

    @pl.when(b == 0)
    def _():
        compute()

    @pl.when((b >= 1) & (b < n_used))
    def _():
        scatter(b - 1)
        compute()

    @pl.when((b >= 1) & (b >= n_used) & (b - 1 < n_used))
    def _():
        scatter(b - 1)

    @pl.when(b == last)
    def _():
        @pl.when((b >= 1) & (b - 1 < n_used))
        def _():
            drain()

        @pl.when(b < n_used)
        def _():
            scatter(b)
            drain()


def _ffn(block_expert, n_used, inv, xbuf, wg, wu, wd, layer, n_slots):
    n_blocks = xbuf.shape[0] // (MOE_BM * ROW_TILES)
    starts = jnp.concatenate([jnp.ones((1,), jnp.int32), (block_expert[1:] != block_expert[:-1]).astype(jnp.int32)])
    parity = (jnp.cumsum(starts) - 1) % 2
    later = jnp.where(block_expert[None, :] > block_expert[:, None], block_expert[None, :], N_EXPERTS)
    nxt = jnp.min(later, axis=1)
    nxt = jnp.where(nxt == N_EXPERTS, -1, nxt).astype(jnp.int32)
    rows = lambda b, be, nu, inv, par, nx: (jnp.minimum(b, nu[0] - 1), 0)
    hbm = pl.BlockSpec(memory_space=pl.ANY)
    return pl.pallas_call(
        functools.partial(_ffn_kernel, n_slots=n_slots, layer=layer),
        out_shape=jax.ShapeDtypeStruct(((n_slots + 2 * MOE_BM) * ROW_TILES, LANES), F32),
        grid_spec=pltpu.PrefetchScalarGridSpec(
            num_scalar_prefetch=5,
            grid=(n_blocks,),
            in_specs=[pl.BlockSpec((MOE_BM * ROW_TILES, LANES), rows), hbm, hbm, hbm],
            out_specs=pl.BlockSpec(memory_space=pl.ANY),
            scratch_shapes=[pltpu.VMEM((2, D_MODEL, D_FF), F32), pltpu.VMEM((2, D_MODEL, D_FF), F32),
                            pltpu.VMEM((2, D_FF, D_MODEL), F32),
                            pltpu.VMEM((D_MODEL, D_FF), BF16), pltpu.VMEM((D_MODEL, D_FF), BF16),
                            pltpu.VMEM((D_FF, D_MODEL), BF16), pltpu.VMEM((MOE_BM, D_MODEL), BF16),
                            pltpu.VMEM((2, MOE_BM * ROW_TILES, LANES), F32), pltpu.SemaphoreType.DMA,
                            pltpu.SemaphoreType.DMA((2,))]),
        compiler_params=_cparams(("arbitrary",)),
        name="moe_ffn",
    )(block_expert, n_used, inv, parity.astype(jnp.int32), nxt, xbuf, wg, wu, wd)


def _combine_rows(o2_ref, h_ref, rt_ref, o_ref):
    tt = h_ref.shape[0]
    rt = rt_ref[...]
    w0, w1 = rt[:, 2:3], rt[:, 3:4]
    for c in range(ROW_TILES):
        lanes = slice(c * LANES, (c + 1) * LANES)
        y0 = o2_ref[pl.ds(c, tt, stride=TOP_K * ROW_TILES), :]
        y1 = o2_ref[pl.ds(ROW_TILES + c, tt, stride=TOP_K * ROW_TILES), :]
        o_ref[:, lanes] = h_ref[:, lanes] + (y0 * w0 + y1 * w1)


def _combine_kernel(o2_ref, h_ref, rt_ref, g_ref, o_ref):
    _combine_rows(o2_ref, h_ref, rt_ref, o_ref)
    o_ref[...] = _rms(o_ref[...], g_ref[...])


def _combine_norm(o2, h1, rt, g, tt):
    t = h1.shape[0]
    return pl.pallas_call(
        _combine_kernel,
        out_shape=jax.ShapeDtypeStruct((t, D_MODEL), F32),
        grid=(t // tt,),
        in_specs=[pl.BlockSpec((tt * TOP_K * ROW_TILES, LANES), lambda i: (i, 0)),
                  pl.BlockSpec((tt, D_MODEL), lambda i: (i, 0)),
                  pl.BlockSpec((tt, LANES), lambda i: (i, 0)),
                  pl.BlockSpec((1, D_MODEL), lambda i: (0, 0))],
        out_specs=pl.BlockSpec((tt, D_MODEL), lambda i: (i, 0)),
        compiler_params=_cparams(("parallel",)),
        name="moe_combine",
    )(o2, h1, rt, g)


def _moe_experts(u2, rk, cnt, wg, wu, wd, layer):
    t = rk.shape[1]
    counts = cnt[0, :N_EXPERTS].astype(jnp.int32)
    padded = (counts + MOE_BM - 1) // MOE_BM * MOE_BM
    pad_end = jnp.cumsum(padded)
    pad_start = pad_end - padded
    n_blocks = (t * TOP_K) // MOE_BM + N_EXPERTS
    is_expert = rk[TOP_K:2 * TOP_K][None] == jnp.arange(N_EXPERTS, dtype=jnp.int32)[:, None, None]
    dest = jnp.sum(jnp.where(is_expert, pad_start[:, None, None], 0), axis=0) + rk[0:TOP_K]
    n_used = (pad_end[-1] // MOE_BM).astype(jnp.int32)
    blocks = jnp.arange(n_blocks, dtype=jnp.int32)
    blk = jnp.minimum(blocks, n_used - 1) * MOE_BM
    block_expert = jnp.minimum(jnp.sum(pad_end[None, :] <= blk[:, None], axis=1), N_EXPERTS - 1).astype(jnp.int32)
    closes_expert = jnp.any(((blocks[:, None] + 1) * MOE_BM == pad_end[None, :]) & (padded[None, :] > 0), axis=1)
    zflag = ((blocks >= n_used) | closes_expert).astype(jnp.int32)
    xbuf, inv = _dispatch(dest, zflag, u2, 2048)
    return _ffn(block_expert, n_used.reshape(1), inv, xbuf, wg, wu, wd, layer, t * TOP_K)


def _attn_kernel(o2_ref, h1_ref, rt_ref, g_ref, w_ref, b_ref, pos_ref, freq_ref, sp_ref, sink_ref, o_ref, h_ref, q_s,
                 kv_s, *, tm, tiles_per_seq):
    seq_start = pl.program_id(0) % tiles_per_seq == 0

    @pl.when(seq_start)
    def _():
        kv_s[0:WINDOW, :] = jnp.zeros((WINDOW, KV_OUT_W), kv_s.dtype)

    _combine_rows(o2_ref, h1_ref, rt_ref, h_ref)
    _project_qkv(h_ref, g_ref, w_ref, b_ref, pos_ref, freq_ref, sp_ref, q_s, kv_s.at[pl.ds(WINDOW, tm)])
    for j in range(tm // WINDOW):
        has_prev = jnp.logical_not(seq_start) if j == 0 else True
        _swa_block(q_s.at[pl.ds(j * WINDOW, WINDOW)], kv_s.at[pl.ds((j + 1) * WINDOW, WINDOW)],
                   kv_s.at[pl.ds(j * WINDOW, WINDOW)], sink_ref, o_ref.at[pl.ds(j * WINDOW, WINDOW)], has_prev)
    kv_s[0:WINDOW, :] = kv_s[tm:tm + WINDOW, :]


def _project_qkv(x_ref, g_ref, w_ref, b_ref, pos_ref, freq_ref, sp_ref, q_ref, kv_ref):
    u = _rms(x_ref[...], g_ref[...]).astype(BF16)
    ang = freq_ref[...] * pos_ref[...].astype(F32)
    spread = lambda v, m: sum(_dot_tn(t.astype(F32), m) for t in _split_terms(v, 3))
    cosv = jnp.cos(ang)
    sinv = jnp.sin(ang)
    c_coef = spread(cosv, sp_ref[0]) + sp_ref[3, 0:1, :]
    s_lo = spread(sinv, sp_ref[1])
    s_hi = spread(sinv, sp_ref[2])

    def rotate(z):
        return z * c_coef + pltpu.roll(z, LANES - ROT_DIM // 2, 1) * s_lo + pltpu.roll(z, ROT_DIM // 2, 1) * s_hi

    scale = SWA_HEAD_DIM ** -0.5
    for j in range(Q_W // LANES):
        sl = slice(j * LANES, (j + 1) * LANES)
        q_ref[:, sl] = (rotate(_dot(u, w_ref[:, sl]) + b_ref[:, sl]) * scale).astype(q_ref.dtype)
    for j in range(KV_OUT_W // LANES):
        sl = slice(Q_W + j * LANES, Q_W + (j + 1) * LANES)
        z = _dot(u, w_ref[:, sl]) + b_ref[:, sl]
        kv_ref[:, j * LANES:(j + 1) * LANES] = (rotate(z) if j % 2 == 0 else z).astype(kv_ref.dtype)


def _rot_tables():
    half = ROT_DIM // 2
    inv_freq = (ROPE_THETA ** (-jnp.arange(0, ROT_DIM, 2, dtype=F32) / ROT_DIM)).reshape(half, 1)
    d = np.arange(LANES) % SWA_HEAD_DIM
    f = np.arange(half)[:, None]
    sp = np.zeros((4, half, LANES), np.float32)
    sp[0] = (d[None, :] < ROT_DIM) & (d[None, :] % half == f)
    sp[1] = -((d[None, :] < half) & (d[None, :] == f)).astype(np.float32)
    sp[2] = (d[None, :] >= half) & (d[None, :] < ROT_DIM) & (d[None, :] - half == f)
    sp[3, 0] = d >= ROT_DIM
    return inv_freq, jnp.asarray(sp)


def _attn(o2, h1, rt, g, w, b, pos, sinks, tm, seq):
    t = h1.shape[0]
    row = lambda i: (i, 0)
    fixed = lambda i: (0, 0)
    wtot = Q_W + KV_OUT_W
    return pl.pallas_call(
        functools.partial(_attn_kernel, tm=tm, tiles_per_seq=seq // tm),
        out_shape=(jax.ShapeDtypeStruct((t, Q_W), BF16), jax.ShapeDtypeStruct((t, D_MODEL), F32)),
        grid=(t // tm,),
        in_specs=[pl.BlockSpec((tm * TOP_K * ROW_TILES, LANES), row), pl.BlockSpec((tm, D_MODEL), row),
                  pl.BlockSpec((tm, LANES), row), pl.BlockSpec((1, D_MODEL), fixed),
                  pl.BlockSpec((D_MODEL, wtot), fixed), pl.BlockSpec((1, wtot), fixed),
                  pl.BlockSpec((1, tm), lambda i: (0, i)), pl.BlockSpec((ROT_DIM // 2, 1), fixed),
                  pl.BlockSpec((4, ROT_DIM // 2, LANES), lambda i: (0, 0, 0)), pl.BlockSpec((1, LANES), fixed)],
        out_specs=(pl.BlockSpec((tm, Q_W), row), pl.BlockSpec((tm, D_MODEL), row)),
        scratch_shapes=[pltpu.VMEM((tm, Q_W), BF16), pltpu.VMEM((tm + WINDOW, KV_OUT_W), BF16)],
        compiler_params=_cparams(("arbitrary",)),
        name="combine_qkv_swa",
    )(o2, h1, rt, g, w, b, pos, *_rot_tables(), sinks)


def _swa_block(q_ref, kvc_ref, kvp_ref, sink_ref, o_ref, has_prev):
    neg = jnp.where(has_prev, 0.0, -jnp.inf).astype(F32)
    kj = lax.broadcasted_iota(jnp.int32, (WINDOW, WINDOW), 0)
    qi = lax.broadcasted_iota(jnp.int32, (WINDOW, WINDOW), 1)
    mask_cur = kj <= qi
    top = kj < SWA_HEAD_DIM
    lane = lax.broadcasted_iota(jnp.int32, (1, LANES), 1)
    keep_lo = jnp.where(lane < SWA_HEAD_DIM, 1.0, 0.0).astype(BF16)
    keep_hi = jnp.where(lane < SWA_HEAD_DIM, 0.0, 1.0).astype(BF16)
    zeros_half = jnp.zeros((SWA_HEAD_DIM, WINDOW), F32)

    def arranged(ref):
        k_nat, k_swp = ref[:, 0:KV_W], ref[:, 2 * KV_W:3 * KV_W]
        vt = jnp.transpose(ref[:, KV_W:2 * KV_W].astype(F32))
        vt_top = lambda g: jnp.concatenate([vt[g * SWA_HEAD_DIM:(g + 1) * SWA_HEAD_DIM], zeros_half], axis=0).astype(BF16)
        vt_bot = lambda g: jnp.concatenate([zeros_half, vt[g * SWA_HEAD_DIM:(g + 1) * SWA_HEAD_DIM]], axis=0).astype(BF16)
        return {(0, 0): (k_nat * keep_lo, vt_top(0)), (0, 1): (k_swp * keep_hi, vt_bot(0)),
                (1, 0): (k_swp * keep_lo, vt_top(1)), (1, 1): (k_nat * keep_hi, vt_bot(1))}

    cur = arranged(kvc_ref)
    prev = arranged(kvp_ref)
    heads = range(SWA_Q_HEADS)
    key = lambda h: (h // SWA_GROUP, h % 2)
    scores = []
    for h in heads:
        qp = q_ref[:, (h // 2) * LANES:(h // 2 + 1) * LANES]
        sc = _dot_nt(cur[key(h)][0], qp)
        sp = _dot_nt(prev[key(h)][0], qp)
        scores.append(jnp.where(mask_cur, sc, sp + neg))
    sinks = [sink_ref[:, h:h + 1] for h in heads]
    maxes = [jnp.maximum(jnp.max(scores[h], axis=0, keepdims=True), sinks[h]) for h in heads]
    probs = [jnp.exp(scores[h] - maxes[h]) for h in heads]
    denoms = [jnp.sum(probs[h], axis=0, keepdims=True) + jnp.exp(sinks[h] - maxes[h]) for h in heads]
    for pr in range(SWA_Q_HEADS // 2):
        num = None
        for h in (2 * pr, 2 * pr + 1):
            pc = jnp.where(mask_cur, probs[h], 0.0).astype(BF16)
            pp = jnp.where(mask_cur, 0.0, probs[h]).astype(BF16)
            part = _dot(cur[key(h)][1], pc) + _dot(prev[key(h)][1], pp)
            num = part if num is None else num + part
        den = jnp.where(top, denoms[2 * pr], denoms[2 * pr + 1])
        o_ref[:, pr * LANES:(pr + 1) * LANES] = jnp.transpose(num / den).astype(o_ref.dtype)


def _pack_in_proj(w_in):
    offs = np.cumsum((0,) + AB_SPLITS)
    part = lambda i: w_in[:, offs[i]:offs[i + 1]]
    w_main = jnp.concatenate([part(0), part(1), part(2), part(5), part(6), part(7), part(8)], axis=1).astype(BF16)
    small = jnp.concatenate([part(3), part(4), part(9)], axis=1)
    w_small = jnp.pad(small, ((0, 0), (0, LANES - small.shape[1]))).astype(BF16)
    return w_main, w_small


def _pack_qkv(w, b):
    hd = SWA_HEAD_DIM
    k0, k1 = slice(Q_W, Q_W + hd), slice(Q_W + hd, Q_W + 2 * hd)
    cols = lambda a: jnp.concatenate([a[..., :Q_W + 2 * KV_W], a[..., k1], a[..., k0]], axis=-1)
    return cols(w).astype(BF16), cols(b).reshape(1, -1).astype(F32)


def _lane_row(v, offset=0):
    return jnp.zeros((1, LANES), F32).at[0, offset:offset + v.shape[0]].set(v.astype(F32))


def kernel(x, positions, mix_norm, ffn_norm, final_norm, ab_w_in, ab_conv_w, ab_conv_b, ml_igate_b, ml_fgate_b, ml_head_norm, gla_w_lr_up, gla_gate_b, gla_head_norm, ab_w_out, swa_w_qkv, swa_b_qkv, swa_sinks, swa_w_o, swa_b_o, router_group_w, router_group_b, router_expert_w, router_expert_b, expert_w_gate, expert_w_up, expert_w_down):
    bsz, seq, d = x.shape
    t = bsz * seq
    h = x.reshape(t, d)
    row = lambda v: v.reshape(1, -1).astype(F32)

    def router_params(layer):
        wr = jnp.zeros((d, LANES), F32).at[:, :N_GROUPS].set(router_group_w[layer])
        wr = wr.at[:, N_GROUPS:N_GROUPS + N_EXPERTS].set(router_expert_w[layer])
        br = jnp.zeros((1, LANES), F32).at[0, :N_GROUPS].set(router_group_b[layer])
        br = br.at[0, N_GROUPS:N_GROUPS + N_EXPERTS].set(router_expert_b[layer])
        w_hi = wr.astype(BF16)
        w_lo = (wr - w_hi.astype(F32)).astype(BF16)
        return jnp.concatenate([w_hi, w_lo], axis=1), br

    def experts(layer):
        return expert_w_gate, expert_w_up, expert_w_down, layer

    w_main, w_small = _pack_in_proj(ab_w_in[0])
    lrup = jnp.zeros((LANES, GLA_QK_W), F32).at[SM_LR:SM_LR + GLA_LOWRANK].set(gla_w_lr_up[0]).astype(BF16)
    y = _mixer(h, row(mix_norm[0]), w_main, w_small, ab_conv_w[0], row(ab_conv_b[0]), _lane_row(ml_igate_b[0]),
               _lane_row(ml_fgate_b[0]), lrup, row(gla_gate_b[0]), row(ml_head_norm[0]), row(gla_head_norm[0]),
               bsz, seq, 256)
    wr, br = router_params(0)
    h1, u2, rt, rk, cnt = _proj_router(y, ab_w_out[0].astype(BF16), jnp.zeros((1, d), F32), h, row(ffn_norm[0]), wr, br, 512)
    o2 = _moe_experts(u2, rk, cnt, *experts(0))

    w_qkv, b_qkv = _pack_qkv(swa_w_qkv[0], swa_b_qkv[0])
    o, h = _attn(o2, h1, rt, row(mix_norm[1]), w_qkv, b_qkv, positions.reshape(1, t), _lane_row(swa_sinks[0]), 512, seq)
    wr, br = router_params(1)
    h1, u2, rt, rk, cnt = _proj_router(o, swa_w_o[0].astype(BF16), row(swa_b_o[0]), h, row(ffn_norm[1]), wr, br, 512)
    out = _combine_norm(_moe_experts(u2, rk, cnt, *experts(1)), h1, rt, row(final_norm), 512)
    return out.reshape(bsz, seq, d)
```

```python
import functools

import jax
import jax.numpy as jnp
import numpy as np
from jax import lax
from jax.experimental import pallas as pl
from jax.experimental.pallas import tpu as pltpu

F32 = jnp.float32
BF16 = jnp.bfloat16

D_MODEL = 1024
EPS = 1e-6
ML_HEADS = 4
ML_DK = 64
ML_DV = 128
GLA_HEADS = 4
GLA_DK = 64
GLA_DV = 128
CHUNK = 64
CONV_K = 4
GLA_LOWRANK = 16
GLA_TAU = 16.0
ML_QK_W = ML_HEADS * ML_DK
ML_V_W = ML_HEADS * ML_DV
GLA_QK_W = GLA_HEADS * GLA_DK
GLA_V_W = GLA_HEADS * GLA_DV
AB_SPLITS = (2 * ML_QK_W, ML_V_W, ML_V_W, ML_HEADS, ML_HEADS, GLA_QK_W, GLA_QK_W, GLA_V_W, GLA_V_W, GLA_LOWRANK)
OFF_MQK = 0
OFF_MV = OFF_MQK + 2 * ML_QK_W
OFF_MO = OFF_MV + ML_V_W
OFF_GQ = OFF_MO + ML_V_W
OFF_GK = OFF_GQ + GLA_QK_W
OFF_GV = OFF_GK + GLA_QK_W
OFF_GG = OFF_GV + GLA_V_W
Z_MAIN_W = OFF_GG + GLA_V_W
SM_I = 0
SM_F = ML_HEADS
SM_LR = 2 * ML_HEADS
LANES = 128
SWA_Q_HEADS = 16
SWA_KV_HEADS = 2
SWA_HEAD_DIM = 64
SWA_GROUP = SWA_Q_HEADS // SWA_KV_HEADS
WINDOW = 128
ROT_DIM = SWA_HEAD_DIM // 4
ROPE_THETA = 500000.0
Q_W = SWA_Q_HEADS * SWA_HEAD_DIM
KV_W = SWA_KV_HEADS * SWA_HEAD_DIM
KV_OUT_W = 3 * KV_W
N_GROUPS = 4
EXPERTS_PER_GROUP = 8
N_EXPERTS = N_GROUPS * EXPERTS_PER_GROUP
TOP_K = 2
D_FF = 512
MOE_BM = 512
ROW_TILES = D_MODEL // LANES

VMEM_LIMIT = 56 * 1024 * 1024


def _cparams(sem):
    return pltpu.CompilerParams(dimension_semantics=sem, vmem_limit_bytes=VMEM_LIMIT)


def _rms(x, g):
    return x * lax.rsqrt(jnp.mean(x * x, axis=-1, keepdims=True) + EPS) * g


def _log_sigmoid(x):
    return jnp.minimum(x, 0.0) - jnp.log1p(jnp.exp(-jnp.abs(x)))


def _sigmoid(x):
    return 1.0 / (1.0 + jnp.exp(-x))


def _dot(a, b):
    return jnp.dot(a, b, preferred_element_type=F32)


def _dot_nt(a, b):
    return lax.dot_general(a, b, (((1,), (1,)), ((), ())), preferred_element_type=F32)


def _dot_tn(a, b):
    return lax.dot_general(a, b, (((0,), (0,)), ((), ())), preferred_element_type=F32)


IN_PROJ_CHUNK = 768
LANE_GROUP = 8
def _split_terms(x, n):
    terms = []
    for _ in range(n):
        t = x.astype(BF16)
        terms.append(t)
        x = x - t.astype(F32)
    return terms


def _dot01_left(m01, x, n):
    return sum(_dot(m01, t) for t in _split_terms(x, n))


def _dot01_right(x, m01, n):
    return sum(_dot(t, m01) for t in _split_terms(x, n))


def _mixer_kernel(xp_ref, xn_ref, g_ref, wm_ref, ws_ref, convw_ref, convb_ref, igb_ref, fgb_ref, lrup_ref, gateb_ref,
                  mlnorm_ref, glanorm_ref, y_ref, zm_s, zs_s, xpad, q_s, k_s, e_pad, hml_s, hgla_s, cn_s, m_s, st_s,
                  *, tb):
    step = pl.program_id(0) * pl.num_programs(1) + pl.program_id(1)

    def project(x, slot):
        u = _rms(x, g_ref[...]).astype(BF16)
        for n0 in range(0, Z_MAIN_W, IN_PROJ_CHUNK):
            zm_s[slot, :, n0:n0 + IN_PROJ_CHUNK] = _dot(u, wm_ref[:, n0:n0 + IN_PROJ_CHUNK]).astype(BF16)
        zs_s[slot] = _dot(u, ws_ref[...])

    @pl.when(step == 0)
    def _():
        project(xp_ref[0:tb, :], 0)

    @pl.when(pl.program_id(1) == 0)
    def _():
        xpad[0:8, :] = jnp.zeros((8, 2 * ML_QK_W), F32)
        cn_s[...] = jnp.zeros(cn_s.shape, F32)
        st_s[...] = jnp.zeros(st_s.shape, F32)
        m_s[...] = jnp.full(m_s.shape, -jnp.inf, F32)

    params = (convw_ref, convb_ref, igb_ref, fgb_ref, lrup_ref, gateb_ref, mlnorm_ref, glanorm_ref)
    scratch = (xpad, q_s, k_s, e_pad, hml_s, hgla_s, cn_s, m_s, st_s)
    project(xp_ref[tb:2 * tb, :], 1)
    _mix_block(zm_s.at[0], zs_s.at[0], *params, y_ref.at[0:tb], *scratch, tb=tb)
    project(xn_ref[...], 0)
    _mix_block(zm_s.at[1], zs_s.at[1], *params, y_ref.at[tb:2 * tb], *scratch, tb=tb)


def _mix_block(zm_ref, zs_ref, convw_ref, convb_ref, igb_ref, fgb_ref, lrup_ref, gateb_ref, mlnorm_ref,
               glanorm_ref, y_ref, xpad, q_s, k_s, e_pad, hml_s, hgla_s, cn_s, m_s, st_s, *, tb):
    nc = tb // CHUNK

    xpad[8:8 + tb, :] = zm_ref[:, OFF_MQK:OFF_MQK + 2 * ML_QK_W].astype(F32)
    conv = convb_ref[...] + convw_ref[3:4, :] * xpad[8:8 + tb, :]
    for j in range(CONV_K - 1):
        conv = conv + convw_ref[j:j + 1, :] * xpad[5 + j:5 + j + tb, :]
    xpad[0:8, :] = xpad[tb:tb + 8, :]
    qk = conv * _sigmoid(conv)
    q_s[...] = qk[:, :ML_QK_W]
    k_s[...] = qk[:, ML_QK_W:] * (ML_DK ** -0.5)

    r_t = lax.broadcasted_iota(jnp.int32, (tb, tb), 0)
    c_t = lax.broadcasted_iota(jnp.int32, (tb, tb), 1)
    tri_blk = jnp.where((r_t // CHUNK == c_t // CHUNK) & (c_t <= r_t), 1.0, 0.0).astype(BF16)
    r_e = lax.broadcasted_iota(jnp.int32, (LANES, 2 * LANES), 0)
    c_e = lax.broadcasted_iota(jnp.int32, (LANES, 2 * LANES), 1)
    spread_dk = jnp.where(c_e // ML_DK == r_e, 1.0, 0.0).astype(BF16)
    mean_dv = jnp.full((ML_DV, ML_DV), 1.0 / ML_DV, F32).astype(BF16)
    row_c = lax.broadcasted_iota(jnp.int32, (CHUNK, CHUNK), 0)
    col_c = lax.broadcasted_iota(jnp.int32, (CHUNK, CHUNK), 1)
    causal = col_c <= row_c
    lane_c = lax.broadcasted_iota(jnp.int32, (CHUNK, LANES), 1)
    ones_dv = jnp.ones((CHUNK, ML_DV), BF16)
    chunk_rows = lambda c: slice(c * CHUNK, (c + 1) * CHUNK)

    zs = zs_ref[...]
    ig = zs + igb_ref[...]
    lf = _log_sigmoid(pltpu.roll(zs, LANES - SM_F, 1) + fgb_ref[...])
    bc = _dot01_left(tri_blk, lf, 3)
    bc3 = bc.reshape(nc, CHUNK, LANES)
    g3 = bc3[:, CHUNK - 1:CHUNK, :]
    a3 = g3 - bc3 + ig.reshape(nc, CHUNK, LANES)
    amax3 = jnp.max(a3, axis=1, keepdims=True)
    wa = jnp.exp(a3 - amax3).reshape(tb, LANES)
    m_run = m_s[...]
    m_prev, s_old, s_in = [], [], []
    for c in range(nc):
        m_new = jnp.maximum(g3[c] + m_run, amax3[c])
        m_prev.append(m_run)
        s_old.append(jnp.exp(g3[c] + m_run - m_new))
        s_in.append(jnp.exp(amax3[c] - m_new))
        m_run = m_new
    m_s[...] = m_run
    e_nat = ig - bc
    e_pad[0:CHUNK, :] = jnp.full((CHUNK, LANES), -jnp.inf, F32)
    e_pad[CHUNK:CHUNK + tb, :] = e_nat
    pos = lax.broadcasted_iota(jnp.int32, (tb, LANES), 0) % CHUNK
    shift = 1
    while shift < CHUNK:
        shifted = e_pad[CHUNK - shift:CHUNK - shift + tb, :]
        e_pad[CHUNK:CHUNK + tb, :] = jnp.maximum(e_pad[CHUNK:CHUNK + tb, :], jnp.where(pos >= shift, shifted, -jnp.inf))
        shift *= 2
    m_intra = bc + e_pad[CHUNK:CHUNK + tb, :]
    il = jnp.concatenate([bc3[c] + m_prev[c] for c in range(nc)], axis=0)
    mt = jnp.maximum(il, m_intra)
    s_inter = jnp.exp(il - mt)
    exp_neg = jnp.exp(-mt)
    xn = bc - mt
    lane_t = lax.broadcasted_iota(jnp.int32, (tb, LANES), 1)

    def lane_terms(v):
        rest, terms = jnp.where(lane_t < ML_HEADS, v, 0.0), []
        for _ in range(3):
            terms.append(rest.astype(BF16).astype(F32))
            rest = rest - terms[-1]
        return terms

    in_group = lambda lanes, g: (lanes >= g * LANE_GROUP) & (lanes < g * LANE_GROUP + ML_HEADS)
    xa, xb, xc = lane_terms(xn)
    ea, eb, ec = lane_terms(e_nat)
    x_all = (xa + pltpu.roll(xb, LANE_GROUP, 1) + pltpu.roll(xc, 2 * LANE_GROUP, 1)
             + jnp.where(in_group(lane_t, 3) | in_group(lane_t, 4) | in_group(lane_t, 5), 1.0, 0.0)).astype(BF16)
    ye_all = pltpu.roll(ea, 3 * LANE_GROUP, 1) + pltpu.roll(eb, 4 * LANE_GROUP, 1) + pltpu.roll(ec, 5 * LANE_GROUP, 1)
    head_lanes = lambda h, groups: functools.reduce(jnp.logical_or, [lane_c == g * LANE_GROUP + h for g in groups])
    wa_x = _dot01_right(wa, spread_dk, 2)
    si_x = _dot01_right(s_inter, spread_dk, 2)
    q_all = q_s[...]
    k_all = k_s[...]
    kw_b = (k_all * wa_x).astype(BF16)
    qs_b = (q_all * si_x).astype(BF16)
    q_b = q_all.astype(BF16)
    k_b = k_all.astype(BF16)

    qkm, upd, vo = {}, {}, {}
    for c in range(nc):
        rows = chunk_rows(c)
        for h in range(ML_HEADS):
            dk = slice(h * ML_DK, (h + 1) * ML_DK)
            y_t = jnp.where(head_lanes(h, (0, 1, 2)), 1.0, jnp.where(head_lanes(h, (3, 4, 5)), ye_all[rows], 0.0))
            d = _dot_nt(x_all[rows], y_t.astype(BF16))
            p = jnp.where(causal, jnp.exp(d), 0.0)
            qkm[c, h] = (_dot_nt(q_b[rows, dk], k_b[rows, dk]) * p).astype(BF16)
            vo[c, h] = jnp.concatenate([zm_ref[rows, OFF_MV + h * ML_DV:OFF_MV + (h + 1) * ML_DV], ones_dv], axis=1)
            upd[c, h] = _dot_tn(kw_b[rows, dk], vo[c, h])
    for h in range(ML_HEADS):
        dk = slice(h * ML_DK, (h + 1) * ML_DK)
        dv = slice(h * ML_DV, (h + 1) * ML_DV)
        cn = cn_s[h]
        for c in range(nc):
            rows = chunk_rows(c)
            res = _dot(qs_b[rows, dk], cn.astype(BF16)) + _dot(qkm[c, h], vo[c, h])
            num, den = res[:, :ML_DV], res[:, ML_DV:]
            hml_s[rows, dv] = num / jnp.maximum(jnp.abs(den), exp_neg[rows, h:h + 1])
            cn = s_old[c][:, h:h + 1] * cn + s_in[c][:, h:h + 1] * upd[c, h]
        cn_s[h] = cn

    la = _log_sigmoid(_dot(zs.astype(BF16), lrup_ref[...]) + gateb_ref[...]) * (1.0 / GLA_TAU)
    bcg = _dot01_left(tri_blk, la, 3)
    bcg3 = bcg.reshape(nc, CHUNK, GLA_QK_W)
    gg3 = bcg3[:, CHUNK - 1:CHUNK, :]
    gq = zm_ref[:, OFF_GQ:OFF_GQ + GLA_QK_W].astype(F32)
    gk = zm_ref[:, OFF_GK:OFF_GK + GLA_QK_W].astype(F32) * (GLA_DK ** -0.5)
    q_dec = (gq * jnp.exp(bcg)).astype(BF16)
    k_inv = (gk * jnp.exp(-bcg)).astype(BF16)
    k_end = (gk * jnp.exp(gg3 - bcg3).reshape(tb, GLA_QK_W)).astype(BF16)
    eg3 = jnp.exp(gg3)
    att, updg = {}, {}
    for c in range(nc):
        rows = chunk_rows(c)
        for h in range(GLA_HEADS):
            dk = slice(h * GLA_DK, (h + 1) * GLA_DK)
            vh = zm_ref[rows, OFF_GV + h * GLA_DV:OFF_GV + (h + 1) * GLA_DV]
            att[c, h] = jnp.where(causal, _dot_nt(q_dec[rows, dk], k_inv[rows, dk]), 0.0).astype(BF16)
            updg[c, h] = _dot_tn(vh, k_end[rows, dk])
    for h in range(GLA_HEADS):
        dk = slice(h * GLA_DK, (h + 1) * GLA_DK)
        st = st_s[h]
        for c in range(nc):
            rows = chunk_rows(c)
            vh = zm_ref[rows, OFF_GV + h * GLA_DV:OFF_GV + (h + 1) * GLA_DV]
            hgla_s[rows, h * GLA_DV:(h + 1) * GLA_DV] = _dot_nt(q_dec[rows, dk], st.astype(BF16)) + _dot(att[c, h], vh)
            st = st * eg3[c][:, dk] + updg[c, h]
        st_s[h] = st

    mean = lambda x: _dot01_right(x, mean_dv, 2)
    for h in range(ML_HEADS):
        sl = slice(h * ML_DV, (h + 1) * ML_DV)
        hh = hml_s[:, sl]
        d = hh - mean(hh)
        hn = d * lax.rsqrt(mean(d * d) + EPS)
        og = zm_ref[:, OFF_MO + h * ML_DV:OFF_MO + (h + 1) * ML_DV].astype(F32)
        y_ref[:, sl] = (hn * mlnorm_ref[:, sl] * _sigmoid(og)).astype(y_ref.dtype)
    for h in range(GLA_HEADS):
        sl = slice(h * GLA_DV, (h + 1) * GLA_DV)
        o = hgla_s[:, sl]
        on = o * lax.rsqrt(mean(o * o) + EPS)
        gg = zm_ref[:, OFF_GG + h * GLA_DV:OFF_GG + (h + 1) * GLA_DV].astype(F32)
        y_ref[:, ML_V_W + h * GLA_DV:ML_V_W + (h + 1) * GLA_DV] = (on * glanorm_ref[:, sl] * (gg * _sigmoid(gg))).astype(y_ref.dtype)


def _mixer(h, g, w_main, w_small, convw, convb, igb, fgb, lrup, gateb, mlnorm, glanorm, bsz, seq, tb):
    nt = seq // (2 * tb)
    last_block = bsz * seq // tb - 1
    const = lambda shape: pl.BlockSpec(shape, lambda b, i: (0,) * len(shape))
    return pl.pallas_call(
        functools.partial(_mixer_kernel, tb=tb),
        out_shape=jax.ShapeDtypeStruct((bsz * seq, ML_V_W + GLA_V_W), BF16),
        grid=(bsz, nt),
        in_specs=[pl.BlockSpec((2 * tb, D_MODEL), lambda b, i: (b * nt + i, 0)),
                  pl.BlockSpec((tb, D_MODEL), lambda b, i: (jnp.minimum(2 * (b * nt + i) + 2, last_block), 0)),
                  const((1, D_MODEL)), const((D_MODEL, Z_MAIN_W)), const((D_MODEL, LANES)),
                  const((CONV_K, 2 * ML_QK_W)), const((1, 2 * ML_QK_W)), const((1, LANES)), const((1, LANES)),
                  const((LANES, GLA_QK_W)), const((1, GLA_QK_W)), const((1, ML_V_W)), const((1, GLA_V_W))],
        out_specs=pl.BlockSpec((2 * tb, ML_V_W + GLA_V_W), lambda b, i: (b * nt + i, 0)),
        scratch_shapes=[pltpu.VMEM((2, tb, Z_MAIN_W), BF16), pltpu.VMEM((2, tb, LANES), F32),
                        pltpu.VMEM((tb + 8, 2 * ML_QK_W), F32),
                        pltpu.VMEM((tb, ML_QK_W), F32), pltpu.VMEM((tb, ML_QK_W), F32),
                        pltpu.VMEM((tb + CHUNK, LANES), F32),
                        pltpu.VMEM((tb, ML_V_W), F32), pltpu.VMEM((tb, GLA_V_W), F32),
                        pltpu.VMEM((ML_HEADS, ML_DK, ML_DV + LANES), F32),
                        pltpu.VMEM((1, LANES), F32), pltpu.VMEM((GLA_HEADS, GLA_DV, GLA_DK), F32)],
        compiler_params=_cparams(("arbitrary", "arbitrary")),
        name="mlstm_gla",
    )(h, h, g, w_main, w_small, convw, convb, igb, fgb, lrup, gateb, mlnorm, glanorm)


def _proj_router_kernel(y_ref, w_ref, b_ref, h_ref, g_ref, wr_ref, br_ref, h1_ref, u2_ref, rt_ref, rk_ref, cnt_ref,
                        base_s, strict_s, rtp_s):
    step = pl.program_id(0)

    @pl.when(step == 0)
    def _():
        _rank_init(base_s, strict_s)
        rtp_s[...] = jnp.zeros(rtp_s.shape, F32)

    _rank_rows(rtp_s[...], step > 0, rk_ref.at[jnp.maximum(step - 1, 0)], cnt_ref, base_s, strict_s)

    h1 = h_ref[...] + (_dot(y_ref[...], w_ref[...]) + b_ref[...])
    h1_ref[...] = h1
    u2 = _rms(h1, g_ref[...])
    for c in range(ROW_TILES):
        u2_ref[pl.ds(c, u2.shape[0], stride=ROW_TILES), :] = u2[:, c * LANES:(c + 1) * LANES]
    u_hi = u2.astype(BF16)
    u_lo = (u2 - u_hi.astype(F32)).astype(BF16)
    part = _dot(u_hi, wr_ref[...])
    logits = part[:, :LANES] + (part[:, LANES:] + _dot(u_lo, wr_ref[:, :LANES])) + br_ref[...]
    lane = lax.broadcasted_iota(jnp.int32, logits.shape, 1)
    lane_f = lane.astype(F32)
    big = float(LANES)
    gl = jnp.where(lane < N_GROUPS, logits, -jnp.inf)
    g_max = jnp.max(gl, axis=-1, keepdims=True)
    g_idx = jnp.min(jnp.where(gl == g_max, lane_f, big), axis=-1, keepdims=True)
    g_p = 1.0 / jnp.sum(jnp.exp(gl - g_max), axis=-1, keepdims=True)
    e_grp = ((lane - N_GROUPS) // EXPERTS_PER_GROUP).astype(F32)
    in_grp = (lane >= N_GROUPS) & (lane < N_GROUPS + N_EXPERTS) & (e_grp == g_idx)
    el = jnp.where(in_grp, logits, -jnp.inf)
    t1 = jnp.max(el, axis=-1, keepdims=True)
    i1 = jnp.min(jnp.where(el == t1, lane_f, big), axis=-1, keepdims=True)
    el2 = jnp.where(lane_f == i1, -jnp.inf, el)
    t2 = jnp.max(el2, axis=-1, keepdims=True)
    i2 = jnp.min(jnp.where(el2 == t2, lane_f, big), axis=-1, keepdims=True)
    e21 = jnp.exp(t2 - t1)
    p1 = 1.0 / (1.0 + e21)
    rt = jnp.where(lane == 0, i1 - N_GROUPS,
                   jnp.where(lane == 1, i2 - N_GROUPS,
                             jnp.where(lane == 2, g_p * p1, jnp.where(lane == 3, g_p * (e21 * p1), 0.0))))
    rt_ref[...] = rt
    rtp_s[...] = rt

    @pl.when(step == pl.num_programs(0) - 1)
    def _():
        _rank_rows(rt, True, rk_ref.at[step], cnt_ref, base_s, strict_s)


def _proj_router(y, w, b, h, g, wr, br, tm):
    t, kdim = y.shape
    row = lambda i: (i, 0)
    fixed = lambda i: (0, 0)
    h1, u2, rt, rk, cnt = pl.pallas_call(
        _proj_router_kernel,
        out_shape=(jax.ShapeDtypeStruct((t, D_MODEL), F32), jax.ShapeDtypeStruct((t * ROW_TILES, LANES), F32),
                   jax.ShapeDtypeStruct((t, LANES), F32), jax.ShapeDtypeStruct((t // tm, 8, tm), jnp.int32),
                   jax.ShapeDtypeStruct((1, LANES), F32)),
        grid=(t // tm,),
        in_specs=[pl.BlockSpec((tm, kdim), row), pl.BlockSpec((kdim, D_MODEL), fixed),
                  pl.BlockSpec((1, D_MODEL), fixed), pl.BlockSpec((tm, D_MODEL), row),
                  pl.BlockSpec((1, D_MODEL), fixed), pl.BlockSpec((D_MODEL, 2 * LANES), fixed),
                  pl.BlockSpec((1, LANES), fixed)],
        out_specs=(pl.BlockSpec((tm, D_MODEL), row), pl.BlockSpec((tm * ROW_TILES, LANES), row),
                   pl.BlockSpec((tm, LANES), row), pl.BlockSpec((t // tm, 8, tm), lambda i: (0, 0, 0)),
                   pl.BlockSpec((1, LANES), fixed)),
        scratch_shapes=[pltpu.VMEM((1, LANES), F32), pltpu.VMEM((tm, tm), BF16), pltpu.VMEM((tm, LANES), F32)],
        compiler_params=_cparams(("arbitrary",)),
        name="proj_router",
    )(y, w, b, h, g, wr, br)
    return h1, u2, rt, rk.transpose(1, 0, 2).reshape(8, t), cnt


def _rank_init(base_s, strict_s):
    tt = strict_s.shape[0]
    base_s[...] = jnp.zeros(base_s.shape, F32)
    r = lax.broadcasted_iota(jnp.int32, (tt, tt), 0)
    c = lax.broadcasted_iota(jnp.int32, (tt, tt), 1)
    strict_s[...] = jnp.where(c < r, 1.0, 0.0).astype(BF16)


def _rank_rows(rt, valid, rk_ref, cnt_ref, base_s, strict_s):
    lane = lax.broadcasted_iota(jnp.int32, rt.shape, 1)
    lane_f = lane.astype(F32)
    e0, e1 = rt[:, 0:1], rt[:, 1:2]
    oh0 = (lane_f == e0) & valid
    oh1 = (lane_f == e1) & valid
    oh = jnp.where(oh0 | oh1, 1.0, 0.0)
    before = _dot(strict_s[...], oh.astype(BF16)) + base_s[...]
    r0 = jnp.sum(jnp.where(oh0, before, 0.0), axis=-1, keepdims=True)
    r1 = jnp.sum(jnp.where(oh1, before, 0.0), axis=-1, keepdims=True)
    table = jnp.where(lane == 0, r0, jnp.where(lane == 1, r1, jnp.where(lane == 2, e0, jnp.where(lane == 3, e1, 0.0))))
    rk_ref[...] = jnp.transpose(table)[0:8, :].astype(jnp.int32)
    base_s[...] = base_s[...] + jnp.sum(oh, axis=0, keepdims=True)
    cnt_ref[...] = base_s[...]


DMA_GROUP = 8


def _row_copy(src, dst, sem):
    return pltpu.make_async_copy(src, dst, sem)


def _row_tile(r):
    return pl.ds(pl.multiple_of(r * ROW_TILES, ROW_TILES), ROW_TILES)


def _dispatch_kernel(dest_ref, zflag_ref, fill_ref, u_ref, xout_hbm, inv_ref, zero_s, sem, *, tt):
    i = pl.program_id(0)
    block_rows = MOE_BM * ROW_TILES

    @pl.when(i == 0)
    def _():
        fill = pltpu.make_async_copy(fill_ref, inv_ref, sem)
        fill.start()
        fill.wait()
        zero_s[...] = jnp.zeros(zero_s.shape, F32)
        zero_block = lambda blk: _row_copy(zero_s, xout_hbm.at[pl.ds(blk * block_rows, block_rows)], sem)
        for blk in range(zflag_ref.shape[0]):
            @pl.when(zflag_ref[blk] != 0)
            def _():
                zero_block(blk).start()
        for blk in range(zflag_ref.shape[0]):
            @pl.when(zflag_ref[blk] != 0)
            def _():
                zero_block(blk).wait()

    def issue(g, carry):
        for jj in range(DMA_GROUP):
            j = g * DMA_GROUP + jj
            src = u_ref.at[_row_tile(j)]
            for k in range(TOP_K):
                d = dest_ref[k, j]
                _row_copy(src, xout_hbm.at[_row_tile(d)], sem).start(priority=k)
                inv_ref[d] = (i * tt + j) * TOP_K + k
        return carry

    lax.fori_loop(0, tt // DMA_GROUP, issue, 0)
    for k in range(TOP_K):
        _row_copy(u_ref, xout_hbm.at[pl.ds(0, tt * ROW_TILES)], sem).wait()


def _dispatch(dest, zflag, u2, tt):
    t = u2.shape[0] // ROW_TILES
    n_rows = zflag.shape[0] * MOE_BM
    row = jnp.arange(n_rows, dtype=jnp.int32)
    fill = t * TOP_K + (row // MOE_BM % 2) * MOE_BM + row % MOE_BM
    return pl.pallas_call(
        functools.partial(_dispatch_kernel, tt=tt),
        out_shape=(jax.ShapeDtypeStruct((n_rows * ROW_TILES, LANES), F32), jax.ShapeDtypeStruct((n_rows,), jnp.int32)),
        grid=(t // tt,),
        in_specs=[pl.BlockSpec((TOP_K, tt), lambda i: (0, i), memory_space=pltpu.SMEM),
                  pl.BlockSpec(memory_space=pltpu.SMEM),
                  pl.BlockSpec(memory_space=pltpu.VMEM),
                  pl.BlockSpec((tt * ROW_TILES, LANES), lambda i: (i, 0))],
        out_specs=(pl.BlockSpec(memory_space=pl.ANY), pl.BlockSpec(memory_space=pltpu.SMEM)),
        scratch_shapes=[pltpu.VMEM((MOE_BM * ROW_TILES, LANES), F32), pltpu.SemaphoreType.DMA],
        compiler_params=_cparams(("arbitrary",)),
        name="moe_dispatch",
    )(dest, zflag, fill, u2)


def _ffn_kernel(be_ref, nu_ref, inv_ref, par_ref, nxt_ref, x_ref, wg_hbm, wu_hbm, wd_hbm, o2_hbm, wg_f, wu_f, wd_f,
                wg_s, wu_s, wd_s, x_s, y_s, sem, wsem, *, n_slots, layer):
    b = pl.program_id(0)

    def weight_copies(e, s):
        return (pltpu.make_async_copy(wg_hbm.at[layer, e], wg_f.at[s], wsem.at[s]),
                pltpu.make_async_copy(wu_hbm.at[layer, e], wu_f.at[s], wsem.at[s]),
                pltpu.make_async_copy(wd_hbm.at[layer, e], wd_f.at[s], wsem.at[s]))

    last = pl.num_programs(0) - 1
    n_used = nu_ref[0]
    block_rows = MOE_BM * ROW_TILES

    def drain():
        _row_copy(y_s.at[0], o2_hbm.at[pl.ds(0, block_rows)], sem).wait()

    def scatter(blk):
        s = blk % 2
        for j in range(MOE_BM):
            slot = inv_ref[blk * MOE_BM + j]
            _row_copy(y_s.at[s, _row_tile(j)], o2_hbm.at[_row_tile(slot)], sem).start(priority=j % 2)

    def compute():
        for c in range(ROW_TILES):
            x_s[:, c * LANES:(c + 1) * LANES] = x_ref[pl.ds(c, MOE_BM, stride=ROW_TILES), :].astype(BF16)
        x = x_s[...]
        a = _dot(x, wg_s[...])
        u = _dot(x, wu_s[...])
        y = _dot(((a * _sigmoid(a)) * u).astype(BF16), wd_s[...])
        for c in range(ROW_TILES):
            y_s[b % 2, pl.ds(c, MOE_BM, stride=ROW_TILES), :] = y[:, c * LANES:(c + 1) * LANES]

    @pl.when(b == 0)
    def _():
        y_s[...] = jnp.zeros(y_s.shape, F32)
        for s in range(2):
            _row_copy(y_s.at[s], o2_hbm.at[pl.ds((n_slots + s * MOE_BM) * ROW_TILES, block_rows)], sem).start()
        for s in range(2):
            drain()

    @pl.when((b >= 2) & (b - 2 < n_used))
    def _():
        drain()

    @pl.when(b == 0)
    def _():
        for c in weight_copies(be_ref[0], 0):
            c.start()

    @pl.when((b == 0) | (be_ref[b] != be_ref[jnp.maximum(b - 1, 0)]))
    def _():
        s = par_ref[b]
        for c in weight_copies(0, s):
            c.wait()
        wg_s[...] = wg_f[s].astype(BF16)
        wu_s[...] = wu_f[s].astype(BF16)
        wd_s[...] = wd_f[s].astype(BF16)

        @pl.when(nxt_ref[b] >= 0)
        def _():
            for c in weight_copies(nxt_ref[b], 1 - s):
                c.start()

    @pl.when(b == 0)
    def _():
        compute()

    @pl.when((b >= 1) & (b < n_used))
    def _():
        scatter(b - 1)
        compute()

    @pl.when((b >= 1) & (b >= n_used) & (b - 1 < n_used))
    def _():
        scatter(b - 1)

    @pl.when(b == last)
    def _():
        @pl.when((b >= 1) & (b - 1 < n_used))
        def _():
            drain()

        @pl.when(b < n_used)
        def _():
            scatter(b)
            drain()


def _ffn(block_expert, n_used, inv, xbuf, wg, wu, wd, layer, n_slots):
    n_blocks = xbuf.shape[0] // (MOE_BM * ROW_TILES)
    starts = jnp.concatenate([jnp.ones((1,), jnp.int32), (block_expert[1:] != block_expert[:-1]).astype(jnp.int32)])
    parity = (jnp.cumsum(starts) - 1) % 2
    later = jnp.where(block_expert[None, :] > block_expert[:, None], block_expert[None, :], N_EXPERTS)
    nxt = jnp.min(later, axis=1)
    nxt = jnp.where(nxt == N_EXPERTS, -1, nxt).astype(jnp.int32)
    rows = lambda b, be, nu, inv, par, nx: (jnp.minimum(b, nu[0] - 1), 0)
    hbm = pl.BlockSpec(memory_space=pl.ANY)
    return pl.pallas_call(
        functools.partial(_ffn_kernel, n_slots=n_slots, layer=layer),
        out_shape=jax.ShapeDtypeStruct(((n_slots + 2 * MOE_BM) * ROW_TILES, LANES), F32),
        grid_spec=pltpu.PrefetchScalarGridSpec(
            num_scalar_prefetch=5,
            grid=(n_blocks,),
            in_specs=[pl.BlockSpec((MOE_BM * ROW_TILES, LANES), rows), hbm, hbm, hbm],
            out_specs=pl.BlockSpec(memory_space=pl.ANY),
            scratch_shapes=[pltpu.VMEM((2, D_MODEL, D_FF), F32), pltpu.VMEM((2, D_MODEL, D_FF), F32),
                            pltpu.VMEM((2, D_FF, D_MODEL), F32),
                            pltpu.VMEM((D_MODEL, D_FF), BF16), pltpu.VMEM((D_MODEL, D_FF), BF16),
                            pltpu.VMEM((D_FF, D_MODEL), BF16), pltpu.VMEM((MOE_BM, D_MODEL), BF16),
                            pltpu.VMEM((2, MOE_BM * ROW_TILES, LANES), F32), pltpu.SemaphoreType.DMA,
                            pltpu.SemaphoreType.DMA((2,))]),
        compiler_params=_cparams(("arbitrary",)),
        name="moe_ffn",
    )(block_expert, n_used, inv, parity.astype(jnp.int32), nxt, xbuf, wg, wu, wd)


def _combine_rows(o2_ref, h_ref, rt_ref, o_ref):
    tt = h_ref.shape[0]
    rt = rt_ref[...]
    w0, w1 = rt[:, 2:3], rt[:, 3:4]
    for c in range(ROW_TILES):
        lanes = slice(c * LANES, (c + 1) * LANES)
        y0 = o2_ref[pl.ds(c, tt, stride=TOP_K * ROW_TILES), :]
        y1 = o2_ref[pl.ds(ROW_TILES + c, tt, stride=TOP_K * ROW_TILES), :]
        o_ref[:, lanes] = h_ref[:, lanes] + (y0 * w0 + y1 * w1)


def _combine_kernel(o2_ref, h_ref, rt_ref, g_ref, o_ref):
    _combine_rows(o2_ref, h_ref, rt_ref, o_ref)
    o_ref[...] = _rms(o_ref[...], g_ref[...])


def _combine_norm(o2, h1, rt, g, tt):
    t = h1.shape[0]
    return pl.pallas_call(
        _combine_kernel,
        out_shape=jax.ShapeDtypeStruct((t, D_MODEL), F32),
        grid=(t // tt,),
        in_specs=[pl.BlockSpec((tt * TOP_K * ROW_TILES, LANES), lambda i: (i, 0)),
                  pl.BlockSpec((tt, D_MODEL), lambda i: (i, 0)),
                  pl.BlockSpec((tt, LANES), lambda i: (i, 0)),
                  pl.BlockSpec((1, D_MODEL), lambda i: (0, 0))],
        out_specs=pl.BlockSpec((tt, D_MODEL), lambda i: (i, 0)),
        compiler_params=_cparams(("parallel",)),
        name="moe_combine",
    )(o2, h1, rt, g)


def _moe_experts(u2, rk, cnt, wg, wu, wd, layer):
    t = rk.shape[1]
    counts = cnt[0, :N_EXPERTS].astype(jnp.int32)
    padded = (counts + MOE_BM - 1) // MOE_BM * MOE_BM
    pad_end = jnp.cumsum(padded)
    pad_start = pad_end - padded
    n_blocks = (t * TOP_K) // MOE_BM + N_EXPERTS
    is_expert = rk[TOP_K:2 * TOP_K][None] == jnp.arange(N_EXPERTS, dtype=jnp.int32)[:, None, None]
    dest = jnp.sum(jnp.where(is_expert, pad_start[:, None, None], 0), axis=0) + rk[0:TOP_K]
    n_used = (pad_end[-1] // MOE_BM).astype(jnp.int32)
    blocks = jnp.arange(n_blocks, dtype=jnp.int32)
    blk = jnp.minimum(blocks, n_used - 1) * MOE_BM
    block_expert = jnp.minimum(jnp.sum(pad_end[None, :] <= blk[:, None], axis=1), N_EXPERTS - 1).astype(jnp.int32)
    closes_expert = jnp.any(((blocks[:, None] + 1) * MOE_BM == pad_end[None, :]) & (padded[None, :] > 0), axis=1)
    zflag = ((blocks >= n_used) | closes_expert).astype(jnp.int32)
    xbuf, inv = _dispatch(dest, zflag, u2, 2048)
    return _ffn(block_expert, n_used.reshape(1), inv, xbuf, wg, wu, wd, layer, t * TOP_K)


def _attn_kernel(o2_ref, h1_ref, rt_ref, g_ref, w_ref, b_ref, pos_ref, freq_ref, sp_ref, sink_ref, o_ref, h_ref, q_s,
                 kv_s, *, tm, tiles_per_seq):
    seq_start = pl.program_id(0) % tiles_per_seq == 0

    @pl.when(seq_start)
    def _():
        kv_s[0:WINDOW, :] = jnp.zeros((WINDOW, KV_OUT_W), kv_s.dtype)

    _combine_rows(o2_ref, h1_ref, rt_ref, h_ref)
    _project_qkv(h_ref, g_ref, w_ref, b_ref, pos_ref, freq_ref, sp_ref, q_s, kv_s.at[pl.ds(WINDOW, tm)])
    for j in range(tm // WINDOW):
        has_prev = jnp.logical_not(seq_start) if j == 0 else True
        _swa_block(q_s.at[pl.ds(j * WINDOW, WINDOW)], kv_s.at[pl.ds((j + 1) * WINDOW, WINDOW)],
                   kv_s.at[pl.ds(j * WINDOW, WINDOW)], sink_ref, o_ref.at[pl.ds(j * WINDOW, WINDOW)], has_prev)
    kv_s[0:WINDOW, :] = kv_s[tm:tm + WINDOW, :]


def _project_qkv(x_ref, g_ref, w_ref, b_ref, pos_ref, freq_ref, sp_ref, q_ref, kv_ref):
    u = _rms(x_ref[...], g_ref[...]).astype(BF16)
    ang = freq_ref[...] * pos_ref[...].astype(F32)
    spread = lambda v, m: sum(_dot_tn(t.astype(F32), m) for t in _split_terms(v, 3))
    cosv = jnp.cos(ang)
    sinv = jnp.sin(ang)
    c_coef = spread(cosv, sp_ref[0]) + sp_ref[3, 0:1, :]
    s_lo = spread(sinv, sp_ref[1])
    s_hi = spread(sinv, sp_ref[2])

    def rotate(z):
        return z * c_coef + pltpu.roll(z, LANES - ROT_DIM // 2, 1) * s_lo + pltpu.roll(z, ROT_DIM // 2, 1) * s_hi

    scale = SWA_HEAD_DIM ** -0.5
    for j in range(0, Q_W // LANES, 2):
        z = _dot(u, w_ref[:, j * LANES:(j + 2) * LANES]) + b_ref[:, j * LANES:(j + 2) * LANES]
        for i in range(2):
            q_ref[:, (j + i) * LANES:(j + i + 1) * LANES] = (rotate(z[:, i * LANES:(i + 1) * LANES]) * scale).astype(q_ref.dtype)
    z = _dot(u, w_ref[:, Q_W:Q_W + 2 * LANES]) + b_ref[:, Q_W:Q_W + 2 * LANES]
    kv_ref[:, 0:LANES] = rotate(z[:, 0:LANES]).astype(kv_ref.dtype)
    kv_ref[:, LANES:2 * LANES] = z[:, LANES:2 * LANES].astype(kv_ref.dtype)
    z = _dot(u, w_ref[:, Q_W + 2 * LANES:Q_W + 3 * LANES]) + b_ref[:, Q_W + 2 * LANES:Q_W + 3 * LANES]
    kv_ref[:, 2 * LANES:3 * LANES] = rotate(z).astype(kv_ref.dtype)


def _rot_tables():
    half = ROT_DIM // 2
    inv_freq = (ROPE_THETA ** (-jnp.arange(0, ROT_DIM, 2, dtype=F32) / ROT_DIM)).reshape(half, 1)
    d = np.arange(LANES) % SWA_HEAD_DIM
    f = np.arange(half)[:, None]
    sp = np.zeros((4, half, LANES), np.float32)
    sp[0] = (d[None, :] < ROT_DIM) & (d[None, :] % half == f)
    sp[1] = -((d[None, :] < half) & (d[None, :] == f)).astype(np.float32)
    sp[2] = (d[None, :] >= half) & (d[None, :] < ROT_DIM) & (d[None, :] - half == f)
    sp[3, 0] = d >= ROT_DIM
    return inv_freq, jnp.asarray(sp)


def _attn(o2, h1, rt, g, w, b, pos, sinks, tm, seq):
    t = h1.shape[0]
    row = lambda i: (i, 0)
    fixed = lambda i: (0, 0)
    wtot = Q_W + KV_OUT_W
    return pl.pallas_call(
        functools.partial(_attn_kernel, tm=tm, tiles_per_seq=seq // tm),
        out_shape=(jax.ShapeDtypeStruct((t, Q_W), BF16), jax.ShapeDtypeStruct((t, D_MODEL), F32)),
        grid=(t // tm,),
        in_specs=[pl.BlockSpec((tm * TOP_K * ROW_TILES, LANES), row), pl.BlockSpec((tm, D_MODEL), row),
                  pl.BlockSpec((tm, LANES), row), pl.BlockSpec((1, D_MODEL), fixed),
                  pl.BlockSpec((D_MODEL, wtot), fixed), pl.BlockSpec((1, wtot), fixed),
                  pl.BlockSpec((1, tm), lambda i: (0, i)), pl.BlockSpec((ROT_DIM // 2, 1), fixed),
                  pl.BlockSpec((4, ROT_DIM // 2, LANES), lambda i: (0, 0, 0)), pl.BlockSpec((1, LANES), fixed)],
        out_specs=(pl.BlockSpec((tm, Q_W), row), pl.BlockSpec((tm, D_MODEL), row)),
        scratch_shapes=[pltpu.VMEM((tm, Q_W), BF16), pltpu.VMEM((tm + WINDOW, KV_OUT_W), BF16)],
        compiler_params=_cparams(("arbitrary",)),
        name="combine_qkv_swa",
    )(o2, h1, rt, g, w, b, pos, *_rot_tables(), sinks)


def _swa_block(q_ref, kvc_ref, kvp_ref, sink_ref, o_ref, has_prev):
    neg = jnp.where(has_prev, 0.0, -jnp.inf).astype(F32)
    kj = lax.broadcasted_iota(jnp.int32, (WINDOW, WINDOW), 0)
    qi = lax.broadcasted_iota(jnp.int32, (WINDOW, WINDOW), 1)
    mask_cur = kj <= qi
    top = kj < SWA_HEAD_DIM
    lane = lax.broadcasted_iota(jnp.int32, (1, LANES), 1)
    keep_lo = jnp.where(lane < SWA_HEAD_DIM, 1.0, 0.0).astype(BF16)
    keep_hi = jnp.where(lane < SWA_HEAD_DIM, 0.0, 1.0).astype(BF16)
    zeros_half = jnp.zeros((SWA_HEAD_DIM, WINDOW), F32)

    def arranged(ref):
        k_nat, k_swp = ref[:, 0:KV_W], ref[:, 2 * KV_W:3 * KV_W]
        vt = jnp.transpose(ref[:, KV_W:2 * KV_W].astype(F32))
        vt_top = lambda g: jnp.concatenate([vt[g * SWA_HEAD_DIM:(g + 1) * SWA_HEAD_DIM], zeros_half], axis=0).astype(BF16)
        vt_bot = lambda g: jnp.concatenate([zeros_half, vt[g * SWA_HEAD_DIM:(g + 1) * SWA_HEAD_DIM]], axis=0).astype(BF16)
        return {(0, 0): (k_nat * keep_lo, vt_top(0)), (0, 1): (k_swp * keep_hi, vt_bot(0)),
                (1, 0): (k_swp * keep_lo, vt_top(1)), (1, 1): (k_nat * keep_hi, vt_bot(1))}

    cur = arranged(kvc_ref)
    prev = arranged(kvp_ref)
    heads = range(SWA_Q_HEADS)
    key = lambda h: (h // SWA_GROUP, h % 2)
    scores = []
    for h in heads:
        qp = q_ref[:, (h // 2) * LANES:(h // 2 + 1) * LANES]
        sc = _dot_nt(cur[key(h)][0], qp)
        sp = _dot_nt(prev[key(h)][0], qp)
        scores.append(jnp.where(mask_cur, sc, sp + neg))
    sinks = [sink_ref[:, h:h + 1] for h in heads]
    maxes = [jnp.maximum(jnp.max(scores[h], axis=0, keepdims=True), sinks[h]) for h in heads]
    probs = [jnp.exp(scores[h] - maxes[h]) for h in heads]
    denoms = [jnp.sum(probs[h], axis=0, keepdims=True) + jnp.exp(sinks[h] - maxes[h]) for h in heads]
    for pr in range(SWA_Q_HEADS // 2):
        num = None
        for h in (2 * pr, 2 * pr + 1):
            pc = jnp.where(mask_cur, probs[h], 0.0).astype(BF16)
            pp = jnp.where(mask_cur, 0.0, probs[h]).astype(BF16)
            part = _dot(cur[key(h)][1], pc) + _dot(prev[key(h)][1], pp)
            num = part if num is None else num + part
        den = jnp.where(top, denoms[2 * pr], denoms[2 * pr + 1])
        o_ref[:, pr * LANES:(pr + 1) * LANES] = jnp.transpose(num / den).astype(o_ref.dtype)


def _pack_in_proj(w_in):
    offs = np.cumsum((0,) + AB_SPLITS)
    part = lambda i: w_in[:, offs[i]:offs[i + 1]]
    w_main = jnp.concatenate([part(0), part(1), part(2), part(5), part(6), part(7), part(8)], axis=1).astype(BF16)
    small = jnp.concatenate([part(3), part(4), part(9)], axis=1)
    w_small = jnp.pad(small, ((0, 0), (0, LANES - small.shape[1]))).astype(BF16)
    return w_main, w_small


def _pack_qkv(w, b):
    hd = SWA_HEAD_DIM
    k0, k1 = slice(Q_W, Q_W + hd), slice(Q_W + hd, Q_W + 2 * hd)
    cols = lambda a: jnp.concatenate([a[..., :Q_W + 2 * KV_W], a[..., k1], a[..., k0]], axis=-1)
    return cols(w).astype(BF16), cols(b).reshape(1, -1).astype(F32)


def _lane_row(v, offset=0):
    return jnp.zeros((1, LANES), F32).at[0, offset:offset + v.shape[0]].set(v.astype(F32))


def kernel(x, positions, mix_norm, ffn_norm, final_norm, ab_w_in, ab_conv_w, ab_conv_b, ml_igate_b, ml_fgate_b, ml_head_norm, gla_w_lr_up, gla_gate_b, gla_head_norm, ab_w_out, swa_w_qkv, swa_b_qkv, swa_sinks, swa_w_o, swa_b_o, router_group_w, router_group_b, router_expert_w, router_expert_b, expert_w_gate, expert_w_up, expert_w_down):
    bsz, seq, d = x.shape
    t = bsz * seq
    h = x.reshape(t, d)
    row = lambda v: v.reshape(1, -1).astype(F32)

    def router_params(layer):
        wr = jnp.zeros((d, LANES), F32).at[:, :N_GROUPS].set(router_group_w[layer])
        wr = wr.at[:, N_GROUPS:N_GROUPS + N_EXPERTS].set(router_expert_w[layer])
        br = jnp.zeros((1, LANES), F32).at[0, :N_GROUPS].set(router_group_b[layer])
        br = br.at[0, N_GROUPS:N_GROUPS + N_EXPERTS].set(router_expert_b[layer])
        w_hi = wr.astype(BF16)
        w_lo = (wr - w_hi.astype(F32)).astype(BF16)
        return jnp.concatenate([w_hi, w_lo], axis=1), br

    def experts(layer):
        return expert_w_gate, expert_w_up, expert_w_down, layer

    w_main, w_small = _pack_in_proj(ab_w_in[0])
    lrup = jnp.zeros((LANES, GLA_QK_W), F32).at[SM_LR:SM_LR + GLA_LOWRANK].set(gla_w_lr_up[0]).astype(BF16)
    y = _mixer(h, row(mix_norm[0]), w_main, w_small, ab_conv_w[0], row(ab_conv_b[0]), _lane_row(ml_igate_b[0]),
               _lane_row(ml_fgate_b[0]), lrup, row(gla_gate_b[0]), row(ml_head_norm[0]), row(gla_head_norm[0]),
               bsz, seq, 256)
    wr, br = router_params(0)
    h1, u2, rt, rk, cnt = _proj_router(y, ab_w_out[0].astype(BF16), jnp.zeros((1, d), F32), h, row(ffn_norm[0]), wr, br, 512)
    o2 = _moe_experts(u2, rk, cnt, *experts(0))

    w_qkv, b_qkv = _pack_qkv(swa_w_qkv[0], swa_b_qkv[0])
    o, h = _attn(o2, h1, rt, row(mix_norm[1]), w_qkv, b_qkv, positions.reshape(1, t), _lane_row(swa_sinks[0]), 512, seq)
    wr, br = router_params(1)
    h1, u2, rt, rk, cnt = _proj_router(o, swa_w_o[0].astype(BF16), row(swa_b_o[0]), h, row(ffn_norm[1]), wr, br, 512)
    out = _combine_norm(_moe_experts(u2, rk, cnt, *experts(1)), h1, rt, row(final_norm), 512)
    return out.reshape(bsz, seq, d)
```

```python
import functools

import jax
import jax.numpy as jnp
import numpy as np
from jax import lax
from jax.experimental import pallas as pl
from jax.experimental.pallas import tpu as pltpu

F32 = jnp.float32
BF16 = jnp.bfloat16

D_MODEL = 1024
EPS = 1e-6
ML_HEADS = 4
ML_DK = 64
ML_DV = 128
GLA_HEADS = 4
GLA_DK = 64
GLA_DV = 128
CHUNK = 64
CONV_K = 4
GLA_LOWRANK = 16
GLA_TAU = 16.0
ML_QK_W = ML_HEADS * ML_DK
ML_V_W = ML_HEADS * ML_DV
GLA_QK_W = GLA_HEADS * GLA_DK
GLA_V_W = GLA_HEADS * GLA_DV
AB_SPLITS = (2 * ML_QK_W, ML_V_W, ML_V_W, ML_HEADS, ML_HEADS, GLA_QK_W, GLA_QK_W, GLA_V_W, GLA_V_W, GLA_LOWRANK)
OFF_MQK = 0
OFF_MV = OFF_MQK + 2 * ML_QK_W
OFF_MO = OFF_MV + ML_V_W
OFF_GQ = OFF_MO + ML_V_W
OFF_GK = OFF_GQ + GLA_QK_W
OFF_GV = OFF_GK + GLA_QK_W
OFF_GG = OFF_GV + GLA_V_W
Z_MAIN_W = OFF_GG + GLA_V_W
SM_I = 0
SM_F = ML_HEADS
SM_LR = 2 * ML_HEADS
LANES = 128
SWA_Q_HEADS = 16
SWA_KV_HEADS = 2
SWA_HEAD_DIM = 64
SWA_GROUP = SWA_Q_HEADS // SWA_KV_HEADS
WINDOW = 128
ROT_DIM = SWA_HEAD_DIM // 4
ROPE_THETA = 500000.0
Q_W = SWA_Q_HEADS * SWA_HEAD_DIM
KV_W = SWA_KV_HEADS * SWA_HEAD_DIM
KV_OUT_W = 3 * KV_W
N_GROUPS = 4
EXPERTS_PER_GROUP = 8
N_EXPERTS = N_GROUPS * EXPERTS_PER_GROUP
TOP_K = 2
D_FF = 512
MOE_BM = 512
ROW_TILES = D_MODEL // LANES

VMEM_LIMIT = 56 * 1024 * 1024


def _cparams(sem):
    return pltpu.CompilerParams(dimension_semantics=sem, vmem_limit_bytes=VMEM_LIMIT)


def _rms(x, g):
    return x * lax.rsqrt(jnp.mean(x * x, axis=-1, keepdims=True) + EPS) * g


def _log_sigmoid(x):
    return jnp.minimum(x, 0.0) - jnp.log1p(jnp.exp(-jnp.abs(x)))


def _sigmoid(x):
    return 1.0 / (1.0 + jnp.exp(-x))


def _dot(a, b):
    return jnp.dot(a, b, preferred_element_type=F32)


def _dot_nt(a, b):
    return lax.dot_general(a, b, (((1,), (1,)), ((), ())), preferred_element_type=F32)


def _dot_tn(a, b):
    return lax.dot_general(a, b, (((0,), (0,)), ((), ())), preferred_element_type=F32)


IN_PROJ_CHUNK = 768
LANE_GROUP = 8
def _split_terms(x, n):
    terms = []
    for _ in range(n):
        t = x.astype(BF16)
        terms.append(t)
        x = x - t.astype(F32)
    return terms


def _dot01_left(m01, x, n):
    return sum(_dot(m01, t) for t in _split_terms(x, n))


def _dot01_right(x, m01, n):
    return sum(_dot(t, m01) for t in _split_terms(x, n))


def _mixer_kernel(xp_ref, xn_ref, g_ref, wm_ref, ws_ref, convw_ref, convb_ref, igb_ref, fgb_ref, lrup_ref, gateb_ref,
                  mlnorm_ref, glanorm_ref, y_ref, zm_s, zs_s, xpad, q_s, k_s, e_pad, hml_s, hgla_s, cn_s, m_s, st_s,
                  *, tb):
    step = pl.program_id(0) * pl.num_programs(1) + pl.program_id(1)

    def project(x, slot):
        u = _rms(x, g_ref[...]).astype(BF16)
        for n0 in range(0, Z_MAIN_W, IN_PROJ_CHUNK):
            zm_s[slot, :, n0:n0 + IN_PROJ_CHUNK] = _dot(u, wm_ref[:, n0:n0 + IN_PROJ_CHUNK]).astype(BF16)
        zs_s[slot] = _dot(u, ws_ref[...])

    @pl.when(step == 0)
    def _():
        project(xp_ref[0:tb, :], 0)

    @pl.when(pl.program_id(1) == 0)
    def _():
        xpad[0:8, :] = jnp.zeros((8, 2 * ML_QK_W), F32)
        cn_s[...] = jnp.zeros(cn_s.shape, F32)
        st_s[...] = jnp.zeros(st_s.shape, F32)
        m_s[...] = jnp.full(m_s.shape, -jnp.inf, F32)

    params = (convw_ref, convb_ref, igb_ref, fgb_ref, lrup_ref, gateb_ref, mlnorm_ref, glanorm_ref)
    scratch = (xpad, q_s, k_s, e_pad, hml_s, hgla_s, cn_s, m_s, st_s)
    project(xp_ref[tb:2 * tb, :], 1)
    _mix_block(zm_s.at[0], zs_s.at[0], *params, y_ref.at[0:tb], *scratch, tb=tb)
    project(xn_ref[...], 0)
    _mix_block(zm_s.at[1], zs_s.at[1], *params, y_ref.at[tb:2 * tb], *scratch, tb=tb)


def _mix_block(zm_ref, zs_ref, convw_ref, convb_ref, igb_ref, fgb_ref, lrup_ref, gateb_ref, mlnorm_ref,
               glanorm_ref, y_ref, xpad, q_s, k_s, e_pad, hml_s, hgla_s, cn_s, m_s, st_s, *, tb):
    nc = tb // CHUNK

    xpad[8:8 + tb, :] = zm_ref[:, OFF_MQK:OFF_MQK + 2 * ML_QK_W].astype(F32)
    conv = convb_ref[...] + convw_ref[3:4, :] * xpad[8:8 + tb, :]
    for j in range(CONV_K - 1):
        conv = conv + convw_ref[j:j + 1, :] * xpad[5 + j:5 + j + tb, :]
    xpad[0:8, :] = xpad[tb:tb + 8, :]
    qk = conv * _sigmoid(conv)
    q_s[...] = qk[:, :ML_QK_W]
    k_s[...] = qk[:, ML_QK_W:] * (ML_DK ** -0.5)

    r_t = lax.broadcasted_iota(jnp.int32, (tb, tb), 0)
    c_t = lax.broadcasted_iota(jnp.int32, (tb, tb), 1)
    tri_blk = jnp.where((r_t // CHUNK == c_t // CHUNK) & (c_t <= r_t), 1.0, 0.0).astype(BF16)
    r_e = lax.broadcasted_iota(jnp.int32, (LANES, 2 * LANES), 0)
    c_e = lax.broadcasted_iota(jnp.int32, (LANES, 2 * LANES), 1)
    spread_dk = jnp.where(c_e // ML_DK == r_e, 1.0, 0.0).astype(BF16)
    mean_dv = jnp.full((ML_DV, ML_DV), 1.0 / ML_DV, F32).astype(BF16)
    row_c = lax.broadcasted_iota(jnp.int32, (CHUNK, CHUNK), 0)
    col_c = lax.broadcasted_iota(jnp.int32, (CHUNK, CHUNK), 1)
    causal = col_c <= row_c
    lane_c = lax.broadcasted_iota(jnp.int32, (CHUNK, LANES), 1)
    ones_dv = jnp.ones((CHUNK, ML_DV), BF16)
    chunk_rows = lambda c: slice(c * CHUNK, (c + 1) * CHUNK)

    zs = zs_ref[...]
    ig = zs + igb_ref[...]
    lf = _log_sigmoid(pltpu.roll(zs, LANES - SM_F, 1) + fgb_ref[...])
    bc = _dot01_left(tri_blk, lf, 3)
    bc3 = bc.reshape(nc, CHUNK, LANES)
    g3 = bc3[:, CHUNK - 1:CHUNK, :]
    a3 = g3 - bc3 + ig.reshape(nc, CHUNK, LANES)
    amax3 = jnp.max(a3, axis=1, keepdims=True)
    wa = jnp.exp(a3 - amax3).reshape(tb, LANES)
    m_run = m_s[...]
    m_prev, s_old, s_in = [], [], []
    for c in range(nc):
        m_new = jnp.maximum(g3[c] + m_run, amax3[c])
        m_prev.append(m_run)
        s_old.append(jnp.exp(g3[c] + m_run - m_new))
        s_in.append(jnp.exp(amax3[c] - m_new))
        m_run = m_new
    m_s[...] = m_run
    e_nat = ig - bc
    e_pad[0:CHUNK, :] = jnp.full((CHUNK, LANES), -jnp.inf, F32)
    e_pad[CHUNK:CHUNK + tb, :] = e_nat
    pos = lax.broadcasted_iota(jnp.int32, (tb, LANES), 0) % CHUNK
    shift = 1
    while shift < CHUNK:
        shifted = e_pad[CHUNK - shift:CHUNK - shift + tb, :]
        e_pad[CHUNK:CHUNK + tb, :] = jnp.maximum(e_pad[CHUNK:CHUNK + tb, :], jnp.where(pos >= shift, shifted, -jnp.inf))
        shift *= 2
    m_intra = bc + e_pad[CHUNK:CHUNK + tb, :]
    il = jnp.concatenate([bc3[c] + m_prev[c] for c in range(nc)], axis=0)
    mt = jnp.maximum(il, m_intra)
    s_inter = jnp.exp(il - mt)
    exp_neg = jnp.exp(-mt)
    xn = bc - mt
    lane_t = lax.broadcasted_iota(jnp.int32, (tb, LANES), 1)

    def lane_terms(v):
        rest, terms = jnp.where(lane_t < ML_HEADS, v, 0.0), []
        for _ in range(3):
            terms.append(rest.astype(BF16).astype(F32))
            rest = rest - terms[-1]
        return terms

    in_group = lambda lanes, g: (lanes >= g * LANE_GROUP) & (lanes < g * LANE_GROUP + ML_HEADS)
    xa, xb, xc = lane_terms(xn)
    ea, eb, ec = lane_terms(e_nat)
    x_all = (xa + pltpu.roll(xb, LANE_GROUP, 1) + pltpu.roll(xc, 2 * LANE_GROUP, 1)
             + jnp.where(in_group(lane_t, 3) | in_group(lane_t, 4) | in_group(lane_t, 5), 1.0, 0.0)).astype(BF16)
    ye_all = pltpu.roll(ea, 3 * LANE_GROUP, 1) + pltpu.roll(eb, 4 * LANE_GROUP, 1) + pltpu.roll(ec, 5 * LANE_GROUP, 1)
    head_lanes = lambda h, groups: functools.reduce(jnp.logical_or, [lane_c == g * LANE_GROUP + h for g in groups])
    wa_x = _dot01_right(wa, spread_dk, 2)
    si_x = _dot01_right(s_inter, spread_dk, 2)
    q_all = q_s[...]
    k_all = k_s[...]
    kw_b = (k_all * wa_x).astype(BF16)
    qs_b = (q_all * si_x).astype(BF16)
    q_b = q_all.astype(BF16)
    k_b = k_all.astype(BF16)

    qkm, upd, vo = {}, {}, {}
    for c in range(nc):
        rows = chunk_rows(c)
        for h in range(ML_HEADS):
            dk = slice(h * ML_DK, (h + 1) * ML_DK)
            y_t = jnp.where(head_lanes(h, (0, 1, 2)), 1.0, jnp.where(head_lanes(h, (3, 4, 5)), ye_all[rows], 0.0))
            d = _dot_nt(x_all[rows], y_t.astype(BF16))
            p = jnp.where(causal, jnp.exp(d), 0.0)
            qkm[c, h] = (_dot_nt(q_b[rows, dk], k_b[rows, dk]) * p).astype(BF16)
            vo[c, h] = jnp.concatenate([zm_ref[rows, OFF_MV + h * ML_DV:OFF_MV + (h + 1) * ML_DV], ones_dv], axis=1)
            upd[c, h] = _dot_tn(kw_b[rows, dk], vo[c, h])
    for h in range(ML_HEADS):
        dk = slice(h * ML_DK, (h + 1) * ML_DK)
        dv = slice(h * ML_DV, (h + 1) * ML_DV)
        cn = cn_s[h]
        for c in range(nc):
            rows = chunk_rows(c)
            res = _dot(qs_b[rows, dk], cn.astype(BF16)) + _dot(qkm[c, h], vo[c, h])
            num, den = res[:, :ML_DV], res[:, ML_DV:]
            hml_s[rows, dv] = num / jnp.maximum(jnp.abs(den), exp_neg[rows, h:h + 1])
            cn = s_old[c][:, h:h + 1] * cn + s_in[c][:, h:h + 1] * upd[c, h]
        cn_s[h] = cn

    la = _log_sigmoid(_dot(zs.astype(BF16), lrup_ref[...]) + gateb_ref[...]) * (1.0 / GLA_TAU)
    bcg = _dot01_left(tri_blk, la, 3)
    bcg3 = bcg.reshape(nc, CHUNK, GLA_QK_W)
    gg3 = bcg3[:, CHUNK - 1:CHUNK, :]
    gq = zm_ref[:, OFF_GQ:OFF_GQ + GLA_QK_W].astype(F32)
    gk = zm_ref[:, OFF_GK:OFF_GK + GLA_QK_W].astype(F32) * (GLA_DK ** -0.5)
    q_dec = (gq * jnp.exp(bcg)).astype(BF16)
    k_inv = (gk * jnp.exp(-bcg)).astype(BF16)
    k_end = (gk * jnp.exp(gg3 - bcg3).reshape(tb, GLA_QK_W)).astype(BF16)
    eg3 = jnp.exp(gg3)
    att, updg = {}, {}
    for c in range(nc):
        rows = chunk_rows(c)
        for h in range(GLA_HEADS):
            dk = slice(h * GLA_DK, (h + 1) * GLA_DK)
            vh = zm_ref[rows, OFF_GV + h * GLA_DV:OFF_GV + (h + 1) * GLA_DV]
            att[c, h] = jnp.where(causal, _dot_nt(q_dec[rows, dk], k_inv[rows, dk]), 0.0).astype(BF16)
            updg[c, h] = _dot_tn(vh, k_end[rows, dk])
    for h in range(GLA_HEADS):
        dk = slice(h * GLA_DK, (h + 1) * GLA_DK)
        st = st_s[h]
        for c in range(nc):
            rows = chunk_rows(c)
            vh = zm_ref[rows, OFF_GV + h * GLA_DV:OFF_GV + (h + 1) * GLA_DV]
            hgla_s[rows, h * GLA_DV:(h + 1) * GLA_DV] = _dot_nt(q_dec[rows, dk], st.astype(BF16)) + _dot(att[c, h], vh)
            st = st * eg3[c][:, dk] + updg[c, h]
        st_s[h] = st

    mean = lambda x: _dot01_right(x, mean_dv, 2)
    for h in range(ML_HEADS):
        sl = slice(h * ML_DV, (h + 1) * ML_DV)
        hh = hml_s[:, sl]
        d = hh - mean(hh)
        hn = d * lax.rsqrt(mean(d * d) + EPS)
        og = zm_ref[:, OFF_MO + h * ML_DV:OFF_MO + (h + 1) * ML_DV].astype(F32)
        y_ref[:, sl] = (hn * mlnorm_ref[:, sl] * _sigmoid(og)).astype(y_ref.dtype)
    for h in range(GLA_HEADS):
        sl = slice(h * GLA_DV, (h + 1) * GLA_DV)
        o = hgla_s[:, sl]
        on = o * lax.rsqrt(mean(o * o) + EPS)
        gg = zm_ref[:, OFF_GG + h * GLA_DV:OFF_GG + (h + 1) * GLA_DV].astype(F32)
        y_ref[:, ML_V_W + h * GLA_DV:ML_V_W + (h + 1) * GLA_DV] = (on * glanorm_ref[:, sl] * (gg * _sigmoid(gg))).astype(y_ref.dtype)


def _mixer(h, g, w_main, w_small, convw, convb, igb, fgb, lrup, gateb, mlnorm, glanorm, bsz, seq, tb):
    nt = seq // (2 * tb)
    last_block = bsz * seq // tb - 1
    const = lambda shape: pl.BlockSpec(shape, lambda b, i: (0,) * len(shape))
    return pl.pallas_call(
        functools.partial(_mixer_kernel, tb=tb),
        out_shape=jax.ShapeDtypeStruct((bsz * seq, ML_V_W + GLA_V_W), BF16),
        grid=(bsz, nt),
        in_specs=[pl.BlockSpec((2 * tb, D_MODEL), lambda b, i: (b * nt + i, 0)),
                  pl.BlockSpec((tb, D_MODEL), lambda b, i: (jnp.minimum(2 * (b * nt + i) + 2, last_block), 0)),
                  const((1, D_MODEL)), const((D_MODEL, Z_MAIN_W)), const((D_MODEL, LANES)),
                  const((CONV_K, 2 * ML_QK_W)), const((1, 2 * ML_QK_W)), const((1, LANES)), const((1, LANES)),
                  const((LANES, GLA_QK_W)), const((1, GLA_QK_W)), const((1, ML_V_W)), const((1, GLA_V_W))],
        out_specs=pl.BlockSpec((2 * tb, ML_V_W + GLA_V_W), lambda b, i: (b * nt + i, 0)),
        scratch_shapes=[pltpu.VMEM((2, tb, Z_MAIN_W), BF16), pltpu.VMEM((2, tb, LANES), F32),
                        pltpu.VMEM((tb + 8, 2 * ML_QK_W), F32),
                        pltpu.VMEM((tb, ML_QK_W), F32), pltpu.VMEM((tb, ML_QK_W), F32),
                        pltpu.VMEM((tb + CHUNK, LANES), F32),
                        pltpu.VMEM((tb, ML_V_W), F32), pltpu.VMEM((tb, GLA_V_W), F32),
                        pltpu.VMEM((ML_HEADS, ML_DK, ML_DV + LANES), F32),
                        pltpu.VMEM((1, LANES), F32), pltpu.VMEM((GLA_HEADS, GLA_DV, GLA_DK), F32)],
        compiler_params=_cparams(("arbitrary", "arbitrary")),
        name="mlstm_gla",
    )(h, h, g, w_main, w_small, convw, convb, igb, fgb, lrup, gateb, mlnorm, glanorm)


def _proj_router_kernel(y_ref, w_ref, b_ref, h_ref, g_ref, wr_ref, br_ref, h1_ref, u2_ref, rt_ref, rk_ref, cnt_ref,
                        base_s, strict_s, rtp_s):
    step = pl.program_id(0)

    @pl.when(step == 0)
    def _():
        _rank_init(base_s, strict_s)
        rtp_s[...] = jnp.zeros(rtp_s.shape, F32)

    _rank_rows(rtp_s[...], step > 0, rk_ref.at[jnp.maximum(step - 1, 0)], cnt_ref, base_s, strict_s)

    h1 = h_ref[...] + (_dot(y_ref[...], w_ref[...]) + b_ref[...])
    h1_ref[...] = h1
    u2 = _rms(h1, g_ref[...])
    for c in range(ROW_TILES):
        u2_ref[pl.ds(c, u2.shape[0], stride=ROW_TILES), :] = u2[:, c * LANES:(c + 1) * LANES]
    u_hi = u2.astype(BF16)
    u_lo = (u2 - u_hi.astype(F32)).astype(BF16)
    part = _dot(u_hi, wr_ref[...])
    logits = part[:, :LANES] + (part[:, LANES:] + _dot(u_lo, wr_ref[:, :LANES])) + br_ref[...]
    lane = lax.broadcasted_iota(jnp.int32, logits.shape, 1)
    lane_f = lane.astype(F32)
    big = float(LANES)
    gl = jnp.where(lane < N_GROUPS, logits, -jnp.inf)
    g_max = jnp.max(gl, axis=-1, keepdims=True)
    g_idx = jnp.min(jnp.where(gl == g_max, lane_f, big), axis=-1, keepdims=True)
    g_p = 1.0 / jnp.sum(jnp.exp(gl - g_max), axis=-1, keepdims=True)
    e_grp = ((lane - N_GROUPS) // EXPERTS_PER_GROUP).astype(F32)
    in_grp = (lane >= N_GROUPS) & (lane < N_GROUPS + N_EXPERTS) & (e_grp == g_idx)
    el = jnp.where(in_grp, logits, -jnp.inf)
    t1 = jnp.max(el, axis=-1, keepdims=True)
    i1 = jnp.min(jnp.where(el == t1, lane_f, big), axis=-1, keepdims=True)
    el2 = jnp.where(lane_f == i1, -jnp.inf, el)
    t2 = jnp.max(el2, axis=-1, keepdims=True)
    i2 = jnp.min(jnp.where(el2 == t2, lane_f, big), axis=-1, keepdims=True)
    e21 = jnp.exp(t2 - t1)
    p1 = 1.0 / (1.0 + e21)
    rt = jnp.where(lane == 0, i1 - N_GROUPS,
                   jnp.where(lane == 1, i2 - N_GROUPS,
                             jnp.where(lane == 2, g_p * p1, jnp.where(lane == 3, g_p * (e21 * p1), 0.0))))
    rt_ref[...] = rt
    rtp_s[...] = rt

    @pl.when(step == pl.num_programs(0) - 1)
    def _():
        _rank_rows(rt, True, rk_ref.at[step], cnt_ref, base_s, strict_s)


def _proj_router(y, w, b, h, g, wr, br, tm):
    t, kdim = y.shape
    row = lambda i: (i, 0)
    fixed = lambda i: (0, 0)
    h1, u2, rt, rk, cnt = pl.pallas_call(
        _proj_router_kernel,
        out_shape=(jax.ShapeDtypeStruct((t, D_MODEL), F32), jax.ShapeDtypeStruct((t * ROW_TILES, LANES), F32),
                   jax.ShapeDtypeStruct((t, LANES), F32), jax.ShapeDtypeStruct((t // tm, 8, tm), jnp.int32),
                   jax.ShapeDtypeStruct((1, LANES), F32)),
        grid=(t // tm,),
        in_specs=[pl.BlockSpec((tm, kdim), row), pl.BlockSpec((kdim, D_MODEL), fixed),
                  pl.BlockSpec((1, D_MODEL), fixed), pl.BlockSpec((tm, D_MODEL), row),
                  pl.BlockSpec((1, D_MODEL), fixed), pl.BlockSpec((D_MODEL, 2 * LANES), fixed),
                  pl.BlockSpec((1, LANES), fixed)],
        out_specs=(pl.BlockSpec((tm, D_MODEL), row), pl.BlockSpec((tm * ROW_TILES, LANES), row),
                   pl.BlockSpec((tm, LANES), row), pl.BlockSpec((t // tm, 8, tm), lambda i: (0, 0, 0)),
                   pl.BlockSpec((1, LANES), fixed)),
        scratch_shapes=[pltpu.VMEM((1, LANES), F32), pltpu.VMEM((tm, tm), BF16), pltpu.VMEM((tm, LANES), F32)],
        compiler_params=_cparams(("arbitrary",)),
        name="proj_router",
    )(y, w, b, h, g, wr, br)
    return h1, u2, rt, rk.transpose(1, 0, 2).reshape(8, t), cnt


def _rank_init(base_s, strict_s):
    tt = strict_s.shape[0]
    base_s[...] = jnp.zeros(base_s.shape, F32)
    r = lax.broadcasted_iota(jnp.int32, (tt, tt), 0)
    c = lax.broadcasted_iota(jnp.int32, (tt, tt), 1)
    strict_s[...] = jnp.where(c < r, 1.0, 0.0).astype(BF16)


def _rank_rows(rt, valid, rk_ref, cnt_ref, base_s, strict_s):
    lane = lax.broadcasted_iota(jnp.int32, rt.shape, 1)
    lane_f = lane.astype(F32)
    e0, e1 = rt[:, 0:1], rt[:, 1:2]
    oh0 = (lane_f == e0) & valid
    oh1 = (lane_f == e1) & valid
    oh = jnp.where(oh0 | oh1, 1.0, 0.0)
    before = _dot(strict_s[...], oh.astype(BF16)) + base_s[...]
    r0 = jnp.sum(jnp.where(oh0, before, 0.0), axis=-1, keepdims=True)
    r1 = jnp.sum(jnp.where(oh1, before, 0.0), axis=-1, keepdims=True)
    table = jnp.where(lane == 0, r0, jnp.where(lane == 1, r1, jnp.where(lane == 2, e0, jnp.where(lane == 3, e1, 0.0))))
    rk_ref[...] = jnp.transpose(table)[0:8, :].astype(jnp.int32)
    base_s[...] = base_s[...] + jnp.sum(oh, axis=0, keepdims=True)
    cnt_ref[...] = base_s[...]


DMA_GROUP = 8


def _row_copy(src, dst, sem):
    return pltpu.make_async_copy(src, dst, sem)


def _row_tile(r):
    return pl.ds(pl.multiple_of(r * ROW_TILES, ROW_TILES), ROW_TILES)


def _dispatch_kernel(dest_ref, zflag_ref, fill_ref, u_ref, xout_hbm, inv_ref, zero_s, sem, *, tt):
    i = pl.program_id(0)
    block_rows = MOE_BM * ROW_TILES

    @pl.when(i == 0)
    def _():
        fill = pltpu.make_async_copy(fill_ref, inv_ref, sem)
        fill.start()
        fill.wait()
        zero_s[...] = jnp.zeros(zero_s.shape, F32)
        zero_block = lambda blk: _row_copy(zero_s, xout_hbm.at[pl.ds(blk * block_rows, block_rows)], sem)
        for blk in range(zflag_ref.shape[0]):
            @pl.when(zflag_ref[blk] != 0)
            def _():
                zero_block(blk).start()
        for blk in range(zflag_ref.shape[0]):
            @pl.when(zflag_ref[blk] != 0)
            def _():
                zero_block(blk).wait()

    def issue(g, carry):
        for jj in range(DMA_GROUP):
            j = g * DMA_GROUP + jj
            src = u_ref.at[_row_tile(j)]
            for k in range(TOP_K):
                d = dest_ref[k, j]
                _row_copy(src, xout_hbm.at[_row_tile(d)], sem).start(priority=k)
                inv_ref[d] = (i * tt + j) * TOP_K + k
        return carry

    lax.fori_loop(0, tt // DMA_GROUP, issue, 0)
    for k in range(TOP_K):
        _row_copy(u_ref, xout_hbm.at[pl.ds(0, tt * ROW_TILES)], sem).wait()


def _dispatch(dest, zflag, u2, tt):
    t = u2.shape[0] // ROW_TILES
    n_rows = zflag.shape[0] * MOE_BM
    row = jnp.arange(n_rows, dtype=jnp.int32)
    fill = t * TOP_K + (row // MOE_BM % 2) * MOE_BM + row % MOE_BM
    return pl.pallas_call(
        functools.partial(_dispatch_kernel, tt=tt),
        out_shape=(jax.ShapeDtypeStruct((n_rows * ROW_TILES, LANES), F32), jax.ShapeDtypeStruct((n_rows,), jnp.int32)),
        grid=(t // tt,),
        in_specs=[pl.BlockSpec((TOP_K, tt), lambda i: (0, i), memory_space=pltpu.SMEM),
                  pl.BlockSpec(memory_space=pltpu.SMEM),
                  pl.BlockSpec(memory_space=pltpu.VMEM),
                  pl.BlockSpec((tt * ROW_TILES, LANES), lambda i: (i, 0))],
        out_specs=(pl.BlockSpec(memory_space=pl.ANY), pl.BlockSpec(memory_space=pltpu.SMEM)),
        scratch_shapes=[pltpu.VMEM((MOE_BM * ROW_TILES, LANES), F32), pltpu.SemaphoreType.DMA],
        compiler_params=_cparams(("arbitrary",)),
        name="moe_dispatch",
    )(dest, zflag, fill, u2)


def _ffn_kernel(be_ref, nu_ref, inv_ref, par_ref, nxt_ref, x_ref, wg_hbm, wu_hbm, wd_hbm, o2_hbm, wg_f, wu_f, wd_f,
                wg_s, wu_s, wd_s, x_s, y_s, sem, wsem, *, n_slots, layer):
    b = pl.program_id(0)

    def weight_copies(e, s):
        return (pltpu.make_async_copy(wg_hbm.at[layer, e], wg_f.at[s], wsem.at[s]),
                pltpu.make_async_copy(wu_hbm.at[layer, e], wu_f.at[s], wsem.at[s]),
                pltpu.make_async_copy(wd_hbm.at[layer, e], wd_f.at[s], wsem.at[s]))

    last = pl.num_programs(0) - 1
    n_used = nu_ref[0]
    block_rows = MOE_BM * ROW_TILES

    def drain():
        _row_copy(y_s.at[0], o2_hbm.at[pl.ds(0, block_rows)], sem).wait()

    def scatter(blk):
        s = blk % 2
        for j in range(MOE_BM):
            slot = inv_ref[blk * MOE_BM + j]
            _row_copy(y_s.at[s, _row_tile(j)], o2_hbm.at[_row_tile(slot)], sem).start(priority=j % 2)

    def compute():
        for c in range(ROW_TILES):
            x_s[:, c * LANES:(c + 1) * LANES] = x_ref[pl.ds(c, MOE_BM, stride=ROW_TILES), :].astype(BF16)
        x = x_s[...]
        a = _dot(x, wg_s[...])
        u = _dot(x, wu_s[...])
        y = _dot(((a * _sigmoid(a)) * u).astype(BF16), wd_s[...])
        for c in range(ROW_TILES):
            y_s[b % 2, pl.ds(c, MOE_BM, stride=ROW_TILES), :] = y[:, c * LANES:(c + 1) * LANES]

    @pl.when(b == 0)
    def _():
        y_s[...] = jnp.zeros(y_s.shape, F32)
        for s in range(2):
            _row_copy(y_s.at[s], o2_hbm.at[pl.ds((n_slots + s * MOE_BM) * ROW_TILES, block_rows)], sem).start()
        for s in range(2):
            drain()

    @pl.when((b >= 2) & (b - 2 < n_used))
    def _():
        drain()

    @pl.when(b == 0)
    def _():
        for c in weight_copies(be_ref[0], 0):
            c.start()

    @pl.when((b == 0) | (be_ref[b] != be_ref[jnp.maximum(b - 1, 0)]))
    def _():
        s = par_ref[b]
        for c in weight_copies(0, s):
            c.wait()
        wg_s[...] = wg_f[s].astype(BF16)
        wu_s[...] = wu_f[s].astype(BF16)
        wd_s[...] = wd_f[s].astype(BF16)

        @pl.when(nxt_ref[b] >= 0)
        def _():
            for c in weight_copies(nxt_ref[b], 1 - s):
                c.start()

    @pl.when(b == 0)
    def _():
        compute()

    @pl.when((b >= 1) & (b < n_used))
    def _():
        scatter(b - 1)
        compute()

    @pl.when((b >= 1) & (b >= n_used) & (b - 1 < n_used))
    def _():
        scatter(b - 1)

    @pl.when(b == last)
    def _():
        @pl.when((b >= 1) & (b - 1 < n_used))
        def _():
            drain()

        @pl.when(b < n_used)
        def _():
            scatter(b)
            drain()


def _ffn(block_expert, n_used, inv, xbuf, wg, wu, wd, layer, n_slots):
    n_blocks = xbuf.shape[0] // (MOE_BM * ROW_TILES)
    starts = jnp.concatenate([jnp.ones((1,), jnp.int32), (block_expert[1:] != block_expert[:-1]).astype(jnp.int32)])
    parity = (jnp.cumsum(starts) - 1) % 2
    later = jnp.where(block_expert[None, :] > block_expert[:, None], block_expert[None, :], N_EXPERTS)
    nxt = jnp.min(later, axis=1)
    nxt = jnp.where(nxt == N_EXPERTS, -1, nxt).astype(jnp.int32)
    rows = lambda b, be, nu, inv, par, nx: (jnp.minimum(b, nu[0] - 1), 0)
    hbm = pl.BlockSpec(memory_space=pl.ANY)
    return pl.pallas_call(
        functools.partial(_ffn_kernel, n_slots=n_slots, layer=layer),
        out_shape=jax.ShapeDtypeStruct(((n_slots + 2 * MOE_BM) * ROW_TILES, LANES), F32),
        grid_spec=pltpu.PrefetchScalarGridSpec(
            num_scalar_prefetch=5,
            grid=(n_blocks,),
            in_specs=[pl.BlockSpec((MOE_BM * ROW_TILES, LANES), rows), hbm, hbm, hbm],
            out_specs=pl.BlockSpec(memory_space=pl.ANY),
            scratch_shapes=[pltpu.VMEM((2, D_MODEL, D_FF), F32), pltpu.VMEM((2, D_MODEL, D_FF), F32),
                            pltpu.VMEM((2, D_FF, D_MODEL), F32),
                            pltpu.VMEM((D_MODEL, D_FF), BF16), pltpu.VMEM((D_MODEL, D_FF), BF16),
                            pltpu.VMEM((D_FF, D_MODEL), BF16), pltpu.VMEM((MOE_BM, D_MODEL), BF16),
                            pltpu.VMEM((2, MOE_BM * ROW_TILES, LANES), F32), pltpu.SemaphoreType.DMA,
                            pltpu.SemaphoreType.DMA((2,))]),
        compiler_params=_cparams(("arbitrary",)),
        name="moe_ffn",
    )(block_expert, n_used, inv, parity.astype(jnp.int32), nxt, xbuf, wg, wu, wd)


def _combine_rows(o2_ref, h_ref, rt_ref, o_ref):
    tt = h_ref.shape[0]
    rt = rt_ref[...]
    w0, w1 = rt[:, 2:3], rt[:, 3:4]
    for c in range(ROW_TILES):
        lanes = slice(c * LANES, (c + 1) * LANES)
        y0 = o2_ref[pl.ds(c, tt, stride=TOP_K * ROW_TILES), :]
        y1 = o2_ref[pl.ds(ROW_TILES + c, tt, stride=TOP_K * ROW_TILES), :]
        o_ref[:, lanes] = h_ref[:, lanes] + (y0 * w0 + y1 * w1)


def _combine_kernel(o2_ref, h_ref, rt_ref, g_ref, o_ref):
    _combine_rows(o2_ref, h_ref, rt_ref, o_ref)
    o_ref[...] = _rms(o_ref[...], g_ref[...])


def _combine_norm(o2, h1, rt, g, tt):
    t = h1.shape[0]
    return pl.pallas_call(
        _combine_kernel,
        out_shape=jax.ShapeDtypeStruct((t, D_MODEL), F32),
        grid=(t // tt,),
        in_specs=[pl.BlockSpec((tt * TOP_K * ROW_TILES, LANES), lambda i: (i, 0)),
                  pl.BlockSpec((tt, D_MODEL), lambda i: (i, 0)),
                  pl.BlockSpec((tt, LANES), lambda i: (i, 0)),
                  pl.BlockSpec((1, D_MODEL), lambda i: (0, 0))],
        out_specs=pl.BlockSpec((tt, D_MODEL), lambda i: (i, 0)),
        compiler_params=_cparams(("parallel",)),
        name="moe_combine",
    )(o2, h1, rt, g)


def _moe_experts(u2, rk, cnt, wg, wu, wd, layer):
    t = rk.shape[1]
    counts = cnt[0, :N_EXPERTS].astype(jnp.int32)
    padded = (counts + MOE_BM - 1) // MOE_BM * MOE_BM
    pad_end = jnp.cumsum(padded)
    pad_start = pad_end - padded
    n_blocks = (t * TOP_K) // MOE_BM + N_EXPERTS
    is_expert = rk[TOP_K:2 * TOP_K][None] == jnp.arange(N_EXPERTS, dtype=jnp.int32)[:, None, None]
    dest = jnp.sum(jnp.where(is_expert, pad_start[:, None, None], 0), axis=0) + rk[0:TOP_K]
    n_used = (pad_end[-1] // MOE_BM).astype(jnp.int32)
    blocks = jnp.arange(n_blocks, dtype=jnp.int32)
    blk = jnp.minimum(blocks, n_used - 1) * MOE_BM
    block_expert = jnp.minimum(jnp.sum(pad_end[None, :] <= blk[:, None], axis=1), N_EXPERTS - 1).astype(jnp.int32)
    closes_expert = jnp.any(((blocks[:, None] + 1) * MOE_BM == pad_end[None, :]) & (padded[None, :] > 0), axis=1)
    zflag = ((blocks >= n_used) | closes_expert).astype(jnp.int32)
    xbuf, inv = _dispatch(dest, zflag, u2, 2048)
    return _ffn(block_expert, n_used.reshape(1), inv, xbuf, wg, wu, wd, layer, t * TOP_K)


def _attn_kernel(o2_ref, h1_ref, rt_ref, g_ref, w_ref, b_ref, pos_ref, freq_ref, sp_ref, sink_ref, o_ref, h_ref, q_s,
                 kv_s, *, tm, tiles_per_seq):
    seq_start = pl.program_id(0) % tiles_per_seq == 0

    @pl.when(seq_start)
    def _():
        kv_s[0:WINDOW, :] = jnp.zeros((WINDOW, KV_OUT_W), kv_s.dtype)

    _combine_rows(o2_ref, h1_ref, rt_ref, h_ref)
    _project_qkv(h_ref, g_ref, w_ref, b_ref, pos_ref, freq_ref, sp_ref, q_s, kv_s.at[pl.ds(WINDOW, tm)])
    for j in range(tm // WINDOW):
        has_prev = jnp.logical_not(seq_start) if j == 0 else True
        _swa_block(q_s.at[pl.ds(j * WINDOW, WINDOW)], kv_s.at[pl.ds((j + 1) * WINDOW, WINDOW)],
                   kv_s.at[pl.ds(j * WINDOW, WINDOW)], sink_ref, o_ref.at[pl.ds(j * WINDOW, WINDOW)], has_prev)
    kv_s[0:WINDOW, :] = kv_s[tm:tm + WINDOW, :]


def _project_qkv(x_ref, g_ref, w_ref, b_ref, pos_ref, freq_ref, sp_ref, q_ref, kv_ref):
    u = _rms(x_ref[...], g_ref[...]).astype(BF16)
    ang = freq_ref[...] * pos_ref[...].astype(F32)
    spread = lambda v, m: sum(_dot_tn(t.astype(F32), m) for t in _split_terms(v, 3))
    cosv = jnp.cos(ang)
    sinv = jnp.sin(ang)
    c_coef = spread(cosv, sp_ref[0]) + sp_ref[3, 0:1, :]
    s_lo = spread(sinv, sp_ref[1])
    s_hi = spread(sinv, sp_ref[2])

    def rotate(z):
        return z * c_coef + pltpu.roll(z, LANES - ROT_DIM // 2, 1) * s_lo + pltpu.roll(z, ROT_DIM // 2, 1) * s_hi

    scale = SWA_HEAD_DIM ** -0.5
    for j in range(0, Q_W // LANES, 2):
        z = _dot(u, w_ref[:, j * LANES:(j + 2) * LANES]) + b_ref[:, j * LANES:(j + 2) * LANES]
        for i in range(2):
            q_ref[:, (j + i) * LANES:(j + i + 1) * LANES] = (rotate(z[:, i * LANES:(i + 1) * LANES]) * scale).astype(q_ref.dtype)
    z = _dot(u, w_ref[:, Q_W:Q_W + 2 * LANES]) + b_ref[:, Q_W:Q_W + 2 * LANES]
    kv_ref[:, 0:LANES] = rotate(z[:, 0:LANES]).astype(kv_ref.dtype)
    kv_ref[:, LANES:2 * LANES] = z[:, LANES:2 * LANES].astype(kv_ref.dtype)
    z = _dot(u, w_ref[:, Q_W + 2 * LANES:Q_W + 3 * LANES]) + b_ref[:, Q_W + 2 * LANES:Q_W + 3 * LANES]
    kv_ref[:, 2 * LANES:3 * LANES] = rotate(z).astype(kv_ref.dtype)


def _rot_tables():
    half = ROT_DIM // 2
    inv_freq = (ROPE_THETA ** (-jnp.arange(0, ROT_DIM, 2, dtype=F32) / ROT_DIM)).reshape(half, 1)
    d = np.arange(LANES) % SWA_HEAD_DIM
    f = np.arange(half)[:, None]
    sp = np.zeros((4, half, LANES), np.float32)
    sp[0] = (d[None, :] < ROT_DIM) & (d[None, :] % half == f)
    sp[1] = -((d[None, :] < half) & (d[None, :] == f)).astype(np.float32)
    sp[2] = (d[None, :] >= half) & (d[None, :] < ROT_DIM) & (d[None, :] - half == f)
    sp[3, 0] = d >= ROT_DIM
    return inv_freq, jnp.asarray(sp)


def _attn(o2, h1, rt, g, w, b, pos, sinks, tm, seq):
    t = h1.shape[0]
    row = lambda i: (i, 0)
    fixed = lambda i: (0, 0)
    wtot = Q_W + KV_OUT_W
    return pl.pallas_call(
        functools.partial(_attn_kernel, tm=tm, tiles_per_seq=seq // tm),
        out_shape=(jax.ShapeDtypeStruct((t, Q_W), BF16), jax.ShapeDtypeStruct((t, D_MODEL), F32)),
        grid=(t // tm,),
        in_specs=[pl.BlockSpec((tm * TOP_K * ROW_TILES, LANES), row), pl.BlockSpec((tm, D_MODEL), row),
                  pl.BlockSpec((tm, LANES), row), pl.BlockSpec((1, D_MODEL), fixed),
                  pl.BlockSpec((D_MODEL, wtot), fixed), pl.BlockSpec((1, wtot), fixed),
                  pl.BlockSpec((1, tm), lambda i: (0, i)), pl.BlockSpec((ROT_DIM // 2, 1), fixed),
                  pl.BlockSpec((4, ROT_DIM // 2, LANES), lambda i: (0, 0, 0)), pl.BlockSpec((1, LANES), fixed)],
        out_specs=(pl.BlockSpec((tm, Q_W), row), pl.BlockSpec((tm, D_MODEL), row)),
        scratch_shapes=[pltpu.VMEM((tm, Q_W), BF16), pltpu.VMEM((tm + WINDOW, KV_OUT_W), BF16)],
        compiler_params=_cparams(("arbitrary",)),
        name="combine_qkv_swa",
    )(o2, h1, rt, g, w, b, pos, *_rot_tables(), sinks)


def _swa_block(q_ref, kvc_ref, kvp_ref, sink_ref, o_ref, has_prev):
    neg = jnp.where(has_prev, 0.0, -jnp.inf).astype(F32)
    kj = lax.broadcasted_iota(jnp.int32, (WINDOW, WINDOW), 0)
    qi = lax.broadcasted_iota(jnp.int32, (WINDOW, WINDOW), 1)
    mask_cur = kj <= qi
    top = kj < SWA_HEAD_DIM
    lane = lax.broadcasted_iota(jnp.int32, (1, LANES), 1)
    keep_lo = jnp.where(lane < SWA_HEAD_DIM, 1.0, 0.0).astype(BF16)
    keep_hi = jnp.where(lane < SWA_HEAD_DIM, 0.0, 1.0).astype(BF16)
    zeros_half = jnp.zeros((SWA_HEAD_DIM, WINDOW), F32)

    def arranged(ref):
        k_nat, k_swp = ref[:, 0:KV_W], ref[:, 2 * KV_W:3 * KV_W]
        vt = jnp.transpose(ref[:, KV_W:2 * KV_W].astype(F32))
        vt_top = lambda g: jnp.concatenate([vt[g * SWA_HEAD_DIM:(g + 1) * SWA_HEAD_DIM], zeros_half], axis=0).astype(BF16)
        vt_bot = lambda g: jnp.concatenate([zeros_half, vt[g * SWA_HEAD_DIM:(g + 1) * SWA_HEAD_DIM]], axis=0).astype(BF16)
        return {(0, 0): (k_nat * keep_lo, vt_top(0)), (0, 1): (k_swp * keep_hi, vt_bot(0)),
                (1, 0): (k_swp * keep_lo, vt_top(1)), (1, 1): (k_nat * keep_hi, vt_bot(1))}

    cur = arranged(kvc_ref)
    prev = arranged(kvp_ref)
    heads = range(SWA_Q_HEADS)
    key = lambda h: (h // SWA_GROUP, h % 2)
    scores = []
    for h in heads:
        qp = q_ref[:, (h // 2) * LANES:(h // 2 + 1) * LANES]
        sc = _dot_nt(cur[key(h)][0], qp)
        sp = _dot_nt(prev[key(h)][0], qp)
        scores.append(jnp.where(mask_cur, sc, sp + neg))
    sinks = [sink_ref[:, h:h + 1] for h in heads]
    maxes = [jnp.maximum(jnp.max(scores[h], axis=0, keepdims=True), sinks[h]) for h in heads]
    probs = [jnp.exp(scores[h] - maxes[h]) for h in heads]
    denoms = [jnp.sum(probs[h], axis=0, keepdims=True) + jnp.exp(sinks[h] - maxes[h]) for h in heads]
    for pr in range(SWA_Q_HEADS // 2):
        num = None
        for h in (2 * pr, 2 * pr + 1):
            pc = jnp.where(mask_cur, probs[h], 0.0).astype(BF16)
            pp = jnp.where(mask_cur, 0.0, probs[h]).astype(BF16)
            part = _dot(cur[key(h)][1], pc) + _dot(prev[key(h)][1], pp)
            num = part if num is None else num + part
        den = jnp.where(top, denoms[2 * pr], denoms[2 * pr + 1])
        o_ref[:, pr * LANES:(pr + 1) * LANES] = jnp.transpose(num / den).astype(o_ref.dtype)


def _pack_in_proj(w_in):
    offs = np.cumsum((0,) + AB_SPLITS)
    part = lambda i: w_in[:, offs[i]:offs[i + 1]]
    w_main = jnp.concatenate([part(0), part(1), part(2), part(5), part(6), part(7), part(8)], axis=1).astype(BF16)
    small = jnp.concatenate([part(3), part(4), part(9)], axis=1)
    w_small = jnp.pad(small, ((0, 0), (0, LANES - small.shape[1]))).astype(BF16)
    return w_main, w_small


def _pack_qkv(w, b):
    hd = SWA_HEAD_DIM
    k0, k1 = slice(Q_W, Q_W + hd), slice(Q_W + hd, Q_W + 2 * hd)
    cols = lambda a: jnp.concatenate([a[..., :Q_W + 2 * KV_W], a[..., k1], a[..., k0]], axis=-1)
    return cols(w).astype(BF16), cols(b).reshape(1, -1).astype(F32)


def _lane_row(v, offset=0):
    return jnp.zeros((1, LANES), F32).at[0, offset:offset + v.shape[0]].set(v.astype(F32))


def kernel(x, positions, mix_norm, ffn_norm, final_norm, ab_w_in, ab_conv_w, ab_conv_b, ml_igate_b, ml_fgate_b, ml_head_norm, gla_w_lr_up, gla_gate_b, gla_head_norm, ab_w_out, swa_w_qkv, swa_b_qkv, swa_sinks, swa_w_o, swa_b_o, router_group_w, router_group_b, router_expert_w, router_expert_b, expert_w_gate, expert_w_up, expert_w_down):
    bsz, seq, d = x.shape
    t = bsz * seq
    h = x.reshape(t, d)
    row = lambda v: v.reshape(1, -1).astype(F32)

    def router_params(layer):
        wr = jnp.zeros((d, LANES), F32).at[:, :N_GROUPS].set(router_group_w[layer])
        wr = wr.at[:, N_GROUPS:N_GROUPS + N_EXPERTS].set(router_expert_w[layer])
        br = jnp.zeros((1, LANES), F32).at[0, :N_GROUPS].set(router_group_b[layer])
        br = br.at[0, N_GROUPS:N_GROUPS + N_EXPERTS].set(router_expert_b[layer])
        w_hi = wr.astype(BF16)
        w_lo = (wr - w_hi.astype(F32)).astype(BF16)
        return jnp.concatenate([w_hi, w_lo], axis=1), br

    def experts(layer):
        return expert_w_gate, expert_w_up, expert_w_down, layer

    w_main, w_small = _pack_in_proj(ab_w_in[0])
    lrup = jnp.zeros((LANES, GLA_QK_W), F32).at[SM_LR:SM_LR + GLA_LOWRANK].set(gla_w_lr_up[0]).astype(BF16)
    y = _mixer(h, row(mix_norm[0]), w_main, w_small, ab_conv_w[0], row(ab_conv_b[0]), _lane_row(ml_igate_b[0]),
               _lane_row(ml_fgate_b[0]), lrup, row(gla_gate_b[0]), row(ml_head_norm[0]), row(gla_head_norm[0]),
               bsz, seq, 256)
    wr, br = router_params(0)
    h1, u2, rt, rk, cnt = _proj_router(y, ab_w_out[0].astype(BF16), jnp.zeros((1, d), F32), h, row(ffn_norm[0]), wr, br, 512)
    o2 = _moe_experts(u2, rk, cnt, *experts(0))

    w_qkv, b_qkv = _pack_qkv(swa_w_qkv[0], swa_b_qkv[0])
    o, h = _attn(o2, h1, rt, row(mix_norm[1]), w_qkv, b_qkv, positions.reshape(1, t), _lane_row(swa_sinks[0]), 512, seq)
    wr, br = router_params(1)
    h1, u2, rt, rk, cnt = _proj_router(o, swa_w_o[0].astype(BF16), row(swa_b_o[0]), h, row(ffn_norm[1]), wr, br, 512)
    out = _combine_norm(_moe_experts(u2, rk, cnt, *experts(1)), h1, rt, row(final_norm), 1024)
    return out.reshape(bsz, seq, d)
```
